```python
import math
import jax, jax.numpy as jnp
from jax import lax
import numpy as np

D_MODEL = 1024
BATCH = 2
SEQ = 16384
DEPTH = 2

D_MIX = D_MODEL
D_GROUP = D_MIX // 4
HEAD_DIM = 64
RET_HEADS = D_GROUP // HEAD_DIM
ML_HEADS = D_GROUP // HEAD_DIM
CHUNK = 128
S5_CH = 16
S5_GROUPS = D_GROUP // S5_CH
S5_STATE = 64
HY_ORDER = 2
HY_EMB = 33
HY_BANDS = (HY_EMB - 1) // 2
HY_FFN = 64
HY_FAST_DECAY = 0.3
HY_SLOW_DECAY = 1.5
HY_TARGET = 1e-2
SHORT_CONV = 3
D_FF = 2816
N_EXPERTS = 8
TOP_K = 2
D_FF_EXPERT = 3584
ROPE_BASE = 10000.0
EPS = 1e-5
DN_ALPHA = (2 * DEPTH) ** 0.25
DN_BETA = (8 * DEPTH) ** -0.25
N_DENSE = (DEPTH + 1) // 2
N_MOE = DEPTH // 2

RET_COLS = 4 * D_GROUP
S5_COLS = D_GROUP
HY_COLS = (HY_ORDER + 1) * D_GROUP
ML_COLS = 4 * D_GROUP + 4 * ML_HEADS
D_IN = RET_COLS + S5_COLS + HY_COLS + ML_COLS

kernel_name = "hybrid_parallel_heads_encoder"


def layer_norm(x, w, b):
    xf = x.astype(jnp.float32)
    mu = jnp.mean(xf, -1, keepdims=True)
    var = jnp.mean(jnp.square(xf - mu), -1, keepdims=True)
    y = (xf - mu) * lax.rsqrt(var + EPS) * w.astype(jnp.float32) + b.astype(jnp.float32)
    return y.astype(x.dtype)


def head_norm(h, w):
    mu = jnp.mean(h, -1, keepdims=True)
    var = jnp.mean(jnp.square(h - mu), -1, keepdims=True)
    y = (h - mu) * lax.rsqrt(var + EPS)
    return y.reshape(h.shape[0], h.shape[1], -1) * w.astype(jnp.float32)


def flip(t):
    return t[:, ::-1]


def short_conv(x, w, b):
    L = x.shape[1]
    pad = SHORT_CONV // 2
    xp = jnp.pad(x, ((0, 0), (pad, pad), (0, 0)))
    y = b
    for j in range(SHORT_CONV):
        y = y + xp[:, j:j + L] * w[j]
    return y


def rotary(t):
    L = t.shape[1]
    half = HEAD_DIM // 2
    inv = ROPE_BASE ** (-jnp.arange(half, dtype=jnp.float32) / half)
    ang = jnp.arange(L, dtype=jnp.float32)[:, None] * inv[None, :]
    cos = jnp.cos(ang)[None, :, None, :]
    sin = jnp.sin(ang)[None, :, None, :]
    t1, t2 = t[..., :half], t[..., half:]
    return jnp.concatenate([t1 * cos - t2 * sin, t2 * cos + t1 * sin], -1)


def to_chunks(t):
    B_, L = t.shape[:2]
    t = t.reshape(B_, L // CHUNK, CHUNK, *t.shape[2:])
    return jnp.moveaxis(t, (1, 3), (0, 2))


def from_chunks(t):
    t = jnp.moveaxis(t, (0, 2), (1, 3))
    return t.reshape(t.shape[0], -1, *t.shape[3:])


def retention_scan(q, k, v, log_gamma, include_diag):
    B_, L, H, d = q.shape
    pos = jnp.arange(CHUNK, dtype=jnp.float32)
    lag = pos[:, None] - pos[None, :]
    mask = lag >= 0 if include_diag else lag > 0
    d_in = jnp.where(mask[None], jnp.exp(log_gamma[:, None, None] * jnp.maximum(lag, 0.0)[None]), 0.0)
    q_dec = jnp.exp(log_gamma[:, None] * (pos[None, :] + 1.0))[None, :, :, None]
    k_dec = jnp.exp(log_gamma[:, None] * (CHUNK - 1.0 - pos[None, :]))[None, :, :, None]
    c_dec = jnp.exp(log_gamma * CHUNK)[None, :, None, None]

    def step(state, blk):
        qi, ki, vi = blk
        s = jnp.einsum("bhnd,bhmd->bhnm", qi, ki) * d_in
        out = jnp.einsum("bhnm,bhme->bhne", s, vi) + jnp.einsum("bhnd,bhde->bhne", qi, state) * q_dec
        state = state * c_dec + jnp.einsum("bhmd,bhme->bhde", ki * k_dec, vi)
        return state, out

    state0 = jnp.zeros((B_, H, d, d), jnp.float32)
    _, out = lax.scan(step, state0, (to_chunks(q), to_chunks(k), to_chunks(v)))
    return from_chunks(out)


def retention_mixer(p, gn_w):
    B_, L, _ = p.shape
    q, k, v, g = jnp.split(p.astype(jnp.float32), 4, axis=-1)
    shp = (B_, L, RET_HEADS, HEAD_DIM)
    q = rotary(q.reshape(shp))
    k = rotary(k.reshape(shp)) * HEAD_DIM ** -0.5
    v = v.reshape(shp)
    log_gamma = jnp.log(1.0 - 2.0 ** (-5.0 - jnp.arange(RET_HEADS, dtype=jnp.float32)))
    o = retention_scan(q, k, v, log_gamma, True) + flip(retention_scan(flip(q), flip(k), flip(v), log_gamma, False))
    return jax.nn.silu(g) * head_norm(o, gn_w)


def s5_mixer(u, a_re, a_im, log_dt, b_re, b_im, c_re, c_im, d_skip, w_glu):
    B_, L, _ = u.shape
    f32 = jnp.float32
    uf = u.astype(f32)
    ug = uf.reshape(B_, L, S5_GROUPS, S5_CH)
    a = lax.complex(a_re.astype(f32), a_im.astype(f32))
    delta = jnp.exp(log_dt.astype(f32))[..., None]
    a_bar = jnp.exp(a * delta)
    b = lax.complex(b_re.astype(f32), b_im.astype(f32))
    b_bar = ((a_bar - 1.0) / a)[..., None] * b[None]
    c = lax.complex(c_re.astype(f32), c_im.astype(f32))

    def combine(e1, e2):
        a1, x1 = e1
        a2, x2 = e2
        return a1 * a2, a2 * x1 + x2

    def run(direction, reverse):
        bu = jnp.einsum("gpc,blgc->blgp", b_bar[direction], ug)
        decay = jnp.broadcast_to(a_bar[direction], bu.shape)
        _, states = lax.associative_scan(combine, (decay, bu), reverse=reverse, axis=1)
        return jnp.einsum("gcp,blgp->blgc", c[direction], states).real

    y = (run(0, False) + run(1, True)).reshape(B_, L, D_GROUP) + d_skip.astype(f32) * uf
    z = jax.nn.gelu(y)
    return z * jax.nn.sigmoid(z @ w_glu.astype(f32))


def hyena_filters(L, w1, b1, w2, b2, w3, freq):
    f32 = jnp.float32
    t = jnp.linspace(0.0, 1.0, L, dtype=f32)[:, None]
    w = 2.0 * math.pi * jnp.arange(L, dtype=f32)[:, None] / L
    bands = jnp.linspace(1e-4, HY_BANDS - 1, HY_BANDS, dtype=f32)[None, :]
    z = jnp.concatenate([t, jnp.cos(bands * w), -jnp.sin(bands * w)], axis=-1)
    fr = freq.astype(f32)
    h = jnp.sin(fr * (z @ w1.astype(f32) + b1.astype(f32)))
    h = jnp.sin(fr * (h @ w2.astype(f32) + b2.astype(f32)))
    h = (h @ w3.astype(f32)).reshape(L, HY_ORDER, 2, D_GROUP)
    max_decay = math.log(HY_TARGET) / HY_FAST_DECAY
    min_decay = math.log(HY_TARGET) / HY_SLOW_DECAY
    rates = jnp.abs(jnp.linspace(min_decay, max_decay, D_GROUP, dtype=f32))
    h = h * jnp.exp(-t * rates)[:, None, None, :]
    return h * lax.rsqrt(jnp.sum(jnp.square(h), axis=(0, 2), keepdims=True))


def fft_long_conv(u, h_fwd, h_bwd):
    L = u.shape[1]
    kern = jnp.concatenate([h_fwd, jnp.zeros_like(h_fwd[:1]), h_bwd[:0:-1]], axis=0)
    kf = jnp.fft.rfft(kern, n=2 * L, axis=0)
    uf = jnp.fft.rfft(u, n=2 * L, axis=1)
    return jnp.fft.irfft(uf * kf[None], n=2 * L, axis=1)[:, :L]


def hyena_mixer(p, conv_w, conv_b, w1, b1, w2, b2, w3, freq, bias):
    L = p.shape[1]
    f32 = jnp.float32
    pc = short_conv(p.astype(f32), conv_w.astype(f32), conv_b.astype(f32))
    parts = jnp.split(pc, HY_ORDER + 1, axis=-1)
    filt = hyena_filters(L, w1, b1, w2, b2, w3, freq)
    bias = bias.astype(f32)
    z = parts[0]
    for n in range(HY_ORDER):
        z = parts[n + 1] * (fft_long_conv(z, filt[:, n, 0], filt[:, n, 1]) + z * bias[n])
    return z


def mlstm_scan(q, k, v, log_i, log_f):
    B_, L, H, d = q.shape
    causal = jnp.tril(jnp.ones((CHUNK, CHUNK), bool))

    def step(carry, blk):
        c_mem, n_mem, m_prev = carry
        qi, ki, vi, ii, fi = blk
        a = jnp.cumsum(fi, axis=-1)
        a_end = a[..., -1]
        dmat = jnp.where(causal, a[..., :, None] - a[..., None, :] + ii[..., None, :], -jnp.inf)
        inter = a + m_prev[..., None]
        m_row = jnp.maximum(inter, jnp.max(dmat, -1))
        s = jnp.einsum("bhnd,bhmd->bhnm", qi, ki) * jnp.exp(dmat - m_row[..., None])
        w_inter = jnp.exp(inter - m_row)
        num = jnp.einsum("bhnm,bhme->bhne", s, vi) + w_inter[..., None] * jnp.einsum("bhnd,bhde->bhne", qi, c_mem)
        den = jnp.sum(s, -1) + w_inter * jnp.einsum("bhnd,bhd->bhn", qi, n_mem)
        h = num / jnp.maximum(jnp.abs(den), jnp.exp(-m_row))[..., None]
        g = a_end[..., None] - a + ii
        m_new = jnp.maximum(a_end + m_prev, jnp.max(g, -1))
        wk = jnp.exp(g - m_new[..., None])[..., None] * ki
        carry_decay = jnp.exp(a_end + m_prev - m_new)
        c_mem = carry_decay[..., None, None] * c_mem + jnp.einsum("bhmd,bhme->bhde", wk, vi)
        n_mem = carry_decay[..., None] * n_mem + jnp.sum(wk, axis=2)
        return (c_mem, n_mem, m_new), h

    carry0 = (jnp.zeros((B_, H, d, d), jnp.float32), jnp.zeros((B_, H, d), jnp.float32), jnp.zeros((B_, H), jnp.float32))
    _, h = lax.scan(step, carry0, (to_chunks(q), to_chunks(k), to_chunks(v), to_chunks(log_i), to_chunks(log_f)))
    return from_chunks(h)


def mlstm_mixer(p, conv_w, conv_b, gate_b, gn_w):
    B_, L, _ = p.shape
    f32 = jnp.float32
    pf = p.astype(f32)
    qk, v, o, gates = jnp.split(pf, [2 * D_GROUP, 3 * D_GROUP, 4 * D_GROUP], axis=-1)
    qk = jax.nn.silu(short_conv(qk, conv_w.astype(f32), conv_b.astype(f32)))
    q, k = jnp.split(qk, 2, axis=-1)
    shp = (B_, L, ML_HEADS, HEAD_DIM)
    q = q.reshape(shp)
    k = k.reshape(shp) * HEAD_DIM ** -0.5
    v = v.reshape(shp)
    gates = gates.reshape(B_, L, 4, ML_HEADS) + gate_b.astype(f32)
    i_fw, f_fw, i_bw, f_bw = gates[:, :, 0], gates[:, :, 1], gates[:, :, 2], gates[:, :, 3]
    h_fw = mlstm_scan(q, k, v, i_fw, jax.nn.log_sigmoid(f_fw))
    h_bw = flip(mlstm_scan(flip(q), flip(k), flip(v), flip(i_bw), flip(jax.nn.log_sigmoid(f_bw))))
    return jax.nn.sigmoid(o) * head_norm(h_fw + h_bw, gn_w)


def token_mixer(h, w_in, w_out, ret_gn_w,
                s5_a_re, s5_a_im, s5_log_dt, s5_b_re, s5_b_im, s5_c_re, s5_c_im, s5_d, s5_w_glu,
                hy_conv_w, hy_conv_b, hy_w1, hy_b1, hy_w2, hy_b2, hy_w3, hy_freq, hy_bias,
                ml_conv_w, ml_conv_b, ml_gate_b, ml_gn_w):
    proj = h @ w_in
    p_ret, p_s5, p_hy, p_ml = jnp.split(proj, [RET_COLS, RET_COLS + S5_COLS, RET_COLS + S5_COLS + HY_COLS], axis=-1)
    y = jnp.concatenate([
        retention_mixer(p_ret, ret_gn_w),
        s5_mixer(p_s5, s5_a_re, s5_a_im, s5_log_dt, s5_b_re, s5_b_im, s5_c_re, s5_c_im, s5_d, s5_w_glu),
        hyena_mixer(p_hy, hy_conv_w, hy_conv_b, hy_w1, hy_b1, hy_w2, hy_b2, hy_w3, hy_freq, hy_bias),
        mlstm_mixer(p_ml, ml_conv_w, ml_conv_b, ml_gate_b, ml_gn_w),
    ], axis=-1)
    return y.astype(h.dtype) @ w_out


def swiglu(x, w1, w3, w2):
    return (jax.nn.silu(x @ w1) * (x @ w3)) @ w2


def moe_swiglu(x, router, w1, w3, w2):
    B_, L, D = x.shape
    xt = x.reshape(B_ * L, D)
    logits = (xt @ router).astype(jnp.float32)
    top_val, top_idx = lax.top_k(logits, TOP_K)
    gate = jax.nn.softmax(top_val, axis=-1)
    combine = jnp.einsum("tk,tke->te", gate, jax.nn.one_hot(top_idx, N_EXPERTS, dtype=jnp.float32))
    out = jnp.zeros_like(xt)
    for e in range(N_EXPERTS):
        out = out + combine[:, e:e + 1].astype(x.dtype) * swiglu(xt, w1[e], w3[e], w2[e])
    return out.reshape(B_, L, D)


def setup_inputs(seed: int = 0) -> dict:
    key = jax.random.key(seed)
    ks = iter(jax.random.split(key, 64))
    f32 = jnp.float32

    def nrm(shape, scale):
        return jax.random.normal(next(ks), shape, f32) * scale

    def gain(shape):
        return 1.0 + nrm(shape, 0.02)

    G, P = S5_GROUPS, S5_STATE
    ig = nrm((DEPTH, 2, ML_HEADS), 0.1)
    fg = jnp.linspace(3.0, 6.0, ML_HEADS, dtype=f32) + nrm((DEPTH, 2, ML_HEADS), 0.01)
    return {
        "x": nrm((BATCH, SEQ, D_MODEL), 1.0),
        "ln_in_w": gain((D_MODEL,)),
        "ln_in_b": nrm((D_MODEL,), 0.02),
        "w_in": nrm((DEPTH, D_MODEL, D_IN), D_MODEL ** -0.5),
        "w_out": nrm((DEPTH, D_MIX, D_MODEL), DN_BETA * D_MIX ** -0.5),
        "ret_gn_w": gain((DEPTH, D_GROUP)),
        "s5_a_re": -0.5 + nrm((DEPTH, 2, G, P), 0.01),
        "s5_a_im": math.pi * jnp.arange(P, dtype=f32) + nrm((DEPTH, 2, G, P), 0.01),
        "s5_log_dt": jax.random.uniform(next(ks), (DEPTH, 2, G), f32, math.log(1e-3), math.log(1e-1)),
        "s5_b_re": nrm((DEPTH, G, P, S5_CH), (2 * S5_CH) ** -0.5),
        "s5_b_im": nrm((DEPTH, G, P, S5_CH), (2 * S5_CH) ** -0.5),
        "s5_c_re": nrm((DEPTH, 2, G, S5_CH, P), (2 * P) ** -0.5),
        "s5_c_im": nrm((DEPTH, 2, G, S5_CH, P), (2 * P) ** -0.5),
        "s5_d": nrm((DEPTH, D_GROUP), 1.0),
        "s5_w_glu": nrm((DEPTH, D_GROUP, D_GROUP), D_GROUP ** -0.5),
        "hy_conv_w": nrm((DEPTH, SHORT_CONV, HY_COLS), SHORT_CONV ** -0.5),
        "hy_conv_b": nrm((DEPTH, HY_COLS), 0.02),
        "hy_w1": nrm((DEPTH, HY_EMB, HY_FFN), HY_EMB ** -0.5),
        "hy_b1": nrm((DEPTH, HY_FFN), 0.1),
        "hy_w2": nrm((DEPTH, HY_FFN, HY_FFN), HY_FFN ** -0.5),
        "hy_b2": nrm((DEPTH, HY_FFN), 0.1),
        "hy_w3": nrm((DEPTH, HY_FFN, HY_ORDER * 2 * D_GROUP), HY_FFN ** -0.5),
        "hy_freq": 1.0 + nrm((DEPTH, HY_FFN), 0.01),
        "hy_bias": nrm((DEPTH, HY_ORDER, D_GROUP), 1.0),
        "ml_conv_w": nrm((DEPTH, SHORT_CONV, 2 * D_GROUP), SHORT_CONV ** -0.5),
        "ml_conv_b": nrm((DEPTH, 2 * D_GROUP), 0.02),
        "ml_gate_b": jnp.stack([ig[:, 0], fg[:, 0], ig[:, 1], fg[:, 1]], axis=1),
        "ml_gn_w": gain((DEPTH, D_GROUP)),
        "ln1_w": gain((DEPTH, D_MODEL)),
        "ln1_b": nrm((DEPTH, D_MODEL), 0.02),
        "ln2_w": gain((DEPTH, D_MODEL)),
        "ln2_b": nrm((DEPTH, D_MODEL), 0.02),
        "ffn_w1": nrm((N_DENSE, D_MODEL, D_FF), D_MODEL ** -0.5),
        "ffn_w3": nrm((N_DENSE, D_MODEL, D_FF), D_MODEL ** -0.5),
        "ffn_w2": nrm((N_DENSE, D_FF, D_MODEL), DN_BETA * D_FF ** -0.5),
        "moe_router": nrm((N_MOE, D_MODEL, N_EXPERTS), D_MODEL ** -0.5),
        "moe_w1": nrm((N_MOE, N_EXPERTS, D_MODEL, D_FF_EXPERT), D_MODEL ** -0.5),
        "moe_w3": nrm((N_MOE, N_EXPERTS, D_MODEL, D_FF_EXPERT), D_MODEL ** -0.5),
        "moe_w2": nrm((N_MOE, N_EXPERTS, D_FF_EXPERT, D_MODEL), DN_BETA * D_FF_EXPERT ** -0.5),
    }


def reference(x, ln_in_w, ln_in_b, w_in, w_out, ret_gn_w,
              s5_a_re, s5_a_im, s5_log_dt, s5_b_re, s5_b_im, s5_c_re, s5_c_im, s5_d, s5_w_glu,
              hy_conv_w, hy_conv_b, hy_w1, hy_b1, hy_w2, hy_b2, hy_w3, hy_freq, hy_bias,
              ml_conv_w, ml_conv_b, ml_gate_b, ml_gn_w,
              ln1_w, ln1_b, ln2_w, ln2_b,
              ffn_w1, ffn_w3, ffn_w2, moe_router, moe_w1, moe_w3, moe_w2):
    h = layer_norm(x, ln_in_w, ln_in_b)
    for l in range(DEPTH):
        mix = token_mixer(h, w_in[l], w_out[l], ret_gn_w[l],
                          s5_a_re[l], s5_a_im[l], s5_log_dt[l], s5_b_re[l], s5_b_im[l],
                          s5_c_re[l], s5_c_im[l], s5_d[l], s5_w_glu[l],
                          hy_conv_w[l], hy_conv_b[l], hy_w1[l], hy_b1[l], hy_w2[l], hy_b2[l],
                          hy_w3[l], hy_freq[l], hy_bias[l],
                          ml_conv_w[l], ml_conv_b[l], ml_gate_b[l], ml_gn_w[l])
        h = layer_norm(DN_ALPHA * h + mix, ln1_w[l], ln1_b[l])
        if l % 2 == 0:
            f = swiglu(h, ffn_w1[l // 2], ffn_w3[l // 2], ffn_w2[l // 2])
        else:
            f = moe_swiglu(h, moe_router[l // 2], moe_w1[l // 2], moe_w3[l // 2], moe_w2[l // 2])
        h = layer_norm(DN_ALPHA * h + f, ln2_w[l], ln2_b[l])
    return h
```

```python
import functools
import math

import numpy as np
import jax
import jax.numpy as jnp
from jax import lax
from jax.experimental import pallas as pl
from jax.experimental.pallas import tpu as pltpu

F32 = jnp.float32
BF16 = jnp.bfloat16

D_MODEL = 1024
DEPTH = 2
D_GROUP = 256
HEAD_DIM = 64
N_HEADS = 4
CHUNK = 128
S5_CH = 16
S5_GROUPS = 16
S5_STATE = 64
HY_ORDER = 2
HY_EMB = 33
HY_BANDS = 16
HY_FFN = 64
HY_FAST_DECAY = 0.3
HY_SLOW_DECAY = 1.5
HY_TARGET = 1e-2
N_EXPERTS = 8
ROPE_BASE = 10000.0
EPS = 1e-5
DN_ALPHA = (2 * DEPTH) ** 0.25

LANES = 128
S5_TC = 32
FFT_N2 = 128
N_EXT = 3840

CB_RQ, CB_RK, CB_RV, CB_RG, CB_S5, CB_HV, CB_HX1, CB_HX2 = 0, 1, 2, 3, 4, 5, 6, 7
CB_MQ, CB_MK, CB_MV, CB_MO, CB_RQR, CB_RKR = 8, 9, 10, 11, 12, 13
GATE_COL128 = 28


def _cparams(sem, vmem_mb=None):
    kw = dict(dimension_semantics=sem)
    if vmem_mb is not None:
        kw["vmem_limit_bytes"] = vmem_mb * 1024 * 1024
    return pltpu.CompilerParams(**kw)


def _tile(n, pref):
    return pref if n % pref == 0 else n


def _split_dot(x, m, parts=3):
    acc = None
    r = x
    for _ in range(parts):
        hi = r.astype(BF16)
        t = jnp.dot(hi, m, preferred_element_type=F32)
        acc = t if acc is None else acc + t
        r = r - hi.astype(F32)
    return acc


def _split_dot_left(m, x, parts=3):
    acc = None
    r = x
    for _ in range(parts):
        hi = r.astype(BF16)
        t = jnp.dot(m, hi, preferred_element_type=F32)
        acc = t if acc is None else acc + t
        r = r - hi.astype(F32)
    return acc


def _dot_nt(a, b):
    return lax.dot_general(a, b, (((1,), (1,)), ((), ())), preferred_element_type=F32)


def _dot_tn(a, b):
    return lax.dot_general(a, b, (((0,), (0,)), ((), ())), preferred_element_type=F32)


def _sigmoid(x):
    return 1.0 / (1.0 + jnp.exp(-x))


def _silu(x):
    return x * _sigmoid(x)


def _log_sigmoid(x):
    return jnp.minimum(x, 0.0) - jnp.log(1.0 + jnp.exp(-jnp.abs(x)))


def _head_masks(dtype):
    lane = lax.broadcasted_iota(jnp.int32, (1, D_GROUP), 1)
    return [((lane >= h * HEAD_DIM) & (lane < (h + 1) * HEAD_DIM)).astype(dtype) for h in range(N_HEADS)]


def _ln_core(x, w, b):
    mu = jnp.mean(x, -1, keepdims=True)
    xc = x - mu
    var = jnp.mean(xc * xc, -1, keepdims=True)
    return xc * lax.rsqrt(var + EPS) * w + b


def _ln_kernel(x_ref, w_ref, b_ref, o_ref):
    o_ref[...] = _ln_core(x_ref[...], w_ref[...], b_ref[...])


def _ln_res_kernel(h_ref, m_ref, w_ref, b_ref, o_ref):
    o_ref[...] = _ln_core(DN_ALPHA * h_ref[...] + m_ref[...], w_ref[...], b_ref[...])


def _layer_norm(x, w, b, res=None):
    t, d = x.shape
    tm = _tile(t, 512)
    row = pl.BlockSpec((tm, d), lambda i: (i, 0))
    vec = pl.BlockSpec((1, d), lambda i: (0, 0))
    w2, b2 = w.reshape(1, d), b.reshape(1, d)
    if res is None:
        return pl.pallas_call(_ln_kernel, out_shape=jax.ShapeDtypeStruct((t, d), F32), grid=(t // tm,),
                              in_specs=[row, vec, vec], out_specs=row,
                              compiler_params=_cparams(("parallel",)), name="layer_norm")(x, w2, b2)
    return pl.pallas_call(_ln_res_kernel, out_shape=jax.ShapeDtypeStruct((t, d), F32), grid=(t // tm,),
                          in_specs=[row, row, vec, vec], out_specs=row,
                          compiler_params=_cparams(("parallel",)), name="layer_norm_res")(x, res, w2, b2)


def _mm_kernel(a_ref, b_ref, o_ref):
    o_ref[...] = jnp.dot(a_ref[...].astype(BF16), b_ref[...], preferred_element_type=F32).astype(o_ref.dtype)


def _mm(a, b, tm=1024, tn=1024, out_dtype=F32, name="matmul"):
    m, k = a.shape
    n = b.shape[1]
    tm, tn = _tile(m, tm), _tile(n, tn)
    return pl.pallas_call(
        _mm_kernel, out_shape=jax.ShapeDtypeStruct((m, n), out_dtype), grid=(m // tm, n // tn),
        in_specs=[pl.BlockSpec((tm, k), lambda i, j: (i, 0)), pl.BlockSpec((k, tn), lambda i, j: (0, j))],
        out_specs=pl.BlockSpec((tm, tn), lambda i, j: (i, j)),
        compiler_params=_cparams(("parallel", "arbitrary"), 48), name=name)(a, b)


def _lmm_kernel(f_ref, x_ref, o_ref):
    o_ref[0] = jnp.dot(f_ref[...], x_ref[0].astype(BF16), preferred_element_type=F32)


def _lmm(f, x, tn=2048, name="left_matmul"):
    bsz, k, n = x.shape
    m = f.shape[0]
    tn = _tile(n, tn)
    return pl.pallas_call(
        _lmm_kernel, out_shape=jax.ShapeDtypeStruct((bsz, m, n), F32), grid=(bsz, n // tn),
        in_specs=[pl.BlockSpec((m, k), lambda b, j: (0, 0)), pl.BlockSpec((1, k, tn), lambda b, j: (b, 0, j))],
        out_specs=pl.BlockSpec((1, m, tn), lambda b, j: (b, 0, j)),
        compiler_params=_cparams(("parallel", "parallel"), 48), name=name)(f, x)


def _shortconv_kernel(x_ref, xp_ref, xn_ref, w_ref, b_ref, o_ref, *, nt, act):
    i = pl.program_id(1)
    x = x_ref[0]
    tl = x.shape[0]
    row = lax.broadcasted_iota(jnp.int32, x.shape, 0)
    prev_row = jnp.where(i == 0, 0.0, xp_ref[0, 7:8, :])
    next_row = jnp.where(i == nt - 1, 0.0, xn_ref[0, 0:1, :])
    x_prev = jnp.where(row == 0, prev_row, pltpu.roll(x, 1, 0))
    x_next = jnp.where(row == tl - 1, next_row, pltpu.roll(x, tl - 1, 0))
    w = w_ref[0]
    y = b_ref[0, 0:1] + x_prev * w[0:1] + x * w[1:2] + x_next * w[2:3]
    if act:
        y = _silu(y)
    o_ref[0] = y


def _shortconv(proj, col0, nblk, w, b, act):
    bsz, l, _ = proj.shape
    tl = _tile(l, 1024)
    nt = l // tl
    w3 = jnp.transpose(w.reshape(3, nblk, D_GROUP), (1, 0, 2))
    w3 = jnp.pad(w3, ((0, 0), (0, 5), (0, 0)))
    b3 = jnp.broadcast_to(b.reshape(nblk, 1, D_GROUP), (nblk, 8, D_GROUP))
    r8 = tl // 8
    return pl.pallas_call(
        functools.partial(_shortconv_kernel, nt=nt, act=act),
        out_shape=jax.ShapeDtypeStruct((bsz, l, nblk * D_GROUP), F32), grid=(bsz, nt, nblk),
        in_specs=[
            pl.BlockSpec((1, tl, D_GROUP), lambda bb, i, j: (bb, i, col0 + j)),
            pl.BlockSpec((1, 8, D_GROUP), lambda bb, i, j: (bb, jnp.maximum(i * r8 - 1, 0), col0 + j)),
            pl.BlockSpec((1, 8, D_GROUP), lambda bb, i, j: (bb, jnp.minimum((i + 1) * r8, l // 8 - 1), col0 + j)),
            pl.BlockSpec((1, 8, D_GROUP), lambda bb, i, j: (j, 0, 0)),
            pl.BlockSpec((1, 8, D_GROUP), lambda bb, i, j: (j, 0, 0)),
        ],
        out_specs=pl.BlockSpec((1, tl, D_GROUP), lambda bb, i, j: (bb, i, j)),
        compiler_params=_cparams(("parallel", "parallel", "parallel")), name="shortconv")(proj, proj, proj, w3, b3)


def _stack_heads(xb, masks):
    return jnp.concatenate([xb * masks[h] for h in range(N_HEADS)], axis=0)


def _compact(s):
    return s[0:64] + s[64:128] + s[128:192] + s[192:256]


def _expand(c, bd):
    return jnp.concatenate([c, c, c, c], axis=0) * bd


def _head_norm(o, avg, gn):
    mu = _split_dot(o, avg, parts=2)
    oc = o - mu
    var = _split_dot(oc * oc, avg, parts=2)
    return oc * lax.rsqrt(var + EPS) * gn


def _ret_kernel(q_ref, qr_ref, k_ref, kr_ref, v_ref, g_ref, cos_ref, sin_ref,
                dsym_ref, qdf_ref, qdb_ref, kdf_ref, kdb_ref, cdec_ref, bd_ref, avg_ref, gn_ref,
                o_ref, sfw_ref, sbw_ref, save_ref, *, cb, nblk):
    p = pl.program_id(1)
    i = pl.program_id(2)
    masks = _head_masks(BF16)
    bd = bd_ref[...]
    cdec = cdec_ref[...]

    def rope_k(rows):
        return k_ref[0, rows] * cos_ref[rows] + kr_ref[0, rows] * sin_ref[rows]

    def kv_update(s, k, decay, vb):
        kv = _dot_tn((k * decay).astype(BF16), vb)
        return s * cdec + kv * bd

    @pl.when(p == 0)
    def _():
        @pl.when(i == 0)
        def _():
            sbw_ref[...] = jnp.zeros_like(sbw_ref)

        blk = nblk - 1 - i
        for c in reversed(range(cb)):
            rows = slice(c * CHUNK, (c + 1) * CHUNK)
            s = sbw_ref[...]
            save_ref[blk * cb + c] = _compact(s)
            sbw_ref[...] = kv_update(s, rope_k(rows), kdb_ref[...], v_ref[0, rows].astype(BF16))

    @pl.when(p == 1)
    def _():
        @pl.when(i == 0)
        def _():
            sfw_ref[...] = jnp.zeros_like(sfw_ref)

        for c in range(cb):
            rows = slice(c * CHUNK, (c + 1) * CHUNK)
            q = q_ref[0, rows] * cos_ref[rows] + qr_ref[0, rows] * sin_ref[rows]
            k = rope_k(rows)
            qb, kb, vb = q.astype(BF16), k.astype(BF16), v_ref[0, rows].astype(BF16)
            s_all = _dot_nt(qb, _stack_heads(kb, masks))
            pmat = (s_all * dsym_ref[...]).astype(BF16)
            o = jnp.dot(pmat, _stack_heads(vb, masks), preferred_element_type=F32)
            sfw = sfw_ref[...]
            sbw = _expand(save_ref[i * cb + c], bd)
            o = o + jnp.dot(qb, sfw.astype(BF16), preferred_element_type=F32) * qdf_ref[...]
            o = o + jnp.dot(qb, sbw.astype(BF16), preferred_element_type=F32) * qdb_ref[...]
            y = _head_norm(o, avg_ref[...], gn_ref[...])
            o_ref[0, rows] = _silu(g_ref[0, rows]) * y
            sfw_ref[...] = kv_update(sfw, k, kdf_ref[...], vb)


def _ret_tables():
    lg = np.log(1.0 - 2.0 ** (-5.0 - np.arange(N_HEADS, dtype=np.float64)))
    pos = np.arange(CHUNK, dtype=np.float64)
    lag = np.abs(pos[:, None] - pos[None, :])
    dsym = np.concatenate([np.exp(lg[h] * lag) for h in range(N_HEADS)], axis=1)
    lane_lg = np.repeat(lg, HEAD_DIM)[None, :]
    qdf = np.exp(lane_lg * (pos[:, None] + 1.0))
    qdb = np.exp(lane_lg * (CHUNK - pos[:, None]))
    kdf = np.exp(lane_lg * (CHUNK - 1.0 - pos[:, None]))
    kdb = np.exp(lane_lg * pos[:, None])
    cdec = np.exp(lane_lg * CHUNK)
    return [jnp.asarray(t, F32) for t in (dsym, qdf, qdb, kdf, kdb, cdec)]


def _block_diag_mask():
    hid = np.arange(D_GROUP) // HEAD_DIM
    return (hid[:, None] == hid[None, :]).astype(np.float32)


def _rope_tables(l):
    half = HEAD_DIM // 2
    inv = ROPE_BASE ** (-jnp.arange(half, dtype=F32) / half)
    ang = jnp.arange(l, dtype=F32)[:, None] * inv[None, :]
    cos, sin = jnp.cos(ang), jnp.sin(ang)
    cos_full = jnp.tile(jnp.concatenate([cos, cos], -1), (1, N_HEADS))
    sin_full = jnp.tile(jnp.concatenate([sin, sin], -1), (1, N_HEADS))
    return cos_full, sin_full


def _retention(proj, gn_w, cos_full, sin_full):
    bsz, l, _ = proj.shape
    nc = l // CHUNK
    cb = 4 if nc % 4 == 0 else 1
    nblk = nc // cb
    tl = cb * CHUNK
    dsym, qdf, qdb, kdf, kdb, cdec = _ret_tables()
    bd = jnp.asarray(_block_diag_mask())
    avg = jnp.asarray(_block_diag_mask() / HEAD_DIM, BF16)

    def both(col):
        return pl.BlockSpec((1, tl, D_GROUP), lambda b, p, i: (b, i + (1 - p) * (nblk - 1 - 2 * i), col))

    def fwd_only(col):
        return pl.BlockSpec((1, tl, D_GROUP), lambda b, p, i: (b, p * i, col))

    tab = pl.BlockSpec((tl, D_GROUP), lambda b, p, i: (i + (1 - p) * (nblk - 1 - 2 * i), 0))

    def const(shape):
        return pl.BlockSpec(shape, lambda b, p, i: (0,) * len(shape))

    return pl.pallas_call(
        functools.partial(_ret_kernel, cb=cb, nblk=nblk),
        out_shape=jax.ShapeDtypeStruct((bsz, l, D_GROUP), F32), grid=(bsz, 2, nblk),
        in_specs=[fwd_only(CB_RQ), fwd_only(CB_RQR), both(CB_RK), both(CB_RKR), both(CB_RV), fwd_only(CB_RG),
                  tab, tab, const((CHUNK, 4 * CHUNK)), const((CHUNK, D_GROUP)), const((CHUNK, D_GROUP)),
                  const((CHUNK, D_GROUP)), const((CHUNK, D_GROUP)), const((1, D_GROUP)),
                  const((D_GROUP, D_GROUP)), const((D_GROUP, D_GROUP)), const((1, D_GROUP))],
        out_specs=pl.BlockSpec((1, tl, D_GROUP), lambda b, p, i: (b, p * i, 0)),
        scratch_shapes=[pltpu.VMEM((D_GROUP, D_GROUP), F32), pltpu.VMEM((D_GROUP, D_GROUP), F32),
                        pltpu.VMEM((nc, HEAD_DIM, D_GROUP), F32)],
        compiler_params=_cparams(("parallel", "arbitrary", "arbitrary"), 48), name="retention",
    )(proj, proj, proj, proj, proj, proj, cos_full, sin_full, dsym, qdf, qdb, kdf, kdb, cdec, bd, avg,
      gn_w.reshape(1, D_GROUP))


def _mlstm_kernel(q_ref, k_ref, v_ref, og_ref, gc_ref, gr_ref, bc_ref, br_ref, ex_ref, lt_ref, ut_ref,
                  ones_ref, obd_ref, bd_ref, avg_ref, gn_ref,
                  o_ref, cfw_ref, cbw_ref, nmfw_ref, nmbw_ref, csave_ref, nmsave_ref, *, cb, nblk):
    p = pl.program_id(1)
    i = pl.program_id(2)
    masks = _head_masks(BF16)
    bd = bd_ref[...]
    lt = lt_ref[...]
    ut = ut_ref[...]
    ri = lax.broadcasted_iota(jnp.int32, (CHUNK, CHUNK), 0)
    ci = lax.broadcasted_iota(jnp.int32, (CHUNK, CHUNK), 1)
    lane = lax.broadcasted_iota(jnp.int32, (1, D_GROUP), 1)

    def gates_expanded(rows):
        return _split_dot(gc_ref[0, rows] + bc_ref[...], ex_ref[...])

    def state_update(c_ref, nm_ref, total, cum, i_x, k, vb):
        m_prev = nm_ref[1:2]
        g = (total - cum) + i_x
        m_new = jnp.maximum(total + m_prev, jnp.max(g, axis=0, keepdims=True))
        wk = jnp.exp(g - m_new) * k
        decay = jnp.exp(total + m_prev - m_new)
        c_ref[...] = c_ref[...] * decay + _dot_tn(wk.astype(BF16), vb) * bd
        nm_ref[0:1] = decay * nm_ref[0:1] + jnp.sum(wk, axis=0, keepdims=True)
        nm_ref[1:2] = m_new

    @pl.when(p == 0)
    def _():
        @pl.when(i == 0)
        def _():
            cbw_ref[...] = jnp.zeros_like(cbw_ref)
            nmbw_ref[...] = jnp.zeros_like(nmbw_ref)

        blk = nblk - 1 - i
        for c in reversed(range(cb)):
            rows = slice(c * CHUNK, (c + 1) * CHUNK)
            csave_ref[blk * cb + c] = _compact(cbw_ref[...])
            nmsave_ref[blk * cb + c] = nmbw_ref[...]
            gx = gates_expanded(rows)
            cum = _split_dot_left(ut, _log_sigmoid(gx[:, 768:1024]))
            k = k_ref[0, rows] * (HEAD_DIM ** -0.5)
            state_update(cbw_ref, nmbw_ref, cum[0:1], cum, gx[:, 512:768], k, v_ref[0, rows].astype(BF16))

    @pl.when(p == 1)
    def _():
        @pl.when(i == 0)
        def _():
            cfw_ref[...] = jnp.zeros_like(cfw_ref)
            nmfw_ref[...] = jnp.zeros_like(nmfw_ref)

        for c in range(cb):
            rows = slice(c * CHUNK, (c + 1) * CHUNK)
            q = q_ref[0, rows]
            k = k_ref[0, rows] * (HEAD_DIM ** -0.5)
            qb, kb, vb = q.astype(BF16), k.astype(BF16), v_ref[0, rows].astype(BF16)
            s_all = _dot_nt(qb, _stack_heads(kb, masks))
            vaug = jnp.concatenate([_stack_heads(vb, masks), ones_ref[...]], axis=1)
            gx = gates_expanded(rows)
            graw = gr_ref[0, :, rows] + br_ref[...]
            gls = _log_sigmoid(graw)
            cum_r_fw = _split_dot(gls, ut)
            cum_r_bw = _split_dot(gls, lt)
            ccomp = csave_ref[i * cb + c]
            nmb = nmsave_ref[i * cb + c]

            def direction(i_x, f_x, tri, cum_r, i_row0, f_row0, mask, c_state, n_vec, m_prev, total_row):
                cum = _split_dot_left(tri, _log_sigmoid(f_x))
                total = cum[total_row:total_row + 1]
                inter = cum + m_prev
                ps, rmax = [], []
                dms = []
                for h in range(N_HEADS):
                    a_col = cum[:, h * HEAD_DIM:h * HEAD_DIM + 1]
                    dm = a_col - cum_r[f_row0 + h:f_row0 + h + 1] + graw[i_row0 + h:i_row0 + h + 1]
                    dm = jnp.where(mask, dm, -jnp.inf)
                    dms.append(dm)
                    rmax.append(jnp.max(dm, axis=-1, keepdims=True))
                rmax256 = jnp.where(lane < 64, rmax[0], jnp.where(lane < 128, rmax[1],
                                    jnp.where(lane < 192, rmax[2], rmax[3])))
                m_row = jnp.maximum(inter, rmax256)
                for h in range(N_HEADS):
                    m_h = m_row[:, h * HEAD_DIM:h * HEAD_DIM + 1]
                    ps.append(s_all[:, h * CHUNK:(h + 1) * CHUNK] * jnp.exp(dms[h] - m_h))
                pmat = jnp.concatenate(ps, axis=1).astype(BF16)
                nd = jnp.dot(pmat, vaug, preferred_element_type=F32)
                w_inter = jnp.exp(inter - m_row)
                qc = jnp.dot(qb, c_state.astype(BF16), preferred_element_type=F32)
                qn = _split_dot(q * n_vec, obd_ref[...])
                num = nd[:, :D_GROUP] + w_inter * qc
                den = nd[:, D_GROUP:] + w_inter * qn
                hdir = num / jnp.maximum(jnp.abs(den), jnp.exp(-m_row))
                return hdir, total, cum

            h_fw, tot_fw, cum_fw = direction(gx[:, 0:256], gx[:, 256:512], lt, cum_r_fw, 0, 4, ri >= ci,
                                             cfw_ref[...], nmfw_ref[0:1], nmfw_ref[1:2], CHUNK - 1)
            h_bw, _, _ = direction(gx[:, 512:768], gx[:, 768:1024], ut, cum_r_bw, 8, 12, ci >= ri,
                                   _expand(ccomp, bd), nmb[0:1], nmb[1:2], 0)
            y = _head_norm(h_fw + h_bw, avg_ref[...], gn_ref[...])
            o_ref[0, rows] = _sigmoid(og_ref[0, rows]) * y
            state_update(cfw_ref, nmfw_ref, tot_fw, cum_fw, gx[:, 0:256], k, vb)


def _mlstm(proj, qk, gates_row, gate_b, gn_w):
    bsz, l, _ = proj.shape
    nc = l // CHUNK
    cb = 2 if nc % 2 == 0 else 1
    nblk = nc // cb
    tl = cb * CHUNK
    bd_np = _block_diag_mask()
    bd = jnp.asarray(bd_np)
    avg = jnp.asarray(bd_np / HEAD_DIM, BF16)
    obd = jnp.asarray(bd_np, BF16)
    ex = np.zeros((LANES, 4 * D_GROUP), np.float32)
    for j in range(16):
        typ, h = divmod(j, N_HEADS)
        ex[j, typ * D_GROUP + h * HEAD_DIM: typ * D_GROUP + (h + 1) * HEAD_DIM] = 1.0
    idx = np.arange(CHUNK)
    lt = (idx[None, :] <= idx[:, None]).astype(np.float32)
    ones_st = np.repeat(np.repeat(np.eye(N_HEADS, dtype=np.float32), CHUNK, 0), HEAD_DIM, 1)
    gb = gate_b.astype(F32).reshape(16)
    bias_col = jnp.pad(gb, (0, LANES - 16)).reshape(1, LANES)
    bias_row = jnp.broadcast_to(gb.reshape(16, 1), (16, CHUNK))

    def both(arr_col, width=D_GROUP):
        return pl.BlockSpec((1, tl, width), lambda b, p, i: (b, i + (1 - p) * (nblk - 1 - 2 * i), arr_col))

    def fwd_only(arr_col):
        return pl.BlockSpec((1, tl, D_GROUP), lambda b, p, i: (b, p * i, arr_col))

    def const(shape):
        return pl.BlockSpec(shape, lambda b, p, i: (0,) * len(shape))

    return pl.pallas_call(
        functools.partial(_mlstm_kernel, cb=cb, nblk=nblk),
        out_shape=jax.ShapeDtypeStruct((bsz, l, D_GROUP), F32), grid=(bsz, 2, nblk),
        in_specs=[fwd_only(0), both(1), both(CB_MV), fwd_only(CB_MO), both(GATE_COL128, LANES),
                  pl.BlockSpec((1, 16, tl), lambda b, p, i: (b, 0, p * i)),
                  const((1, LANES)), const((16, CHUNK)), const((LANES, 4 * D_GROUP)),
                  const((CHUNK, CHUNK)), const((CHUNK, CHUNK)), const((4 * CHUNK, D_GROUP)),
                  const((D_GROUP, D_GROUP)), const((D_GROUP, D_GROUP)), const((D_GROUP, D_GROUP)),
                  const((1, D_GROUP))],
        out_specs=pl.BlockSpec((1, tl, D_GROUP), lambda b, p, i: (b, p * i, 0)),
        scratch_shapes=[pltpu.VMEM((D_GROUP, D_GROUP), F32), pltpu.VMEM((D_GROUP, D_GROUP), F32),
                        pltpu.VMEM((8, D_GROUP), F32), pltpu.VMEM((8, D_GROUP), F32),
                        pltpu.VMEM((nc, HEAD_DIM, D_GROUP), F32), pltpu.VMEM((nc, 8, D_GROUP), F32)],
        compiler_params=_cparams(("parallel", "arbitrary", "arbitrary"), 48), name="mlstm",
    )(qk, qk, proj, proj, proj, gates_row, bias_col, bias_row, jnp.asarray(ex, BF16), jnp.asarray(lt, BF16),
      jnp.asarray(lt.T, BF16), jnp.asarray(ones_st, BF16), obd, bd, avg, gn_w.reshape(1, D_GROUP))


def _s5_kernel(u_ref, mt_ref, bg_ref, cg_ref, pa_ref, pb_ref, o_ref, *, nsteps):
    ub = u_ref[0].astype(BF16)
    e = jnp.dot(ub, bg_ref[0], preferred_element_type=F32)
    r = e.shape[0]
    row = lax.broadcasted_iota(jnp.int32, (r, LANES), 0)
    xf, xb = e[:, :LANES], e[:, LANES:]
    pa, pb = pa_ref[0], pb_ref[0]
    for s in range(nsteps):
        sh = 1 << s
        a_f, b_f = pa[s:s + 1, :LANES], pb[s:s + 1, :LANES]
        a_b, b_b = pa[s:s + 1, LANES:], pb[s:s + 1, LANES:]
        yf = jnp.where(row >= sh, pltpu.roll(xf, sh, 0), 0.0)
        yb = jnp.where(row < r - sh, pltpu.roll(xb, r - sh, 0), 0.0)
        xf = xf + a_f * yf + b_f * pltpu.roll(yf, LANES // 2, 1)
        xb = xb + a_b * yb + b_b * pltpu.roll(yb, LANES // 2, 1)
    sprev = jnp.where(row >= 1, pltpu.roll(xf, 1, 0), 0.0)
    snext = jnp.where(row < r - 1, pltpu.roll(xb, r - 1, 0), 0.0)
    st = jnp.concatenate([sprev, snext], axis=1).astype(BF16)
    o_ref[0] = (jnp.dot(ub, mt_ref[0], preferred_element_type=F32)
                + jnp.dot(st, cg_ref[0], preferred_element_type=F32))


def _s5_tables(a_re, a_im, log_dt, b_re, b_im, c_re, c_im, d_skip, tc, nsteps):
    g, p, ch = S5_GROUPS, S5_STATE, S5_CH
    a = lax.complex(a_re.astype(F32), a_im.astype(F32))
    delta = jnp.exp(log_dt.astype(F32))[..., None]
    la = a * delta

    def apow(n):
        n = jnp.asarray(n, F32)
        return jnp.exp(la[:, :, None, :] * n[None, None, :, None])

    a_bar = jnp.exp(la)
    b = lax.complex(b_re.astype(F32), b_im.astype(F32))
    b_bar = ((a_bar - 1.0) / a)[..., None] * b[None]
    c = lax.complex(c_re.astype(F32), c_im.astype(F32))
    taus = np.arange(tc)
    kk = jnp.real(jnp.einsum("dgop,dgtp,dgpi->dgtoi", c, apow(taus), b_bar))
    diff = taus[None, :] - taus[:, None]
    k0 = kk[0][:, np.clip(diff, 0, None)]
    k1 = kk[1][:, np.clip(-diff, 0, None)]
    dsk = d_skip.astype(F32).reshape(g, ch)[:, :, None] * jnp.eye(ch, dtype=F32)[None]
    kdiag = kk[0][:, 0] + kk[1][:, 0] + dsk
    dm = diff[None, :, :, None, None]
    kfull = jnp.where(dm > 0, k0, jnp.where(dm < 0, k1, kdiag[:, None, None]))
    mt = jnp.transpose(kfull, (0, 1, 4, 2, 3)).reshape(g, tc * ch, tc * ch)

    zf = apow(tc - 1 - taus)[0][..., None] * b_bar[0][:, None]
    zb = apow(taus)[1][..., None] * b_bar[1][:, None]

    def to_rows(z):
        return jnp.transpose(z, (0, 1, 3, 2)).reshape(g, tc * ch, p)

    bg = jnp.concatenate([to_rows(jnp.real(zf)), to_rows(jnp.imag(zf)),
                          to_rows(jnp.real(zb)), to_rows(jnp.imag(zb))], axis=-1)

    yf = c[0][:, None] * apow(taus + 1)[0][:, :, None, :]
    yb = c[1][:, None] * apow(tc - taus)[1][:, :, None, :]

    def to_cols(z):
        return jnp.transpose(z, (0, 3, 1, 2)).reshape(g, p, tc * ch)

    cg = jnp.concatenate([to_cols(jnp.real(yf)), -to_cols(jnp.imag(yf)),
                          to_cols(jnp.real(yb)), -to_cols(jnp.imag(yb))], axis=1)

    steps = tc * (2.0 ** np.arange(nsteps))
    pw = apow(steps)
    re0, im0, re1, im1 = jnp.real(pw[0]), jnp.imag(pw[0]), jnp.real(pw[1]), jnp.imag(pw[1])
    pa = jnp.concatenate([re0, re0, re1, re1], axis=-1)
    pb = jnp.concatenate([-im0, im0, -im1, im1], axis=-1)
    pad = (-nsteps) % 8
    pa = jnp.pad(pa, ((0, 0), (0, pad), (0, 0)))
    pb = jnp.pad(pb, ((0, 0), (0, pad), (0, 0)))
    return mt.astype(BF16), bg.astype(BF16), cg.astype(BF16), pa, pb


def _s5_glu_kernel(y_ref, w_ref, o_ref):
    y = y_ref[...]
    z = 0.5 * y * (1.0 + jnp.tanh(math.sqrt(2.0 / math.pi) * (y + 0.044715 * (y * y * y))))
    o_ref[...] = z * _sigmoid(jnp.dot(z.astype(BF16), w_ref[...], preferred_element_type=F32))


def _s5(proj, a_re, a_im, log_dt, b_re, b_im, c_re, c_im, d_skip, w_glu):
    bsz, l, _ = proj.shape
    tc = S5_TC
    r = l // tc
    nsteps = max(1, int(math.ceil(math.log2(r))))
    w = tc * S5_CH
    mt, bg, cg, pa, pb = _s5_tables(a_re, a_im, log_dt, b_re, b_im, c_re, c_im, d_skip, tc, nsteps)
    u = proj[:, :, CB_S5 * D_GROUP:(CB_S5 + 1) * D_GROUP]
    ug = jnp.transpose(u.reshape(bsz, r, tc, S5_GROUPS, S5_CH), (3, 0, 1, 2, 4)).reshape(S5_GROUPS, bsz * r, w)
    ns8 = pa.shape[1]
    yg = pl.pallas_call(
        functools.partial(_s5_kernel, nsteps=nsteps),
        out_shape=jax.ShapeDtypeStruct((S5_GROUPS, bsz * r, w), F32), grid=(S5_GROUPS, bsz),
        in_specs=[pl.BlockSpec((1, r, w), lambda g, b: (g, b, 0)),
                  pl.BlockSpec((1, w, w), lambda g, b: (g, 0, 0)),
                  pl.BlockSpec((1, w, D_GROUP), lambda g, b: (g, 0, 0)),
                  pl.BlockSpec((1, D_GROUP, w), lambda g, b: (g, 0, 0)),
                  pl.BlockSpec((1, ns8, D_GROUP), lambda g, b: (g, 0, 0)),
                  pl.BlockSpec((1, ns8, D_GROUP), lambda g, b: (g, 0, 0))],
        out_specs=pl.BlockSpec((1, r, w), lambda g, b: (g, b, 0)),
        compiler_params=_cparams(("parallel", "parallel"), 48), name="s5_ssm")(ug, mt, bg, cg, pa, pb)
    y = jnp.transpose(yg.reshape(S5_GROUPS, bsz, r, tc, S5_CH), (1, 2, 3, 0, 4)).reshape(bsz * l, D_GROUP)
    t = bsz * l
    tm = _tile(t, 2048)
    out = pl.pallas_call(
        _s5_glu_kernel, out_shape=jax.ShapeDtypeStruct((t, D_GROUP), F32), grid=(t // tm,),
        in_specs=[pl.BlockSpec((tm, D_GROUP), lambda i: (i, 0)), pl.BlockSpec((D_GROUP, D_GROUP), lambda i: (0, 0))],
        out_specs=pl.BlockSpec((tm, D_GROUP), lambda i: (i, 0)),
        compiler_params=_cparams(("parallel",)), name="s5_glu")(y, w_glu.astype(BF16))
    return out.reshape(bsz, l, D_GROUP)


def _hy_filter_kernel(z_ref, w1_ref, b1_ref, w2_ref, b2_ref, w3_ref, fr_ref, dec_ref, h_ref, ss_ref):
    i = pl.program_id(0)
    hp = lax.Precision.HIGHEST
    fr = fr_ref[...]
    a = jnp.sin(fr * (jnp.dot(z_ref[...], w1_ref[...], precision=hp, preferred_element_type=F32) + b1_ref[...]))
    a = jnp.sin(fr * (jnp.dot(a, w2_ref[...], precision=hp, preferred_element_type=F32) + b2_ref[...]))
    h = jnp.dot(a, w3_ref[...], precision=hp, preferred_element_type=F32)
    dec = dec_ref[...]
    h = h * jnp.concatenate([dec, dec, dec, dec], axis=1)
    h_ref[...] = h

    @pl.when(i == 0)
    def _():
        ss_ref[...] = jnp.zeros_like(ss_ref)

    ss_ref[...] += jnp.sum(h * h, axis=0, keepdims=True)


def _hy_filters(l, w1, b1, w2, b2, w3, freq):
    t = jnp.linspace(0.0, 1.0, l, dtype=F32)[:, None]
    w = 2.0 * math.pi * jnp.arange(l, dtype=F32)[:, None] / l
    bands = jnp.linspace(1e-4, HY_BANDS - 1, HY_BANDS, dtype=F32)[None, :]
    z = jnp.concatenate([t, jnp.cos(bands * w), -jnp.sin(bands * w)], axis=-1)
    z = jnp.pad(z, ((0, 0), (0, LANES - HY_EMB)))
    max_decay = math.log(HY_TARGET) / HY_FAST_DECAY
    min_decay = math.log(HY_TARGET) / HY_SLOW_DECAY
    rates = jnp.abs(jnp.linspace(min_decay, max_decay, D_GROUP, dtype=F32))
    dec = jnp.exp(-t * rates)
    pf = LANES - HY_FFN
    w1p = jnp.pad(w1.astype(F32), ((0, LANES - HY_EMB), (0, pf)))
    w2p = jnp.pad(w2.astype(F32), ((0, pf), (0, pf)))
    w3p = jnp.pad(w3.astype(F32), ((0, pf), (0, 0)))
    b1p = jnp.pad(b1.astype(F32), (0, pf)).reshape(1, LANES)
    b2p = jnp.pad(b2.astype(F32), (0, pf)).reshape(1, LANES)
    frp = jnp.pad(freq.astype(F32), (0, pf)).reshape(1, LANES)
    nout = HY_ORDER * 2 * D_GROUP
    tl = _tile(l, 512)

    def const(shape):
        return pl.BlockSpec(shape, lambda i: (0, 0))

    return pl.pallas_call(
        _hy_filter_kernel,
        out_shape=(jax.ShapeDtypeStruct((l, nout), F32), jax.ShapeDtypeStruct((1, nout), F32)), grid=(l // tl,),
        in_specs=[pl.BlockSpec((tl, LANES), lambda i: (i, 0)), const((LANES, LANES)), const((1, LANES)),
                  const((LANES, LANES)), const((1, LANES)), const((LANES, nout)), const((1, LANES)),
                  pl.BlockSpec((tl, D_GROUP), lambda i: (i, 0))],
        out_specs=(pl.BlockSpec((tl, nout), lambda i: (i, 0)), const((1, nout))),
        compiler_params=_cparams(("arbitrary",)), name="hyena_filter_mlp")(z, w1p, b1p, w2p, b2p, w3p, frp, dec)


def _dft_consts(n1):
    n2 = FFT_N2
    n = n1 * n2
    i1 = np.arange(n1, dtype=np.float64)
    th = 2.0 * np.pi * np.outer(i1, i1) / n1
    c1, s1 = np.cos(th), np.sin(th)
    fa_full = np.concatenate([c1, -s1], axis=0)
    fa_half = fa_full[:, : n1 // 2]
    ga = np.concatenate([c1[: n1 // 2], -s1[: n1 // 2]], axis=1) / n
    i2 = np.arange(n2, dtype=np.float64)
    ph = 2.0 * np.pi * np.outer(i2, i2) / n2
    c2, s2 = np.cos(ph), np.sin(ph)
    fb = np.block([[c2, s2], [-s2, c2]])
    fbc = np.block([[c2, -s2], [s2, c2]])
    ps = 2.0 * np.pi * np.outer(i1, i2) / n
    twr = np.broadcast_to(np.cos(ps)[:, :, None], (n1, n2, LANES))
    twi = np.broadcast_to(-np.sin(ps)[:, :, None], (n1, n2, LANES))
    as_bf = lambda x: jnp.asarray(x, BF16)
    return dict(fa_full=as_bf(fa_full), fa_half=as_bf(fa_half), ga=as_bf(ga), fb=as_bf(fb), fbc=as_bf(fbc),
                twr=jnp.asarray(twr, F32), twi=jnp.asarray(twi, F32))


def _lane_tile(x, reps):
    return x if reps == 1 else jnp.concatenate([x] * reps, axis=-1)


def _hy_spec_kernel(a_ref, twr_ref, twi_ref, fb_ref, ss_ref, o_ref, *, kb, reps):
    scale = lax.rsqrt(ss_ref[...])
    for j in range(kb):
        ar, ai = a_ref[0, j], a_ref[1, j]
        twr, twi = _lane_tile(twr_ref[j], reps), _lane_tile(twi_ref[j], reps)
        br = twr * ar - twi * ai
        bi = twr * ai + twi * ar
        x = jnp.dot(fb_ref[...], jnp.concatenate([br, bi], axis=0).astype(BF16), preferred_element_type=F32)
        o_ref[0, j] = x[:FFT_N2] * scale
        o_ref[1, j] = x[FFT_N2:] * scale


def _hy_mid_kernel(a_ref, h_ref, twr_ref, twi_ref, fb_ref, fbc_ref, o_ref, *, kb, reps):
    for j in range(kb):
        ar, ai = a_ref[0, 0, j], a_ref[0, 1, j]
        twr, twi = _lane_tile(twr_ref[j], reps), _lane_tile(twi_ref[j], reps)
        br = twr * ar - twi * ai
        bi = twr * ai + twi * ar
        x = jnp.dot(fb_ref[...], jnp.concatenate([br, bi], axis=0).astype(BF16), preferred_element_type=F32)
        xr, xi = x[:FFT_N2], x[FFT_N2:]
        hr, hi = h_ref[0, j], h_ref[1, j]
        yr = xr * hr - xi * hi
        yi = xr * hi + xi * hr
        z = jnp.dot(fbc_ref[...], jnp.concatenate([yr, yi], axis=0).astype(BF16), preferred_element_type=F32)
        zr, zi = z[:FFT_N2], z[FFT_N2:]
        o_ref[0, 0, j] = twr * zr + twi * zi
        o_ref[0, 1, j] = twr * zi - twi * zr


def _hy_out_kernel(g_ref, z_ref, x_ref, v_ref, b_ref, o_ref):
    y = jnp.dot(g_ref[...], z_ref[0].astype(BF16), preferred_element_type=F32)
    v = v_ref[0]
    o_ref[0] = x_ref[0] * (y + v * b_ref[...])


def _hyena(proj, conv_w, conv_b, w1, b1, w2, b2, w3, freq, bias):
    bsz, l, _ = proj.shape
    n1 = 2 * l // FFT_N2
    nh = n1 // 2
    c = D_GROUP
    dc = _dft_consts(n1)
    pc = _shortconv(proj, CB_HV, 3, conv_w, conv_b, act=False)

    h, ss = _hy_filters(l, w1, b1, w2, b2, w3, freq)
    h4 = h.reshape(l, HY_ORDER, 2, c)
    ss4 = ss.reshape(HY_ORDER, 2, c)
    ssn = (ss4[:, 0] + ss4[:, 1]).reshape(1, HY_ORDER * c)
    hf, hb = h4[:, :, 0], h4[:, :, 1]
    kern = jnp.concatenate([hf, jnp.zeros_like(hf[:1]), hb[:0:-1]], axis=0).reshape(1, n1, FFT_N2 * HY_ORDER * c)
    ncf = HY_ORDER * c
    ka = _lmm(dc["fa_full"], kern, name="hyena_filter_dft1").reshape(2, n1, FFT_N2, ncf)
    kb = 8 if n1 % 8 == 0 else 1
    cbw = 256
    reps = cbw // LANES
    hspec = pl.pallas_call(
        functools.partial(_hy_spec_kernel, kb=kb, reps=reps),
        out_shape=jax.ShapeDtypeStruct((2, n1, FFT_N2, ncf), F32), grid=(ncf // cbw, n1 // kb),
        in_specs=[pl.BlockSpec((2, kb, FFT_N2, cbw), lambda j, k: (0, k, 0, j)),
                  pl.BlockSpec((kb, FFT_N2, LANES), lambda j, k: (k, 0, 0)),
                  pl.BlockSpec((kb, FFT_N2, LANES), lambda j, k: (k, 0, 0)),
                  pl.BlockSpec((2 * FFT_N2, 2 * FFT_N2), lambda j, k: (0, 0)),
                  pl.BlockSpec((1, cbw), lambda j, k: (0, j))],
        out_specs=pl.BlockSpec((2, kb, FFT_N2, cbw), lambda j, k: (0, k, 0, j)),
        compiler_params=_cparams(("parallel", "parallel"), 48), name="hyena_filter_dft2",
    )(ka, dc["twr"], dc["twi"], dc["fb"], ssn)


    def long_conv_gate(zin, order, xcol):
        zarr, zcol = zin
        zw = zarr.shape[-1] // c
        zv = zarr.reshape(bsz, nh, FFT_N2 * zarr.shape[-1])
        a = pl.pallas_call(
            _lmm_kernel, out_shape=jax.ShapeDtypeStruct((bsz, 2 * n1, FFT_N2 * c), F32),
            grid=(bsz, FFT_N2),
            in_specs=[pl.BlockSpec((2 * n1, nh), lambda b, j: (0, 0)),
                      pl.BlockSpec((1, nh, c), lambda b, j: (b, 0, j * zw + zcol))],
            out_specs=pl.BlockSpec((1, 2 * n1, c), lambda b, j: (b, 0, j)),
            compiler_params=_cparams(("parallel", "parallel"), 48), name="hyena_dft1")(dc["fa_half"], zv)
        a = a.reshape(bsz, 2, n1, FFT_N2, c)
        zmid = pl.pallas_call(
            functools.partial(_hy_mid_kernel, kb=kb, reps=reps),
            out_shape=jax.ShapeDtypeStruct((bsz, 2, n1, FFT_N2, c), F32), grid=(bsz, n1 // kb),
            in_specs=[pl.BlockSpec((1, 2, kb, FFT_N2, c), lambda b, k: (b, 0, k, 0, 0)),
                      pl.BlockSpec((2, kb, FFT_N2, c), lambda b, k: (0, k, 0, order)),
                      pl.BlockSpec((kb, FFT_N2, LANES), lambda b, k: (k, 0, 0)),
                      pl.BlockSpec((kb, FFT_N2, LANES), lambda b, k: (k, 0, 0)),
                      pl.BlockSpec((2 * FFT_N2, 2 * FFT_N2), lambda b, k: (0, 0)),
                      pl.BlockSpec((2 * FFT_N2, 2 * FFT_N2), lambda b, k: (0, 0))],
            out_specs=pl.BlockSpec((1, 2, kb, FFT_N2, c), lambda b, k: (b, 0, k, 0, 0)),
            compiler_params=_cparams(("parallel", "parallel"), 48), name="hyena_dft_mid",
        )(a, hspec, dc["twr"], dc["twi"], dc["fb"], dc["fbc"])
        zmid = zmid.reshape(bsz, 2 * n1, FFT_N2 * c)
        pcv = pc.reshape(bsz, nh, FFT_N2 * 3 * c)
        out = pl.pallas_call(
            _hy_out_kernel, out_shape=jax.ShapeDtypeStruct((bsz, nh, FFT_N2 * c), F32), grid=(bsz, FFT_N2),
            in_specs=[pl.BlockSpec((nh, 2 * n1), lambda b, j: (0, 0)),
                      pl.BlockSpec((1, 2 * n1, c), lambda b, j: (b, 0, j)),
                      pl.BlockSpec((1, nh, c), lambda b, j: (b, 0, j * 3 + xcol)),
                      pl.BlockSpec((1, nh, c), lambda b, j: (b, 0, j * zw + zcol)),
                      pl.BlockSpec((1, c), lambda b, j: (0, 0))],
            out_specs=pl.BlockSpec((1, nh, c), lambda b, j: (b, 0, j)),
            compiler_params=_cparams(("parallel", "parallel"), 48), name="hyena_idft_gate",
        )(dc["ga"], zmid, pcv, zv, bias[order].astype(F32).reshape(1, c))
        return out.reshape(bsz, l, c)

    z1 = long_conv_gate((pc, 0), 0, 1)
    z2 = long_conv_gate((z1, 0), 1, 2)
    return z2


def _ffn_kernel(x_ref, cmb_ref, w1_ref, w3_ref, w2_ref, o_ref, xb_ref, acc_ref, *, ne, nf, routed):
    e = pl.program_id(1)
    f = pl.program_id(2)

    @pl.when((e == 0) & (f == 0))
    def _():
        xb_ref[...] = x_ref[...].astype(BF16)
        acc_ref[...] = jnp.zeros_like(acc_ref)

    xb = xb_ref[...]
    a = jnp.dot(xb, w1_ref[0], preferred_element_type=F32)
    b = jnp.dot(xb, w3_ref[0], preferred_element_type=F32)
    hid = _silu(a) * b
    if routed:
        lane = lax.broadcasted_iota(jnp.int32, cmb_ref.shape, 1)
        hid = hid * jnp.sum(jnp.where(lane == e, cmb_ref[...], 0.0), axis=1, keepdims=True)
    acc_ref[...] += jnp.dot(hid.astype(BF16), w2_ref[0], preferred_element_type=F32)

    @pl.when((e == ne - 1) & (f == nf - 1))
    def _():
        o_ref[...] = acc_ref[...]


def _ffn(x, cmb, w1, w3, w2, routed):
    t, d = x.shape
    ne, _, ff = w1.shape
    tm = _tile(t, 1024)
    tf = 512 if ff % 512 == 0 else (256 if ff % 256 == 0 else ff)
    nf = ff // tf
    return pl.pallas_call(
        functools.partial(_ffn_kernel, ne=ne, nf=nf, routed=routed),
        out_shape=jax.ShapeDtypeStruct((t, d), F32), grid=(t // tm, ne, nf),
        in_specs=[pl.BlockSpec((tm, d), lambda i, e, f: (i, 0)),
                  pl.BlockSpec((tm, LANES), lambda i, e, f: (i, 0)),
                  pl.BlockSpec((1, d, tf), lambda i, e, f: (e, 0, f)),
                  pl.BlockSpec((1, d, tf), lambda i, e, f: (e, 0, f)),
                  pl.BlockSpec((1, tf, d), lambda i, e, f: (e, f, 0))],
        out_specs=pl.BlockSpec((tm, d), lambda i, e, f: (i, 0)),
        scratch_shapes=[pltpu.VMEM((tm, d), BF16), pltpu.VMEM((tm, d), F32)],
        compiler_params=_cparams(("parallel", "arbitrary", "arbitrary"), 52), name="swiglu_ffn",
    )(x, cmb, w1, w3, w2)


def _router_kernel(x_ref, rh_ref, rl_ref, o_ref):
    x = x_ref[...]
    xh = x.astype(BF16)
    xl = (x - xh.astype(F32)).astype(BF16)
    logits = (jnp.dot(xh, rh_ref[...], preferred_element_type=F32)
              + jnp.dot(xl, rh_ref[...], preferred_element_type=F32)
              + jnp.dot(xh, rl_ref[...], preferred_element_type=F32))
    lane = lax.broadcasted_iota(jnp.int32, logits.shape, 1).astype(F32)
    logits = jnp.where(lane < N_EXPERTS, logits, -jnp.inf)
    m1 = jnp.max(logits, axis=1, keepdims=True)
    i1 = jnp.min(jnp.where(logits == m1, lane, float(LANES)), axis=1, keepdims=True)
    rest = jnp.where(lane == i1, -jnp.inf, logits)
    m2 = jnp.max(rest, axis=1, keepdims=True)
    i2 = jnp.min(jnp.where(rest == m2, lane, float(LANES)), axis=1, keepdims=True)
    e2 = jnp.exp(m2 - m1)
    g1 = 1.0 / (1.0 + e2)
    g2 = e2 / (1.0 + e2)
    o_ref[...] = jnp.where(lane == i1, g1, 0.0) + jnp.where(lane == i2, g2, 0.0)


def _router(x, router):
    t, d = x.shape
    r = jnp.pad(router.astype(F32), ((0, 0), (0, LANES - N_EXPERTS)))
    rh = r.astype(BF16)
    rl = (r - rh.astype(F32)).astype(BF16)
    tm = _tile(t, 1024)
    return pl.pallas_call(
        _router_kernel, out_shape=jax.ShapeDtypeStruct((t, LANES), F32), grid=(t // tm,),
        in_specs=[pl.BlockSpec((tm, d), lambda i: (i, 0)), pl.BlockSpec((d, LANES), lambda i: (0, 0)),
                  pl.BlockSpec((d, LANES), lambda i: (0, 0))],
        out_specs=pl.BlockSpec((tm, LANES), lambda i: (i, 0)),
        compiler_params=_cparams(("parallel",)), name="moe_router")(x, rh, rl)


def _extended_w_in(w_in):
    w = w_in.astype(F32)
    scale = HEAD_DIM ** -0.5

    def rot_half(cols):
        c4 = cols.reshape(-1, N_HEADS, 2, HEAD_DIM // 2)
        return jnp.stack([-c4[:, :, 1], c4[:, :, 0]], axis=2).reshape(-1, D_GROUP)

    wq = w[:, 0:256]
    wk = w[:, 256:512] * scale
    main = jnp.concatenate([wq, wk, w[:, 512:3072]], axis=1)
    gates = jnp.pad(w[:, 3072:3088], ((0, 0), (0, LANES - 16)))
    ext = jnp.concatenate([main, rot_half(wq), rot_half(wk), gates], axis=1)
    return jnp.pad(ext, ((0, 0), (0, N_EXT - ext.shape[1]))).astype(BF16)


def kernel(x, ln_in_w, ln_in_b, w_in, w_out, ret_gn_w, s5_a_re, s5_a_im, s5_log_dt, s5_b_re, s5_b_im, s5_c_re, s5_c_im, s5_d, s5_w_glu, hy_conv_w, hy_conv_b, hy_w1, hy_b1, hy_w2, hy_b2, hy_w3, hy_freq, hy_bias, ml_conv_w, ml_conv_b, ml_gate_b, ml_gn_w, ln1_w, ln1_b, ln2_w, ln2_b, ffn_w1, ffn_w3, ffn_w2, moe_router, moe_w1, moe_w3, moe_w2):
    bsz, l, d = x.shape
    t = bsz * l
    cos_full, sin_full = _rope_tables(l)
    h = _layer_norm(x.reshape(t, d), ln_in_w, ln_in_b)
    ones_cmb = jnp.ones((t, LANES), F32)
    for layer in range(DEPTH):
        proj = _mm(h, _extended_w_in(w_in[layer]), tm=1024, tn=1280, name="in_proj").reshape(bsz, l, N_EXT)
        y_ret = _retention(proj, ret_gn_w[layer], cos_full, sin_full)
        y_s5 = _s5(proj, s5_a_re[layer], s5_a_im[layer], s5_log_dt[layer], s5_b_re[layer], s5_b_im[layer],
                   s5_c_re[layer], s5_c_im[layer], s5_d[layer], s5_w_glu[layer])
        y_hy = _hyena(proj, hy_conv_w[layer], hy_conv_b[layer], hy_w1[layer], hy_b1[layer], hy_w2[layer],
                      hy_b2[layer], hy_w3[layer], hy_freq[layer], hy_bias[layer])
        qk = _shortconv(proj, CB_MQ, 2, ml_conv_w[layer], ml_conv_b[layer], act=True)
        gates_row = jnp.transpose(proj[:, :, GATE_COL128 * LANES:GATE_COL128 * LANES + 16], (0, 2, 1))
        y_ml = _mlstm(proj, qk, gates_row, ml_gate_b[layer], ml_gn_w[layer])
        y = jnp.concatenate([y_ret, y_s5, y_hy, y_ml], axis=-1).reshape(t, d)
        mix = _mm(y, w_out[layer].astype(BF16), tm=1024, tn=1024, name="out_proj")
        h = _layer_norm(h, ln1_w[layer], ln1_b[layer], res=mix)
        if layer % 2 == 0:
            j = layer // 2
            f = _ffn(h, ones_cmb, ffn_w1[j][None].astype(BF16), ffn_w3[j][None].astype(BF16),
                     ffn_w2[j][None].astype(BF16), routed=False)
        else:
            j = layer // 2
            cmb = _router(h, moe_router[j])
            f = _ffn(h, cmb, moe_w1[j].astype(BF16), moe_w3[j].astype(BF16), moe_w2[j].astype(BF16), routed=True)
        h = _layer_norm(h, ln2_w[layer], ln2_b[layer], res=f)
    return h.reshape(bsz, l, d)
```

```python
import functools
import math

import numpy as np
import jax
import jax.numpy as jnp
from jax import lax
from jax.experimental import pallas as pl
from jax.experimental.pallas import tpu as pltpu

F32 = jnp.float32
BF16 = jnp.bfloat16

D_MODEL = 1024
DEPTH = 2
D_GROUP = 256
HEAD_DIM = 64
N_HEADS = 4
CHUNK = 128
S5_CH = 16
S5_GROUPS = 16
S5_STATE = 64
HY_ORDER = 2
HY_EMB = 33
HY_BANDS = 16
HY_FFN = 64
HY_FAST_DECAY = 0.3
HY_SLOW_DECAY = 1.5
HY_TARGET = 1e-2
N_EXPERTS = 8
ROPE_BASE = 10000.0
EPS = 1e-5
DN_ALPHA = (2 * DEPTH) ** 0.25

LANES = 128
S5_TC = 32
FFT_N2 = 128
N_EXT = 3840

CB_RQ, CB_RK, CB_RV, CB_RG, CB_S5, CB_HV, CB_HX1, CB_HX2 = 0, 1, 2, 3, 4, 5, 6, 7
CB_MQ, CB_MK, CB_MV, CB_MO, CB_RQR, CB_RKR = 8, 9, 10, 11, 12, 13
GATE_COL128 = 28


def _cparams(sem, vmem_mb=None):
    kw = dict(dimension_semantics=sem)
    if vmem_mb is not None:
        kw["vmem_limit_bytes"] = vmem_mb * 1024 * 1024
    return pltpu.CompilerParams(**kw)


def _tile(n, pref):
    return pref if n % pref == 0 else n


def _split_dot(x, m, parts=3):
    acc = None
    r = x
    for _ in range(parts):
        hi = r.astype(BF16)
        t = jnp.dot(hi, m, preferred_element_type=F32)
        acc = t if acc is None else acc + t
        r = r - hi.astype(F32)
    return acc


def _split_dot_left(m, x, parts=3):
    acc = None
    r = x
    for _ in range(parts):
        hi = r.astype(BF16)
        t = jnp.dot(m, hi, preferred_element_type=F32)
        acc = t if acc is None else acc + t
        r = r - hi.astype(F32)
    return acc


def _dot_nt(a, b):
    return lax.dot_general(a, b, (((1,), (1,)), ((), ())), preferred_element_type=F32)


def _dot_tn(a, b):
    return lax.dot_general(a, b, (((0,), (0,)), ((), ())), preferred_element_type=F32)


def _sigmoid(x):
    return 1.0 / (1.0 + jnp.exp(-x))


def _silu(x):
    return x * _sigmoid(x)


def _log_sigmoid(x):
    return jnp.minimum(x, 0.0) - jnp.log(1.0 + jnp.exp(-jnp.abs(x)))


def _head_masks(dtype):
    lane = lax.broadcasted_iota(jnp.int32, (1, D_GROUP), 1)
    return [((lane >= h * HEAD_DIM) & (lane < (h + 1) * HEAD_DIM)).astype(dtype) for h in range(N_HEADS)]


def _ln_core(x, w, b):
    mu = jnp.mean(x, -1, keepdims=True)
    xc = x - mu
    var = jnp.mean(xc * xc, -1, keepdims=True)
    return xc * lax.rsqrt(var + EPS) * w + b


def _ln_kernel(x_ref, w_ref, b_ref, o_ref):
    o_ref[...] = _ln_core(x_ref[...], w_ref[...], b_ref[...])


def _ln_res_kernel(h_ref, m_ref, w_ref, b_ref, o_ref):
    o_ref[...] = _ln_core(DN_ALPHA * h_ref[...] + m_ref[...], w_ref[...], b_ref[...])


def _layer_norm(x, w, b, res=None):
    t, d = x.shape
    tm = _tile(t, 512)
    row = pl.BlockSpec((tm, d), lambda i: (i, 0))
    vec = pl.BlockSpec((1, d), lambda i: (0, 0))
    w2, b2 = w.reshape(1, d), b.reshape(1, d)
    if res is None:
        return pl.pallas_call(_ln_kernel, out_shape=jax.ShapeDtypeStruct((t, d), F32), grid=(t // tm,),
                              in_specs=[row, vec, vec], out_specs=row,
                              compiler_params=_cparams(("parallel",)), name="layer_norm")(x, w2, b2)
    return pl.pallas_call(_ln_res_kernel, out_shape=jax.ShapeDtypeStruct((t, d), F32), grid=(t // tm,),
                          in_specs=[row, row, vec, vec], out_specs=row,
                          compiler_params=_cparams(("parallel",)), name="layer_norm_res")(x, res, w2, b2)


def _mm_kernel(a_ref, b_ref, o_ref):
    o_ref[...] = jnp.dot(a_ref[...].astype(BF16), b_ref[...], preferred_element_type=F32).astype(o_ref.dtype)


def _mm(a, b, tm=1024, tn=1024, out_dtype=F32, name="matmul"):
    m, k = a.shape
    n = b.shape[1]
    tm, tn = _tile(m, tm), _tile(n, tn)
    return pl.pallas_call(
        _mm_kernel, out_shape=jax.ShapeDtypeStruct((m, n), out_dtype), grid=(m // tm, n // tn),
        in_specs=[pl.BlockSpec((tm, k), lambda i, j: (i, 0)), pl.BlockSpec((k, tn), lambda i, j: (0, j))],
        out_specs=pl.BlockSpec((tm, tn), lambda i, j: (i, j)),
        compiler_params=_cparams(("parallel", "arbitrary"), 48), name=name)(a, b)


def _lmm_kernel(f_ref, x_ref, o_ref):
    o_ref[0] = jnp.dot(f_ref[...], x_ref[0].astype(BF16), preferred_element_type=F32)


def _lmm(f, x, tn=2048, name="left_matmul"):
    bsz, k, n = x.shape
    m = f.shape[0]
    tn = _tile(n, tn)
    return pl.pallas_call(
        _lmm_kernel, out_shape=jax.ShapeDtypeStruct((bsz, m, n), F32), grid=(bsz, n // tn),
        in_specs=[pl.BlockSpec((m, k), lambda b, j: (0, 0)), pl.BlockSpec((1, k, tn), lambda b, j: (b, 0, j))],
        out_specs=pl.BlockSpec((1, m, tn), lambda b, j: (b, 0, j)),
        compiler_params=_cparams(("parallel", "parallel"), 48), name=name)(f, x)


def _shortconv_kernel(x_ref, xp_ref, xn_ref, w_ref, b_ref, o_ref, *, nt, act):
    i = pl.program_id(1)
    x = x_ref[0]
    tl = x.shape[0]
    row = lax.broadcasted_iota(jnp.int32, x.shape, 0)
    prev_row = jnp.where(i == 0, 0.0, xp_ref[0, 7:8, :])
    next_row = jnp.where(i == nt - 1, 0.0, xn_ref[0, 0:1, :])
    x_prev = jnp.where(row == 0, prev_row, pltpu.roll(x, 1, 0))
    x_next = jnp.where(row == tl - 1, next_row, pltpu.roll(x, tl - 1, 0))
    w = w_ref[0]
    y = b_ref[0, 0:1] + x_prev * w[0:1] + x * w[1:2] + x_next * w[2:3]
    if act:
        y = _silu(y)
    o_ref[0] = y


def _shortconv(proj, col0, nblk, w, b, act):
    bsz, l, _ = proj.shape
    tl = _tile(l, 1024)
    nt = l // tl
    w3 = jnp.transpose(w.reshape(3, nblk, D_GROUP), (1, 0, 2))
    w3 = jnp.pad(w3, ((0, 0), (0, 5), (0, 0)))
    b3 = jnp.broadcast_to(b.reshape(nblk, 1, D_GROUP), (nblk, 8, D_GROUP))
    r8 = tl // 8
    return pl.pallas_call(
        functools.partial(_shortconv_kernel, nt=nt, act=act),
        out_shape=jax.ShapeDtypeStruct((bsz, l, nblk * D_GROUP), F32), grid=(bsz, nt, nblk),
        in_specs=[
            pl.BlockSpec((1, tl, D_GROUP), lambda bb, i, j: (bb, i, col0 + j)),
            pl.BlockSpec((1, 8, D_GROUP), lambda bb, i, j: (bb, jnp.maximum(i * r8 - 1, 0), col0 + j)),
            pl.BlockSpec((1, 8, D_GROUP), lambda bb, i, j: (bb, jnp.minimum((i + 1) * r8, l // 8 - 1), col0 + j)),
            pl.BlockSpec((1, 8, D_GROUP), lambda bb, i, j: (j, 0, 0)),
            pl.BlockSpec((1, 8, D_GROUP), lambda bb, i, j: (j, 0, 0)),
        ],
        out_specs=pl.BlockSpec((1, tl, D_GROUP), lambda bb, i, j: (bb, i, j)),
        compiler_params=_cparams(("parallel", "parallel", "parallel")), name="shortconv")(proj, proj, proj, w3, b3)


def _stack_heads(xb, masks):
    return jnp.concatenate([xb * masks[h] for h in range(N_HEADS)], axis=0)


def _compact(s):
    return s[0:64] + s[64:128] + s[128:192] + s[192:256]


def _expand(c, bd):
    return jnp.concatenate([c, c, c, c], axis=0) * bd


def _head_norm(o, avg, gn):
    mu = _split_dot(o, avg, parts=2)
    oc = o - mu
    var = _split_dot(oc * oc, avg, parts=2)
    return oc * lax.rsqrt(var + EPS) * gn


def _ret_kernel(q_ref, qr_ref, k_ref, kr_ref, v_ref, g_ref, cos_ref, sin_ref,
                dsym_ref, qdf_ref, qdb_ref, kdf_ref, kdb_ref, cdec_ref, bd_ref, avg_ref, gn_ref,
                o_ref, sfw_ref, sbw_ref, save_ref, *, cb, nblk):
    p = pl.program_id(1)
    i = pl.program_id(2)
    masks = _head_masks(BF16)
    bd = bd_ref[...]
    cdec = cdec_ref[...]

    def rope_k(rows):
        return k_ref[0, rows] * cos_ref[rows] + kr_ref[0, rows] * sin_ref[rows]

    def kv_update(s, k, decay, vb):
        kv = _dot_tn((k * decay).astype(BF16), vb)
        return s * cdec + kv * bd

    @pl.when(p == 0)
    def _():
        @pl.when(i == 0)
        def _():
            sbw_ref[...] = jnp.zeros_like(sbw_ref)

        blk = nblk - 1 - i
        for c in reversed(range(cb)):
            rows = slice(c * CHUNK, (c + 1) * CHUNK)
            s = sbw_ref[...]
            save_ref[blk * cb + c] = _compact(s)
            sbw_ref[...] = kv_update(s, rope_k(rows), kdb_ref[...], v_ref[0, rows].astype(BF16))

    @pl.when(p == 1)
    def _():
        @pl.when(i == 0)
        def _():
            sfw_ref[...] = jnp.zeros_like(sfw_ref)

        for c in range(cb):
            rows = slice(c * CHUNK, (c + 1) * CHUNK)
            q = q_ref[0, rows] * cos_ref[rows] + qr_ref[0, rows] * sin_ref[rows]
            k = rope_k(rows)
            qb, kb, vb = q.astype(BF16), k.astype(BF16), v_ref[0, rows].astype(BF16)
            s_all = _dot_nt(qb, _stack_heads(kb, masks))
            pmat = (s_all * dsym_ref[...]).astype(BF16)
            o = jnp.dot(pmat, _stack_heads(vb, masks), preferred_element_type=F32)
            sfw = sfw_ref[...]
            sbw = _expand(save_ref[i * cb + c], bd)
            o = o + jnp.dot(qb, sfw.astype(BF16), preferred_element_type=F32) * qdf_ref[...]
            o = o + jnp.dot(qb, sbw.astype(BF16), preferred_element_type=F32) * qdb_ref[...]
            y = _head_norm(o, avg_ref[...], gn_ref[...])
            o_ref[0, rows] = _silu(g_ref[0, rows]) * y
            sfw_ref[...] = kv_update(sfw, k, kdf_ref[...], vb)


def _ret_tables():
    lg = np.log(1.0 - 2.0 ** (-5.0 - np.arange(N_HEADS, dtype=np.float64)))
    pos = np.arange(CHUNK, dtype=np.float64)
    lag = np.abs(pos[:, None] - pos[None, :])
    dsym = np.concatenate([np.exp(lg[h] * lag) for h in range(N_HEADS)], axis=1)
    lane_lg = np.repeat(lg, HEAD_DIM)[None, :]
    qdf = np.exp(lane_lg * (pos[:, None] + 1.0))
    qdb = np.exp(lane_lg * (CHUNK - pos[:, None]))
    kdf = np.exp(lane_lg * (CHUNK - 1.0 - pos[:, None]))
    kdb = np.exp(lane_lg * pos[:, None])
    cdec = np.exp(lane_lg * CHUNK)
    return [jnp.asarray(t, F32) for t in (dsym, qdf, qdb, kdf, kdb, cdec)]


def _block_diag_mask():
    hid = np.arange(D_GROUP) // HEAD_DIM
    return (hid[:, None] == hid[None, :]).astype(np.float32)


def _rope_tables(l):
    half = HEAD_DIM // 2
    inv = ROPE_BASE ** (-jnp.arange(half, dtype=F32) / half)
    ang = jnp.arange(l, dtype=F32)[:, None] * inv[None, :]
    cos, sin = jnp.cos(ang), jnp.sin(ang)
    cos_full = jnp.tile(jnp.concatenate([cos, cos], -1), (1, N_HEADS))
    sin_full = jnp.tile(jnp.concatenate([sin, sin], -1), (1, N_HEADS))
    return cos_full, sin_full


def _retention(proj, gn_w, cos_full, sin_full):
    bsz, l, _ = proj.shape
    nc = l // CHUNK
    cb = 4 if nc % 4 == 0 else 1
    nblk = nc // cb
    tl = cb * CHUNK
    dsym, qdf, qdb, kdf, kdb, cdec = _ret_tables()
    bd = jnp.asarray(_block_diag_mask())
    avg = jnp.asarray(_block_diag_mask() / HEAD_DIM, BF16)

    def both(col):
        return pl.BlockSpec((1, tl, D_GROUP), lambda b, p, i: (b, i + (1 - p) * (nblk - 1 - 2 * i), col))

    def fwd_only(col):
        return pl.BlockSpec((1, tl, D_GROUP), lambda b, p, i: (b, p * i, col))

    tab = pl.BlockSpec((tl, D_GROUP), lambda b, p, i: (i + (1 - p) * (nblk - 1 - 2 * i), 0))

    def const(shape):
        return pl.BlockSpec(shape, lambda b, p, i: (0,) * len(shape))

    return pl.pallas_call(
        functools.partial(_ret_kernel, cb=cb, nblk=nblk),
        out_shape=jax.ShapeDtypeStruct((bsz, l, D_GROUP), F32), grid=(bsz, 2, nblk),
        in_specs=[fwd_only(CB_RQ), fwd_only(CB_RQR), both(CB_RK), both(CB_RKR), both(CB_RV), fwd_only(CB_RG),
                  tab, tab, const((CHUNK, 4 * CHUNK)), const((CHUNK, D_GROUP)), const((CHUNK, D_GROUP)),
                  const((CHUNK, D_GROUP)), const((CHUNK, D_GROUP)), const((1, D_GROUP)),
                  const((D_GROUP, D_GROUP)), const((D_GROUP, D_GROUP)), const((1, D_GROUP))],
        out_specs=pl.BlockSpec((1, tl, D_GROUP), lambda b, p, i: (b, p * i, 0)),
        scratch_shapes=[pltpu.VMEM((D_GROUP, D_GROUP), F32), pltpu.VMEM((D_GROUP, D_GROUP), F32),
                        pltpu.VMEM((nc, HEAD_DIM, D_GROUP), F32)],
        compiler_params=_cparams(("parallel", "arbitrary", "arbitrary"), 48), name="retention",
    )(proj, proj, proj, proj, proj, proj, cos_full, sin_full, dsym, qdf, qdb, kdf, kdb, cdec, bd, avg,
      gn_w.reshape(1, D_GROUP))


def _mlstm_kernel(q_ref, k_ref, v_ref, og_ref, gc_ref, gr_ref, bc_ref, br_ref, ex_ref, lt_ref, ut_ref,
                  ones_ref, obd_ref, bd_ref, avg_ref, gn_ref,
                  o_ref, cfw_ref, cbw_ref, nmfw_ref, nmbw_ref, csave_ref, nmsave_ref, *, cb, nblk):
    p = pl.program_id(1)
    i = pl.program_id(2)
    masks = _head_masks(BF16)
    bd = bd_ref[...]
    lt = lt_ref[...]
    ut = ut_ref[...]
    ri = lax.broadcasted_iota(jnp.int32, (CHUNK, CHUNK), 0)
    ci = lax.broadcasted_iota(jnp.int32, (CHUNK, CHUNK), 1)
    lane = lax.broadcasted_iota(jnp.int32, (1, D_GROUP), 1)

    def gates_expanded(rows):
        return _split_dot(gc_ref[0, rows] + bc_ref[...], ex_ref[...])

    def state_update(c_ref, nm_ref, total, cum, i_x, k, vb):
        m_prev = nm_ref[1:2]
        g = (total - cum) + i_x
        m_new = jnp.maximum(total + m_prev, jnp.max(g, axis=0, keepdims=True))
        wk = jnp.exp(g - m_new) * k
        decay = jnp.exp(total + m_prev - m_new)
        c_ref[...] = c_ref[...] * decay + _dot_tn(wk.astype(BF16), vb) * bd
        nm_ref[0:1] = decay * nm_ref[0:1] + jnp.sum(wk, axis=0, keepdims=True)
        nm_ref[1:2] = m_new

    @pl.when(p == 0)
    def _():
        @pl.when(i == 0)
        def _():
            cbw_ref[...] = jnp.zeros_like(cbw_ref)
            nmbw_ref[...] = jnp.zeros_like(nmbw_ref)

        blk = nblk - 1 - i
        for c in reversed(range(cb)):
            rows = slice(c * CHUNK, (c + 1) * CHUNK)
            csave_ref[blk * cb + c] = _compact(cbw_ref[...])
            nmsave_ref[blk * cb + c] = nmbw_ref[...]
            gx = gates_expanded(rows)
            cum = _split_dot_left(ut, _log_sigmoid(gx[:, 768:1024]))
            k = k_ref[0, rows] * (HEAD_DIM ** -0.5)
            state_update(cbw_ref, nmbw_ref, cum[0:1], cum, gx[:, 512:768], k, v_ref[0, rows].astype(BF16))

    @pl.when(p == 1)
    def _():
        @pl.when(i == 0)
        def _():
            cfw_ref[...] = jnp.zeros_like(cfw_ref)
            nmfw_ref[...] = jnp.zeros_like(nmfw_ref)

        for c in range(cb):
            rows = slice(c * CHUNK, (c + 1) * CHUNK)
            q = q_ref[0, rows]
            k = k_ref[0, rows] * (HEAD_DIM ** -0.5)
            qb, kb, vb = q.astype(BF16), k.astype(BF16), v_ref[0, rows].astype(BF16)
            s_all = _dot_nt(qb, _stack_heads(kb, masks))
            vaug = jnp.concatenate([_stack_heads(vb, masks), ones_ref[...]], axis=1)
            gx = gates_expanded(rows)
            graw = gr_ref[0, :, rows] + br_ref[...]
            gls = _log_sigmoid(graw)
            cum_r_fw = _split_dot(gls, ut)
            cum_r_bw = _split_dot(gls, lt)
            ccomp = csave_ref[i * cb + c]
            nmb = nmsave_ref[i * cb + c]

            def direction(i_x, f_x, tri, cum_r, i_row0, f_row0, mask, c_state, n_vec, m_prev, total_row):
                cum = _split_dot_left(tri, _log_sigmoid(f_x))
                total = cum[total_row:total_row + 1]
                inter = cum + m_prev
                ps, rmax = [], []
                dms = []
                for h in range(N_HEADS):
                    a_col = cum[:, h * HEAD_DIM:h * HEAD_DIM + 1]
                    dm = a_col - cum_r[f_row0 + h:f_row0 + h + 1] + graw[i_row0 + h:i_row0 + h + 1]
                    dm = jnp.where(mask, dm, -jnp.inf)
                    dms.append(dm)
                    rmax.append(jnp.max(dm, axis=-1, keepdims=True))
                rmax256 = jnp.where(lane < 64, rmax[0], jnp.where(lane < 128, rmax[1],
                                    jnp.where(lane < 192, rmax[2], rmax[3])))
                m_row = jnp.maximum(inter, rmax256)
                for h in range(N_HEADS):
                    m_h = m_row[:, h * HEAD_DIM:h * HEAD_DIM + 1]
                    ps.append(s_all[:, h * CHUNK:(h + 1) * CHUNK] * jnp.exp(dms[h] - m_h))
                pmat = jnp.concatenate(ps, axis=1).astype(BF16)
                nd = jnp.dot(pmat, vaug, preferred_element_type=F32)
                w_inter = jnp.exp(inter - m_row)
                qc = jnp.dot(qb, c_state.astype(BF16), preferred_element_type=F32)
                qn = _split_dot(q * n_vec, obd_ref[...])
                num = nd[:, :D_GROUP] + w_inter * qc
                den = nd[:, D_GROUP:] + w_inter * qn
                hdir = num / jnp.maximum(jnp.abs(den), jnp.exp(-m_row))
                return hdir, total, cum

            h_fw, tot_fw, cum_fw = direction(gx[:, 0:256], gx[:, 256:512], lt, cum_r_fw, 0, 4, ri >= ci,
                                             cfw_ref[...], nmfw_ref[0:1], nmfw_ref[1:2], CHUNK - 1)
            h_bw, _, _ = direction(gx[:, 512:768], gx[:, 768:1024], ut, cum_r_bw, 8, 12, ci >= ri,
                                   _expand(ccomp, bd), nmb[0:1], nmb[1:2], 0)
            y = _head_norm(h_fw + h_bw, avg_ref[...], gn_ref[...])
            o_ref[0, rows] = _sigmoid(og_ref[0, rows]) * y
            state_update(cfw_ref, nmfw_ref, tot_fw, cum_fw, gx[:, 0:256], k, vb)


def _mlstm(proj, qk, gates_row, gate_b, gn_w):
    bsz, l, _ = proj.shape
    nc = l // CHUNK
    cb = 2 if nc % 2 == 0 else 1
    nblk = nc // cb
    tl = cb * CHUNK
    bd_np = _block_diag_mask()
    bd = jnp.asarray(bd_np)
    avg = jnp.asarray(bd_np / HEAD_DIM, BF16)
    obd = jnp.asarray(bd_np, BF16)
    ex = np.zeros((LANES, 4 * D_GROUP), np.float32)
    for j in range(16):
        typ, h = divmod(j, N_HEADS)
        ex[j, typ * D_GROUP + h * HEAD_DIM: typ * D_GROUP + (h + 1) * HEAD_DIM] = 1.0
    idx = np.arange(CHUNK)
    lt = (idx[None, :] <= idx[:, None]).astype(np.float32)
    ones_st = np.repeat(np.repeat(np.eye(N_HEADS, dtype=np.float32), CHUNK, 0), HEAD_DIM, 1)
    gb = gate_b.astype(F32).reshape(16)
    bias_col = jnp.pad(gb, (0, LANES - 16)).reshape(1, LANES)
    bias_row = jnp.broadcast_to(gb.reshape(16, 1), (16, CHUNK))

    def both(arr_col, width=D_GROUP):
        return pl.BlockSpec((1, tl, width), lambda b, p, i: (b, i + (1 - p) * (nblk - 1 - 2 * i), arr_col))

    def fwd_only(arr_col):
        return pl.BlockSpec((1, tl, D_GROUP), lambda b, p, i: (b, p * i, arr_col))

    def const(shape):
        return pl.BlockSpec(shape, lambda b, p, i: (0,) * len(shape))

    return pl.pallas_call(
        functools.partial(_mlstm_kernel, cb=cb, nblk=nblk),
        out_shape=jax.ShapeDtypeStruct((bsz, l, D_GROUP), F32), grid=(bsz, 2, nblk),
        in_specs=[fwd_only(0), both(1), both(CB_MV), fwd_only(CB_MO), both(GATE_COL128, LANES),
                  pl.BlockSpec((1, 16, tl), lambda b, p, i: (b, 0, p * i)),
                  const((1, LANES)), const((16, CHUNK)), const((LANES, 4 * D_GROUP)),
                  const((CHUNK, CHUNK)), const((CHUNK, CHUNK)), const((4 * CHUNK, D_GROUP)),
                  const((D_GROUP, D_GROUP)), const((D_GROUP, D_GROUP)), const((D_GROUP, D_GROUP)),
                  const((1, D_GROUP))],
        out_specs=pl.BlockSpec((1, tl, D_GROUP), lambda b, p, i: (b, p * i, 0)),
        scratch_shapes=[pltpu.VMEM((D_GROUP, D_GROUP), F32), pltpu.VMEM((D_GROUP, D_GROUP), F32),
                        pltpu.VMEM((8, D_GROUP), F32), pltpu.VMEM((8, D_GROUP), F32),
                        pltpu.VMEM((nc, HEAD_DIM, D_GROUP), F32), pltpu.VMEM((nc, 8, D_GROUP), F32)],
        compiler_params=_cparams(("parallel", "arbitrary", "arbitrary"), 48), name="mlstm",
    )(qk, qk, proj, proj, proj, gates_row, bias_col, bias_row, jnp.asarray(ex, BF16), jnp.asarray(lt, BF16),
      jnp.asarray(lt.T, BF16), jnp.asarray(ones_st, BF16), obd, bd, avg, gn_w.reshape(1, D_GROUP))


def _s5_kernel(u_ref, mt_ref, bg_ref, cg_ref, pa_ref, pb_ref, o_ref, *, nsteps):
    ub = u_ref[0].astype(BF16)
    e = jnp.dot(ub, bg_ref[0], preferred_element_type=F32)
    r = e.shape[0]
    row = lax.broadcasted_iota(jnp.int32, (r, LANES), 0)
    xf, xb = e[:, :LANES], e[:, LANES:]
    pa, pb = pa_ref[0], pb_ref[0]
    for s in range(nsteps):
        sh = 1 << s
        a_f, b_f = pa[s:s + 1, :LANES], pb[s:s + 1, :LANES]
        a_b, b_b = pa[s:s + 1, LANES:], pb[s:s + 1, LANES:]
        yf = jnp.where(row >= sh, pltpu.roll(xf, sh, 0), 0.0)
        yb = jnp.where(row < r - sh, pltpu.roll(xb, r - sh, 0), 0.0)
        xf = xf + a_f * yf + b_f * pltpu.roll(yf, LANES // 2, 1)
        xb = xb + a_b * yb + b_b * pltpu.roll(yb, LANES // 2, 1)
    sprev = jnp.where(row >= 1, pltpu.roll(xf, 1, 0), 0.0)
    snext = jnp.where(row < r - 1, pltpu.roll(xb, r - 1, 0), 0.0)
    st = jnp.concatenate([sprev, snext], axis=1).astype(BF16)
    o_ref[0] = (jnp.dot(ub, mt_ref[0], preferred_element_type=F32)
                + jnp.dot(st, cg_ref[0], preferred_element_type=F32))


def _s5_tables(a_re, a_im, log_dt, b_re, b_im, c_re, c_im, d_skip, tc, nsteps):
    g, p, ch = S5_GROUPS, S5_STATE, S5_CH
    a = lax.complex(a_re.astype(F32), a_im.astype(F32))
    delta = jnp.exp(log_dt.astype(F32))[..., None]
    la = a * delta

    def apow(n):
        n = jnp.asarray(n, F32)
        return jnp.exp(la[:, :, None, :] * n[None, None, :, None])

    a_bar = jnp.exp(la)
    b = lax.complex(b_re.astype(F32), b_im.astype(F32))
    b_bar = ((a_bar - 1.0) / a)[..., None] * b[None]
    c = lax.complex(c_re.astype(F32), c_im.astype(F32))
    taus = np.arange(tc)
    kk = jnp.real(jnp.einsum("dgop,dgtp,dgpi->dgtoi", c, apow(taus), b_bar))
    diff = taus[None, :] - taus[:, None]
    k0 = kk[0][:, np.clip(diff, 0, None)]
    k1 = kk[1][:, np.clip(-diff, 0, None)]
    dsk = d_skip.astype(F32).reshape(g, ch)[:, :, None] * jnp.eye(ch, dtype=F32)[None]
    kdiag = kk[0][:, 0] + kk[1][:, 0] + dsk
    dm = diff[None, :, :, None, None]
    kfull = jnp.where(dm > 0, k0, jnp.where(dm < 0, k1, kdiag[:, None, None]))
    mt = jnp.transpose(kfull, (0, 1, 4, 2, 3)).reshape(g, tc * ch, tc * ch)

    zf = apow(tc - 1 - taus)[0][..., None] * b_bar[0][:, None]
    zb = apow(taus)[1][..., None] * b_bar[1][:, None]

    def to_rows(z):
        return jnp.transpose(z, (0, 1, 3, 2)).reshape(g, tc * ch, p)

    bg = jnp.concatenate([to_rows(jnp.real(zf)), to_rows(jnp.imag(zf)),
                          to_rows(jnp.real(zb)), to_rows(jnp.imag(zb))], axis=-1)

    yf = c[0][:, None] * apow(taus + 1)[0][:, :, None, :]
    yb = c[1][:, None] * apow(tc - taus)[1][:, :, None, :]

    def to_cols(z):
        return jnp.transpose(z, (0, 3, 1, 2)).reshape(g, p, tc * ch)

    cg = jnp.concatenate([to_cols(jnp.real(yf)), -to_cols(jnp.imag(yf)),
                          to_cols(jnp.real(yb)), -to_cols(jnp.imag(yb))], axis=1)

    steps = tc * (2.0 ** np.arange(nsteps))
    pw = apow(steps)
    re0, im0, re1, im1 = jnp.real(pw[0]), jnp.imag(pw[0]), jnp.real(pw[1]), jnp.imag(pw[1])
    pa = jnp.concatenate([re0, re0, re1, re1], axis=-1)
    pb = jnp.concatenate([-im0, im0, -im1, im1], axis=-1)
    pad = (-nsteps) % 8
    pa = jnp.pad(pa, ((0, 0), (0, pad), (0, 0)))
    pb = jnp.pad(pb, ((0, 0), (0, pad), (0, 0)))
    return mt.astype(BF16), bg.astype(BF16), cg.astype(BF16), pa, pb


def _s5_glu_kernel(y_ref, w_ref, o_ref):
    y = y_ref[...]
    z = 0.5 * y * (1.0 + jnp.tanh(math.sqrt(2.0 / math.pi) * (y + 0.044715 * (y * y * y))))
    o_ref[...] = z * _sigmoid(jnp.dot(z.astype(BF16), w_ref[...], preferred_element_type=F32))


def _s5(proj, a_re, a_im, log_dt, b_re, b_im, c_re, c_im, d_skip, w_glu):
    bsz, l, _ = proj.shape
    tc = S5_TC
    r = l // tc
    nsteps = max(1, int(math.ceil(math.log2(r))))
    w = tc * S5_CH
    mt, bg, cg, pa, pb = _s5_tables(a_re, a_im, log_dt, b_re, b_im, c_re, c_im, d_skip, tc, nsteps)
    u = proj[:, :, CB_S5 * D_GROUP:(CB_S5 + 1) * D_GROUP]
    ug = jnp.transpose(u.reshape(bsz, r, tc, S5_GROUPS, S5_CH), (3, 0, 1, 2, 4)).reshape(S5_GROUPS, bsz * r, w)
    ns8 = pa.shape[1]
    yg = pl.pallas_call(
        functools.partial(_s5_kernel, nsteps=nsteps),
        out_shape=jax.ShapeDtypeStruct((S5_GROUPS, bsz * r, w), F32), grid=(S5_GROUPS, bsz),
        in_specs=[pl.BlockSpec((1, r, w), lambda g, b: (g, b, 0)),
                  pl.BlockSpec((1, w, w), lambda g, b: (g, 0, 0)),
                  pl.BlockSpec((1, w, D_GROUP), lambda g, b: (g, 0, 0)),
                  pl.BlockSpec((1, D_GROUP, w), lambda g, b: (g, 0, 0)),
                  pl.BlockSpec((1, ns8, D_GROUP), lambda g, b: (g, 0, 0)),
                  pl.BlockSpec((1, ns8, D_GROUP), lambda g, b: (g, 0, 0))],
        out_specs=pl.BlockSpec((1, r, w), lambda g, b: (g, b, 0)),
        compiler_params=_cparams(("parallel", "parallel"), 48), name="s5_ssm")(ug, mt, bg, cg, pa, pb)
    y = jnp.transpose(yg.reshape(S5_GROUPS, bsz, r, tc, S5_CH), (1, 2, 3, 0, 4)).reshape(bsz * l, D_GROUP)
    t = bsz * l
    tm = _tile(t, 2048)
    out = pl.pallas_call(
        _s5_glu_kernel, out_shape=jax.ShapeDtypeStruct((t, D_GROUP), F32), grid=(t // tm,),
        in_specs=[pl.BlockSpec((tm, D_GROUP), lambda i: (i, 0)), pl.BlockSpec((D_GROUP, D_GROUP), lambda i: (0, 0))],
        out_specs=pl.BlockSpec((tm, D_GROUP), lambda i: (i, 0)),
        compiler_params=_cparams(("parallel",)), name="s5_glu")(y, w_glu.astype(BF16))
    return out.reshape(bsz, l, D_GROUP)


def _hy_filter_kernel(z_ref, w1_ref, b1_ref, w2_ref, b2_ref, w3_ref, fr_ref, dec_ref, h_ref, ss_ref):
    i = pl.program_id(0)
    hp = lax.Precision.HIGHEST
    fr = fr_ref[...]
    a = jnp.sin(fr * (jnp.dot(z_ref[...], w1_ref[...], precision=hp, preferred_element_type=F32) + b1_ref[...]))
    a = jnp.sin(fr * (jnp.dot(a, w2_ref[...], precision=hp, preferred_element_type=F32) + b2_ref[...]))
    h = jnp.dot(a, w3_ref[...], precision=hp, preferred_element_type=F32)
    dec = dec_ref[...]
    h = h * jnp.concatenate([dec, dec, dec, dec], axis=1)
    h_ref[...] = h

    @pl.when(i == 0)
    def _():
        ss_ref[...] = jnp.zeros_like(ss_ref)

    ss_ref[...] += jnp.sum(h * h, axis=0, keepdims=True)


def _hy_filters(l, w1, b1, w2, b2, w3, freq):
    t = jnp.linspace(0.0, 1.0, l, dtype=F32)[:, None]
    w = 2.0 * math.pi * jnp.arange(l, dtype=F32)[:, None] / l
    bands = jnp.linspace(1e-4, HY_BANDS - 1, HY_BANDS, dtype=F32)[None, :]
    z = jnp.concatenate([t, jnp.cos(bands * w), -jnp.sin(bands * w)], axis=-1)
    z = jnp.pad(z, ((0, 0), (0, LANES - HY_EMB)))
    max_decay = math.log(HY_TARGET) / HY_FAST_DECAY
    min_decay = math.log(HY_TARGET) / HY_SLOW_DECAY
    rates = jnp.abs(jnp.linspace(min_decay, max_decay, D_GROUP, dtype=F32))
    dec = jnp.exp(-t * rates)
    pf = LANES - HY_FFN
    w1p = jnp.pad(w1.astype(F32), ((0, LANES - HY_EMB), (0, pf)))
    w2p = jnp.pad(w2.astype(F32), ((0, pf), (0, pf)))
    w3p = jnp.pad(w3.astype(F32), ((0, pf), (0, 0)))
    b1p = jnp.pad(b1.astype(F32), (0, pf)).reshape(1, LANES)
    b2p = jnp.pad(b2.astype(F32), (0, pf)).reshape(1, LANES)
    frp = jnp.pad(freq.astype(F32), (0, pf)).reshape(1, LANES)
    nout = HY_ORDER * 2 * D_GROUP
    tl = _tile(l, 512)

    def const(shape):
        return pl.BlockSpec(shape, lambda i: (0, 0))

    return pl.pallas_call(
        _hy_filter_kernel,
        out_shape=(jax.ShapeDtypeStruct((l, nout), F32), jax.ShapeDtypeStruct((1, nout), F32)), grid=(l // tl,),
        in_specs=[pl.BlockSpec((tl, LANES), lambda i: (i, 0)), const((LANES, LANES)), const((1, LANES)),
                  const((LANES, LANES)), const((1, LANES)), const((LANES, nout)), const((1, LANES)),
                  pl.BlockSpec((tl, D_GROUP), lambda i: (i, 0))],
        out_specs=(pl.BlockSpec((tl, nout), lambda i: (i, 0)), const((1, nout))),
        compiler_params=_cparams(("arbitrary",)), name="hyena_filter_mlp")(z, w1p, b1p, w2p, b2p, w3p, frp, dec)


def _dft_consts(n1):
    n2 = FFT_N2
    n = n1 * n2
    i1 = np.arange(n1, dtype=np.float64)
    th = 2.0 * np.pi * np.outer(i1, i1) / n1
    c1, s1 = np.cos(th), np.sin(th)
    fa_full = np.concatenate([c1, -s1], axis=0)
    fa_half = fa_full[:, : n1 // 2]
    ga = np.concatenate([c1[: n1 // 2], -s1[: n1 // 2]], axis=1) / n
    i2 = np.arange(n2, dtype=np.float64)
    ph = 2.0 * np.pi * np.outer(i2, i2) / n2
    c2, s2 = np.cos(ph), np.sin(ph)
    fb = np.block([[c2, s2], [-s2, c2]])
    fbc = np.block([[c2, -s2], [s2, c2]])
    ps = 2.0 * np.pi * np.outer(i1, i2) / n
    twr = np.broadcast_to(np.cos(ps)[:, :, None], (n1, n2, LANES))
    twi = np.broadcast_to(-np.sin(ps)[:, :, None], (n1, n2, LANES))
    as_bf = lambda x: jnp.asarray(x, BF16)
    return dict(fa_full=as_bf(fa_full), fa_half=as_bf(fa_half), ga=as_bf(ga), fb=as_bf(fb), fbc=as_bf(fbc),
                twr=jnp.asarray(twr, F32), twi=jnp.asarray(twi, F32))


def _lane_tile(x, reps):
    return x if reps == 1 else jnp.concatenate([x] * reps, axis=-1)


def _hy_spec_kernel(a_ref, twr_ref, twi_ref, fb_ref, ss_ref, o_ref, *, kb, reps):
    scale = lax.rsqrt(ss_ref[...])
    for j in range(kb):
        ar, ai = a_ref[0, j], a_ref[1, j]
        twr, twi = _lane_tile(twr_ref[j], reps), _lane_tile(twi_ref[j], reps)
        br = twr * ar - twi * ai
        bi = twr * ai + twi * ar
        x = jnp.dot(fb_ref[...], jnp.concatenate([br, bi], axis=0).astype(BF16), preferred_element_type=F32)
        o_ref[0, j] = x[:FFT_N2] * scale
        o_ref[1, j] = x[FFT_N2:] * scale


def _hy_mid_kernel(a_ref, h_ref, twr_ref, twi_ref, fb_ref, fbc_ref, o_ref, *, kb, reps):
    for j in range(kb):
        ar, ai = a_ref[0, 0, j], a_ref[0, 1, j]
        twr, twi = _lane_tile(twr_ref[j], reps), _lane_tile(twi_ref[j], reps)
        br = twr * ar - twi * ai
        bi = twr * ai + twi * ar
        x = jnp.dot(fb_ref[...], jnp.concatenate([br, bi], axis=0).astype(BF16), preferred_element_type=F32)
        xr, xi = x[:FFT_N2], x[FFT_N2:]
        hr, hi = h_ref[0, j], h_ref[1, j]
        yr = xr * hr - xi * hi
        yi = xr * hi + xi * hr
        z = jnp.dot(fbc_ref[...], jnp.concatenate([yr, yi], axis=0).astype(BF16), preferred_element_type=F32)
        zr, zi = z[:FFT_N2], z[FFT_N2:]
        o_ref[0, 0, j] = twr * zr + twi * zi
        o_ref[0, 1, j] = twr * zi - twi * zr


def _hy_out_kernel(g_ref, z_ref, x_ref, v_ref, b_ref, o_ref):
    y = jnp.dot(g_ref[...], z_ref[0].astype(BF16), preferred_element_type=F32)
    v = v_ref[0]
    o_ref[0] = x_ref[0] * (y + v * b_ref[...])


def _hyena(proj, conv_w, conv_b, w1, b1, w2, b2, w3, freq, bias):
    bsz, l, _ = proj.shape
    n1 = 2 * l // FFT_N2
    nh = n1 // 2
    c = D_GROUP
    dc = _dft_consts(n1)
    pc = _shortconv(proj, CB_HV, 3, conv_w, conv_b, act=False)

    h, ss = _hy_filters(l, w1, b1, w2, b2, w3, freq)
    h4 = h.reshape(l, HY_ORDER, 2, c)
    ss4 = ss.reshape(HY_ORDER, 2, c)
    ssn = (ss4[:, 0] + ss4[:, 1]).reshape(1, HY_ORDER * c)
    hf, hb = h4[:, :, 0], h4[:, :, 1]
    kern = jnp.concatenate([hf, jnp.zeros_like(hf[:1]), hb[:0:-1]], axis=0).reshape(1, n1, FFT_N2 * HY_ORDER * c)
    ncf = HY_ORDER * c
    ka = _lmm(dc["fa_full"], kern, name="hyena_filter_dft1").reshape(2, n1, FFT_N2, ncf)
    kb = 8 if n1 % 8 == 0 else 1
    cbw = 256
    reps = cbw // LANES
    hspec = pl.pallas_call(
        functools.partial(_hy_spec_kernel, kb=kb, reps=reps),
        out_shape=jax.ShapeDtypeStruct((2, n1, FFT_N2, ncf), F32), grid=(ncf // cbw, n1 // kb),
        in_specs=[pl.BlockSpec((2, kb, FFT_N2, cbw), lambda j, k: (0, k, 0, j)),
                  pl.BlockSpec((kb, FFT_N2, LANES), lambda j, k: (k, 0, 0)),
                  pl.BlockSpec((kb, FFT_N2, LANES), lambda j, k: (k, 0, 0)),
                  pl.BlockSpec((2 * FFT_N2, 2 * FFT_N2), lambda j, k: (0, 0)),
                  pl.BlockSpec((1, cbw), lambda j, k: (0, j))],
        out_specs=pl.BlockSpec((2, kb, FFT_N2, cbw), lambda j, k: (0, k, 0, j)),
        compiler_params=_cparams(("parallel", "parallel"), 48), name="hyena_filter_dft2",
    )(ka, dc["twr"], dc["twi"], dc["fb"], ssn)


    def long_conv_gate(zin, order, xcol):
        zarr, zcol = zin
        zw = zarr.shape[-1] // c
        zv = zarr.reshape(bsz, nh, FFT_N2 * zarr.shape[-1])
        a = pl.pallas_call(
            _lmm_kernel, out_shape=jax.ShapeDtypeStruct((bsz, 2 * n1, FFT_N2 * c), F32),
            grid=(bsz, FFT_N2),
            in_specs=[pl.BlockSpec((2 * n1, nh), lambda b, j: (0, 0)),
                      pl.BlockSpec((1, nh, c), lambda b, j: (b, 0, j * zw + zcol))],
            out_specs=pl.BlockSpec((1, 2 * n1, c), lambda b, j: (b, 0, j)),
            compiler_params=_cparams(("parallel", "parallel"), 48), name="hyena_dft1")(dc["fa_half"], zv)
        a = a.reshape(bsz, 2, n1, FFT_N2, c)
        zmid = pl.pallas_call(
            functools.partial(_hy_mid_kernel, kb=kb, reps=reps),
            out_shape=jax.ShapeDtypeStruct((bsz, 2, n1, FFT_N2, c), F32), grid=(bsz, n1 // kb),
            in_specs=[pl.BlockSpec((1, 2, kb, FFT_N2, c), lambda b, k: (b, 0, k, 0, 0)),
                      pl.BlockSpec((2, kb, FFT_N2, c), lambda b, k: (0, k, 0, order)),
                      pl.BlockSpec((kb, FFT_N2, LANES), lambda b, k: (k, 0, 0)),
                      pl.BlockSpec((kb, FFT_N2, LANES), lambda b, k: (k, 0, 0)),
                      pl.BlockSpec((2 * FFT_N2, 2 * FFT_N2), lambda b, k: (0, 0)),
                      pl.BlockSpec((2 * FFT_N2, 2 * FFT_N2), lambda b, k: (0, 0))],
            out_specs=pl.BlockSpec((1, 2, kb, FFT_N2, c), lambda b, k: (b, 0, k, 0, 0)),
            compiler_params=_cparams(("parallel", "parallel"), 48), name="hyena_dft_mid",
        )(a, hspec, dc["twr"], dc["twi"], dc["fb"], dc["fbc"])
        zmid = zmid.reshape(bsz, 2 * n1, FFT_N2 * c)
        pcv = pc.reshape(bsz, nh, FFT_N2 * 3 * c)
        out = pl.pallas_call(
            _hy_out_kernel, out_shape=jax.ShapeDtypeStruct((bsz, nh, FFT_N2 * c), F32), grid=(bsz, FFT_N2),
            in_specs=[pl.BlockSpec((nh, 2 * n1), lambda b, j: (0, 0)),
                      pl.BlockSpec((1, 2 * n1, c), lambda b, j: (b, 0, j)),
                      pl.BlockSpec((1, nh, c), lambda b, j: (b, 0, j * 3 + xcol)),
                      pl.BlockSpec((1, nh, c), lambda b, j: (b, 0, j * zw + zcol)),
                      pl.BlockSpec((1, c), lambda b, j: (0, 0))],
            out_specs=pl.BlockSpec((1, nh, c), lambda b, j: (b, 0, j)),
            compiler_params=_cparams(("parallel", "parallel"), 48), name="hyena_idft_gate",
        )(dc["ga"], zmid, pcv, zv, bias[order].astype(F32).reshape(1, c))
        return out.reshape(bsz, l, c)

    z1 = long_conv_gate((pc, 0), 0, 1)
    z2 = long_conv_gate((z1, 0), 1, 2)
    return z2


def _ffn_kernel(x_ref, cmb_ref, w1_ref, w3_ref, w2_ref, o_ref, xb_ref, acc_ref, *, ne, nf, routed):
    e = pl.program_id(1)
    f = pl.program_id(2)

    @pl.when((e == 0) & (f == 0))
    def _():
        xb_ref[...] = x_ref[...].astype(BF16)
        acc_ref[...] = jnp.zeros_like(acc_ref)

    xb = xb_ref[...]
    a = jnp.dot(xb, w1_ref[0], preferred_element_type=F32)
    b = jnp.dot(xb, w3_ref[0], preferred_element_type=F32)
    hid = _silu(a) * b
    if routed:
        lane = lax.broadcasted_iota(jnp.int32, cmb_ref.shape, 1)
        hid = hid * jnp.sum(jnp.where(lane == e, cmb_ref[...], 0.0), axis=1, keepdims=True)
    acc_ref[...] += jnp.dot(hid.astype(BF16), w2_ref[0], preferred_element_type=F32)

    @pl.when((e == ne - 1) & (f == nf - 1))
    def _():
        o_ref[...] = acc_ref[...]


def _ffn(x, cmb, w1, w3, w2, routed):
    t, d = x.shape
    ne, _, ff = w1.shape
    tm = _tile(t, 1024)
    tf = 512 if ff % 512 == 0 else (256 if ff % 256 == 0 else ff)
    nf = ff // tf
    return pl.pallas_call(
        functools.partial(_ffn_kernel, ne=ne, nf=nf, routed=routed),
        out_shape=jax.ShapeDtypeStruct((t, d), F32), grid=(t // tm, ne, nf),
        in_specs=[pl.BlockSpec((tm, d), lambda i, e, f: (i, 0)),
                  pl.BlockSpec((tm, LANES), lambda i, e, f: (i, 0)),
                  pl.BlockSpec((1, d, tf), lambda i, e, f: (e, 0, f)),
                  pl.BlockSpec((1, d, tf), lambda i, e, f: (e, 0, f)),
                  pl.BlockSpec((1, tf, d), lambda i, e, f: (e, f, 0))],
        out_specs=pl.BlockSpec((tm, d), lambda i, e, f: (i, 0)),
        scratch_shapes=[pltpu.VMEM((tm, d), BF16), pltpu.VMEM((tm, d), F32)],
        compiler_params=_cparams(("parallel", "arbitrary", "arbitrary"), 52), name="swiglu_ffn",
    )(x, cmb, w1, w3, w2)


MOE_TB = 1024
MOE_SUB = 288
MOE_SLOT = 384
MOE_CUM = 256


def _moe_kernel(cnt_ref, x_ref, cmb_ref, cmbt_ref, lt_ref, ut_ref, w1_ref, w3_ref, w2_ref, o_ref,
                xb_ref, xs_ref, ys_ref, gs_ref, posc_ref, posr_ref, *, nf):
    i = pl.program_id(0)
    e = pl.program_id(1)
    f = pl.program_id(2)
    tb = x_ref.shape[0]
    count = cnt_ref[i * N_EXPERTS + e]
    npass = (count + (MOE_SUB - 1)) // MOE_SUB

    @pl.when((e == 0) & (f == 0))
    def _():
        xb_ref[...] = x_ref[...].astype(BF16)
        o_ref[...] = jnp.zeros_like(o_ref)
        ys_ref[...] = jnp.zeros_like(ys_ref)
        carry_c = jnp.zeros((1, LANES), F32)
        carry_r = jnp.zeros((N_EXPERTS, 1), F32)
        for c in range(tb // MOE_CUM):
            rows = slice(c * MOE_CUM, (c + 1) * MOE_CUM)
            mc = (cmb_ref[rows] > 0.0).astype(F32)
            inc = jnp.dot(lt_ref[...], mc.astype(BF16), preferred_element_type=F32) + carry_c
            posc_ref[rows] = jnp.where(mc > 0.0, inc - 1.0, -1.0)
            carry_c = inc[MOE_CUM - 1:MOE_CUM]
            mr = (cmbt_ref[:, rows] > 0.0).astype(F32)
            incr = jnp.dot(mr.astype(BF16), ut_ref[...], preferred_element_type=F32) + carry_r
            posr_ref[:, rows] = jnp.where(mr > 0.0, incr - 1.0, -1.0)
            carry_r = incr[:, MOE_CUM - 1:MOE_CUM]

    @pl.when(f == 0)
    def _():
        pos_row = posr_ref[pl.ds(e, 1), :]
        gate_row = cmbt_ref[pl.ds(e, 1), :]
        slot = lax.broadcasted_iota(jnp.int32, (MOE_SUB, tb), 0).astype(F32)

        def gather(j, carry):
            base = pl.multiple_of(j * MOE_SLOT, LANES)
            hit = pos_row == slot + (j * MOE_SUB).astype(F32)
            xs_ref[pl.ds(base, MOE_SUB), :] = jnp.dot(
                jnp.where(hit, 1.0, 0.0).astype(BF16), xb_ref[...], preferred_element_type=F32).astype(BF16)
            g = jnp.sum(jnp.where(hit, gate_row, 0.0), axis=1, keepdims=True)
            gs_ref[pl.ds(base, MOE_SUB), :] = jnp.broadcast_to(g, (MOE_SUB, LANES))
            ys_ref[pl.ds(base, MOE_SUB), :] = jnp.zeros((MOE_SUB, ys_ref.shape[1]), F32)
            return carry

        lax.fori_loop(0, npass, gather, 0)

    def expert(j, carry):
        base = pl.multiple_of(j * MOE_SLOT, LANES)
        xs = xs_ref[pl.ds(base, MOE_SUB), :]
        a = jnp.dot(xs, w1_ref[0], preferred_element_type=F32)
        b = jnp.dot(xs, w3_ref[0], preferred_element_type=F32)
        hid = _silu(a) * b * gs_ref[pl.ds(base, MOE_SUB), 0:1]
        ys_ref[pl.ds(base, MOE_SUB), :] += jnp.dot(hid.astype(BF16), w2_ref[0], preferred_element_type=F32)
        return carry

    lax.fori_loop(0, npass, expert, 0)

    @pl.when(f == nf - 1)
    def _():
        lane = lax.broadcasted_iota(jnp.int32, (tb, LANES), 1)
        pos_col = jnp.sum(jnp.where(lane == e, posc_ref[...], 0.0), axis=1, keepdims=True)
        slot = lax.broadcasted_iota(jnp.int32, (tb, MOE_SLOT), 1)
        slot = jnp.where(slot < MOE_SUB, slot, -2 * tb).astype(F32)

        def scatter(j, carry):
            base = pl.multiple_of(j * MOE_SLOT, LANES)
            hit = pos_col == slot + (j * MOE_SUB).astype(F32)
            o_ref[...] += jnp.dot(jnp.where(hit, 1.0, 0.0).astype(BF16),
                                  ys_ref[pl.ds(base, MOE_SLOT), :].astype(BF16), preferred_element_type=F32)
            return carry

        lax.fori_loop(0, npass, scatter, 0)


def _moe(x, cmb, w1, w3, w2):
    t, d = x.shape
    ne, _, ff = w1.shape
    tb = _tile(t, MOE_TB)
    nb = t // tb
    tf = 896 if ff % 896 == 0 else ff
    nf = ff // tf
    max_pass = -(-tb // MOE_SUB)
    cmbt = jnp.transpose(cmb[:, :N_EXPERTS])
    counts = jnp.sum((cmb[:, :N_EXPERTS] > 0.0).reshape(nb, tb, N_EXPERTS), axis=1).astype(jnp.int32).reshape(-1)
    idx = np.arange(MOE_CUM)
    lt = jnp.asarray(idx[None, :] <= idx[:, None], BF16)
    grid_spec = pltpu.PrefetchScalarGridSpec(
        num_scalar_prefetch=1, grid=(nb, ne, nf),
        in_specs=[pl.BlockSpec((tb, d), lambda i, e, f, c: (i, 0)),
                  pl.BlockSpec((tb, LANES), lambda i, e, f, c: (i, 0)),
                  pl.BlockSpec((N_EXPERTS, tb), lambda i, e, f, c: (0, i)),
                  pl.BlockSpec((MOE_CUM, MOE_CUM), lambda i, e, f, c: (0, 0)),
                  pl.BlockSpec((MOE_CUM, MOE_CUM), lambda i, e, f, c: (0, 0)),
                  pl.BlockSpec((1, d, tf), lambda i, e, f, c: (e, 0, f)),
                  pl.BlockSpec((1, d, tf), lambda i, e, f, c: (e, 0, f)),
                  pl.BlockSpec((1, tf, d), lambda i, e, f, c: (e, f, 0))],
        out_specs=pl.BlockSpec((tb, d), lambda i, e, f, c: (i, 0)),
        scratch_shapes=[pltpu.VMEM((tb, d), BF16), pltpu.VMEM((max_pass * MOE_SLOT, d), BF16),
                        pltpu.VMEM((max_pass * MOE_SLOT, d), F32), pltpu.VMEM((max_pass * MOE_SLOT, LANES), F32),
                        pltpu.VMEM((tb, LANES), F32), pltpu.VMEM((N_EXPERTS, tb), F32)])
    return pl.pallas_call(
        functools.partial(_moe_kernel, nf=nf), out_shape=jax.ShapeDtypeStruct((t, d), F32), grid_spec=grid_spec,
        compiler_params=_cparams(("parallel", "arbitrary", "arbitrary"), 56), name="moe_routed",
    )(counts, x, cmb, cmbt, lt, jnp.transpose(lt), w1, w3, w2)


def _router_kernel(x_ref, rh_ref, rl_ref, o_ref):
    x = x_ref[...]
    xh = x.astype(BF16)
    xl = (x - xh.astype(F32)).astype(BF16)
    logits = (jnp.dot(xh, rh_ref[...], preferred_element_type=F32)
              + jnp.dot(xl, rh_ref[...], preferred_element_type=F32)
              + jnp.dot(xh, rl_ref[...], preferred_element_type=F32))
    lane = lax.broadcasted_iota(jnp.int32, logits.shape, 1).astype(F32)
    logits = jnp.where(lane < N_EXPERTS, logits, -jnp.inf)
    m1 = jnp.max(logits, axis=1, keepdims=True)
    i1 = jnp.min(jnp.where(logits == m1, lane, float(LANES)), axis=1, keepdims=True)
    rest = jnp.where(lane == i1, -jnp.inf, logits)
    m2 = jnp.max(rest, axis=1, keepdims=True)
    i2 = jnp.min(jnp.where(rest == m2, lane, float(LANES)), axis=1, keepdims=True)
    e2 = jnp.exp(m2 - m1)
    g1 = 1.0 / (1.0 + e2)
    g2 = e2 / (1.0 + e2)
    o_ref[...] = jnp.where(lane == i1, g1, 0.0) + jnp.where(lane == i2, g2, 0.0)


def _router(x, router):
    t, d = x.shape
    r = jnp.pad(router.astype(F32), ((0, 0), (0, LANES - N_EXPERTS)))
    rh = r.astype(BF16)
    rl = (r - rh.astype(F32)).astype(BF16)
    tm = _tile(t, 1024)
    return pl.pallas_call(
        _router_kernel, out_shape=jax.ShapeDtypeStruct((t, LANES), F32), grid=(t // tm,),
        in_specs=[pl.BlockSpec((tm, d), lambda i: (i, 0)), pl.BlockSpec((d, LANES), lambda i: (0, 0)),
                  pl.BlockSpec((d, LANES), lambda i: (0, 0))],
        out_specs=pl.BlockSpec((tm, LANES), lambda i: (i, 0)),
        compiler_params=_cparams(("parallel",)), name="moe_router")(x, rh, rl)


def _extended_w_in(w_in):
    w = w_in.astype(F32)
    scale = HEAD_DIM ** -0.5

    def rot_half(cols):
        c4 = cols.reshape(-1, N_HEADS, 2, HEAD_DIM // 2)
        return jnp.stack([-c4[:, :, 1], c4[:, :, 0]], axis=2).reshape(-1, D_GROUP)

    wq = w[:, 0:256]
    wk = w[:, 256:512] * scale
    main = jnp.concatenate([wq, wk, w[:, 512:3072]], axis=1)
    gates = jnp.pad(w[:, 3072:3088], ((0, 0), (0, LANES - 16)))
    ext = jnp.concatenate([main, rot_half(wq), rot_half(wk), gates], axis=1)
    return jnp.pad(ext, ((0, 0), (0, N_EXT - ext.shape[1]))).astype(BF16)


def kernel(x, ln_in_w, ln_in_b, w_in, w_out, ret_gn_w, s5_a_re, s5_a_im, s5_log_dt, s5_b_re, s5_b_im, s5_c_re, s5_c_im, s5_d, s5_w_glu, hy_conv_w, hy_conv_b, hy_w1, hy_b1, hy_w2, hy_b2, hy_w3, hy_freq, hy_bias, ml_conv_w, ml_conv_b, ml_gate_b, ml_gn_w, ln1_w, ln1_b, ln2_w, ln2_b, ffn_w1, ffn_w3, ffn_w2, moe_router, moe_w1, moe_w3, moe_w2):
    bsz, l, d = x.shape
    t = bsz * l
    cos_full, sin_full = _rope_tables(l)
    h = _layer_norm(x.reshape(t, d), ln_in_w, ln_in_b)
    ones_cmb = jnp.ones((t, LANES), F32)
    for layer in range(DEPTH):
        proj = _mm(h, _extended_w_in(w_in[layer]), tm=1024, tn=1280, name="in_proj").reshape(bsz, l, N_EXT)
        y_ret = _retention(proj, ret_gn_w[layer], cos_full, sin_full)
        y_s5 = _s5(proj, s5_a_re[layer], s5_a_im[layer], s5_log_dt[layer], s5_b_re[layer], s5_b_im[layer],
                   s5_c_re[layer], s5_c_im[layer], s5_d[layer], s5_w_glu[layer])
        y_hy = _hyena(proj, hy_conv_w[layer], hy_conv_b[layer], hy_w1[layer], hy_b1[layer], hy_w2[layer],
                      hy_b2[layer], hy_w3[layer], hy_freq[layer], hy_bias[layer])
        qk = _shortconv(proj, CB_MQ, 2, ml_conv_w[layer], ml_conv_b[layer], act=True)
        gates_row = jnp.transpose(proj[:, :, GATE_COL128 * LANES:GATE_COL128 * LANES + 16], (0, 2, 1))
        y_ml = _mlstm(proj, qk, gates_row, ml_gate_b[layer], ml_gn_w[layer])
        y = jnp.concatenate([y_ret, y_s5, y_hy, y_ml], axis=-1).reshape(t, d)
        mix = _mm(y, w_out[layer].astype(BF16), tm=1024, tn=1024, name="out_proj")
        h = _layer_norm(h, ln1_w[layer], ln1_b[layer], res=mix)
        if layer % 2 == 0:
            j = layer // 2
            f = _ffn(h, ones_cmb, ffn_w1[j][None].astype(BF16), ffn_w3[j][None].astype(BF16),
                     ffn_w2[j][None].astype(BF16), routed=False)
        else:
            j = layer // 2
            cmb = _router(h, moe_router[j])
            f = _moe(h, cmb, moe_w1[j].astype(BF16), moe_w3[j].astype(BF16), moe_w2[j].astype(BF16))
        h = _layer_norm(h, ln2_w[layer], ln2_b[layer], res=f)
    return h.reshape(bsz, l, d)
```

```python
import functools
import math

import numpy as np
import jax
import jax.numpy as jnp
from jax import lax
from jax.experimental import pallas as pl
from jax.experimental.pallas import tpu as pltpu

F32 = jnp.float32
BF16 = jnp.bfloat16

D_MODEL = 1024
DEPTH = 2
D_GROUP = 256
HEAD_DIM = 64
N_HEADS = 4
CHUNK = 128
S5_CH = 16
S5_GROUPS = 16
S5_STATE = 64
HY_ORDER = 2
HY_EMB = 33
HY_BANDS = 16
HY_FFN = 64
HY_FAST_DECAY = 0.3
HY_SLOW_DECAY = 1.5
HY_TARGET = 1e-2
N_EXPERTS = 8
ROPE_BASE = 10000.0
EPS = 1e-5
DN_ALPHA = (2 * DEPTH) ** 0.25

LANES = 128
S5_TC = 32
FFT_NB = 256
FFT_QB = 2
N_EXT = 3840

CB_RQ, CB_RK, CB_RV, CB_RG, CB_S5, CB_HV, CB_HX1, CB_HX2 = 0, 1, 2, 3, 4, 5, 6, 7
CB_MQ, CB_MK, CB_MV, CB_MO, CB_RQR, CB_RKR = 8, 9, 10, 11, 12, 13
GATE_COL128 = 28


def _cparams(sem, vmem_mb=None):
    kw = dict(dimension_semantics=sem)
    if vmem_mb is not None:
        kw["vmem_limit_bytes"] = vmem_mb * 1024 * 1024
    return pltpu.CompilerParams(**kw)


def _tile(n, pref):
    return pref if n % pref == 0 else n


def _split_dot(x, m, parts=3):
    acc = None
    r = x
    for _ in range(parts):
        hi = r.astype(BF16)
        t = jnp.dot(hi, m, preferred_element_type=F32)
        acc = t if acc is None else acc + t
        r = r - hi.astype(F32)
    return acc


def _split_dot_left(m, x, parts=3):
    acc = None
    r = x
    for _ in range(parts):
        hi = r.astype(BF16)
        t = jnp.dot(m, hi, preferred_element_type=F32)
        acc = t if acc is None else acc + t
        r = r - hi.astype(F32)
    return acc


def _dot_nt(a, b):
    return lax.dot_general(a, b, (((1,), (1,)), ((), ())), preferred_element_type=F32)


def _dot_tn(a, b):
    return lax.dot_general(a, b, (((0,), (0,)), ((), ())), preferred_element_type=F32)


def _sigmoid(x):
    return 1.0 / (1.0 + jnp.exp(-x))


def _silu(x):
    return x * _sigmoid(x)


def _log_sigmoid(x):
    return jnp.minimum(x, 0.0) - jnp.log(1.0 + jnp.exp(-jnp.abs(x)))


def _head_masks(dtype):
    lane = lax.broadcasted_iota(jnp.int32, (1, D_GROUP), 1)
    return [((lane >= h * HEAD_DIM) & (lane < (h + 1) * HEAD_DIM)).astype(dtype) for h in range(N_HEADS)]


def _ln_core(x, w, b):
    mu = jnp.mean(x, -1, keepdims=True)
    xc = x - mu
    var = jnp.mean(xc * xc, -1, keepdims=True)
    return xc * lax.rsqrt(var + EPS) * w + b


def _ln_kernel(x_ref, w_ref, b_ref, o_ref):
    o_ref[...] = _ln_core(x_ref[...], w_ref[...], b_ref[...])


def _ln_res_kernel(h_ref, m_ref, w_ref, b_ref, o_ref):
    o_ref[...] = _ln_core(DN_ALPHA * h_ref[...] + m_ref[...], w_ref[...], b_ref[...])


def _layer_norm(x, w, b, res=None):
    t, d = x.shape
    tm = _tile(t, 512)
    row = pl.BlockSpec((tm, d), lambda i: (i, 0))
    vec = pl.BlockSpec((1, d), lambda i: (0, 0))
    w2, b2 = w.reshape(1, d), b.reshape(1, d)
    if res is None:
        return pl.pallas_call(_ln_kernel, out_shape=jax.ShapeDtypeStruct((t, d), F32), grid=(t // tm,),
                              in_specs=[row, vec, vec], out_specs=row,
                              compiler_params=_cparams(("parallel",)), name="layer_norm")(x, w2, b2)
    return pl.pallas_call(_ln_res_kernel, out_shape=jax.ShapeDtypeStruct((t, d), F32), grid=(t // tm,),
                          in_specs=[row, row, vec, vec], out_specs=row,
                          compiler_params=_cparams(("parallel",)), name="layer_norm_res")(x, res, w2, b2)


def _mm_kernel(a_ref, b_ref, o_ref):
    o_ref[...] = jnp.dot(a_ref[...].astype(BF16), b_ref[...], preferred_element_type=F32).astype(o_ref.dtype)


def _mm(a, b, tm=1024, tn=1024, out_dtype=F32, name="matmul"):
    m, k = a.shape
    n = b.shape[1]
    tm, tn = _tile(m, tm), _tile(n, tn)
    return pl.pallas_call(
        _mm_kernel, out_shape=jax.ShapeDtypeStruct((m, n), out_dtype), grid=(m // tm, n // tn),
        in_specs=[pl.BlockSpec((tm, k), lambda i, j: (i, 0)), pl.BlockSpec((k, tn), lambda i, j: (0, j))],
        out_specs=pl.BlockSpec((tm, tn), lambda i, j: (i, j)),
        compiler_params=_cparams(("parallel", "arbitrary"), 48), name=name)(a, b)


def _shortconv_kernel(x_ref, xp_ref, xn_ref, w_ref, b_ref, o_ref, *, nt, act):
    i = pl.program_id(1)
    x = x_ref[0]
    tl = x.shape[0]
    row = lax.broadcasted_iota(jnp.int32, x.shape, 0)
    prev_row = jnp.where(i == 0, 0.0, xp_ref[0, 7:8, :])
    next_row = jnp.where(i == nt - 1, 0.0, xn_ref[0, 0:1, :])
    x_prev = jnp.where(row == 0, prev_row, pltpu.roll(x, 1, 0))
    x_next = jnp.where(row == tl - 1, next_row, pltpu.roll(x, tl - 1, 0))
    w = w_ref[0]
    y = b_ref[0, 0:1] + x_prev * w[0:1] + x * w[1:2] + x_next * w[2:3]
    if act:
        y = _silu(y)
    o_ref[0] = y


def _shortconv(proj, col0, nblk, w, b, act):
    bsz, l, _ = proj.shape
    tl = _tile(l, 1024)
    nt = l // tl
    w3 = jnp.transpose(w.reshape(3, nblk, D_GROUP), (1, 0, 2))
    w3 = jnp.pad(w3, ((0, 0), (0, 5), (0, 0)))
    b3 = jnp.broadcast_to(b.reshape(nblk, 1, D_GROUP), (nblk, 8, D_GROUP))
    r8 = tl // 8
    return pl.pallas_call(
        functools.partial(_shortconv_kernel, nt=nt, act=act),
        out_shape=jax.ShapeDtypeStruct((bsz, l, nblk * D_GROUP), F32), grid=(bsz, nt, nblk),
        in_specs=[
            pl.BlockSpec((1, tl, D_GROUP), lambda bb, i, j: (bb, i, col0 + j)),
            pl.BlockSpec((1, 8, D_GROUP), lambda bb, i, j: (bb, jnp.maximum(i * r8 - 1, 0), col0 + j)),
            pl.BlockSpec((1, 8, D_GROUP), lambda bb, i, j: (bb, jnp.minimum((i + 1) * r8, l // 8 - 1), col0 + j)),
            pl.BlockSpec((1, 8, D_GROUP), lambda bb, i, j: (j, 0, 0)),
            pl.BlockSpec((1, 8, D_GROUP), lambda bb, i, j: (j, 0, 0)),
        ],
        out_specs=pl.BlockSpec((1, tl, D_GROUP), lambda bb, i, j: (bb, i, j)),
        compiler_params=_cparams(("parallel", "parallel", "parallel")), name="shortconv")(proj, proj, proj, w3, b3)


def _stack_heads(xb, masks):
    return jnp.concatenate([xb * masks[h] for h in range(N_HEADS)], axis=0)


def _compact(s):
    return s[0:64] + s[64:128] + s[128:192] + s[192:256]


def _expand(c, bd):
    return jnp.concatenate([c, c, c, c], axis=0) * bd


def _head_norm(o, avg, gn):
    mu = _split_dot(o, avg, parts=2)
    oc = o - mu
    var = _split_dot(oc * oc, avg, parts=2)
    return oc * lax.rsqrt(var + EPS) * gn


def _ret_kernel(q_ref, qr_ref, k_ref, kr_ref, v_ref, g_ref, cos_ref, sin_ref,
                dsym_ref, qdf_ref, qdb_ref, kdf_ref, kdb_ref, cdec_ref, bd_ref, avg_ref, gn_ref,
                o_ref, sfw_ref, sbw_ref, save_ref, *, cb, nblk):
    p = pl.program_id(1)
    i = pl.program_id(2)
    masks = _head_masks(BF16)
    bd = bd_ref[...]
    cdec = cdec_ref[...]

    def rope_k(rows):
        return k_ref[0, rows] * cos_ref[rows] + kr_ref[0, rows] * sin_ref[rows]

    def kv_update(s, k, decay, vb):
        kv = _dot_tn((k * decay).astype(BF16), vb)
        return s * cdec + kv * bd

    @pl.when(p == 0)
    def _():
        @pl.when(i == 0)
        def _():
            sbw_ref[...] = jnp.zeros_like(sbw_ref)

        blk = nblk - 1 - i
        for c in reversed(range(cb)):
            rows = slice(c * CHUNK, (c + 1) * CHUNK)
            s = sbw_ref[...]
            save_ref[blk * cb + c] = _compact(s)
            sbw_ref[...] = kv_update(s, rope_k(rows), kdb_ref[...], v_ref[0, rows].astype(BF16))

    @pl.when(p == 1)
    def _():
        @pl.when(i == 0)
        def _():
            sfw_ref[...] = jnp.zeros_like(sfw_ref)

        for c in range(cb):
            rows = slice(c * CHUNK, (c + 1) * CHUNK)
            q = q_ref[0, rows] * cos_ref[rows] + qr_ref[0, rows] * sin_ref[rows]
            k = rope_k(rows)
            qb, kb, vb = q.astype(BF16), k.astype(BF16), v_ref[0, rows].astype(BF16)
            s_all = _dot_nt(qb, _stack_heads(kb, masks))
            pmat = (s_all * dsym_ref[...]).astype(BF16)
            o = jnp.dot(pmat, _stack_heads(vb, masks), preferred_element_type=F32)
            sfw = sfw_ref[...]
            sbw = _expand(save_ref[i * cb + c], bd)
            o = o + jnp.dot(qb, sfw.astype(BF16), preferred_element_type=F32) * qdf_ref[...]
            o = o + jnp.dot(qb, sbw.astype(BF16), preferred_element_type=F32) * qdb_ref[...]
            y = _head_norm(o, avg_ref[...], gn_ref[...])
            o_ref[0, rows] = _silu(g_ref[0, rows]) * y
            sfw_ref[...] = kv_update(sfw, k, kdf_ref[...], vb)


def _ret_tables():
    lg = np.log(1.0 - 2.0 ** (-5.0 - np.arange(N_HEADS, dtype=np.float64)))
    pos = np.arange(CHUNK, dtype=np.float64)
    lag = np.abs(pos[:, None] - pos[None, :])
    dsym = np.concatenate([np.exp(lg[h] * lag) for h in range(N_HEADS)], axis=1)
    lane_lg = np.repeat(lg, HEAD_DIM)[None, :]
    qdf = np.exp(lane_lg * (pos[:, None] + 1.0))
    qdb = np.exp(lane_lg * (CHUNK - pos[:, None]))
    kdf = np.exp(lane_lg * (CHUNK - 1.0 - pos[:, None]))
    kdb = np.exp(lane_lg * pos[:, None])
    cdec = np.exp(lane_lg * CHUNK)
    return [jnp.asarray(t, F32) for t in (dsym, qdf, qdb, kdf, kdb, cdec)]


def _block_diag_mask():
    hid = np.arange(D_GROUP) // HEAD_DIM
    return (hid[:, None] == hid[None, :]).astype(np.float32)


def _rope_tables(l):
    half = HEAD_DIM // 2
    inv = ROPE_BASE ** (-jnp.arange(half, dtype=F32) / half)
    ang = jnp.arange(l, dtype=F32)[:, None] * inv[None, :]
    cos, sin = jnp.cos(ang), jnp.sin(ang)
    cos_full = jnp.tile(jnp.concatenate([cos, cos], -1), (1, N_HEADS))
    sin_full = jnp.tile(jnp.concatenate([sin, sin], -1), (1, N_HEADS))
    return cos_full, sin_full


def _retention(proj, gn_w, cos_full, sin_full):
    bsz, l, _ = proj.shape
    nc = l // CHUNK
    cb = 4 if nc % 4 == 0 else 1
    nblk = nc // cb
    tl = cb * CHUNK
    dsym, qdf, qdb, kdf, kdb, cdec = _ret_tables()
    bd = jnp.asarray(_block_diag_mask())
    avg = jnp.asarray(_block_diag_mask() / HEAD_DIM, BF16)

    def both(col):
        return pl.BlockSpec((1, tl, D_GROUP), lambda b, p, i: (b, i + (1 - p) * (nblk - 1 - 2 * i), col))

    def fwd_only(col):
        return pl.BlockSpec((1, tl, D_GROUP), lambda b, p, i: (b, p * i, col))

    tab = pl.BlockSpec((tl, D_GROUP), lambda b, p, i: (i + (1 - p) * (nblk - 1 - 2 * i), 0))

    def const(shape):
        return pl.BlockSpec(shape, lambda b, p, i: (0,) * len(shape))

    return pl.pallas_call(
        functools.partial(_ret_kernel, cb=cb, nblk=nblk),
        out_shape=jax.ShapeDtypeStruct((bsz, l, D_GROUP), F32), grid=(bsz, 2, nblk),
        in_specs=[fwd_only(CB_RQ), fwd_only(CB_RQR), both(CB_RK), both(CB_RKR), both(CB_RV), fwd_only(CB_RG),
                  tab, tab, const((CHUNK, 4 * CHUNK)), const((CHUNK, D_GROUP)), const((CHUNK, D_GROUP)),
                  const((CHUNK, D_GROUP)), const((CHUNK, D_GROUP)), const((1, D_GROUP)),
                  const((D_GROUP, D_GROUP)), const((D_GROUP, D_GROUP)), const((1, D_GROUP))],
        out_specs=pl.BlockSpec((1, tl, D_GROUP), lambda b, p, i: (b, p * i, 0)),
        scratch_shapes=[pltpu.VMEM((D_GROUP, D_GROUP), F32), pltpu.VMEM((D_GROUP, D_GROUP), F32),
                        pltpu.VMEM((nc, HEAD_DIM, D_GROUP), F32)],
        compiler_params=_cparams(("parallel", "arbitrary", "arbitrary"), 48), name="retention",
    )(proj, proj, proj, proj, proj, proj, cos_full, sin_full, dsym, qdf, qdb, kdf, kdb, cdec, bd, avg,
      gn_w.reshape(1, D_GROUP))


def _mlstm_kernel(q_ref, k_ref, v_ref, og_ref, gc_ref, gr_ref, bc_ref, br_ref, ex_ref, lt_ref, ut_ref,
                  ones_ref, obd_ref, bd_ref, avg_ref, gn_ref,
                  o_ref, cfw_ref, cbw_ref, nmfw_ref, nmbw_ref, csave_ref, nmsave_ref, *, cb, nblk):
    p = pl.program_id(1)
    i = pl.program_id(2)
    masks = _head_masks(BF16)
    bd = bd_ref[...]
    lt = lt_ref[...]
    ut = ut_ref[...]
    ri = lax.broadcasted_iota(jnp.int32, (CHUNK, CHUNK), 0)
    ci = lax.broadcasted_iota(jnp.int32, (CHUNK, CHUNK), 1)
    lane = lax.broadcasted_iota(jnp.int32, (1, D_GROUP), 1)

    def gates_expanded(rows):
        return _split_dot(gc_ref[0, rows] + bc_ref[...], ex_ref[...])

    def state_update(c_ref, nm_ref, total, cum, i_x, k, vb):
        m_prev = nm_ref[1:2]
        g = (total - cum) + i_x
        m_new = jnp.maximum(total + m_prev, jnp.max(g, axis=0, keepdims=True))
        wk = jnp.exp(g - m_new) * k
        decay = jnp.exp(total + m_prev - m_new)
        c_ref[...] = c_ref[...] * decay + _dot_tn(wk.astype(BF16), vb) * bd
        nm_ref[0:1] = decay * nm_ref[0:1] + jnp.sum(wk, axis=0, keepdims=True)
        nm_ref[1:2] = m_new

    @pl.when(p == 0)
    def _():
        @pl.when(i == 0)
        def _():
            cbw_ref[...] = jnp.zeros_like(cbw_ref)
            nmbw_ref[...] = jnp.zeros_like(nmbw_ref)

        blk = nblk - 1 - i
        for c in reversed(range(cb)):
            rows = slice(c * CHUNK, (c + 1) * CHUNK)
            csave_ref[blk * cb + c] = _compact(cbw_ref[...])
            nmsave_ref[blk * cb + c] = nmbw_ref[...]
            gx = gates_expanded(rows)
            cum = _split_dot_left(ut, _log_sigmoid(gx[:, 768:1024]))
            k = k_ref[0, rows] * (HEAD_DIM ** -0.5)
            state_update(cbw_ref, nmbw_ref, cum[0:1], cum, gx[:, 512:768], k, v_ref[0, rows].astype(BF16))

    @pl.when(p == 1)
    def _():
        @pl.when(i == 0)
        def _():
            cfw_ref[...] = jnp.zeros_like(cfw_ref)
            nmfw_ref[...] = jnp.zeros_like(nmfw_ref)

        for c in range(cb):
            rows = slice(c * CHUNK, (c + 1) * CHUNK)
            q = q_ref[0, rows]
            k = k_ref[0, rows] * (HEAD_DIM ** -0.5)
            qb, kb, vb = q.astype(BF16), k.astype(BF16), v_ref[0, rows].astype(BF16)
            s_all = _dot_nt(qb, _stack_heads(kb, masks))
            vaug = jnp.concatenate([_stack_heads(vb, masks), ones_ref[...]], axis=1)
            gx = gates_expanded(rows)
            graw = gr_ref[0, :, rows] + br_ref[...]
            gls = _log_sigmoid(graw)
            cum_r_fw = _split_dot(gls, ut)
            cum_r_bw = _split_dot(gls, lt)
            ccomp = csave_ref[i * cb + c]
            nmb = nmsave_ref[i * cb + c]

            def direction(i_x, f_x, tri, cum_r, i_row0, f_row0, mask, c_state, n_vec, m_prev, total_row):
                cum = _split_dot_left(tri, _log_sigmoid(f_x))
                total = cum[total_row:total_row + 1]
                inter = cum + m_prev
                ps, rmax = [], []
                dms = []
                for h in range(N_HEADS):
                    a_col = cum[:, h * HEAD_DIM:h * HEAD_DIM + 1]
                    dm = a_col - cum_r[f_row0 + h:f_row0 + h + 1] + graw[i_row0 + h:i_row0 + h + 1]
                    dm = jnp.where(mask, dm, -jnp.inf)
                    dms.append(dm)
                    rmax.append(jnp.max(dm, axis=-1, keepdims=True))
                rmax256 = jnp.where(lane < 64, rmax[0], jnp.where(lane < 128, rmax[1],
                                    jnp.where(lane < 192, rmax[2], rmax[3])))
                m_row = jnp.maximum(inter, rmax256)
                for h in range(N_HEADS):
                    m_h = m_row[:, h * HEAD_DIM:h * HEAD_DIM + 1]
                    ps.append(s_all[:, h * CHUNK:(h + 1) * CHUNK] * jnp.exp(dms[h] - m_h))
                pmat = jnp.concatenate(ps, axis=1).astype(BF16)
                nd = jnp.dot(pmat, vaug, preferred_element_type=F32)
                w_inter = jnp.exp(inter - m_row)
                qc = jnp.dot(qb, c_state.astype(BF16), preferred_element_type=F32)
                qn = _split_dot(q * n_vec, obd_ref[...])
                num = nd[:, :D_GROUP] + w_inter * qc
                den = nd[:, D_GROUP:] + w_inter * qn
                hdir = num / jnp.maximum(jnp.abs(den), jnp.exp(-m_row))
                return hdir, total, cum

            h_fw, tot_fw, cum_fw = direction(gx[:, 0:256], gx[:, 256:512], lt, cum_r_fw, 0, 4, ri >= ci,
                                             cfw_ref[...], nmfw_ref[0:1], nmfw_ref[1:2], CHUNK - 1)
            h_bw, _, _ = direction(gx[:, 512:768], gx[:, 768:1024], ut, cum_r_bw, 8, 12, ci >= ri,
                                   _expand(ccomp, bd), nmb[0:1], nmb[1:2], 0)
            y = _head_norm(h_fw + h_bw, avg_ref[...], gn_ref[...])
            o_ref[0, rows] = _sigmoid(og_ref[0, rows]) * y
            state_update(cfw_ref, nmfw_ref, tot_fw, cum_fw, gx[:, 0:256], k, vb)


def _mlstm(proj, qk, gates_row, gate_b, gn_w):
    bsz, l, _ = proj.shape
    nc = l // CHUNK
    cb = 2 if nc % 2 == 0 else 1
    nblk = nc // cb
    tl = cb * CHUNK
    bd_np = _block_diag_mask()
    bd = jnp.asarray(bd_np)
    avg = jnp.asarray(bd_np / HEAD_DIM, BF16)
    obd = jnp.asarray(bd_np, BF16)
    ex = np.zeros((LANES, 4 * D_GROUP), np.float32)
    for j in range(16):
        typ, h = divmod(j, N_HEADS)
        ex[j, typ * D_GROUP + h * HEAD_DIM: typ * D_GROUP + (h + 1) * HEAD_DIM] = 1.0
    idx = np.arange(CHUNK)
    lt = (idx[None, :] <= idx[:, None]).astype(np.float32)
    ones_st = np.repeat(np.repeat(np.eye(N_HEADS, dtype=np.float32), CHUNK, 0), HEAD_DIM, 1)
    gb = gate_b.astype(F32).reshape(16)
    bias_col = jnp.pad(gb, (0, LANES - 16)).reshape(1, LANES)
    bias_row = jnp.broadcast_to(gb.reshape(16, 1), (16, CHUNK))

    def both(arr_col, width=D_GROUP):
        return pl.BlockSpec((1, tl, width), lambda b, p, i: (b, i + (1 - p) * (nblk - 1 - 2 * i), arr_col))

    def fwd_only(arr_col):
        return pl.BlockSpec((1, tl, D_GROUP), lambda b, p, i: (b, p * i, arr_col))

    def const(shape):
        return pl.BlockSpec(shape, lambda b, p, i: (0,) * len(shape))

    return pl.pallas_call(
        functools.partial(_mlstm_kernel, cb=cb, nblk=nblk),
        out_shape=jax.ShapeDtypeStruct((bsz, l, D_GROUP), F32), grid=(bsz, 2, nblk),
        in_specs=[fwd_only(0), both(1), both(CB_MV), fwd_only(CB_MO), both(GATE_COL128, LANES),
                  pl.BlockSpec((1, 16, tl), lambda b, p, i: (b, 0, p * i)),
                  const((1, LANES)), const((16, CHUNK)), const((LANES, 4 * D_GROUP)),
                  const((CHUNK, CHUNK)), const((CHUNK, CHUNK)), const((4 * CHUNK, D_GROUP)),
                  const((D_GROUP, D_GROUP)), const((D_GROUP, D_GROUP)), const((D_GROUP, D_GROUP)),
                  const((1, D_GROUP))],
        out_specs=pl.BlockSpec((1, tl, D_GROUP), lambda b, p, i: (b, p * i, 0)),
        scratch_shapes=[pltpu.VMEM((D_GROUP, D_GROUP), F32), pltpu.VMEM((D_GROUP, D_GROUP), F32),
                        pltpu.VMEM((8, D_GROUP), F32), pltpu.VMEM((8, D_GROUP), F32),
                        pltpu.VMEM((nc, HEAD_DIM, D_GROUP), F32), pltpu.VMEM((nc, 8, D_GROUP), F32)],
        compiler_params=_cparams(("parallel", "arbitrary", "arbitrary"), 48), name="mlstm",
    )(qk, qk, proj, proj, proj, gates_row, bias_col, bias_row, jnp.asarray(ex, BF16), jnp.asarray(lt, BF16),
      jnp.asarray(lt.T, BF16), jnp.asarray(ones_st, BF16), obd, bd, avg, gn_w.reshape(1, D_GROUP))


def _s5_kernel(u_ref, mt_ref, bg_ref, cg_ref, pa_ref, pb_ref, o_ref, *, nsteps):
    ub = u_ref[0].astype(BF16)
    e = jnp.dot(ub, bg_ref[0], preferred_element_type=F32)
    r = e.shape[0]
    row = lax.broadcasted_iota(jnp.int32, (r, LANES), 0)
    xf, xb = e[:, :LANES], e[:, LANES:]
    pa, pb = pa_ref[0], pb_ref[0]
    for s in range(nsteps):
        sh = 1 << s
        a_f, b_f = pa[s:s + 1, :LANES], pb[s:s + 1, :LANES]
        a_b, b_b = pa[s:s + 1, LANES:], pb[s:s + 1, LANES:]
        yf = jnp.where(row >= sh, pltpu.roll(xf, sh, 0), 0.0)
        yb = jnp.where(row < r - sh, pltpu.roll(xb, r - sh, 0), 0.0)
        xf = xf + a_f * yf + b_f * pltpu.roll(yf, LANES // 2, 1)
        xb = xb + a_b * yb + b_b * pltpu.roll(yb, LANES // 2, 1)
    sprev = jnp.where(row >= 1, pltpu.roll(xf, 1, 0), 0.0)
    snext = jnp.where(row < r - 1, pltpu.roll(xb, r - 1, 0), 0.0)
    st = jnp.concatenate([sprev, snext], axis=1).astype(BF16)
    o_ref[0] = (jnp.dot(ub, mt_ref[0], preferred_element_type=F32)
                + jnp.dot(st, cg_ref[0], preferred_element_type=F32))


def _s5_tables(a_re, a_im, log_dt, b_re, b_im, c_re, c_im, d_skip, tc, nsteps):
    g, p, ch = S5_GROUPS, S5_STATE, S5_CH
    hp = lax.Precision.HIGHEST
    are, aim = a_re.astype(F32), a_im.astype(F32)
    delta = jnp.exp(log_dt.astype(F32))[..., None]
    lre, lim = are * delta, aim * delta

    class Cx:
        def __init__(self, re, im):
            self.re, self.im = re, im

        def __mul__(self, o):
            return Cx(self.re * o.re - self.im * o.im, self.re * o.im + self.im * o.re)

        def __getitem__(self, idx):
            return Cx(self.re[idx], self.im[idx])

    def apow(n):
        n = jnp.asarray(n, F32)[None, None, :, None]
        mag, ang = jnp.exp(lre[:, :, None, :] * n), lim[:, :, None, :] * n
        return Cx(mag * jnp.cos(ang), mag * jnp.sin(ang))

    abr, abi = jnp.exp(lre) * jnp.cos(lim), jnp.exp(lre) * jnp.sin(lim)
    den = are * are + aim * aim
    quo = Cx(((abr - 1.0) * are + abi * aim) / den, (abi * are - (abr - 1.0) * aim) / den)
    b_bar = quo[..., None] * Cx(b_re.astype(F32)[None], b_im.astype(F32)[None])
    c = Cx(c_re.astype(F32), c_im.astype(F32))
    taus = np.arange(tc)
    cp = c[:, :, None] * apow(taus)[:, :, :, None, :]
    kk = (jnp.einsum("dgtop,dgpi->dgtoi", cp.re, b_bar.re, precision=hp)
          - jnp.einsum("dgtop,dgpi->dgtoi", cp.im, b_bar.im, precision=hp))
    diff = taus[None, :] - taus[:, None]
    k0 = kk[0][:, np.clip(diff, 0, None)]
    k1 = kk[1][:, np.clip(-diff, 0, None)]
    dsk = d_skip.astype(F32).reshape(g, ch)[:, :, None] * jnp.eye(ch, dtype=F32)[None]
    kdiag = kk[0][:, 0] + kk[1][:, 0] + dsk
    dm = diff[None, :, :, None, None]
    kfull = jnp.where(dm > 0, k0, jnp.where(dm < 0, k1, kdiag[:, None, None]))
    mt = jnp.transpose(kfull, (0, 1, 4, 2, 3)).reshape(g, tc * ch, tc * ch)

    zf = apow(tc - 1 - taus)[0][..., None] * b_bar[0][:, None]
    zb = apow(taus)[1][..., None] * b_bar[1][:, None]

    def to_rows(z):
        return jnp.transpose(z, (0, 1, 3, 2)).reshape(g, tc * ch, p)

    bg = jnp.concatenate([to_rows(zf.re), to_rows(zf.im), to_rows(zb.re), to_rows(zb.im)], axis=-1)

    yf = c[0][:, None] * apow(taus + 1)[0][:, :, None, :]
    yb = c[1][:, None] * apow(tc - taus)[1][:, :, None, :]

    def to_cols(z):
        return jnp.transpose(z, (0, 3, 1, 2)).reshape(g, p, tc * ch)

    cg = jnp.concatenate([to_cols(yf.re), -to_cols(yf.im), to_cols(yb.re), -to_cols(yb.im)], axis=1)

    steps = tc * (2.0 ** np.arange(nsteps))
    pw = apow(steps)
    re0, im0, re1, im1 = pw.re[0], pw.im[0], pw.re[1], pw.im[1]
    pa = jnp.concatenate([re0, re0, re1, re1], axis=-1)
    pb = jnp.concatenate([-im0, im0, -im1, im1], axis=-1)
    pad = (-nsteps) % 8
    pa = jnp.pad(pa, ((0, 0), (0, pad), (0, 0)))
    pb = jnp.pad(pb, ((0, 0), (0, pad), (0, 0)))
    return mt.astype(BF16), bg.astype(BF16), cg.astype(BF16), pa, pb


def _s5_glu_kernel(y_ref, w_ref, o_ref):
    y = y_ref[...]
    z = 0.5 * y * (1.0 + jnp.tanh(math.sqrt(2.0 / math.pi) * (y + 0.044715 * (y * y * y))))
    o_ref[...] = z * _sigmoid(jnp.dot(z.astype(BF16), w_ref[...], preferred_element_type=F32))


def _s5(proj, a_re, a_im, log_dt, b_re, b_im, c_re, c_im, d_skip, w_glu):
    bsz, l, _ = proj.shape
    tc = S5_TC
    r = l // tc
    nsteps = max(1, int(math.ceil(math.log2(r))))
    w = tc * S5_CH
    mt, bg, cg, pa, pb = _s5_tables(a_re, a_im, log_dt, b_re, b_im, c_re, c_im, d_skip, tc, nsteps)
    u = proj[:, :, CB_S5 * D_GROUP:(CB_S5 + 1) * D_GROUP].astype(BF16)
    ug = jnp.transpose(u.reshape(bsz, r, tc, S5_GROUPS, S5_CH), (3, 0, 1, 2, 4)).reshape(S5_GROUPS, bsz * r, w)
    ns8 = pa.shape[1]
    yg = pl.pallas_call(
        functools.partial(_s5_kernel, nsteps=nsteps),
        out_shape=jax.ShapeDtypeStruct((S5_GROUPS, bsz * r, w), F32), grid=(S5_GROUPS, bsz),
        in_specs=[pl.BlockSpec((1, r, w), lambda g, b: (g, b, 0)),
                  pl.BlockSpec((1, w, w), lambda g, b: (g, 0, 0)),
                  pl.BlockSpec((1, w, D_GROUP), lambda g, b: (g, 0, 0)),
                  pl.BlockSpec((1, D_GROUP, w), lambda g, b: (g, 0, 0)),
                  pl.BlockSpec((1, ns8, D_GROUP), lambda g, b: (g, 0, 0)),
                  pl.BlockSpec((1, ns8, D_GROUP), lambda g, b: (g, 0, 0))],
        out_specs=pl.BlockSpec((1, r, w), lambda g, b: (g, b, 0)),
        compiler_params=_cparams(("parallel", "parallel"), 48), name="s5_ssm")(ug, mt, bg, cg, pa, pb)
    y = jnp.transpose(yg.reshape(S5_GROUPS, bsz, r, tc, S5_CH), (1, 2, 3, 0, 4)).reshape(bsz * l, D_GROUP)
    t = bsz * l
    tm = _tile(t, 2048)
    out = pl.pallas_call(
        _s5_glu_kernel, out_shape=jax.ShapeDtypeStruct((t, D_GROUP), F32), grid=(t // tm,),
        in_specs=[pl.BlockSpec((tm, D_GROUP), lambda i: (i, 0)), pl.BlockSpec((D_GROUP, D_GROUP), lambda i: (0, 0))],
        out_specs=pl.BlockSpec((tm, D_GROUP), lambda i: (i, 0)),
        compiler_params=_cparams(("parallel",)), name="s5_glu")(y, w_glu.astype(BF16))
    return out.reshape(bsz, l, D_GROUP)


def _hy_filter_kernel(z_ref, w1_ref, b1_ref, w2_ref, b2_ref, w3_ref, fr_ref, dec_ref, h_ref, ss_ref):
    i = pl.program_id(0)
    hp = lax.Precision.HIGHEST
    fr = fr_ref[...]
    a = jnp.sin(fr * (jnp.dot(z_ref[...], w1_ref[...], precision=hp, preferred_element_type=F32) + b1_ref[...]))
    a = jnp.sin(fr * (jnp.dot(a, w2_ref[...], precision=hp, preferred_element_type=F32) + b2_ref[...]))
    h = jnp.dot(a, w3_ref[...], precision=hp, preferred_element_type=F32)
    dec = dec_ref[...]
    h = h * jnp.concatenate([dec, dec, dec, dec], axis=1)
    h_ref[...] = h

    @pl.when(i == 0)
    def _():
        ss_ref[...] = jnp.zeros_like(ss_ref)

    ss_ref[...] += jnp.sum(h * h, axis=0, keepdims=True)


def _hy_filters(l, w1, b1, w2, b2, w3, freq):
    t = jnp.linspace(0.0, 1.0, l, dtype=F32)[:, None]
    w = 2.0 * math.pi * jnp.arange(l, dtype=F32)[:, None] / l
    bands = jnp.linspace(1e-4, HY_BANDS - 1, HY_BANDS, dtype=F32)[None, :]
    z = jnp.concatenate([t, jnp.cos(bands * w), -jnp.sin(bands * w)], axis=-1)
    z = jnp.pad(z, ((0, 0), (0, LANES - HY_EMB)))
    max_decay = math.log(HY_TARGET) / HY_FAST_DECAY
    min_decay = math.log(HY_TARGET) / HY_SLOW_DECAY
    rates = jnp.abs(jnp.linspace(min_decay, max_decay, D_GROUP, dtype=F32))
    dec = jnp.exp(-t * rates)
    pf = LANES - HY_FFN
    w1p = jnp.pad(w1.astype(F32), ((0, LANES - HY_EMB), (0, pf)))
    w2p = jnp.pad(w2.astype(F32), ((0, pf), (0, pf)))
    w3p = jnp.pad(w3.astype(F32), ((0, pf), (0, 0)))
    b1p = jnp.pad(b1.astype(F32), (0, pf)).reshape(1, LANES)
    b2p = jnp.pad(b2.astype(F32), (0, pf)).reshape(1, LANES)
    frp = jnp.pad(freq.astype(F32), (0, pf)).reshape(1, LANES)
    nout = HY_ORDER * 2 * D_GROUP
    tl = _tile(l, 512)

    def const(shape):
        return pl.BlockSpec(shape, lambda i: (0, 0))

    return pl.pallas_call(
        _hy_filter_kernel,
        out_shape=(jax.ShapeDtypeStruct((l, nout), F32), jax.ShapeDtypeStruct((1, nout), F32)), grid=(l // tl,),
        in_specs=[pl.BlockSpec((tl, LANES), lambda i: (i, 0)), const((LANES, LANES)), const((1, LANES)),
                  const((LANES, LANES)), const((1, LANES)), const((LANES, nout)), const((1, LANES)),
                  pl.BlockSpec((tl, D_GROUP), lambda i: (i, 0))],
        out_specs=(pl.BlockSpec((tl, nout), lambda i: (i, 0)), const((1, nout))),
        compiler_params=_cparams(("arbitrary",)), name="hyena_filter_mlp")(z, w1p, b1p, w2p, b2p, w3p, frp, dec)


def _dft_consts(na):
    nb = FFT_NB
    n = na * nb
    ia = np.arange(na, dtype=np.float64)
    th = 2.0 * np.pi * np.outer(ia, ia) / na
    c1, s1 = np.cos(th), np.sin(th)
    eye8 = np.eye(8)
    fa_full = np.concatenate([c1, -s1], axis=0)
    g_full = np.kron(fa_full, eye8)
    g_half = np.kron(fa_full[:, : na // 2], eye8)
    g_out = np.kron(np.concatenate([c1[: na // 2], -s1[: na // 2]], axis=1) / n, eye8)
    ib = np.arange(nb, dtype=np.float64)
    ph = 2.0 * np.pi * np.outer(ib, ib) / nb
    c2, s2 = np.cos(ph), np.sin(ph)
    fb = np.block([[c2, s2], [-s2, c2]])
    fbc = np.block([[c2, -s2], [s2, c2]])
    ps = 2.0 * np.pi * np.outer(ia, ib) / n
    twr = np.broadcast_to(np.cos(ps)[:, :, None], (na, nb, LANES))
    twi = np.broadcast_to(-np.sin(ps)[:, :, None], (na, nb, LANES))
    as_bf = lambda x: jnp.asarray(x, BF16)
    return dict(g_full=as_bf(g_full), g_half=as_bf(g_half), g_out=as_bf(g_out), fb=as_bf(fb), fbc=as_bf(fbc),
                twr=jnp.asarray(twr, F32), twi=jnp.asarray(twi, F32))


def _lane_tile(x, reps):
    return x if reps == 1 else jnp.concatenate([x] * reps, axis=-1)


def _hy_spec_kernel(a_ref, twr_ref, twi_ref, fb_ref, ss_ref, o_ref, *, kb, reps):
    scale = lax.rsqrt(ss_ref[...])
    for j in range(kb):
        ar, ai = a_ref[0, j], a_ref[1, j]
        twr, twi = _lane_tile(twr_ref[j], reps), _lane_tile(twi_ref[j], reps)
        br = twr * ar - twi * ai
        bi = twr * ai + twi * ar
        x = jnp.dot(fb_ref[...], jnp.concatenate([br, bi], axis=0).astype(BF16), preferred_element_type=F32)
        o_ref[0, j] = x[:FFT_NB] * scale
        o_ref[1, j] = x[FFT_NB:] * scale


def _hy_mid_kernel(a_ref, h_ref, twr_ref, twi_ref, fb_ref, fbc_ref, o_ref, *, kb, reps):
    for j in range(kb):
        ar, ai = a_ref[0, 0, j], a_ref[0, 1, j]
        twr, twi = _lane_tile(twr_ref[j], reps), _lane_tile(twi_ref[j], reps)
        br = twr * ar - twi * ai
        bi = twr * ai + twi * ar
        x = jnp.dot(fb_ref[...], jnp.concatenate([br, bi], axis=0).astype(BF16), preferred_element_type=F32)
        xr, xi = x[:FFT_NB], x[FFT_NB:]
        hr, hi = h_ref[0, j], h_ref[1, j]
        yr = xr * hr - xi * hi
        yi = xr * hi + xi * hr
        z = jnp.dot(fbc_ref[...], jnp.concatenate([yr, yi], axis=0).astype(BF16), preferred_element_type=F32)
        zr, zi = z[:FFT_NB], z[FFT_NB:]
        o_ref[0, 0, j] = twr * zr + twi * zi
        o_ref[0, 1, j] = twr * zi - twi * zr


def _hy_dft1_kernel(g_ref, x_ref, o_ref, *, qb):
    na_in, c = x_ref.shape[1], x_ref.shape[4]
    na = o_ref.shape[2]
    for q in range(qb):
        x = x_ref[0, :, q].reshape(na_in * 8, c).astype(BF16)
        a = jnp.dot(g_ref[...], x, preferred_element_type=F32)
        o_ref[0, :, :, q] = a.reshape(2, na, 8, c)


def _hy_dft1(g, x5, ncol, name):
    bsz, na_in, nq = x5.shape[:3]
    na = g.shape[0] // 16
    c = D_GROUP
    qb = FFT_QB
    return pl.pallas_call(
        functools.partial(_hy_dft1_kernel, qb=qb),
        out_shape=jax.ShapeDtypeStruct((bsz, 2, na, nq, 8, ncol * c), F32), grid=(bsz, ncol, nq // qb),
        in_specs=[pl.BlockSpec(g.shape, lambda b, j, q: (0, 0)),
                  pl.BlockSpec((1, na_in, qb, 8, c), lambda b, j, q: (b, 0, q, 0, j))],
        out_specs=pl.BlockSpec((1, 2, na, qb, 8, c), lambda b, j, q: (b, 0, 0, q, 0, j)),
        compiler_params=_cparams(("parallel", "parallel", "parallel"), 48), name=name)(g, x5)


def _hy_out_kernel(g_ref, z_ref, x_ref, v_ref, b_ref, o_ref, *, qb):
    na2, c = z_ref.shape[1] * z_ref.shape[2], z_ref.shape[5]
    nah = o_ref.shape[1]
    bias = b_ref[...].reshape(1, 1, c)
    for q in range(qb):
        z = z_ref[0, :, :, q].reshape(na2 * 8, c).astype(BF16)
        y = jnp.dot(g_ref[...], z, preferred_element_type=F32).reshape(nah, 8, c)
        o_ref[0, :, q] = x_ref[0, :, q] * (y + v_ref[0, :, q] * bias)


def _hyena(proj, conv_w, conv_b, w1, b1, w2, b2, w3, freq, bias):
    bsz, l, _ = proj.shape
    nb = FFT_NB
    na = 2 * l // nb
    nah = na // 2
    nq = nb // 8
    c = D_GROUP
    qb = FFT_QB
    dc = _dft_consts(na)
    pc = _shortconv(proj, CB_HV, 3, conv_w, conv_b, act=False)

    h, ss = _hy_filters(l, w1, b1, w2, b2, w3, freq)
    h4 = h.reshape(l, HY_ORDER, 2, c)
    ss4 = ss.reshape(HY_ORDER, 2, c)
    ssn = (ss4[:, 0] + ss4[:, 1]).reshape(1, HY_ORDER * c)
    hf, hb = h4[:, :, 0], h4[:, :, 1]
    ncf = HY_ORDER * c
    kern = jnp.concatenate([hf, jnp.zeros_like(hf[:1]), hb[:0:-1]], axis=0).reshape(1, na, nq, 8, ncf)
    ka = _hy_dft1(dc["g_full"], kern, HY_ORDER, "hyena_filter_dft1").reshape(2, na, nb, ncf)
    kb = 4 if na % 4 == 0 else 1
    reps = c // LANES
    tw = pl.BlockSpec((kb, nb, LANES), lambda j, k: (k, 0, 0))
    mat = pl.BlockSpec((2 * nb, 2 * nb), lambda j, k: (0, 0))
    hspec = pl.pallas_call(
        functools.partial(_hy_spec_kernel, kb=kb, reps=reps),
        out_shape=jax.ShapeDtypeStruct((2, na, nb, ncf), F32), grid=(ncf // c, na // kb),
        in_specs=[pl.BlockSpec((2, kb, nb, c), lambda j, k: (0, k, 0, j)), tw, tw, mat,
                  pl.BlockSpec((1, c), lambda j, k: (0, j))],
        out_specs=pl.BlockSpec((2, kb, nb, c), lambda j, k: (0, k, 0, j)),
        compiler_params=_cparams(("parallel", "parallel"), 48), name="hyena_filter_dft2",
    )(ka, dc["twr"], dc["twi"], dc["fb"], ssn)

    pc5 = pc.reshape(bsz, nah, nq, 8, 3 * c)

    def long_conv_gate(z5, order, xcol):
        a = _hy_dft1(dc["g_half"], z5, 1, "hyena_dft1").reshape(bsz, 2, na, nb, c)
        zmid = pl.pallas_call(
            functools.partial(_hy_mid_kernel, kb=kb, reps=reps),
            out_shape=jax.ShapeDtypeStruct((bsz, 2, na, nb, c), F32), grid=(bsz, na // kb),
            in_specs=[pl.BlockSpec((1, 2, kb, nb, c), lambda b, k: (b, 0, k, 0, 0)),
                      pl.BlockSpec((2, kb, nb, c), lambda b, k: (0, k, 0, order)), tw, tw, mat, mat],
            out_specs=pl.BlockSpec((1, 2, kb, nb, c), lambda b, k: (b, 0, k, 0, 0)),
            compiler_params=_cparams(("parallel", "parallel"), 48), name="hyena_dft_mid",
        )(a, hspec, dc["twr"], dc["twi"], dc["fb"], dc["fbc"])
        zmid = zmid.reshape(bsz, 2, na, nq, 8, c)
        sig = lambda col: pl.BlockSpec((1, nah, qb, 8, c), lambda b, q: (b, 0, q, 0, col))
        return pl.pallas_call(
            functools.partial(_hy_out_kernel, qb=qb),
            out_shape=jax.ShapeDtypeStruct((bsz, nah, nq, 8, c), F32), grid=(bsz, nq // qb),
            in_specs=[pl.BlockSpec(dc["g_out"].shape, lambda b, q: (0, 0)),
                      pl.BlockSpec((1, 2, na, qb, 8, c), lambda b, q: (b, 0, 0, q, 0, 0)),
                      sig(xcol), sig(0), pl.BlockSpec((1, c), lambda b, q: (0, 0))],
            out_specs=sig(0),
            compiler_params=_cparams(("parallel", "parallel"), 48), name="hyena_idft_gate",
        )(dc["g_out"], zmid, pc5, z5, bias[order].astype(F32).reshape(1, c))

    z1 = long_conv_gate(pc5, 0, 1)
    z2 = long_conv_gate(z1, 1, 2)
    return z2.reshape(bsz, l, c)


def _ffn_kernel(x_ref, cmb_ref, w1_ref, w3_ref, w2_ref, o_ref, xb_ref, acc_ref, *, ne, nf, routed):
    e = pl.program_id(1)
    f = pl.program_id(2)

    @pl.when((e == 0) & (f == 0))
    def _():
        xb_ref[...] = x_ref[...].astype(BF16)
        acc_ref[...] = jnp.zeros_like(acc_ref)

    xb = xb_ref[...]
    a = jnp.dot(xb, w1_ref[0], preferred_element_type=F32)
    b = jnp.dot(xb, w3_ref[0], preferred_element_type=F32)
    hid = _silu(a) * b
    if routed:
        lane = lax.broadcasted_iota(jnp.int32, cmb_ref.shape, 1)
        hid = hid * jnp.sum(jnp.where(lane == e, cmb_ref[...], 0.0), axis=1, keepdims=True)
    acc_ref[...] += jnp.dot(hid.astype(BF16), w2_ref[0], preferred_element_type=F32)

    @pl.when((e == ne - 1) & (f == nf - 1))
    def _():
        o_ref[...] = acc_ref[...]


def _ffn(x, cmb, w1, w3, w2, routed):
    t, d = x.shape
    ne, _, ff = w1.shape
    tm = _tile(t, 1024)
    tf = 512 if ff % 512 == 0 else (256 if ff % 256 == 0 else ff)
    nf = ff // tf
    return pl.pallas_call(
        functools.partial(_ffn_kernel, ne=ne, nf=nf, routed=routed),
        out_shape=jax.ShapeDtypeStruct((t, d), F32), grid=(t // tm, ne, nf),
        in_specs=[pl.BlockSpec((tm, d), lambda i, e, f: (i, 0)),
                  pl.BlockSpec((tm, LANES), lambda i, e, f: (i, 0)),
                  pl.BlockSpec((1, d, tf), lambda i, e, f: (e, 0, f)),
                  pl.BlockSpec((1, d, tf), lambda i, e, f: (e, 0, f)),
                  pl.BlockSpec((1, tf, d), lambda i, e, f: (e, f, 0))],
        out_specs=pl.BlockSpec((tm, d), lambda i, e, f: (i, 0)),
        scratch_shapes=[pltpu.VMEM((tm, d), BF16), pltpu.VMEM((tm, d), F32)],
        compiler_params=_cparams(("parallel", "arbitrary", "arbitrary"), 52), name="swiglu_ffn",
    )(x, cmb, w1, w3, w2)


MOE_TB = 1024
MOE_SUB = 288
MOE_SLOT = 384
MOE_CUM = 256


def _moe_kernel(cnt_ref, x_ref, cmb_ref, cmbt_ref, lt_ref, ut_ref, w1_ref, w3_ref, w2_ref, o_ref,
                xb_ref, xs_ref, ys_ref, gs_ref, posc_ref, posr_ref, *, nf):
    i = pl.program_id(0)
    e = pl.program_id(1)
    f = pl.program_id(2)
    tb = x_ref.shape[0]
    count = cnt_ref[i * N_EXPERTS + e]
    npass = (count + (MOE_SUB - 1)) // MOE_SUB

    @pl.when((e == 0) & (f == 0))
    def _():
        xb_ref[...] = x_ref[...].astype(BF16)
        o_ref[...] = jnp.zeros_like(o_ref)
        ys_ref[...] = jnp.zeros_like(ys_ref)
        carry_c = jnp.zeros((1, LANES), F32)
        carry_r = jnp.zeros((N_EXPERTS, 1), F32)
        for c in range(tb // MOE_CUM):
            rows = slice(c * MOE_CUM, (c + 1) * MOE_CUM)
            mc = (cmb_ref[rows] > 0.0).astype(F32)
            inc = jnp.dot(lt_ref[...], mc.astype(BF16), preferred_element_type=F32) + carry_c
            posc_ref[rows] = jnp.where(mc > 0.0, inc - 1.0, -1.0)
            carry_c = inc[MOE_CUM - 1:MOE_CUM]
            mr = (cmbt_ref[:, rows] > 0.0).astype(F32)
            incr = jnp.dot(mr.astype(BF16), ut_ref[...], preferred_element_type=F32) + carry_r
            posr_ref[:, rows] = jnp.where(mr > 0.0, incr - 1.0, -1.0)
            carry_r = incr[:, MOE_CUM - 1:MOE_CUM]

    @pl.when(f == 0)
    def _():
        pos_row = posr_ref[pl.ds(e, 1), :]
        gate_row = cmbt_ref[pl.ds(e, 1), :]
        slot = lax.broadcasted_iota(jnp.int32, (MOE_SUB, tb), 0).astype(F32)

        def gather(j, carry):
            base = pl.multiple_of(j * MOE_SLOT, LANES)
            hit = pos_row == slot + (j * MOE_SUB).astype(F32)
            xs_ref[pl.ds(base, MOE_SUB), :] = jnp.dot(
                jnp.where(hit, 1.0, 0.0).astype(BF16), xb_ref[...], preferred_element_type=F32).astype(BF16)
            g = jnp.sum(jnp.where(hit, gate_row, 0.0), axis=1, keepdims=True)
            gs_ref[pl.ds(base, MOE_SUB), :] = jnp.broadcast_to(g, (MOE_SUB, LANES))
            ys_ref[pl.ds(base, MOE_SUB), :] = jnp.zeros((MOE_SUB, ys_ref.shape[1]), F32)
            return carry

        lax.fori_loop(0, npass, gather, 0)

    def expert(j, carry):
        base = pl.multiple_of(j * MOE_SLOT, LANES)
        xs = xs_ref[pl.ds(base, MOE_SUB), :]
        a = jnp.dot(xs, w1_ref[0], preferred_element_type=F32)
        b = jnp.dot(xs, w3_ref[0], preferred_element_type=F32)
        hid = _silu(a) * b * gs_ref[pl.ds(base, MOE_SUB), 0:1]
        ys_ref[pl.ds(base, MOE_SUB), :] += jnp.dot(hid.astype(BF16), w2_ref[0], preferred_element_type=F32)
        return carry

    lax.fori_loop(0, npass, expert, 0)

    @pl.when(f == nf - 1)
    def _():
        lane = lax.broadcasted_iota(jnp.int32, (tb, LANES), 1)
        pos_col = jnp.sum(jnp.where(lane == e, posc_ref[...], 0.0), axis=1, keepdims=True)
        slot = lax.broadcasted_iota(jnp.int32, (tb, MOE_SLOT), 1)
        slot = jnp.where(slot < MOE_SUB, slot, -2 * tb).astype(F32)

        def scatter(j, carry):
            base = pl.multiple_of(j * MOE_SLOT, LANES)
            hit = pos_col == slot + (j * MOE_SUB).astype(F32)
            o_ref[...] += jnp.dot(jnp.where(hit, 1.0, 0.0).astype(BF16),
                                  ys_ref[pl.ds(base, MOE_SLOT), :].astype(BF16), preferred_element_type=F32)
            return carry

        lax.fori_loop(0, npass, scatter, 0)


def _moe(x, cmb, w1, w3, w2):
    t, d = x.shape
    ne, _, ff = w1.shape
    tb = _tile(t, MOE_TB)
    nb = t // tb
    tf = 896 if ff % 896 == 0 else ff
    nf = ff // tf
    max_pass = -(-tb // MOE_SUB)
    cmbt = jnp.transpose(cmb[:, :N_EXPERTS])
    counts = jnp.sum((cmb[:, :N_EXPERTS] > 0.0).reshape(nb, tb, N_EXPERTS), axis=1).astype(jnp.int32).reshape(-1)
    idx = np.arange(MOE_CUM)
    lt = jnp.asarray(idx[None, :] <= idx[:, None], BF16)
    grid_spec = pltpu.PrefetchScalarGridSpec(
        num_scalar_prefetch=1, grid=(nb, ne, nf),
        in_specs=[pl.BlockSpec((tb, d), lambda i, e, f, c: (i, 0)),
                  pl.BlockSpec((tb, LANES), lambda i, e, f, c: (i, 0)),
                  pl.BlockSpec((N_EXPERTS, tb), lambda i, e, f, c: (0, i)),
                  pl.BlockSpec((MOE_CUM, MOE_CUM), lambda i, e, f, c: (0, 0)),
                  pl.BlockSpec((MOE_CUM, MOE_CUM), lambda i, e, f, c: (0, 0)),
                  pl.BlockSpec((1, d, tf), lambda i, e, f, c: (e, 0, f)),
                  pl.BlockSpec((1, d, tf), lambda i, e, f, c: (e, 0, f)),
                  pl.BlockSpec((1, tf, d), lambda i, e, f, c: (e, f, 0))],
        out_specs=pl.BlockSpec((tb, d), lambda i, e, f, c: (i, 0)),
        scratch_shapes=[pltpu.VMEM((tb, d), BF16), pltpu.VMEM((max_pass * MOE_SLOT, d), BF16),
                        pltpu.VMEM((max_pass * MOE_SLOT, d), F32), pltpu.VMEM((max_pass * MOE_SLOT, LANES), F32),
                        pltpu.VMEM((tb, LANES), F32), pltpu.VMEM((N_EXPERTS, tb), F32)])
    return pl.pallas_call(
        functools.partial(_moe_kernel, nf=nf), out_shape=jax.ShapeDtypeStruct((t, d), F32), grid_spec=grid_spec,
        compiler_params=_cparams(("parallel", "arbitrary", "arbitrary"), 56), name="moe_routed",
    )(counts, x, cmb, cmbt, lt, jnp.transpose(lt), w1, w3, w2)


def _router_kernel(x_ref, rh_ref, rl_ref, o_ref):
    x = x_ref[...]
    xh = x.astype(BF16)
    xl = (x - xh.astype(F32)).astype(BF16)
    logits = (jnp.dot(xh, rh_ref[...], preferred_element_type=F32)
              + jnp.dot(xl, rh_ref[...], preferred_element_type=F32)
              + jnp.dot(xh, rl_ref[...], preferred_element_type=F32))
    lane = lax.broadcasted_iota(jnp.int32, logits.shape, 1).astype(F32)
    logits = jnp.where(lane < N_EXPERTS, logits, -jnp.inf)
    m1 = jnp.max(logits, axis=1, keepdims=True)
    i1 = jnp.min(jnp.where(logits == m1, lane, float(LANES)), axis=1, keepdims=True)
    rest = jnp.where(lane == i1, -jnp.inf, logits)
    m2 = jnp.max(rest, axis=1, keepdims=True)
    i2 = jnp.min(jnp.where(rest == m2, lane, float(LANES)), axis=1, keepdims=True)
    e2 = jnp.exp(m2 - m1)
    g1 = 1.0 / (1.0 + e2)
    g2 = e2 / (1.0 + e2)
    o_ref[...] = jnp.where(lane == i1, g1, 0.0) + jnp.where(lane == i2, g2, 0.0)


def _router(x, router):
    t, d = x.shape
    r = jnp.pad(router.astype(F32), ((0, 0), (0, LANES - N_EXPERTS)))
    rh = r.astype(BF16)
    rl = (r - rh.astype(F32)).astype(BF16)
    tm = _tile(t, 1024)
    return pl.pallas_call(
        _router_kernel, out_shape=jax.ShapeDtypeStruct((t, LANES), F32), grid=(t // tm,),
        in_specs=[pl.BlockSpec((tm, d), lambda i: (i, 0)), pl.BlockSpec((d, LANES), lambda i: (0, 0)),
                  pl.BlockSpec((d, LANES), lambda i: (0, 0))],
        out_specs=pl.BlockSpec((tm, LANES), lambda i: (i, 0)),
        compiler_params=_cparams(("parallel",)), name="moe_router")(x, rh, rl)


def _extended_w_in(w_in):
    w = w_in.astype(F32)
    scale = HEAD_DIM ** -0.5

    def rot_half(cols):
        c4 = cols.reshape(-1, N_HEADS, 2, HEAD_DIM // 2)
        return jnp.stack([-c4[:, :, 1], c4[:, :, 0]], axis=2).reshape(-1, D_GROUP)

    wq = w[:, 0:256]
    wk = w[:, 256:512] * scale
    main = jnp.concatenate([wq, wk, w[:, 512:3072]], axis=1)
    gates = jnp.pad(w[:, 3072:3088], ((0, 0), (0, LANES - 16)))
    ext = jnp.concatenate([main, rot_half(wq), rot_half(wk), gates], axis=1)
    return jnp.pad(ext, ((0, 0), (0, N_EXT - ext.shape[1]))).astype(BF16)


def kernel(x, ln_in_w, ln_in_b, w_in, w_out, ret_gn_w, s5_a_re, s5_a_im, s5_log_dt, s5_b_re, s5_b_im, s5_c_re, s5_c_im, s5_d, s5_w_glu, hy_conv_w, hy_conv_b, hy_w1, hy_b1, hy_w2, hy_b2, hy_w3, hy_freq, hy_bias, ml_conv_w, ml_conv_b, ml_gate_b, ml_gn_w, ln1_w, ln1_b, ln2_w, ln2_b, ffn_w1, ffn_w3, ffn_w2, moe_router, moe_w1, moe_w3, moe_w2):
    bsz, l, d = x.shape
    t = bsz * l
    cos_full, sin_full = _rope_tables(l)
    h = _layer_norm(x.reshape(t, d), ln_in_w, ln_in_b)
    ones_cmb = jnp.ones((t, LANES), F32)
    for layer in range(DEPTH):
        proj = _mm(h, _extended_w_in(w_in[layer]), tm=1024, tn=1280, name="in_proj").reshape(bsz, l, N_EXT)
        y_ret = _retention(proj, ret_gn_w[layer], cos_full, sin_full)
        y_s5 = _s5(proj, s5_a_re[layer], s5_a_im[layer], s5_log_dt[layer], s5_b_re[layer], s5_b_im[layer],
                   s5_c_re[layer], s5_c_im[layer], s5_d[layer], s5_w_glu[layer])
        y_hy = _hyena(proj, hy_conv_w[layer], hy_conv_b[layer], hy_w1[layer], hy_b1[layer], hy_w2[layer],
                      hy_b2[layer], hy_w3[layer], hy_freq[layer], hy_bias[layer])
        qk = _shortconv(proj, CB_MQ, 2, ml_conv_w[layer], ml_conv_b[layer], act=True)
        gates_row = jnp.transpose(proj[:, :, GATE_COL128 * LANES:GATE_COL128 * LANES + 16], (0, 2, 1))
        y_ml = _mlstm(proj, qk, gates_row, ml_gate_b[layer], ml_gn_w[layer])
        y = jnp.concatenate([y_ret, y_s5, y_hy, y_ml], axis=-1).reshape(t, d)
        mix = _mm(y, w_out[layer].astype(BF16), tm=1024, tn=1024, name="out_proj")
        h = _layer_norm(h, ln1_w[layer], ln1_b[layer], res=mix)
        if layer % 2 == 0:
            j = layer // 2
            f = _ffn(h, ones_cmb, ffn_w1[j][None].astype(BF16), ffn_w3[j][None].astype(BF16),
                     ffn_w2[j][None].astype(BF16), routed=False)
        else:
            j = layer // 2
            cmb = _router(h, moe_router[j])
            f = _moe(h, cmb, moe_w1[j].astype(BF16), moe_w3[j].astype(BF16), moe_w2[j].astype(BF16))
        h = _layer_norm(h, ln2_w[layer], ln2_b[layer], res=f)
    return h.reshape(bsz, l, d)
```

```python
import functools
import math

import numpy as np
import jax
import jax.numpy as jnp
from jax import lax
from jax.experimental import pallas as pl
from jax.experimental.pallas import tpu as pltpu

F32 = jnp.float32
BF16 = jnp.bfloat16

D_MODEL = 1024
DEPTH = 2
D_GROUP = 256
HEAD_DIM = 64
N_HEADS = 4
CHUNK = 128
S5_CH = 16
S5_GROUPS = 16
S5_STATE = 64
HY_ORDER = 2
HY_EMB = 33
HY_BANDS = 16
HY_FFN = 64
HY_FAST_DECAY = 0.3
HY_SLOW_DECAY = 1.5
HY_TARGET = 1e-2
N_EXPERTS = 8
ROPE_BASE = 10000.0
EPS = 1e-5
DN_ALPHA = (2 * DEPTH) ** 0.25

LANES = 128
S5_TC = 32
FFT_NB = 256
FFT_QB = 2
N_EXT = 3840

CB_RQ, CB_RK, CB_RV, CB_RG, CB_S5, CB_HV, CB_HX1, CB_HX2 = 0, 1, 2, 3, 4, 5, 6, 7
CB_MQ, CB_MK, CB_MV, CB_MO, CB_RQR, CB_RKR = 8, 9, 10, 11, 12, 13
GATE_COL128 = 28


def _cparams(sem, vmem_mb=None):
    kw = dict(dimension_semantics=sem)
    if vmem_mb is not None:
        kw["vmem_limit_bytes"] = vmem_mb * 1024 * 1024
    return pltpu.CompilerParams(**kw)


def _tile(n, pref):
    return pref if n % pref == 0 else n


def _split_dot(x, m, parts=3):
    acc = None
    r = x
    for _ in range(parts):
        hi = r.astype(BF16)
        t = jnp.dot(hi, m, preferred_element_type=F32)
        acc = t if acc is None else acc + t
        r = r - hi.astype(F32)
    return acc


def _split_dot_left(m, x, parts=3):
    acc = None
    r = x
    for _ in range(parts):
        hi = r.astype(BF16)
        t = jnp.dot(m, hi, preferred_element_type=F32)
        acc = t if acc is None else acc + t
        r = r - hi.astype(F32)
    return acc


def _dot_nt(a, b):
    return lax.dot_general(a, b, (((1,), (1,)), ((), ())), preferred_element_type=F32)


def _dot_tn(a, b):
    return lax.dot_general(a, b, (((0,), (0,)), ((), ())), preferred_element_type=F32)


def _sigmoid(x):
    return 1.0 / (1.0 + jnp.exp(-x))


def _silu(x):
    return x * _sigmoid(x)


def _log_sigmoid(x):
    return jnp.minimum(x, 0.0) - jnp.log(1.0 + jnp.exp(-jnp.abs(x)))


def _head_masks(dtype):
    lane = lax.broadcasted_iota(jnp.int32, (1, D_GROUP), 1)
    return [((lane >= h * HEAD_DIM) & (lane < (h + 1) * HEAD_DIM)).astype(dtype) for h in range(N_HEADS)]


def _ln_core(x, w, b):
    mu = jnp.mean(x, -1, keepdims=True)
    xc = x - mu
    var = jnp.mean(xc * xc, -1, keepdims=True)
    return xc * lax.rsqrt(var + EPS) * w + b


def _ln_kernel(x_ref, w_ref, b_ref, o_ref):
    o_ref[...] = _ln_core(x_ref[...], w_ref[...], b_ref[...])


def _layer_norm(x, w, b):
    t, d = x.shape
    tm = _tile(t, 512)
    row = pl.BlockSpec((tm, d), lambda i: (i, 0))
    vec = pl.BlockSpec((1, d), lambda i: (0, 0))
    return pl.pallas_call(_ln_kernel, out_shape=jax.ShapeDtypeStruct((t, d), F32), grid=(t // tm,),
                          in_specs=[row, vec, vec], out_specs=row,
                          compiler_params=_cparams(("parallel",)), name="layer_norm")(x, w.reshape(1, d), b.reshape(1, d))


def _mm_kernel(a_ref, b_ref, o_ref):
    o_ref[...] = jnp.dot(a_ref[...].astype(BF16), b_ref[...], preferred_element_type=F32).astype(o_ref.dtype)


def _mm(a, b, tm=1024, tn=1024, out_dtype=F32, name="matmul"):
    m, k = a.shape
    n = b.shape[1]
    tm, tn = _tile(m, tm), _tile(n, tn)
    return pl.pallas_call(
        _mm_kernel, out_shape=jax.ShapeDtypeStruct((m, n), out_dtype), grid=(m // tm, n // tn),
        in_specs=[pl.BlockSpec((tm, k), lambda i, j: (i, 0)), pl.BlockSpec((k, tn), lambda i, j: (0, j))],
        out_specs=pl.BlockSpec((tm, tn), lambda i, j: (i, j)),
        compiler_params=_cparams(("parallel", "arbitrary"), 48), name=name)(a, b)


def _outproj_ln_kernel(y0_ref, y1_ref, y2_ref, y3_ref, w_ref, h_ref, lw_ref, lb_ref, o_ref):
    mix = None
    for g, y_ref in enumerate((y0_ref, y1_ref, y2_ref, y3_ref)):
        part = jnp.dot(y_ref[...].astype(BF16), w_ref[g * D_GROUP:(g + 1) * D_GROUP, :], preferred_element_type=F32)
        mix = part if mix is None else mix + part
    o_ref[...] = _ln_core(DN_ALPHA * h_ref[...] + mix, lw_ref[...], lb_ref[...])


def _outproj_ln(ys, w_out, h, lw, lb):
    t, d = h.shape
    tm = _tile(t, 1024)
    grp = pl.BlockSpec((tm, D_GROUP), lambda i: (i, 0))
    row = pl.BlockSpec((tm, d), lambda i: (i, 0))
    vec = pl.BlockSpec((1, d), lambda i: (0, 0))
    return pl.pallas_call(
        _outproj_ln_kernel, out_shape=jax.ShapeDtypeStruct((t, d), F32), grid=(t // tm,),
        in_specs=[grp, grp, grp, grp, pl.BlockSpec((d, d), lambda i: (0, 0)), row, vec, vec], out_specs=row,
        compiler_params=_cparams(("parallel",), 48), name="out_proj_ln",
    )(*ys, w_out.astype(BF16), h, lw.reshape(1, d), lb.reshape(1, d))


def _shortconv_kernel(x_ref, xp_ref, xn_ref, w_ref, b_ref, o_ref, *, nt, act):
    i = pl.program_id(1)
    x = x_ref[0]
    tl = x.shape[0]
    row = lax.broadcasted_iota(jnp.int32, x.shape, 0)
    prev_row = jnp.where(i == 0, 0.0, xp_ref[0, 7:8, :])
    next_row = jnp.where(i == nt - 1, 0.0, xn_ref[0, 0:1, :])
    x_prev = jnp.where(row == 0, prev_row, pltpu.roll(x, 1, 0))
    x_next = jnp.where(row == tl - 1, next_row, pltpu.roll(x, tl - 1, 0))
    w = w_ref[0]
    y = b_ref[0, 0:1] + x_prev * w[0:1] + x * w[1:2] + x_next * w[2:3]
    if act:
        y = _silu(y)
    o_ref[0] = y


def _shortconv(proj, col0, nblk, w, b, act):
    bsz, l, _ = proj.shape
    tl = _tile(l, 1024)
    nt = l // tl
    w3 = jnp.transpose(w.reshape(3, nblk, D_GROUP), (1, 0, 2))
    w3 = jnp.pad(w3, ((0, 0), (0, 5), (0, 0)))
    b3 = jnp.broadcast_to(b.reshape(nblk, 1, D_GROUP), (nblk, 8, D_GROUP))
    r8 = tl // 8
    return pl.pallas_call(
        functools.partial(_shortconv_kernel, nt=nt, act=act),
        out_shape=jax.ShapeDtypeStruct((bsz, l, nblk * D_GROUP), F32), grid=(bsz, nt, nblk),
        in_specs=[
            pl.BlockSpec((1, tl, D_GROUP), lambda bb, i, j: (bb, i, col0 + j)),
            pl.BlockSpec((1, 8, D_GROUP), lambda bb, i, j: (bb, jnp.maximum(i * r8 - 1, 0), col0 + j)),
            pl.BlockSpec((1, 8, D_GROUP), lambda bb, i, j: (bb, jnp.minimum((i + 1) * r8, l // 8 - 1), col0 + j)),
            pl.BlockSpec((1, 8, D_GROUP), lambda bb, i, j: (j, 0, 0)),
            pl.BlockSpec((1, 8, D_GROUP), lambda bb, i, j: (j, 0, 0)),
        ],
        out_specs=pl.BlockSpec((1, tl, D_GROUP), lambda bb, i, j: (bb, i, j)),
        compiler_params=_cparams(("parallel", "parallel", "parallel")), name="shortconv")(proj, proj, proj, w3, b3)


def _stack_heads(xb, masks):
    return jnp.concatenate([xb * masks[h] for h in range(N_HEADS)], axis=0)


def _compact(s):
    return s[0:64] + s[64:128] + s[128:192] + s[192:256]


def _expand(c, bd):
    return jnp.concatenate([c, c, c, c], axis=0) * bd


def _head_norm(o, avg, gn):
    mu = _split_dot(o, avg, parts=2)
    oc = o - mu
    var = _split_dot(oc * oc, avg, parts=2)
    return oc * lax.rsqrt(var + EPS) * gn


def _ret_kernel(q_ref, qr_ref, k_ref, kr_ref, v_ref, g_ref, cos_ref, sin_ref,
                dsym_ref, qdf_ref, qdb_ref, kdf_ref, kdb_ref, cdec_ref, bd_ref, avg_ref, gn_ref,
                o_ref, sfw_ref, sbw_ref, save_ref, *, cb, nblk):
    p = pl.program_id(0)
    i = pl.program_id(1)
    bsz = q_ref.shape[0]
    masks = _head_masks(BF16)
    bd = bd_ref[...]
    cdec = cdec_ref[...]

    def rope_k(b, rows):
        return k_ref[b, rows] * cos_ref[rows] + kr_ref[b, rows] * sin_ref[rows]

    def kv_update(s, k, decay, vb):
        kv = _dot_tn((k * decay).astype(BF16), vb)
        return s * cdec + kv * bd

    @pl.when(p == 0)
    def _():
        @pl.when(i == 0)
        def _():
            sbw_ref[...] = jnp.zeros_like(sbw_ref)

        blk = nblk - 1 - i
        for c in reversed(range(cb)):
            rows = slice(c * CHUNK, (c + 1) * CHUNK)
            for b in range(bsz):
                s = sbw_ref[b]
                save_ref[b, blk * cb + c] = _compact(s)
                sbw_ref[b] = kv_update(s, rope_k(b, rows), kdb_ref[...], v_ref[b, rows].astype(BF16))

    @pl.when(p == 1)
    def _():
        @pl.when(i == 0)
        def _():
            sfw_ref[...] = jnp.zeros_like(sfw_ref)

        for c in range(cb):
            rows = slice(c * CHUNK, (c + 1) * CHUNK)
            for b in range(bsz):
                q = q_ref[b, rows] * cos_ref[rows] + qr_ref[b, rows] * sin_ref[rows]
                k = rope_k(b, rows)
                qb, kb, vb = q.astype(BF16), k.astype(BF16), v_ref[b, rows].astype(BF16)
                s_all = _dot_nt(qb, _stack_heads(kb, masks))
                pmat = (s_all * dsym_ref[...]).astype(BF16)
                o = jnp.dot(pmat, _stack_heads(vb, masks), preferred_element_type=F32)
                sfw = sfw_ref[b]
                sbw = _expand(save_ref[b, i * cb + c], bd)
                o = o + jnp.dot(qb, sfw.astype(BF16), preferred_element_type=F32) * qdf_ref[...]
                o = o + jnp.dot(qb, sbw.astype(BF16), preferred_element_type=F32) * qdb_ref[...]
                y = _head_norm(o, avg_ref[...], gn_ref[...])
                o_ref[b, rows] = _silu(g_ref[b, rows]) * y
                sfw_ref[b] = kv_update(sfw, k, kdf_ref[...], vb)


def _ret_tables():
    lg = np.log(1.0 - 2.0 ** (-5.0 - np.arange(N_HEADS, dtype=np.float64)))
    pos = np.arange(CHUNK, dtype=np.float64)
    lag = np.abs(pos[:, None] - pos[None, :])
    dsym = np.concatenate([np.exp(lg[h] * lag) for h in range(N_HEADS)], axis=1)
    lane_lg = np.repeat(lg, HEAD_DIM)[None, :]
    qdf = np.exp(lane_lg * (pos[:, None] + 1.0))
    qdb = np.exp(lane_lg * (CHUNK - pos[:, None]))
    kdf = np.exp(lane_lg * (CHUNK - 1.0 - pos[:, None]))
    kdb = np.exp(lane_lg * pos[:, None])
    cdec = np.exp(lane_lg * CHUNK)
    return [jnp.asarray(t, F32) for t in (dsym, qdf, qdb, kdf, kdb, cdec)]


def _block_diag_mask():
    hid = np.arange(D_GROUP) // HEAD_DIM
    return (hid[:, None] == hid[None, :]).astype(np.float32)


def _rope_tables(l):
    half = HEAD_DIM // 2
    inv = ROPE_BASE ** (-np.arange(half, dtype=np.float64) / half)
    ang = np.arange(l, dtype=np.float64)[:, None] * inv[None, :]
    cos, sin = np.cos(ang), np.sin(ang)
    cos_full = np.tile(np.concatenate([cos, cos], -1), (1, N_HEADS))
    sin_full = np.tile(np.concatenate([sin, sin], -1), (1, N_HEADS))
    return jnp.asarray(cos_full, F32), jnp.asarray(sin_full, F32)


def _retention(proj, gn_w, cos_full, sin_full):
    bsz, l, _ = proj.shape
    nc = l // CHUNK
    cb = 4 if nc % 4 == 0 else 1
    nblk = nc // cb
    tl = cb * CHUNK
    dsym, qdf, qdb, kdf, kdb, cdec = _ret_tables()
    bd = jnp.asarray(_block_diag_mask())
    avg = jnp.asarray(_block_diag_mask() / HEAD_DIM, BF16)

    def both(col):
        return pl.BlockSpec((bsz, tl, D_GROUP), lambda p, i: (0, i + (1 - p) * (nblk - 1 - 2 * i), col))

    def fwd_only(col):
        return pl.BlockSpec((bsz, tl, D_GROUP), lambda p, i: (0, p * i, col))

    tab = pl.BlockSpec((tl, D_GROUP), lambda p, i: (i + (1 - p) * (nblk - 1 - 2 * i), 0))

    def const(shape):
        return pl.BlockSpec(shape, lambda p, i: (0,) * len(shape))

    return pl.pallas_call(
        functools.partial(_ret_kernel, cb=cb, nblk=nblk),
        out_shape=jax.ShapeDtypeStruct((bsz, l, D_GROUP), F32), grid=(2, nblk),
        in_specs=[fwd_only(CB_RQ), fwd_only(CB_RQR), both(CB_RK), both(CB_RKR), both(CB_RV), fwd_only(CB_RG),
                  tab, tab, const((CHUNK, 4 * CHUNK)), const((CHUNK, D_GROUP)), const((CHUNK, D_GROUP)),
                  const((CHUNK, D_GROUP)), const((CHUNK, D_GROUP)), const((1, D_GROUP)),
                  const((D_GROUP, D_GROUP)), const((D_GROUP, D_GROUP)), const((1, D_GROUP))],
        out_specs=pl.BlockSpec((bsz, tl, D_GROUP), lambda p, i: (0, p * i, 0)),
        scratch_shapes=[pltpu.VMEM((bsz, D_GROUP, D_GROUP), F32), pltpu.VMEM((bsz, D_GROUP, D_GROUP), F32),
                        pltpu.VMEM((bsz, nc, HEAD_DIM, D_GROUP), F32)],
        compiler_params=_cparams(("arbitrary", "arbitrary"), 48), name="retention",
    )(proj, proj, proj, proj, proj, proj, cos_full, sin_full, dsym, qdf, qdb, kdf, kdb, cdec, bd, avg,
      gn_w.reshape(1, D_GROUP))


def _mlstm_kernel(q_ref, k_ref, v_ref, og_ref, gc_ref, gr_ref, bc_ref, br_ref, ex_ref, lt_ref, ut_ref,
                  ones_ref, obd_ref, bd_ref, avg_ref, gn_ref,
                  o_ref, cfw_ref, cbw_ref, nmfw_ref, nmbw_ref, csave_ref, nmsave_ref, *, cb, nblk):
    p = pl.program_id(0)
    i = pl.program_id(1)
    bsz = q_ref.shape[0]
    masks = _head_masks(BF16)
    bd = bd_ref[...]
    lt = lt_ref[...]
    ut = ut_ref[...]
    ri = lax.broadcasted_iota(jnp.int32, (CHUNK, CHUNK), 0)
    ci = lax.broadcasted_iota(jnp.int32, (CHUNK, CHUNK), 1)
    lane = lax.broadcasted_iota(jnp.int32, (1, D_GROUP), 1)

    def gates_expanded(b, rows):
        return _split_dot(gc_ref[b, rows] + bc_ref[...], ex_ref[...])

    def state_update(c_ref, nm_ref, b, total, cum, i_x, k, vb):
        m_prev = nm_ref[b, 1:2]
        g = (total - cum) + i_x
        m_new = jnp.maximum(total + m_prev, jnp.max(g, axis=0, keepdims=True))
        wk = jnp.exp(g - m_new) * k
        decay = jnp.exp(total + m_prev - m_new)
        c_ref[b] = c_ref[b] * decay + _dot_tn(wk.astype(BF16), vb) * bd
        nm_ref[b, 0:1] = decay * nm_ref[b, 0:1] + jnp.sum(wk, axis=0, keepdims=True)
        nm_ref[b, 1:2] = m_new

    @pl.when(p == 0)
    def _():
        @pl.when(i == 0)
        def _():
            cbw_ref[...] = jnp.zeros_like(cbw_ref)
            nmbw_ref[...] = jnp.zeros_like(nmbw_ref)

        blk = nblk - 1 - i
        for c in reversed(range(cb)):
            rows = slice(c * CHUNK, (c + 1) * CHUNK)
            for b in range(bsz):
                csave_ref[b, blk * cb + c] = _compact(cbw_ref[b])
                nmsave_ref[b, blk * cb + c] = nmbw_ref[b]
                gx = gates_expanded(b, rows)
                cum = _split_dot_left(ut, _log_sigmoid(gx[:, 768:1024]))
                k = k_ref[b, rows] * (HEAD_DIM ** -0.5)
                state_update(cbw_ref, nmbw_ref, b, cum[0:1], cum, gx[:, 512:768], k, v_ref[b, rows].astype(BF16))

    def chunk_out(b, rows, cidx):
        q = q_ref[b, rows]
        k = k_ref[b, rows] * (HEAD_DIM ** -0.5)
        qb, kb, vb = q.astype(BF16), k.astype(BF16), v_ref[b, rows].astype(BF16)
        s_all = _dot_nt(qb, _stack_heads(kb, masks))
        vaug = jnp.concatenate([_stack_heads(vb, masks), ones_ref[...]], axis=1)
        gx = gates_expanded(b, rows)
        graw = gr_ref[b, :, rows] + br_ref[...]
        gls = _log_sigmoid(graw)
        cum_r_fw = _split_dot(gls, ut)
        cum_r_bw = _split_dot(gls, lt)
        ccomp = csave_ref[b, cidx]
        nmb = nmsave_ref[b, cidx]

        def direction(i_x, f_x, tri, cum_r, i_row0, f_row0, mask, c_state, n_vec, m_prev, total_row):
            cum = _split_dot_left(tri, _log_sigmoid(f_x))
            total = cum[total_row:total_row + 1]
            inter = cum + m_prev
            ps, rmax = [], []
            dms = []
            for h in range(N_HEADS):
                a_col = cum[:, h * HEAD_DIM:h * HEAD_DIM + 1]
                dm = a_col - cum_r[f_row0 + h:f_row0 + h + 1] + graw[i_row0 + h:i_row0 + h + 1]
                dm = jnp.where(mask, dm, -jnp.inf)
                dms.append(dm)
                rmax.append(jnp.max(dm, axis=-1, keepdims=True))
            rmax256 = jnp.where(lane < 64, rmax[0], jnp.where(lane < 128, rmax[1],
                                jnp.where(lane < 192, rmax[2], rmax[3])))
            m_row = jnp.maximum(inter, rmax256)
            for h in range(N_HEADS):
                m_h = m_row[:, h * HEAD_DIM:h * HEAD_DIM + 1]
                ps.append(s_all[:, h * CHUNK:(h + 1) * CHUNK] * jnp.exp(dms[h] - m_h))
            pmat = jnp.concatenate(ps, axis=1).astype(BF16)
            nd = jnp.dot(pmat, vaug, preferred_element_type=F32)
            w_inter = jnp.exp(inter - m_row)
            qc = jnp.dot(qb, c_state.astype(BF16), preferred_element_type=F32)
            qn = _split_dot(q * n_vec, obd_ref[...])
            num = nd[:, :D_GROUP] + w_inter * qc
            den = nd[:, D_GROUP:] + w_inter * qn
            hdir = num / jnp.maximum(jnp.abs(den), jnp.exp(-m_row))
            return hdir, total, cum

        h_fw, tot_fw, cum_fw = direction(gx[:, 0:256], gx[:, 256:512], lt, cum_r_fw, 0, 4, ri >= ci,
                                         cfw_ref[b], nmfw_ref[b, 0:1], nmfw_ref[b, 1:2], CHUNK - 1)
        h_bw, _, _ = direction(gx[:, 512:768], gx[:, 768:1024], ut, cum_r_bw, 8, 12, ci >= ri,
                               _expand(ccomp, bd), nmb[0:1], nmb[1:2], 0)
        y = _head_norm(h_fw + h_bw, avg_ref[...], gn_ref[...])
        o_ref[b, rows] = _sigmoid(og_ref[b, rows]) * y
        state_update(cfw_ref, nmfw_ref, b, tot_fw, cum_fw, gx[:, 0:256], k, vb)

    @pl.when(p == 1)
    def _():
        @pl.when(i == 0)
        def _():
            cfw_ref[...] = jnp.zeros_like(cfw_ref)
            nmfw_ref[...] = jnp.zeros_like(nmfw_ref)

        for c in range(cb):
            rows = slice(c * CHUNK, (c + 1) * CHUNK)
            for b in range(bsz):
                chunk_out(b, rows, i * cb + c)


def _mlstm(proj, qk, gates_row, gate_b, gn_w):
    bsz, l, _ = proj.shape
    nc = l // CHUNK
    cb = 2 if nc % 2 == 0 else 1
    nblk = nc // cb
    tl = cb * CHUNK
    bd_np = _block_diag_mask()
    bd = jnp.asarray(bd_np)
    avg = jnp.asarray(bd_np / HEAD_DIM, BF16)
    obd = jnp.asarray(bd_np, BF16)
    ex = np.zeros((LANES, 4 * D_GROUP), np.float32)
    for j in range(16):
        typ, h = divmod(j, N_HEADS)
        ex[j, typ * D_GROUP + h * HEAD_DIM: typ * D_GROUP + (h + 1) * HEAD_DIM] = 1.0
    idx = np.arange(CHUNK)
    lt = (idx[None, :] <= idx[:, None]).astype(np.float32)
    ones_st = np.repeat(np.repeat(np.eye(N_HEADS, dtype=np.float32), CHUNK, 0), HEAD_DIM, 1)
    gb = gate_b.astype(F32).reshape(16)
    bias_col = jnp.pad(gb, (0, LANES - 16)).reshape(1, LANES)
    bias_row = jnp.broadcast_to(gb.reshape(16, 1), (16, CHUNK))

    def both(arr_col, width=D_GROUP):
        return pl.BlockSpec((bsz, tl, width), lambda p, i: (0, i + (1 - p) * (nblk - 1 - 2 * i), arr_col))

    def fwd_only(arr_col):
        return pl.BlockSpec((bsz, tl, D_GROUP), lambda p, i: (0, p * i, arr_col))

    def const(shape):
        return pl.BlockSpec(shape, lambda p, i: (0,) * len(shape))

    return pl.pallas_call(
        functools.partial(_mlstm_kernel, cb=cb, nblk=nblk),
        out_shape=jax.ShapeDtypeStruct((bsz, l, D_GROUP), F32), grid=(2, nblk),
        in_specs=[fwd_only(0), both(1), both(CB_MV), fwd_only(CB_MO), both(GATE_COL128, LANES),
                  pl.BlockSpec((bsz, 16, tl), lambda p, i: (0, 0, p * i)),
                  const((1, LANES)), const((16, CHUNK)), const((LANES, 4 * D_GROUP)),
                  const((CHUNK, CHUNK)), const((CHUNK, CHUNK)), const((4 * CHUNK, D_GROUP)),
                  const((D_GROUP, D_GROUP)), const((D_GROUP, D_GROUP)), const((D_GROUP, D_GROUP)),
                  const((1, D_GROUP))],
        out_specs=pl.BlockSpec((bsz, tl, D_GROUP), lambda p, i: (0, p * i, 0)),
        scratch_shapes=[pltpu.VMEM((bsz, D_GROUP, D_GROUP), F32), pltpu.VMEM((bsz, D_GROUP, D_GROUP), F32),
                        pltpu.VMEM((bsz, 8, D_GROUP), F32), pltpu.VMEM((bsz, 8, D_GROUP), F32),
                        pltpu.VMEM((bsz, nc, HEAD_DIM, D_GROUP), F32), pltpu.VMEM((bsz, nc, 8, D_GROUP), F32)],
        compiler_params=_cparams(("arbitrary", "arbitrary"), 48), name="mlstm",
    )(qk, qk, proj, proj, proj, gates_row, bias_col, bias_row, jnp.asarray(ex, BF16), jnp.asarray(lt, BF16),
      jnp.asarray(lt.T, BF16), jnp.asarray(ones_st, BF16), obd, bd, avg, gn_w.reshape(1, D_GROUP))


def _s5_kernel(u_ref, mt_ref, bg_ref, cg_ref, pa_ref, pb_ref, o_ref, *, nsteps):
    ub = u_ref[0].astype(BF16)
    e = jnp.dot(ub, bg_ref[0], preferred_element_type=F32)
    r = e.shape[0]
    row = lax.broadcasted_iota(jnp.int32, (r, LANES), 0)
    xf, xb = e[:, :LANES], e[:, LANES:]
    pa, pb = pa_ref[0], pb_ref[0]
    for s in range(nsteps):
        sh = 1 << s
        a_f, b_f = pa[s:s + 1, :LANES], pb[s:s + 1, :LANES]
        a_b, b_b = pa[s:s + 1, LANES:], pb[s:s + 1, LANES:]
        yf = jnp.where(row >= sh, pltpu.roll(xf, sh, 0), 0.0)
        yb = jnp.where(row < r - sh, pltpu.roll(xb, r - sh, 0), 0.0)
        xf = xf + a_f * yf + b_f * pltpu.roll(yf, LANES // 2, 1)
        xb = xb + a_b * yb + b_b * pltpu.roll(yb, LANES // 2, 1)
    sprev = jnp.where(row >= 1, pltpu.roll(xf, 1, 0), 0.0)
    snext = jnp.where(row < r - 1, pltpu.roll(xb, r - 1, 0), 0.0)
    st = jnp.concatenate([sprev, snext], axis=1).astype(BF16)
    o_ref[0] = (jnp.dot(ub, mt_ref[0], preferred_element_type=F32)
                + jnp.dot(st, cg_ref[0], preferred_element_type=F32))


def _s5_tables(a_re, a_im, log_dt, b_re, b_im, c_re, c_im, d_skip, tc, nsteps):
    g, p, ch = S5_GROUPS, S5_STATE, S5_CH
    hp = lax.Precision.HIGHEST
    are, aim = a_re.astype(F32), a_im.astype(F32)
    delta = jnp.exp(log_dt.astype(F32))[..., None]
    lre, lim = are * delta, aim * delta

    class Cx:
        def __init__(self, re, im):
            self.re, self.im = re, im

        def __mul__(self, o):
            return Cx(self.re * o.re - self.im * o.im, self.re * o.im + self.im * o.re)

        def __getitem__(self, idx):
            return Cx(self.re[idx], self.im[idx])

    def apow(n):
        n = jnp.asarray(n, F32)[None, None, :, None]
        mag, ang = jnp.exp(lre[:, :, None, :] * n), lim[:, :, None, :] * n
        return Cx(mag * jnp.cos(ang), mag * jnp.sin(ang))

    abr, abi = jnp.exp(lre) * jnp.cos(lim), jnp.exp(lre) * jnp.sin(lim)
    den = are * are + aim * aim
    quo = Cx(((abr - 1.0) * are + abi * aim) / den, (abi * are - (abr - 1.0) * aim) / den)
    b_bar = quo[..., None] * Cx(b_re.astype(F32)[None], b_im.astype(F32)[None])
    c = Cx(c_re.astype(F32), c_im.astype(F32))
    taus = np.arange(tc)
    cp = c[:, :, None] * apow(taus)[:, :, :, None, :]
    kk = (jnp.einsum("dgtop,dgpi->dgtoi", cp.re, b_bar.re, precision=hp)
          - jnp.einsum("dgtop,dgpi->dgtoi", cp.im, b_bar.im, precision=hp))
    diff = taus[None, :] - taus[:, None]
    k0 = kk[0][:, np.clip(diff, 0, None)]
    k1 = kk[1][:, np.clip(-diff, 0, None)]
    dsk = d_skip.astype(F32).reshape(g, ch)[:, :, None] * jnp.eye(ch, dtype=F32)[None]
    kdiag = kk[0][:, 0] + kk[1][:, 0] + dsk
    dm = diff[None, :, :, None, None]
    kfull = jnp.where(dm > 0, k0, jnp.where(dm < 0, k1, kdiag[:, None, None]))
    mt = jnp.transpose(kfull, (0, 1, 4, 2, 3)).reshape(g, tc * ch, tc * ch)

    zf = apow(tc - 1 - taus)[0][..., None] * b_bar[0][:, None]
    zb = apow(taus)[1][..., None] * b_bar[1][:, None]

    def to_rows(z):
        return jnp.transpose(z, (0, 1, 3, 2)).reshape(g, tc * ch, p)

    bg = jnp.concatenate([to_rows(zf.re), to_rows(zf.im), to_rows(zb.re), to_rows(zb.im)], axis=-1)

    yf = c[0][:, None] * apow(taus + 1)[0][:, :, None, :]
    yb = c[1][:, None] * apow(tc - taus)[1][:, :, None, :]

    def to_cols(z):
        return jnp.transpose(z, (0, 3, 1, 2)).reshape(g, p, tc * ch)

    cg = jnp.concatenate([to_cols(yf.re), -to_cols(yf.im), to_cols(yb.re), -to_cols(yb.im)], axis=1)

    steps = tc * (2.0 ** np.arange(nsteps))
    pw = apow(steps)
    re0, im0, re1, im1 = pw.re[0], pw.im[0], pw.re[1], pw.im[1]
    pa = jnp.concatenate([re0, re0, re1, re1], axis=-1)
    pb = jnp.concatenate([-im0, im0, -im1, im1], axis=-1)
    pad = (-nsteps) % 8
    pa = jnp.pad(pa, ((0, 0), (0, pad), (0, 0)))
    pb = jnp.pad(pb, ((0, 0), (0, pad), (0, 0)))
    return mt.astype(BF16), bg.astype(BF16), cg.astype(BF16), pa, pb


def _s5_glu_kernel(y_ref, w_ref, o_ref):
    y = y_ref[...]
    z = 0.5 * y * (1.0 + jnp.tanh(math.sqrt(2.0 / math.pi) * (y + 0.044715 * (y * y * y))))
    o_ref[...] = z * _sigmoid(jnp.dot(z.astype(BF16), w_ref[...], preferred_element_type=F32))


def _s5(proj, a_re, a_im, log_dt, b_re, b_im, c_re, c_im, d_skip, w_glu):
    bsz, l, _ = proj.shape
    tc = S5_TC
    r = l // tc
    nsteps = max(1, int(math.ceil(math.log2(r))))
    w = tc * S5_CH
    mt, bg, cg, pa, pb = _s5_tables(a_re, a_im, log_dt, b_re, b_im, c_re, c_im, d_skip, tc, nsteps)
    u = proj[:, :, CB_S5 * D_GROUP:(CB_S5 + 1) * D_GROUP].astype(BF16)
    ug = jnp.transpose(u.reshape(bsz, r, tc, S5_GROUPS, S5_CH), (3, 0, 1, 2, 4)).reshape(S5_GROUPS, bsz * r, w)
    ns8 = pa.shape[1]
    yg = pl.pallas_call(
        functools.partial(_s5_kernel, nsteps=nsteps),
        out_shape=jax.ShapeDtypeStruct((S5_GROUPS, bsz * r, w), F32), grid=(S5_GROUPS, bsz),
        in_specs=[pl.BlockSpec((1, r, w), lambda g, b: (g, b, 0)),
                  pl.BlockSpec((1, w, w), lambda g, b: (g, 0, 0)),
                  pl.BlockSpec((1, w, D_GROUP), lambda g, b: (g, 0, 0)),
                  pl.BlockSpec((1, D_GROUP, w), lambda g, b: (g, 0, 0)),
                  pl.BlockSpec((1, ns8, D_GROUP), lambda g, b: (g, 0, 0)),
                  pl.BlockSpec((1, ns8, D_GROUP), lambda g, b: (g, 0, 0))],
        out_specs=pl.BlockSpec((1, r, w), lambda g, b: (g, b, 0)),
        compiler_params=_cparams(("parallel", "parallel"), 48), name="s5_ssm")(ug, mt, bg, cg, pa, pb)
    y = jnp.transpose(yg.reshape(S5_GROUPS, bsz, r, tc, S5_CH), (1, 2, 3, 0, 4)).reshape(bsz * l, D_GROUP)
    t = bsz * l
    tm = _tile(t, 2048)
    out = pl.pallas_call(
        _s5_glu_kernel, out_shape=jax.ShapeDtypeStruct((t, D_GROUP), F32), grid=(t // tm,),
        in_specs=[pl.BlockSpec((tm, D_GROUP), lambda i: (i, 0)), pl.BlockSpec((D_GROUP, D_GROUP), lambda i: (0, 0))],
        out_specs=pl.BlockSpec((tm, D_GROUP), lambda i: (i, 0)),
        compiler_params=_cparams(("parallel",)), name="s5_glu")(y, w_glu.astype(BF16))
    return out.reshape(bsz, l, D_GROUP)


def _hy_filter_kernel(z_ref, w1_ref, b1_ref, w2_ref, b2_ref, w3_ref, fr_ref, dec_ref, h_ref, ss_ref):
    i = pl.program_id(0)
    hp = lax.Precision.HIGHEST
    fr = fr_ref[...]
    a = jnp.sin(fr * (jnp.dot(z_ref[...], w1_ref[...], precision=hp, preferred_element_type=F32) + b1_ref[...]))
    a = jnp.sin(fr * (jnp.dot(a, w2_ref[...], precision=hp, preferred_element_type=F32) + b2_ref[...]))
    h = jnp.dot(a, w3_ref[...], precision=hp, preferred_element_type=F32)
    dec = dec_ref[...]
    h = h * jnp.concatenate([dec, dec, dec, dec], axis=1)
    h_ref[...] = h

    @pl.when(i == 0)
    def _():
        ss_ref[...] = jnp.zeros_like(ss_ref)

    ss_ref[...] += jnp.sum(h * h, axis=0, keepdims=True)


def _hy_filters(l, w1, b1, w2, b2, w3, freq):
    t = np.linspace(0.0, 1.0, l)[:, None]
    w = 2.0 * np.pi * np.arange(l, dtype=np.float64)[:, None] / l
    bands = np.linspace(1e-4, HY_BANDS - 1, HY_BANDS)[None, :]
    z = np.concatenate([t, np.cos(bands * w), -np.sin(bands * w)], axis=-1)
    z = jnp.asarray(np.pad(z, ((0, 0), (0, LANES - HY_EMB))), F32)
    max_decay = math.log(HY_TARGET) / HY_FAST_DECAY
    min_decay = math.log(HY_TARGET) / HY_SLOW_DECAY
    rates = np.abs(np.linspace(min_decay, max_decay, D_GROUP))
    dec = jnp.asarray(np.exp(-t * rates), F32)
    pf = LANES - HY_FFN
    w1p = jnp.pad(w1.astype(F32), ((0, LANES - HY_EMB), (0, pf)))
    w2p = jnp.pad(w2.astype(F32), ((0, pf), (0, pf)))
    w3p = jnp.pad(w3.astype(F32), ((0, pf), (0, 0)))
    b1p = jnp.pad(b1.astype(F32), (0, pf)).reshape(1, LANES)
    b2p = jnp.pad(b2.astype(F32), (0, pf)).reshape(1, LANES)
    frp = jnp.pad(freq.astype(F32), (0, pf)).reshape(1, LANES)
    nout = HY_ORDER * 2 * D_GROUP
    tl = _tile(l, 512)

    def const(shape):
        return pl.BlockSpec(shape, lambda i: (0, 0))

    return pl.pallas_call(
        _hy_filter_kernel,
        out_shape=(jax.ShapeDtypeStruct((l, nout), F32), jax.ShapeDtypeStruct((1, nout), F32)), grid=(l // tl,),
        in_specs=[pl.BlockSpec((tl, LANES), lambda i: (i, 0)), const((LANES, LANES)), const((1, LANES)),
                  const((LANES, LANES)), const((1, LANES)), const((LANES, nout)), const((1, LANES)),
                  pl.BlockSpec((tl, D_GROUP), lambda i: (i, 0))],
        out_specs=(pl.BlockSpec((tl, nout), lambda i: (i, 0)), const((1, nout))),
        compiler_params=_cparams(("arbitrary",)), name="hyena_filter_mlp")(z, w1p, b1p, w2p, b2p, w3p, frp, dec)


def _dft_consts(na):
    nb = FFT_NB
    n = na * nb
    ia = np.arange(na, dtype=np.float64)
    th = 2.0 * np.pi * np.outer(ia, ia) / na
    c1, s1 = np.cos(th), np.sin(th)
    eye8 = np.eye(8)
    fa_full = np.concatenate([c1, -s1], axis=0)
    g_full = np.kron(fa_full, eye8)
    g_half = np.kron(fa_full[:, : na // 2], eye8)
    g_out = np.kron(np.concatenate([c1[: na // 2], -s1[: na // 2]], axis=1) / n, eye8)
    ib = np.arange(nb, dtype=np.float64)
    ph = 2.0 * np.pi * np.outer(ib, ib) / nb
    c2, s2 = np.cos(ph), np.sin(ph)
    fb = np.block([[c2, s2], [-s2, c2]])
    fbc = np.block([[c2, -s2], [s2, c2]])
    ps = 2.0 * np.pi * np.outer(ia, ib) / n
    twr = np.broadcast_to(np.cos(ps)[:, :, None], (na, nb, LANES))
    twi = np.broadcast_to(-np.sin(ps)[:, :, None], (na, nb, LANES))
    as_bf = lambda x: jnp.asarray(x, BF16)
    return dict(g_full=as_bf(g_full), g_half=as_bf(g_half), g_out=as_bf(g_out), fb=as_bf(fb), fbc=as_bf(fbc),
                twr=jnp.asarray(twr, F32), twi=jnp.asarray(twi, F32))


def _lane_tile(x, reps):
    return x if reps == 1 else jnp.concatenate([x] * reps, axis=-1)


def _hy_spec_kernel(a_ref, twr_ref, twi_ref, fb_ref, ss_ref, o_ref, *, kb, reps):
    scale = lax.rsqrt(ss_ref[...])
    for j in range(kb):
        ar, ai = a_ref[0, j], a_ref[1, j]
        twr, twi = _lane_tile(twr_ref[j], reps), _lane_tile(twi_ref[j], reps)
        br = twr * ar - twi * ai
        bi = twr * ai + twi * ar
        x = jnp.dot(fb_ref[...], jnp.concatenate([br, bi], axis=0).astype(BF16), preferred_element_type=F32)
        o_ref[0, j] = x[:FFT_NB] * scale
        o_ref[1, j] = x[FFT_NB:] * scale


def _hy_mid_kernel(a_ref, h_ref, twr_ref, twi_ref, fb_ref, fbc_ref, o_ref, *, kb, reps):
    for j in range(kb):
        ar, ai = a_ref[0, 0, j], a_ref[0, 1, j]
        twr, twi = _lane_tile(twr_ref[j], reps), _lane_tile(twi_ref[j], reps)
        br = twr * ar - twi * ai
        bi = twr * ai + twi * ar
        x = jnp.dot(fb_ref[...], jnp.concatenate([br, bi], axis=0).astype(BF16), preferred_element_type=F32)
        xr, xi = x[:FFT_NB], x[FFT_NB:]
        hr, hi = h_ref[0, j], h_ref[1, j]
        yr = xr * hr - xi * hi
        yi = xr * hi + xi * hr
        z = jnp.dot(fbc_ref[...], jnp.concatenate([yr, yi], axis=0).astype(BF16), preferred_element_type=F32)
        zr, zi = z[:FFT_NB], z[FFT_NB:]
        o_ref[0, 0, j] = twr * zr + twi * zi
        o_ref[0, 1, j] = twr * zi - twi * zr


def _hy_dft1_kernel(g_ref, x_ref, o_ref, *, qb):
    na_in, c = x_ref.shape[1], x_ref.shape[4]
    na = o_ref.shape[2]
    for q in range(qb):
        x = x_ref[0, :, q].reshape(na_in * 8, c).astype(BF16)
        a = jnp.dot(g_ref[...], x, preferred_element_type=F32)
        o_ref[0, :, :, q] = a.reshape(2, na, 8, c)


def _hy_dft1(g, x5, ncol, name):
    bsz, na_in, nq = x5.shape[:3]
    na = g.shape[0] // 16
    c = D_GROUP
    qb = FFT_QB
    return pl.pallas_call(
        functools.partial(_hy_dft1_kernel, qb=qb),
        out_shape=jax.ShapeDtypeStruct((bsz, 2, na, nq, 8, ncol * c), F32), grid=(bsz, ncol, nq // qb),
        in_specs=[pl.BlockSpec(g.shape, lambda b, j, q: (0, 0)),
                  pl.BlockSpec((1, na_in, qb, 8, c), lambda b, j, q: (b, 0, q, 0, j))],
        out_specs=pl.BlockSpec((1, 2, na, qb, 8, c), lambda b, j, q: (b, 0, 0, q, 0, j)),
        compiler_params=_cparams(("parallel", "parallel", "parallel"), 48), name=name)(g, x5)


def _hy_out_kernel(g_ref, z_ref, x_ref, v_ref, b_ref, o_ref, *, qb):
    na2, c = z_ref.shape[1] * z_ref.shape[2], z_ref.shape[5]
    nah = o_ref.shape[1]
    bias = b_ref[...].reshape(1, 1, c)
    for q in range(qb):
        z = z_ref[0, :, :, q].reshape(na2 * 8, c).astype(BF16)
        y = jnp.dot(g_ref[...], z, preferred_element_type=F32).reshape(nah, 8, c)
        o_ref[0, :, q] = x_ref[0, :, q] * (y + v_ref[0, :, q] * bias)


def _hyena(proj, conv_w, conv_b, w1, b1, w2, b2, w3, freq, bias):
    bsz, l, _ = proj.shape
    nb = FFT_NB
    na = 2 * l // nb
    nah = na // 2
    nq = nb // 8
    c = D_GROUP
    qb = FFT_QB
    dc = _dft_consts(na)
    pc = _shortconv(proj, CB_HV, 3, conv_w, conv_b, act=False)

    h, ss = _hy_filters(l, w1, b1, w2, b2, w3, freq)
    h4 = h.reshape(l, HY_ORDER, 2, c)
    ss4 = ss.reshape(HY_ORDER, 2, c)
    ssn = (ss4[:, 0] + ss4[:, 1]).reshape(1, HY_ORDER * c)
    hf, hb = h4[:, :, 0], h4[:, :, 1]
    ncf = HY_ORDER * c
    kern = jnp.concatenate([hf, jnp.zeros_like(hf[:1]), hb[:0:-1]], axis=0).reshape(1, na, nq, 8, ncf)
    ka = _hy_dft1(dc["g_full"], kern, HY_ORDER, "hyena_filter_dft1").reshape(2, na, nb, ncf)
    kb = 4 if na % 4 == 0 else 1
    reps = c // LANES
    tw = pl.BlockSpec((kb, nb, LANES), lambda j, k: (k, 0, 0))
    mat = pl.BlockSpec((2 * nb, 2 * nb), lambda j, k: (0, 0))
    hspec = pl.pallas_call(
        functools.partial(_hy_spec_kernel, kb=kb, reps=reps),
        out_shape=jax.ShapeDtypeStruct((2, na, nb, ncf), F32), grid=(ncf // c, na // kb),
        in_specs=[pl.BlockSpec((2, kb, nb, c), lambda j, k: (0, k, 0, j)), tw, tw, mat,
                  pl.BlockSpec((1, c), lambda j, k: (0, j))],
        out_specs=pl.BlockSpec((2, kb, nb, c), lambda j, k: (0, k, 0, j)),
        compiler_params=_cparams(("parallel", "parallel"), 48), name="hyena_filter_dft2",
    )(ka, dc["twr"], dc["twi"], dc["fb"], ssn)

    pc5 = pc.reshape(bsz, nah, nq, 8, 3 * c)

    def long_conv_gate(z5, order, xcol):
        a = _hy_dft1(dc["g_half"], z5, 1, "hyena_dft1").reshape(bsz, 2, na, nb, c)
        zmid = pl.pallas_call(
            functools.partial(_hy_mid_kernel, kb=kb, reps=reps),
            out_shape=jax.ShapeDtypeStruct((bsz, 2, na, nb, c), F32), grid=(bsz, na // kb),
            in_specs=[pl.BlockSpec((1, 2, kb, nb, c), lambda b, k: (b, 0, k, 0, 0)),
                      pl.BlockSpec((2, kb, nb, c), lambda b, k: (0, k, 0, order)), tw, tw, mat, mat],
            out_specs=pl.BlockSpec((1, 2, kb, nb, c), lambda b, k: (b, 0, k, 0, 0)),
            compiler_params=_cparams(("parallel", "parallel"), 48), name="hyena_dft_mid",
        )(a, hspec, dc["twr"], dc["twi"], dc["fb"], dc["fbc"])
        zmid = zmid.reshape(bsz, 2, na, nq, 8, c)
        sig = lambda col: pl.BlockSpec((1, nah, qb, 8, c), lambda b, q: (b, 0, q, 0, col))
        return pl.pallas_call(
            functools.partial(_hy_out_kernel, qb=qb),
            out_shape=jax.ShapeDtypeStruct((bsz, nah, nq, 8, c), F32), grid=(bsz, nq // qb),
            in_specs=[pl.BlockSpec(dc["g_out"].shape, lambda b, q: (0, 0)),
                      pl.BlockSpec((1, 2, na, qb, 8, c), lambda b, q: (b, 0, 0, q, 0, 0)),
                      sig(xcol), sig(0), pl.BlockSpec((1, c), lambda b, q: (0, 0))],
            out_specs=sig(0),
            compiler_params=_cparams(("parallel", "parallel"), 48), name="hyena_idft_gate",
        )(dc["g_out"], zmid, pc5, z5, bias[order].astype(F32).reshape(1, c))

    z1 = long_conv_gate(pc5, 0, 1)
    z2 = long_conv_gate(z1, 1, 2)
    return z2.reshape(bsz, l, c)


def _ffn_kernel(x_ref, w1_ref, w3_ref, w2_ref, lw_ref, lb_ref, o_ref, xb_ref, acc_ref, *, nf):
    f = pl.program_id(1)

    @pl.when(f == 0)
    def _():
        xb_ref[...] = x_ref[...].astype(BF16)
        acc_ref[...] = jnp.zeros_like(acc_ref)

    xb = xb_ref[...]
    a = jnp.dot(xb, w1_ref[...], preferred_element_type=F32)
    b = jnp.dot(xb, w3_ref[...], preferred_element_type=F32)
    acc_ref[...] += jnp.dot((_silu(a) * b).astype(BF16), w2_ref[...], preferred_element_type=F32)

    @pl.when(f == nf - 1)
    def _():
        o_ref[...] = _ln_core(DN_ALPHA * x_ref[...] + acc_ref[...], lw_ref[...], lb_ref[...])


def _ffn_ln(x, w1, w3, w2, lw, lb):
    t, d = x.shape
    ff = w1.shape[1]
    tm = _tile(t, 1024)
    tf = 512 if ff % 512 == 0 else (256 if ff % 256 == 0 else ff)
    nf = ff // tf
    vec = pl.BlockSpec((1, d), lambda i, f: (0, 0))
    return pl.pallas_call(
        functools.partial(_ffn_kernel, nf=nf),
        out_shape=jax.ShapeDtypeStruct((t, d), F32), grid=(t // tm, nf),
        in_specs=[pl.BlockSpec((tm, d), lambda i, f: (i, 0)),
                  pl.BlockSpec((d, tf), lambda i, f: (0, f)),
                  pl.BlockSpec((d, tf), lambda i, f: (0, f)),
                  pl.BlockSpec((tf, d), lambda i, f: (f, 0)), vec, vec],
        out_specs=pl.BlockSpec((tm, d), lambda i, f: (i, 0)),
        scratch_shapes=[pltpu.VMEM((tm, d), BF16), pltpu.VMEM((tm, d), F32)],
        compiler_params=_cparams(("parallel", "arbitrary"), 52), name="swiglu_ffn_ln",
    )(x, w1, w3, w2, lw.reshape(1, d), lb.reshape(1, d))


MOE_TB = 1024
MOE_SUB = 288
MOE_SLOT = 384
MOE_CUM = 256


def _moe_kernel(cnt_ref, x_ref, cmb_ref, cmbt_ref, lt_ref, ut_ref, w1_ref, w3_ref, w2_ref, lw_ref, lb_ref, o_ref,
                xb_ref, xs_ref, ys_ref, gs_ref, posc_ref, posr_ref, *, nf):
    i = pl.program_id(0)
    e = pl.program_id(1)
    f = pl.program_id(2)
    tb = x_ref.shape[0]
    count = cnt_ref[i * N_EXPERTS + e]
    npass = (count + (MOE_SUB - 1)) // MOE_SUB

    @pl.when((e == 0) & (f == 0))
    def _():
        xb_ref[...] = x_ref[...].astype(BF16)
        o_ref[...] = jnp.zeros_like(o_ref)
        ys_ref[...] = jnp.zeros_like(ys_ref)
        carry_c = jnp.zeros((1, LANES), F32)
        carry_r = jnp.zeros((N_EXPERTS, 1), F32)
        for c in range(tb // MOE_CUM):
            rows = slice(c * MOE_CUM, (c + 1) * MOE_CUM)
            mc = (cmb_ref[rows] > 0.0).astype(F32)
            inc = jnp.dot(lt_ref[...], mc.astype(BF16), preferred_element_type=F32) + carry_c
            posc_ref[rows] = jnp.where(mc > 0.0, inc - 1.0, -1.0)
            carry_c = inc[MOE_CUM - 1:MOE_CUM]
            mr = (cmbt_ref[:, rows] > 0.0).astype(F32)
            incr = jnp.dot(mr.astype(BF16), ut_ref[...], preferred_element_type=F32) + carry_r
            posr_ref[:, rows] = jnp.where(mr > 0.0, incr - 1.0, -1.0)
            carry_r = incr[:, MOE_CUM - 1:MOE_CUM]

    @pl.when(f == 0)
    def _():
        pos_row = posr_ref[pl.ds(e, 1), :]
        gate_row = cmbt_ref[pl.ds(e, 1), :]
        slot = lax.broadcasted_iota(jnp.int32, (MOE_SUB, tb), 0).astype(F32)

        def gather(j, carry):
            base = pl.multiple_of(j * MOE_SLOT, LANES)
            hit = pos_row == slot + (j * MOE_SUB).astype(F32)
            xs_ref[pl.ds(base, MOE_SUB), :] = jnp.dot(
                jnp.where(hit, 1.0, 0.0).astype(BF16), xb_ref[...], preferred_element_type=F32).astype(BF16)
            g = jnp.sum(jnp.where(hit, gate_row, 0.0), axis=1, keepdims=True)
            gs_ref[pl.ds(base, MOE_SUB), :] = jnp.broadcast_to(g, (MOE_SUB, LANES))
            ys_ref[pl.ds(base, MOE_SUB), :] = jnp.zeros((MOE_SUB, ys_ref.shape[1]), F32)
            return carry

        lax.fori_loop(0, npass, gather, 0)

    def expert(j, carry):
        base = pl.multiple_of(j * MOE_SLOT, LANES)
        xs = xs_ref[pl.ds(base, MOE_SUB), :]
        a = jnp.dot(xs, w1_ref[0], preferred_element_type=F32)
        b = jnp.dot(xs, w3_ref[0], preferred_element_type=F32)
        hid = _silu(a) * b * gs_ref[pl.ds(base, MOE_SUB), 0:1]
        ys_ref[pl.ds(base, MOE_SUB), :] += jnp.dot(hid.astype(BF16), w2_ref[0], preferred_element_type=F32)
        return carry

    lax.fori_loop(0, npass, expert, 0)

    @pl.when(f == nf - 1)
    def _():
        lane = lax.broadcasted_iota(jnp.int32, (tb, LANES), 1)
        pos_col = jnp.sum(jnp.where(lane == e, posc_ref[...], 0.0), axis=1, keepdims=True)
        slot = lax.broadcasted_iota(jnp.int32, (tb, MOE_SLOT), 1)
        slot = jnp.where(slot < MOE_SUB, slot, -2 * tb).astype(F32)

        def scatter(j, carry):
            base = pl.multiple_of(j * MOE_SLOT, LANES)
            hit = pos_col == slot + (j * MOE_SUB).astype(F32)
            o_ref[...] += jnp.dot(jnp.where(hit, 1.0, 0.0).astype(BF16),
                                  ys_ref[pl.ds(base, MOE_SLOT), :].astype(BF16), preferred_element_type=F32)
            return carry

        lax.fori_loop(0, npass, scatter, 0)

        @pl.when(e == pl.num_programs(1) - 1)
        def _():
            o_ref[...] = _ln_core(DN_ALPHA * x_ref[...] + o_ref[...], lw_ref[...], lb_ref[...])


def _moe_ln(x, cmb, w1, w3, w2, lw, lb):
    t, d = x.shape
    ne, _, ff = w1.shape
    tb = _tile(t, MOE_TB)
    nb = t // tb
    tf = 896 if ff % 896 == 0 else ff
    nf = ff // tf
    max_pass = -(-tb // MOE_SUB)
    cmbt = jnp.transpose(cmb[:, :N_EXPERTS])
    counts = jnp.sum((cmb[:, :N_EXPERTS] > 0.0).reshape(nb, tb, N_EXPERTS), axis=1).astype(jnp.int32).reshape(-1)
    idx = np.arange(MOE_CUM)
    lt = jnp.asarray(idx[None, :] <= idx[:, None], BF16)
    grid_spec = pltpu.PrefetchScalarGridSpec(
        num_scalar_prefetch=1, grid=(nb, ne, nf),
        in_specs=[pl.BlockSpec((tb, d), lambda i, e, f, c: (i, 0)),
                  pl.BlockSpec((tb, LANES), lambda i, e, f, c: (i, 0)),
                  pl.BlockSpec((N_EXPERTS, tb), lambda i, e, f, c: (0, i)),
                  pl.BlockSpec((MOE_CUM, MOE_CUM), lambda i, e, f, c: (0, 0)),
                  pl.BlockSpec((MOE_CUM, MOE_CUM), lambda i, e, f, c: (0, 0)),
                  pl.BlockSpec((1, d, tf), lambda i, e, f, c: (e, 0, f)),
                  pl.BlockSpec((1, d, tf), lambda i, e, f, c: (e, 0, f)),
                  pl.BlockSpec((1, tf, d), lambda i, e, f, c: (e, f, 0)),
                  pl.BlockSpec((1, d), lambda i, e, f, c: (0, 0)),
                  pl.BlockSpec((1, d), lambda i, e, f, c: (0, 0))],
        out_specs=pl.BlockSpec((tb, d), lambda i, e, f, c: (i, 0)),
        scratch_shapes=[pltpu.VMEM((tb, d), BF16), pltpu.VMEM((max_pass * MOE_SLOT, d), BF16),
                        pltpu.VMEM((max_pass * MOE_SLOT, d), F32), pltpu.VMEM((max_pass * MOE_SLOT, LANES), F32),
                        pltpu.VMEM((tb, LANES), F32), pltpu.VMEM((N_EXPERTS, tb), F32)])
    return pl.pallas_call(
        functools.partial(_moe_kernel, nf=nf), out_shape=jax.ShapeDtypeStruct((t, d), F32), grid_spec=grid_spec,
        compiler_params=_cparams(("parallel", "arbitrary", "arbitrary"), 56), name="moe_routed",
    )(counts, x, cmb, cmbt, lt, jnp.transpose(lt), w1, w3, w2, lw.reshape(1, d), lb.reshape(1, d))


def _router_kernel(x_ref, rh_ref, rl_ref, o_ref):
    x = x_ref[...]
    xh = x.astype(BF16)
    xl = (x - xh.astype(F32)).astype(BF16)
    logits = (jnp.dot(xh, rh_ref[...], preferred_element_type=F32)
              + jnp.dot(xl, rh_ref[...], preferred_element_type=F32)
              + jnp.dot(xh, rl_ref[...], preferred_element_type=F32))
    lane = lax.broadcasted_iota(jnp.int32, logits.shape, 1).astype(F32)
    logits = jnp.where(lane < N_EXPERTS, logits, -jnp.inf)
    m1 = jnp.max(logits, axis=1, keepdims=True)
    i1 = jnp.min(jnp.where(logits == m1, lane, float(LANES)), axis=1, keepdims=True)
    rest = jnp.where(lane == i1, -jnp.inf, logits)
    m2 = jnp.max(rest, axis=1, keepdims=True)
    i2 = jnp.min(jnp.where(rest == m2, lane, float(LANES)), axis=1, keepdims=True)
    e2 = jnp.exp(m2 - m1)
    g1 = 1.0 / (1.0 + e2)
    g2 = e2 / (1.0 + e2)
    o_ref[...] = jnp.where(lane == i1, g1, 0.0) + jnp.where(lane == i2, g2, 0.0)


def _router(x, router):
    t, d = x.shape
    r = jnp.pad(router.astype(F32), ((0, 0), (0, LANES - N_EXPERTS)))
    rh = r.astype(BF16)
    rl = (r - rh.astype(F32)).astype(BF16)
    tm = _tile(t, 1024)
    return pl.pallas_call(
        _router_kernel, out_shape=jax.ShapeDtypeStruct((t, LANES), F32), grid=(t // tm,),
        in_specs=[pl.BlockSpec((tm, d), lambda i: (i, 0)), pl.BlockSpec((d, LANES), lambda i: (0, 0)),
                  pl.BlockSpec((d, LANES), lambda i: (0, 0))],
        out_specs=pl.BlockSpec((tm, LANES), lambda i: (i, 0)),
        compiler_params=_cparams(("parallel",)), name="moe_router")(x, rh, rl)


def _extended_w_in(w_in):
    w = w_in.astype(F32)
    scale = HEAD_DIM ** -0.5

    def rot_half(cols):
        c4 = cols.reshape(-1, N_HEADS, 2, HEAD_DIM // 2)
        return jnp.stack([-c4[:, :, 1], c4[:, :, 0]], axis=2).reshape(-1, D_GROUP)

    wq = w[:, 0:256]
    wk = w[:, 256:512] * scale
    main = jnp.concatenate([wq, wk, w[:, 512:3072]], axis=1)
    gates = jnp.pad(w[:, 3072:3088], ((0, 0), (0, LANES - 16)))
    ext = jnp.concatenate([main, rot_half(wq), rot_half(wk), gates], axis=1)
    return jnp.pad(ext, ((0, 0), (0, N_EXT - ext.shape[1]))).astype(BF16)


def kernel(x, ln_in_w, ln_in_b, w_in, w_out, ret_gn_w, s5_a_re, s5_a_im, s5_log_dt, s5_b_re, s5_b_im, s5_c_re, s5_c_im, s5_d, s5_w_glu, hy_conv_w, hy_conv_b, hy_w1, hy_b1, hy_w2, hy_b2, hy_w3, hy_freq, hy_bias, ml_conv_w, ml_conv_b, ml_gate_b, ml_gn_w, ln1_w, ln1_b, ln2_w, ln2_b, ffn_w1, ffn_w3, ffn_w2, moe_router, moe_w1, moe_w3, moe_w2):
    bsz, l, d = x.shape
    t = bsz * l
    cos_full, sin_full = _rope_tables(l)
    h = _layer_norm(x.reshape(t, d), ln_in_w, ln_in_b)
    for layer in range(DEPTH):
        proj = _mm(h, _extended_w_in(w_in[layer]), tm=1024, tn=1280, name="in_proj").reshape(bsz, l, N_EXT)
        y_ret = _retention(proj, ret_gn_w[layer], cos_full, sin_full)
        y_s5 = _s5(proj, s5_a_re[layer], s5_a_im[layer], s5_log_dt[layer], s5_b_re[layer], s5_b_im[layer],
                   s5_c_re[layer], s5_c_im[layer], s5_d[layer], s5_w_glu[layer])
        y_hy = _hyena(proj, hy_conv_w[layer], hy_conv_b[layer], hy_w1[layer], hy_b1[layer], hy_w2[layer],
                      hy_b2[layer], hy_w3[layer], hy_freq[layer], hy_bias[layer])
        qk = _shortconv(proj, CB_MQ, 2, ml_conv_w[layer], ml_conv_b[layer], act=True)
        gates_row = jnp.transpose(proj[:, :, GATE_COL128 * LANES:GATE_COL128 * LANES + 16], (0, 2, 1))
        y_ml = _mlstm(proj, qk, gates_row, ml_gate_b[layer], ml_gn_w[layer])
        ys = [y.reshape(t, D_GROUP) for y in (y_ret, y_s5, y_hy, y_ml)]
        h = _outproj_ln(ys, w_out[layer], h, ln1_w[layer], ln1_b[layer])
        j = layer // 2
        if layer % 2 == 0:
            h = _ffn_ln(h, ffn_w1[j].astype(BF16), ffn_w3[j].astype(BF16), ffn_w2[j].astype(BF16),
                        ln2_w[layer], ln2_b[layer])
        else:
            cmb = _router(h, moe_router[j])
            h = _moe_ln(h, cmb, moe_w1[j].astype(BF16), moe_w3[j].astype(BF16), moe_w2[j].astype(BF16),
                        ln2_w[layer], ln2_b[layer])
    return h.reshape(bsz, l, d)
```

```python
import functools
import math

import numpy as np
import jax
import jax.numpy as jnp
from jax import lax
from jax.experimental import pallas as pl
from jax.experimental.pallas import tpu as pltpu

F32 = jnp.float32
BF16 = jnp.bfloat16

D_MODEL = 1024
DEPTH = 2
D_GROUP = 256
HEAD_DIM = 64
N_HEADS = 4
CHUNK = 128
S5_CH = 16
S5_GROUPS = 16
S5_STATE = 64
HY_ORDER = 2
HY_EMB = 33
HY_BANDS = 16
HY_FFN = 64
HY_FAST_DECAY = 0.3
HY_SLOW_DECAY = 1.5
HY_TARGET = 1e-2
N_EXPERTS = 8
ROPE_BASE = 10000.0
EPS = 1e-5
DN_ALPHA = (2 * DEPTH) ** 0.25

LANES = 128
S5_TC = 32
FFT_NB = 256
FFT_QB = 2
N_EXT = 3840

CB_RQ, CB_RK, CB_RV, CB_RG, CB_S5, CB_HV, CB_HX1, CB_HX2 = 0, 1, 2, 3, 4, 5, 6, 7
CB_MQ, CB_MK, CB_MV, CB_MO, CB_RQR, CB_RKR = 8, 9, 10, 11, 12, 13
GATE_COL128 = 28


def _cparams(sem, vmem_mb=None):
    kw = dict(dimension_semantics=sem)
    if vmem_mb is not None:
        kw["vmem_limit_bytes"] = vmem_mb * 1024 * 1024
    return pltpu.CompilerParams(**kw)


def _tile(n, pref):
    return pref if n % pref == 0 else n


def _split_dot(x, m, parts=3):
    acc = None
    r = x
    for _ in range(parts):
        hi = r.astype(BF16)
        t = jnp.dot(hi, m, preferred_element_type=F32)
        acc = t if acc is None else acc + t
        r = r - hi.astype(F32)
    return acc


def _split_dot_left(m, x, parts=3):
    acc = None
    r = x
    for _ in range(parts):
        hi = r.astype(BF16)
        t = jnp.dot(m, hi, preferred_element_type=F32)
        acc = t if acc is None else acc + t
        r = r - hi.astype(F32)
    return acc


def _dot_nt(a, b):
    return lax.dot_general(a, b, (((1,), (1,)), ((), ())), preferred_element_type=F32)


def _dot_tn(a, b):
    return lax.dot_general(a, b, (((0,), (0,)), ((), ())), preferred_element_type=F32)


def _sigmoid(x):
    return 1.0 / (1.0 + jnp.exp(-x))


def _silu(x):
    return x * _sigmoid(x)


def _log_sigmoid(x):
    return jnp.minimum(x, 0.0) - jnp.log(1.0 + jnp.exp(-jnp.abs(x)))


def _head_masks(dtype):
    lane = lax.broadcasted_iota(jnp.int32, (1, D_GROUP), 1)
    return [((lane >= h * HEAD_DIM) & (lane < (h + 1) * HEAD_DIM)).astype(dtype) for h in range(N_HEADS)]


def _ln_core(x, w, b):
    mu = jnp.mean(x, -1, keepdims=True)
    xc = x - mu
    var = jnp.mean(xc * xc, -1, keepdims=True)
    return xc * lax.rsqrt(var + EPS) * w + b


def _ln_kernel(x_ref, w_ref, b_ref, o_ref):
    o_ref[...] = _ln_core(x_ref[...], w_ref[...], b_ref[...])


def _layer_norm(x, w, b):
    t, d = x.shape
    tm = _tile(t, 512)
    row = pl.BlockSpec((tm, d), lambda i: (i, 0))
    vec = pl.BlockSpec((1, d), lambda i: (0, 0))
    return pl.pallas_call(_ln_kernel, out_shape=jax.ShapeDtypeStruct((t, d), F32), grid=(t // tm,),
                          in_specs=[row, vec, vec], out_specs=row,
                          compiler_params=_cparams(("parallel",)), name="layer_norm")(x, w.reshape(1, d), b.reshape(1, d))


def _mm_kernel(a_ref, b_ref, o_ref):
    o_ref[...] = jnp.dot(a_ref[...].astype(BF16), b_ref[...], preferred_element_type=F32).astype(o_ref.dtype)


def _mm(a, b, tm=1024, tn=1024, out_dtype=F32, name="matmul"):
    m, k = a.shape
    n = b.shape[1]
    tm, tn = _tile(m, tm), _tile(n, tn)
    return pl.pallas_call(
        _mm_kernel, out_shape=jax.ShapeDtypeStruct((m, n), out_dtype), grid=(m // tm, n // tn),
        in_specs=[pl.BlockSpec((tm, k), lambda i, j: (i, 0)), pl.BlockSpec((k, tn), lambda i, j: (0, j))],
        out_specs=pl.BlockSpec((tm, tn), lambda i, j: (i, j)),
        compiler_params=_cparams(("parallel", "arbitrary"), 48), name=name)(a, b)


def _outproj_ln_kernel(y0_ref, y1_ref, y2_ref, y3_ref, w_ref, h_ref, lw_ref, lb_ref, o_ref):
    mix = None
    for g, y_ref in enumerate((y0_ref, y1_ref, y2_ref, y3_ref)):
        part = jnp.dot(y_ref[...].astype(BF16), w_ref[g * D_GROUP:(g + 1) * D_GROUP, :], preferred_element_type=F32)
        mix = part if mix is None else mix + part
    o_ref[...] = _ln_core(DN_ALPHA * h_ref[...] + mix, lw_ref[...], lb_ref[...])


def _outproj_ln(ys, w_out, h, lw, lb):
    t, d = h.shape
    tm = _tile(t, 1024)
    grp = pl.BlockSpec((tm, D_GROUP), lambda i: (i, 0))
    row = pl.BlockSpec((tm, d), lambda i: (i, 0))
    vec = pl.BlockSpec((1, d), lambda i: (0, 0))
    return pl.pallas_call(
        _outproj_ln_kernel, out_shape=jax.ShapeDtypeStruct((t, d), F32), grid=(t // tm,),
        in_specs=[grp, grp, grp, grp, pl.BlockSpec((d, d), lambda i: (0, 0)), row, vec, vec], out_specs=row,
        compiler_params=_cparams(("parallel",), 48), name="out_proj_ln",
    )(*ys, w_out.astype(BF16), h, lw.reshape(1, d), lb.reshape(1, d))


def _shortconv_kernel(x_ref, xp_ref, xn_ref, w_ref, b_ref, o_ref, *, nt, act):
    i = pl.program_id(1)
    x = x_ref[0]
    tl = x.shape[0]
    row = lax.broadcasted_iota(jnp.int32, x.shape, 0)
    prev_row = jnp.where(i == 0, 0.0, xp_ref[0, 7:8, :])
    next_row = jnp.where(i == nt - 1, 0.0, xn_ref[0, 0:1, :])
    x_prev = jnp.where(row == 0, prev_row, pltpu.roll(x, 1, 0))
    x_next = jnp.where(row == tl - 1, next_row, pltpu.roll(x, tl - 1, 0))
    w = w_ref[0]
    y = b_ref[0, 0:1] + x_prev * w[0:1] + x * w[1:2] + x_next * w[2:3]
    if act:
        y = _silu(y)
    o_ref[0] = y


def _shortconv(proj, col0, nblk, w, b, act):
    bsz, l, _ = proj.shape
    tl = _tile(l, 1024)
    nt = l // tl
    w3 = jnp.transpose(w.reshape(3, nblk, D_GROUP), (1, 0, 2))
    w3 = jnp.pad(w3, ((0, 0), (0, 5), (0, 0)))
    b3 = jnp.broadcast_to(b.reshape(nblk, 1, D_GROUP), (nblk, 8, D_GROUP))
    r8 = tl // 8
    return pl.pallas_call(
        functools.partial(_shortconv_kernel, nt=nt, act=act),
        out_shape=jax.ShapeDtypeStruct((bsz, l, nblk * D_GROUP), F32), grid=(bsz, nt, nblk),
        in_specs=[
            pl.BlockSpec((1, tl, D_GROUP), lambda bb, i, j: (bb, i, col0 + j)),
            pl.BlockSpec((1, 8, D_GROUP), lambda bb, i, j: (bb, jnp.maximum(i * r8 - 1, 0), col0 + j)),
            pl.BlockSpec((1, 8, D_GROUP), lambda bb, i, j: (bb, jnp.minimum((i + 1) * r8, l // 8 - 1), col0 + j)),
            pl.BlockSpec((1, 8, D_GROUP), lambda bb, i, j: (j, 0, 0)),
            pl.BlockSpec((1, 8, D_GROUP), lambda bb, i, j: (j, 0, 0)),
        ],
        out_specs=pl.BlockSpec((1, tl, D_GROUP), lambda bb, i, j: (bb, i, j)),
        compiler_params=_cparams(("parallel", "parallel", "parallel")), name="shortconv")(proj, proj, proj, w3, b3)


def _stack_heads(xb, masks):
    return jnp.concatenate([xb * masks[h] for h in range(N_HEADS)], axis=0)


def _compact(s):
    return s[0:64] + s[64:128] + s[128:192] + s[192:256]


def _expand(c, bd):
    return jnp.concatenate([c, c, c, c], axis=0) * bd


def _head_norm(o, avg, gn):
    mu = _split_dot(o, avg, parts=2)
    oc = o - mu
    var = _split_dot(oc * oc, avg, parts=2)
    return oc * lax.rsqrt(var + EPS) * gn


def _ret_kernel(q_ref, qr_ref, k_ref, kr_ref, v_ref, g_ref, cos_ref, sin_ref,
                dsym_ref, qdf_ref, qdb_ref, kdf_ref, kdb_ref, cdec_ref, bd_ref, avg_ref, gn_ref,
                o_ref, sfw_ref, sbw_ref, save_ref, *, cb, nblk):
    p = pl.program_id(0)
    i = pl.program_id(1)
    bsz = q_ref.shape[0]
    masks = _head_masks(BF16)
    bd = bd_ref[...]
    cdec = cdec_ref[...]

    def rope_k(b, rows):
        return k_ref[b, rows] * cos_ref[rows] + kr_ref[b, rows] * sin_ref[rows]

    def kv_update(s, k, decay, vb):
        kv = _dot_tn((k * decay).astype(BF16), vb)
        return s * cdec + kv * bd

    @pl.when(p == 0)
    def _():
        @pl.when(i == 0)
        def _():
            sbw_ref[...] = jnp.zeros_like(sbw_ref)

        blk = nblk - 1 - i
        for c in reversed(range(cb)):
            rows = slice(c * CHUNK, (c + 1) * CHUNK)
            for b in range(bsz):
                s = sbw_ref[b]
                save_ref[b, blk * cb + c] = _compact(s)
                sbw_ref[b] = kv_update(s, rope_k(b, rows), kdb_ref[...], v_ref[b, rows].astype(BF16))

    @pl.when(p == 1)
    def _():
        @pl.when(i == 0)
        def _():
            sfw_ref[...] = jnp.zeros_like(sfw_ref)

        for c in range(cb):
            rows = slice(c * CHUNK, (c + 1) * CHUNK)
            for b in range(bsz):
                q = q_ref[b, rows] * cos_ref[rows] + qr_ref[b, rows] * sin_ref[rows]
                k = rope_k(b, rows)
                qb, kb, vb = q.astype(BF16), k.astype(BF16), v_ref[b, rows].astype(BF16)
                s_all = _dot_nt(qb, _stack_heads(kb, masks))
                pmat = (s_all * dsym_ref[...]).astype(BF16)
                o = jnp.dot(pmat, _stack_heads(vb, masks), preferred_element_type=F32)
                sfw = sfw_ref[b]
                sbw = _expand(save_ref[b, i * cb + c], bd)
                o = o + jnp.dot(qb, sfw.astype(BF16), preferred_element_type=F32) * qdf_ref[...]
                o = o + jnp.dot(qb, sbw.astype(BF16), preferred_element_type=F32) * qdb_ref[...]
                y = _head_norm(o, avg_ref[...], gn_ref[...])
                o_ref[b, rows] = _silu(g_ref[b, rows]) * y
                sfw_ref[b] = kv_update(sfw, k, kdf_ref[...], vb)


def _ret_tables():
    lg = np.log(1.0 - 2.0 ** (-5.0 - np.arange(N_HEADS, dtype=np.float64)))
    pos = np.arange(CHUNK, dtype=np.float64)
    lag = np.abs(pos[:, None] - pos[None, :])
    dsym = np.concatenate([np.exp(lg[h] * lag) for h in range(N_HEADS)], axis=1)
    lane_lg = np.repeat(lg, HEAD_DIM)[None, :]
    qdf = np.exp(lane_lg * (pos[:, None] + 1.0))
    qdb = np.exp(lane_lg * (CHUNK - pos[:, None]))
    kdf = np.exp(lane_lg * (CHUNK - 1.0 - pos[:, None]))
    kdb = np.exp(lane_lg * pos[:, None])
    cdec = np.exp(lane_lg * CHUNK)
    return [jnp.asarray(t, F32) for t in (dsym, qdf, qdb, kdf, kdb, cdec)]


def _block_diag_mask():
    hid = np.arange(D_GROUP) // HEAD_DIM
    return (hid[:, None] == hid[None, :]).astype(np.float32)


def _rope_tables(l):
    half = HEAD_DIM // 2
    inv = ROPE_BASE ** (-np.arange(half, dtype=np.float64) / half)
    ang = np.arange(l, dtype=np.float64)[:, None] * inv[None, :]
    cos, sin = np.cos(ang), np.sin(ang)
    cos_full = np.tile(np.concatenate([cos, cos], -1), (1, N_HEADS))
    sin_full = np.tile(np.concatenate([sin, sin], -1), (1, N_HEADS))
    return jnp.asarray(cos_full, F32), jnp.asarray(sin_full, F32)


def _retention(proj, gn_w, cos_full, sin_full):
    bsz, l, _ = proj.shape
    nc = l // CHUNK
    cb = 4 if nc % 4 == 0 else 1
    nblk = nc // cb
    tl = cb * CHUNK
    dsym, qdf, qdb, kdf, kdb, cdec = _ret_tables()
    bd = jnp.asarray(_block_diag_mask())
    avg = jnp.asarray(_block_diag_mask() / HEAD_DIM, BF16)

    def both(col):
        return pl.BlockSpec((bsz, tl, D_GROUP), lambda p, i: (0, i + (1 - p) * (nblk - 1 - 2 * i), col))

    def fwd_only(col):
        return pl.BlockSpec((bsz, tl, D_GROUP), lambda p, i: (0, p * i, col))

    tab = pl.BlockSpec((tl, D_GROUP), lambda p, i: (i + (1 - p) * (nblk - 1 - 2 * i), 0))

    def const(shape):
        return pl.BlockSpec(shape, lambda p, i: (0,) * len(shape))

    return pl.pallas_call(
        functools.partial(_ret_kernel, cb=cb, nblk=nblk),
        out_shape=jax.ShapeDtypeStruct((bsz, l, D_GROUP), F32), grid=(2, nblk),
        in_specs=[fwd_only(CB_RQ), fwd_only(CB_RQR), both(CB_RK), both(CB_RKR), both(CB_RV), fwd_only(CB_RG),
                  tab, tab, const((CHUNK, 4 * CHUNK)), const((CHUNK, D_GROUP)), const((CHUNK, D_GROUP)),
                  const((CHUNK, D_GROUP)), const((CHUNK, D_GROUP)), const((1, D_GROUP)),
                  const((D_GROUP, D_GROUP)), const((D_GROUP, D_GROUP)), const((1, D_GROUP))],
        out_specs=pl.BlockSpec((bsz, tl, D_GROUP), lambda p, i: (0, p * i, 0)),
        scratch_shapes=[pltpu.VMEM((bsz, D_GROUP, D_GROUP), F32), pltpu.VMEM((bsz, D_GROUP, D_GROUP), F32),
                        pltpu.VMEM((bsz, nc, HEAD_DIM, D_GROUP), F32)],
        compiler_params=_cparams(("arbitrary", "arbitrary"), 48), name="retention",
    )(proj, proj, proj, proj, proj, proj, cos_full, sin_full, dsym, qdf, qdb, kdf, kdb, cdec, bd, avg,
      gn_w.reshape(1, D_GROUP))


def _mlstm_kernel(q_ref, k_ref, v_ref, og_ref, gc_ref, gr_ref, bc_ref, br_ref, ex_ref, lt_ref, ut_ref,
                  ones_ref, obd_ref, bd_ref, avg_ref, gn_ref,
                  o_ref, cfw_ref, cbw_ref, nmfw_ref, nmbw_ref, csave_ref, nmsave_ref, *, cb, nblk):
    p = pl.program_id(0)
    i = pl.program_id(1)
    bsz = q_ref.shape[0]
    masks = _head_masks(BF16)
    bd = bd_ref[...]
    lt = lt_ref[...]
    ut = ut_ref[...]
    ri = lax.broadcasted_iota(jnp.int32, (CHUNK, CHUNK), 0)
    ci = lax.broadcasted_iota(jnp.int32, (CHUNK, CHUNK), 1)
    lane = lax.broadcasted_iota(jnp.int32, (1, D_GROUP), 1)

    def gates_expanded(b, rows):
        return _split_dot(gc_ref[b, rows] + bc_ref[...], ex_ref[...])

    def state_update(c_ref, nm_ref, b, total, cum, i_x, k, vb):
        m_prev = nm_ref[b, 1:2]
        g = (total - cum) + i_x
        m_new = jnp.maximum(total + m_prev, jnp.max(g, axis=0, keepdims=True))
        wk = jnp.exp(g - m_new) * k
        decay = jnp.exp(total + m_prev - m_new)
        c_ref[b] = c_ref[b] * decay + _dot_tn(wk.astype(BF16), vb) * bd
        nm_ref[b, 0:1] = decay * nm_ref[b, 0:1] + jnp.sum(wk, axis=0, keepdims=True)
        nm_ref[b, 1:2] = m_new

    @pl.when(p == 0)
    def _():
        @pl.when(i == 0)
        def _():
            cbw_ref[...] = jnp.zeros_like(cbw_ref)
            nmbw_ref[...] = jnp.zeros_like(nmbw_ref)

        blk = nblk - 1 - i
        for c in reversed(range(cb)):
            rows = slice(c * CHUNK, (c + 1) * CHUNK)
            for b in range(bsz):
                csave_ref[b, blk * cb + c] = _compact(cbw_ref[b])
                nmsave_ref[b, blk * cb + c] = nmbw_ref[b]
                gx = gates_expanded(b, rows)
                cum = _split_dot_left(ut, _log_sigmoid(gx[:, 768:1024]))
                k = k_ref[b, rows] * (HEAD_DIM ** -0.5)
                state_update(cbw_ref, nmbw_ref, b, cum[0:1], cum, gx[:, 512:768], k, v_ref[b, rows].astype(BF16))

    def chunk_out(b, rows, cidx):
        q = q_ref[b, rows]
        k = k_ref[b, rows] * (HEAD_DIM ** -0.5)
        qb, kb, vb = q.astype(BF16), k.astype(BF16), v_ref[b, rows].astype(BF16)
        s_all = _dot_nt(qb, _stack_heads(kb, masks))
        vaug = jnp.concatenate([_stack_heads(vb, masks), ones_ref[...]], axis=1)
        gx = gates_expanded(b, rows)
        graw = gr_ref[b, :, rows] + br_ref[...]
        gls = _log_sigmoid(graw)
        cum_r_fw = _split_dot(gls, ut)
        cum_r_bw = _split_dot(gls, lt)
        ccomp = csave_ref[b, cidx]
        nmb = nmsave_ref[b, cidx]

        def direction(i_x, f_x, tri, cum_r, i_row0, f_row0, mask, c_state, n_vec, m_prev, total_row):
            cum = _split_dot_left(tri, _log_sigmoid(f_x))
            total = cum[total_row:total_row + 1]
            inter = cum + m_prev
            ps, rmax = [], []
            dms = []
            for h in range(N_HEADS):
                a_col = cum[:, h * HEAD_DIM:h * HEAD_DIM + 1]
                dm = a_col - cum_r[f_row0 + h:f_row0 + h + 1] + graw[i_row0 + h:i_row0 + h + 1]
                dm = jnp.where(mask, dm, -jnp.inf)
                dms.append(dm)
                rmax.append(jnp.max(dm, axis=-1, keepdims=True))
            rmax256 = jnp.where(lane < 64, rmax[0], jnp.where(lane < 128, rmax[1],
                                jnp.where(lane < 192, rmax[2], rmax[3])))
            m_row = jnp.maximum(inter, rmax256)
            for h in range(N_HEADS):
                m_h = m_row[:, h * HEAD_DIM:h * HEAD_DIM + 1]
                ps.append(s_all[:, h * CHUNK:(h + 1) * CHUNK] * jnp.exp(dms[h] - m_h))
            pmat = jnp.concatenate(ps, axis=1).astype(BF16)
            nd = jnp.dot(pmat, vaug, preferred_element_type=F32)
            w_inter = jnp.exp(inter - m_row)
            qc = jnp.dot(qb, c_state.astype(BF16), preferred_element_type=F32)
            qn = _split_dot(q * n_vec, obd_ref[...])
            num = nd[:, :D_GROUP] + w_inter * qc
            den = nd[:, D_GROUP:] + w_inter * qn
            hdir = num / jnp.maximum(jnp.abs(den), jnp.exp(-m_row))
            return hdir, total, cum

        h_fw, tot_fw, cum_fw = direction(gx[:, 0:256], gx[:, 256:512], lt, cum_r_fw, 0, 4, ri >= ci,
                                         cfw_ref[b], nmfw_ref[b, 0:1], nmfw_ref[b, 1:2], CHUNK - 1)
        h_bw, _, _ = direction(gx[:, 512:768], gx[:, 768:1024], ut, cum_r_bw, 8, 12, ci >= ri,
                               _expand(ccomp, bd), nmb[0:1], nmb[1:2], 0)
        y = _head_norm(h_fw + h_bw, avg_ref[...], gn_ref[...])
        o_ref[b, rows] = _sigmoid(og_ref[b, rows]) * y
        state_update(cfw_ref, nmfw_ref, b, tot_fw, cum_fw, gx[:, 0:256], k, vb)

    @pl.when(p == 1)
    def _():
        @pl.when(i == 0)
        def _():
            cfw_ref[...] = jnp.zeros_like(cfw_ref)
            nmfw_ref[...] = jnp.zeros_like(nmfw_ref)

        for c in range(cb):
            rows = slice(c * CHUNK, (c + 1) * CHUNK)
            for b in range(bsz):
                chunk_out(b, rows, i * cb + c)


def _mlstm(proj, qk, gates_row, gate_b, gn_w):
    bsz, l, _ = proj.shape
    nc = l // CHUNK
    cb = 2 if nc % 2 == 0 else 1
    nblk = nc // cb
    tl = cb * CHUNK
    bd_np = _block_diag_mask()
    bd = jnp.asarray(bd_np)
    avg = jnp.asarray(bd_np / HEAD_DIM, BF16)
    obd = jnp.asarray(bd_np, BF16)
    ex = np.zeros((LANES, 4 * D_GROUP), np.float32)
    for j in range(16):
        typ, h = divmod(j, N_HEADS)
        ex[j, typ * D_GROUP + h * HEAD_DIM: typ * D_GROUP + (h + 1) * HEAD_DIM] = 1.0
    idx = np.arange(CHUNK)
    lt = (idx[None, :] <= idx[:, None]).astype(np.float32)
    ones_st = np.repeat(np.repeat(np.eye(N_HEADS, dtype=np.float32), CHUNK, 0), HEAD_DIM, 1)
    gb = gate_b.astype(F32).reshape(16)
    bias_col = jnp.pad(gb, (0, LANES - 16)).reshape(1, LANES)
    bias_row = jnp.broadcast_to(gb.reshape(16, 1), (16, CHUNK))

    def both(arr_col, width=D_GROUP):
        return pl.BlockSpec((bsz, tl, width), lambda p, i: (0, i + (1 - p) * (nblk - 1 - 2 * i), arr_col))

    def fwd_only(arr_col):
        return pl.BlockSpec((bsz, tl, D_GROUP), lambda p, i: (0, p * i, arr_col))

    def const(shape):
        return pl.BlockSpec(shape, lambda p, i: (0,) * len(shape))

    return pl.pallas_call(
        functools.partial(_mlstm_kernel, cb=cb, nblk=nblk),
        out_shape=jax.ShapeDtypeStruct((bsz, l, D_GROUP), F32), grid=(2, nblk),
        in_specs=[fwd_only(0), both(1), both(CB_MV), fwd_only(CB_MO), both(GATE_COL128, LANES),
                  pl.BlockSpec((bsz, 16, tl), lambda p, i: (0, 0, p * i)),
                  const((1, LANES)), const((16, CHUNK)), const((LANES, 4 * D_GROUP)),
                  const((CHUNK, CHUNK)), const((CHUNK, CHUNK)), const((4 * CHUNK, D_GROUP)),
                  const((D_GROUP, D_GROUP)), const((D_GROUP, D_GROUP)), const((D_GROUP, D_GROUP)),
                  const((1, D_GROUP))],
        out_specs=pl.BlockSpec((bsz, tl, D_GROUP), lambda p, i: (0, p * i, 0)),
        scratch_shapes=[pltpu.VMEM((bsz, D_GROUP, D_GROUP), F32), pltpu.VMEM((bsz, D_GROUP, D_GROUP), F32),
                        pltpu.VMEM((bsz, 8, D_GROUP), F32), pltpu.VMEM((bsz, 8, D_GROUP), F32),
                        pltpu.VMEM((bsz, nc, HEAD_DIM, D_GROUP), F32), pltpu.VMEM((bsz, nc, 8, D_GROUP), F32)],
        compiler_params=_cparams(("arbitrary", "arbitrary"), 48), name="mlstm",
    )(qk, qk, proj, proj, proj, gates_row, bias_col, bias_row, jnp.asarray(ex, BF16), jnp.asarray(lt, BF16),
      jnp.asarray(lt.T, BF16), jnp.asarray(ones_st, BF16), obd, bd, avg, gn_w.reshape(1, D_GROUP))


def _s5_kernel(u_ref, mt_ref, bg_ref, cg_ref, pa_ref, pb_ref, o_ref, *, nsteps):
    ub = u_ref[0].astype(BF16)
    e = jnp.dot(ub, bg_ref[0], preferred_element_type=F32)
    r = e.shape[0]
    row = lax.broadcasted_iota(jnp.int32, (r, LANES), 0)
    xf, xb = e[:, :LANES], e[:, LANES:]
    pa, pb = pa_ref[0], pb_ref[0]
    for s in range(nsteps):
        sh = 1 << s
        a_f, b_f = pa[s:s + 1, :LANES], pb[s:s + 1, :LANES]
        a_b, b_b = pa[s:s + 1, LANES:], pb[s:s + 1, LANES:]
        yf = jnp.where(row >= sh, pltpu.roll(xf, sh, 0), 0.0)
        yb = jnp.where(row < r - sh, pltpu.roll(xb, r - sh, 0), 0.0)
        xf = xf + a_f * yf + b_f * pltpu.roll(yf, LANES // 2, 1)
        xb = xb + a_b * yb + b_b * pltpu.roll(yb, LANES // 2, 1)
    sprev = jnp.where(row >= 1, pltpu.roll(xf, 1, 0), 0.0)
    snext = jnp.where(row < r - 1, pltpu.roll(xb, r - 1, 0), 0.0)
    st = jnp.concatenate([sprev, snext], axis=1).astype(BF16)
    o_ref[0] = (jnp.dot(ub, mt_ref[0], preferred_element_type=F32)
                + jnp.dot(st, cg_ref[0], preferred_element_type=F32))


def _s5_tables(a_re, a_im, log_dt, b_re, b_im, c_re, c_im, d_skip, tc, nsteps):
    g, p, ch = S5_GROUPS, S5_STATE, S5_CH
    hp = lax.Precision.HIGHEST
    are, aim = a_re.astype(F32), a_im.astype(F32)
    delta = jnp.exp(log_dt.astype(F32))[..., None]
    lre, lim = are * delta, aim * delta

    class Cx:
        def __init__(self, re, im):
            self.re, self.im = re, im

        def __mul__(self, o):
            return Cx(self.re * o.re - self.im * o.im, self.re * o.im + self.im * o.re)

        def __getitem__(self, idx):
            return Cx(self.re[idx], self.im[idx])

    def apow(n):
        n = jnp.asarray(n, F32)[None, None, :, None]
        mag, ang = jnp.exp(lre[:, :, None, :] * n), lim[:, :, None, :] * n
        return Cx(mag * jnp.cos(ang), mag * jnp.sin(ang))

    abr, abi = jnp.exp(lre) * jnp.cos(lim), jnp.exp(lre) * jnp.sin(lim)
    den = are * are + aim * aim
    quo = Cx(((abr - 1.0) * are + abi * aim) / den, (abi * are - (abr - 1.0) * aim) / den)
    b_bar = quo[..., None] * Cx(b_re.astype(F32)[None], b_im.astype(F32)[None])
    c = Cx(c_re.astype(F32), c_im.astype(F32))
    taus = np.arange(tc)
    cp = c[:, :, None] * apow(taus)[:, :, :, None, :]
    kk = (jnp.einsum("dgtop,dgpi->dgtoi", cp.re, b_bar.re, precision=hp)
          - jnp.einsum("dgtop,dgpi->dgtoi", cp.im, b_bar.im, precision=hp))
    dsk = d_skip.astype(F32).reshape(g, ch)[:, :, None] * jnp.eye(ch, dtype=F32)[None]
    kdiag = kk[0][:, 0] + kk[1][:, 0] + dsk
    lags = jnp.concatenate([kk[1][:, :0:-1], kdiag[:, None], kk[0][:, 1:]], axis=1)
    diff = taus[None, :] - taus[:, None]
    sel = (diff[None] + (tc - 1) == np.arange(2 * tc - 1)[:, None, None]).astype(np.float32)
    mt = jnp.einsum("jst,gjoi->gsito", jnp.asarray(sel), lags, precision=hp).reshape(g, tc * ch, tc * ch)

    zf = apow(tc - 1 - taus)[0][..., None] * b_bar[0][:, None]
    zb = apow(taus)[1][..., None] * b_bar[1][:, None]

    def to_rows(z):
        return jnp.transpose(z, (0, 1, 3, 2)).reshape(g, tc * ch, p)

    bg = jnp.concatenate([to_rows(zf.re), to_rows(zf.im), to_rows(zb.re), to_rows(zb.im)], axis=-1)

    yf = c[0][:, None] * apow(taus + 1)[0][:, :, None, :]
    yb = c[1][:, None] * apow(tc - taus)[1][:, :, None, :]

    def to_cols(z):
        return jnp.transpose(z, (0, 3, 1, 2)).reshape(g, p, tc * ch)

    cg = jnp.concatenate([to_cols(yf.re), -to_cols(yf.im), to_cols(yb.re), -to_cols(yb.im)], axis=1)

    steps = tc * (2.0 ** np.arange(nsteps))
    pw = apow(steps)
    re0, im0, re1, im1 = pw.re[0], pw.im[0], pw.re[1], pw.im[1]
    pa = jnp.concatenate([re0, re0, re1, re1], axis=-1)
    pb = jnp.concatenate([-im0, im0, -im1, im1], axis=-1)
    pad = (-nsteps) % 8
    pa = jnp.pad(pa, ((0, 0), (0, pad), (0, 0)))
    pb = jnp.pad(pb, ((0, 0), (0, pad), (0, 0)))
    return mt.astype(BF16), bg.astype(BF16), cg.astype(BF16), pa, pb


def _s5_glu_kernel(y_ref, w_ref, o_ref):
    y = y_ref[...]
    z = 0.5 * y * (1.0 + jnp.tanh(math.sqrt(2.0 / math.pi) * (y + 0.044715 * (y * y * y))))
    o_ref[...] = z * _sigmoid(jnp.dot(z.astype(BF16), w_ref[...], preferred_element_type=F32))


def _s5(proj, a_re, a_im, log_dt, b_re, b_im, c_re, c_im, d_skip, w_glu):
    bsz, l, _ = proj.shape
    tc = S5_TC
    r = l // tc
    nsteps = max(1, int(math.ceil(math.log2(r))))
    w = tc * S5_CH
    mt, bg, cg, pa, pb = _s5_tables(a_re, a_im, log_dt, b_re, b_im, c_re, c_im, d_skip, tc, nsteps)
    u = proj[:, :, CB_S5 * D_GROUP:(CB_S5 + 1) * D_GROUP]
    ug = jnp.transpose(u.reshape(bsz, r, tc, S5_GROUPS, S5_CH), (3, 0, 1, 2, 4)).reshape(S5_GROUPS, bsz * r, w)
    ns8 = pa.shape[1]
    yg = pl.pallas_call(
        functools.partial(_s5_kernel, nsteps=nsteps),
        out_shape=jax.ShapeDtypeStruct((S5_GROUPS, bsz * r, w), F32), grid=(S5_GROUPS, bsz),
        in_specs=[pl.BlockSpec((1, r, w), lambda g, b: (g, b, 0)),
                  pl.BlockSpec((1, w, w), lambda g, b: (g, 0, 0)),
                  pl.BlockSpec((1, w, D_GROUP), lambda g, b: (g, 0, 0)),
                  pl.BlockSpec((1, D_GROUP, w), lambda g, b: (g, 0, 0)),
                  pl.BlockSpec((1, ns8, D_GROUP), lambda g, b: (g, 0, 0)),
                  pl.BlockSpec((1, ns8, D_GROUP), lambda g, b: (g, 0, 0))],
        out_specs=pl.BlockSpec((1, r, w), lambda g, b: (g, b, 0)),
        compiler_params=_cparams(("parallel", "parallel"), 48), name="s5_ssm")(ug, mt, bg, cg, pa, pb)
    y = jnp.transpose(yg.reshape(S5_GROUPS, bsz, r, tc, S5_CH), (1, 2, 3, 0, 4)).reshape(bsz * l, D_GROUP)
    t = bsz * l
    tm = _tile(t, 2048)
    out = pl.pallas_call(
        _s5_glu_kernel, out_shape=jax.ShapeDtypeStruct((t, D_GROUP), F32), grid=(t // tm,),
        in_specs=[pl.BlockSpec((tm, D_GROUP), lambda i: (i, 0)), pl.BlockSpec((D_GROUP, D_GROUP), lambda i: (0, 0))],
        out_specs=pl.BlockSpec((tm, D_GROUP), lambda i: (i, 0)),
        compiler_params=_cparams(("parallel",)), name="s5_glu")(y, w_glu.astype(BF16))
    return out.reshape(bsz, l, D_GROUP)


def _hy_filter_kernel(z_ref, w1_ref, b1_ref, w2_ref, b2_ref, w3_ref, fr_ref, dec_ref, h_ref, ss_ref):
    i = pl.program_id(0)
    hp = lax.Precision.HIGHEST
    fr = fr_ref[...]
    a = jnp.sin(fr * (jnp.dot(z_ref[...], w1_ref[...], precision=hp, preferred_element_type=F32) + b1_ref[...]))
    a = jnp.sin(fr * (jnp.dot(a, w2_ref[...], precision=hp, preferred_element_type=F32) + b2_ref[...]))
    h = jnp.dot(a, w3_ref[...], precision=hp, preferred_element_type=F32)
    df, db = dec_ref[:, :D_GROUP], dec_ref[:, D_GROUP:]
    h = h * jnp.concatenate([df, df, db, db], axis=1)

    @pl.when(i == 0)
    def _():
        ss_ref[...] = jnp.zeros_like(ss_ref)

    ss_ref[...] += jnp.sum(h * h, axis=0, keepdims=True)
    nhalf = h.shape[1] // 2
    row = lax.broadcasted_iota(jnp.int32, h.shape, 0)
    col = lax.broadcasted_iota(jnp.int32, h.shape, 1)
    h_ref[...] = jnp.where((row == 0) & (col >= nhalf) & (i == 0), 0.0, h)


def _hy_filters(l, w1, b1, w2, b2, w3, freq):
    t = np.linspace(0.0, 1.0, l)[:, None]
    w = 2.0 * np.pi * np.arange(l, dtype=np.float64)[:, None] / l
    bands = np.linspace(1e-4, HY_BANDS - 1, HY_BANDS)[None, :]
    z = np.concatenate([t, np.cos(bands * w), -np.sin(bands * w)], axis=-1)
    max_decay = math.log(HY_TARGET) / HY_FAST_DECAY
    min_decay = math.log(HY_TARGET) / HY_SLOW_DECAY
    rates = np.abs(np.linspace(min_decay, max_decay, D_GROUP))
    dec = np.exp(-t * rates)
    rev = np.concatenate([[0], np.arange(l - 1, 0, -1)])
    half = LANES // 2
    zz = np.zeros((l, LANES))
    zz[:, :HY_EMB] = z
    zz[:, half:half + HY_EMB] = z[rev]
    zz = jnp.asarray(zz, F32)
    dec2 = jnp.asarray(np.concatenate([dec, dec[rev]], axis=1), F32)

    def two(m):
        m = m.astype(F32)
        top = jnp.pad(m, ((0, half - m.shape[0]), (0, half - m.shape[1])))
        zero = jnp.zeros_like(top)
        return jnp.concatenate([jnp.concatenate([top, zero], 1), jnp.concatenate([zero, top], 1)], 0)

    def twice(v):
        v = jnp.pad(v.astype(F32), (0, half - v.shape[0]))
        return jnp.concatenate([v, v]).reshape(1, LANES)

    w3r = w3.astype(F32).reshape(HY_FFN, HY_ORDER, 2, D_GROUP)
    nhalf = HY_ORDER * D_GROUP
    w3f = jnp.pad(w3r[:, :, 0].reshape(HY_FFN, nhalf), ((0, half - HY_FFN), (0, 0)))
    w3b = jnp.pad(w3r[:, :, 1].reshape(HY_FFN, nhalf), ((0, half - HY_FFN), (0, 0)))
    zero = jnp.zeros_like(w3f)
    w3p = jnp.concatenate([jnp.concatenate([w3f, zero], 1), jnp.concatenate([zero, w3b], 1)], 0)
    nout = 2 * nhalf
    tl = _tile(l, 512)

    def const(shape):
        return pl.BlockSpec(shape, lambda i: (0, 0))

    return pl.pallas_call(
        _hy_filter_kernel,
        out_shape=(jax.ShapeDtypeStruct((l, nout), F32), jax.ShapeDtypeStruct((1, nout), F32)), grid=(l // tl,),
        in_specs=[pl.BlockSpec((tl, LANES), lambda i: (i, 0)), const((LANES, LANES)), const((1, LANES)),
                  const((LANES, LANES)), const((1, LANES)), const((LANES, nout)), const((1, LANES)),
                  pl.BlockSpec((tl, 2 * D_GROUP), lambda i: (i, 0))],
        out_specs=(pl.BlockSpec((tl, nout), lambda i: (i, 0)), const((1, nout))),
        compiler_params=_cparams(("arbitrary",)), name="hyena_filter_mlp",
    )(zz, two(w1), twice(b1), two(w2), twice(b2), w3p, twice(freq), dec2)


def _dft_consts(na):
    nb = FFT_NB
    n = na * nb
    ia = np.arange(na, dtype=np.float64)
    th = 2.0 * np.pi * np.outer(ia, ia) / na
    c1, s1 = np.cos(th), np.sin(th)
    eye8 = np.eye(8)
    fa_full = np.concatenate([c1, -s1], axis=0)
    g_full = np.kron(fa_full, eye8)
    g_half = np.kron(fa_full[:, : na // 2], eye8)
    g_out = np.kron(np.concatenate([c1[: na // 2], -s1[: na // 2]], axis=1) / n, eye8)
    ib = np.arange(nb, dtype=np.float64)
    ph = 2.0 * np.pi * np.outer(ib, ib) / nb
    c2, s2 = np.cos(ph), np.sin(ph)
    fb = np.block([[c2, s2], [-s2, c2]])
    fbc = np.block([[c2, -s2], [s2, c2]])
    ps = 2.0 * np.pi * np.outer(ia, ib) / n
    twr = np.broadcast_to(np.cos(ps)[:, :, None], (na, nb, LANES))
    twi = np.broadcast_to(-np.sin(ps)[:, :, None], (na, nb, LANES))
    as_bf = lambda x: jnp.asarray(x, BF16)
    return dict(g_full=as_bf(g_full), g_half=as_bf(g_half), g_out=as_bf(g_out), fb=as_bf(fb), fbc=as_bf(fbc),
                twr=jnp.asarray(twr, F32), twi=jnp.asarray(twi, F32))


def _lane_tile(x, reps):
    return x if reps == 1 else jnp.concatenate([x] * reps, axis=-1)


def _hy_spec_kernel(a_ref, twr_ref, twi_ref, fb_ref, ss_ref, o_ref, *, kb, reps):
    scale = lax.rsqrt(ss_ref[...])
    for j in range(kb):
        ar, ai = a_ref[0, j], a_ref[1, j]
        twr, twi = _lane_tile(twr_ref[j], reps), _lane_tile(twi_ref[j], reps)
        br = twr * ar - twi * ai
        bi = twr * ai + twi * ar
        x = jnp.dot(fb_ref[...], jnp.concatenate([br, bi], axis=0).astype(BF16), preferred_element_type=F32)
        o_ref[0, j] = x[:FFT_NB] * scale
        o_ref[1, j] = x[FFT_NB:] * scale


def _hy_mid_kernel(a_ref, h_ref, twr_ref, twi_ref, fb_ref, fbc_ref, o_ref, *, kb, reps):
    for j in range(kb):
        ar, ai = a_ref[0, 0, j], a_ref[0, 1, j]
        twr, twi = _lane_tile(twr_ref[j], reps), _lane_tile(twi_ref[j], reps)
        br = twr * ar - twi * ai
        bi = twr * ai + twi * ar
        x = jnp.dot(fb_ref[...], jnp.concatenate([br, bi], axis=0).astype(BF16), preferred_element_type=F32)
        xr, xi = x[:FFT_NB], x[FFT_NB:]
        hr, hi = h_ref[0, j], h_ref[1, j]
        yr = xr * hr - xi * hi
        yi = xr * hi + xi * hr
        z = jnp.dot(fbc_ref[...], jnp.concatenate([yr, yi], axis=0).astype(BF16), preferred_element_type=F32)
        zr, zi = z[:FFT_NB], z[FFT_NB:]
        o_ref[0, 0, j] = twr * zr + twi * zi
        o_ref[0, 1, j] = twr * zi - twi * zr


def _hy_dft1_kernel(g_ref, x_ref, o_ref, *, qb):
    na_in, c = x_ref.shape[1], x_ref.shape[4]
    na = o_ref.shape[2]
    for q in range(qb):
        x = x_ref[0, :, q].reshape(na_in * 8, c).astype(BF16)
        a = jnp.dot(g_ref[...], x, preferred_element_type=F32)
        o_ref[0, :, :, q] = a.reshape(2, na, 8, c)


def _hy_fdft1_kernel(g_ref, xt_ref, xb_ref, o_ref, *, qb):
    nah, c = xt_ref.shape[1], xt_ref.shape[4]
    na = o_ref.shape[2]
    kh = nah * 8
    for q in range(qb):
        xt = xt_ref[0, :, q].reshape(kh, c).astype(BF16)
        xb = xb_ref[0, :, q].reshape(kh, c).astype(BF16)
        a = (jnp.dot(g_ref[:, :kh], xt, preferred_element_type=F32)
             + jnp.dot(g_ref[:, kh:], xb, preferred_element_type=F32))
        o_ref[0, :, :, q] = a.reshape(2, na, 8, c)


def _hy_dft1(g, x5, ncol, name):
    bsz, na_in, nq = x5.shape[:3]
    na = g.shape[0] // 16
    c = D_GROUP
    qb = FFT_QB
    return pl.pallas_call(
        functools.partial(_hy_dft1_kernel, qb=qb),
        out_shape=jax.ShapeDtypeStruct((bsz, 2, na, nq, 8, ncol * c), F32), grid=(bsz, ncol, nq // qb),
        in_specs=[pl.BlockSpec(g.shape, lambda b, j, q: (0, 0)),
                  pl.BlockSpec((1, na_in, qb, 8, c), lambda b, j, q: (b, 0, q, 0, j))],
        out_specs=pl.BlockSpec((1, 2, na, qb, 8, c), lambda b, j, q: (b, 0, 0, q, 0, j)),
        compiler_params=_cparams(("parallel", "parallel", "parallel"), 48), name=name)(g, x5)


def _hy_out_kernel(g_ref, z_ref, x_ref, v_ref, b_ref, o_ref, *, qb):
    na2, c = z_ref.shape[1] * z_ref.shape[2], z_ref.shape[5]
    nah = o_ref.shape[1]
    bias = b_ref[...].reshape(1, 1, c)
    for q in range(qb):
        z = z_ref[0, :, :, q].reshape(na2 * 8, c).astype(BF16)
        y = jnp.dot(g_ref[...], z, preferred_element_type=F32).reshape(nah, 8, c)
        o_ref[0, :, q] = x_ref[0, :, q] * (y + v_ref[0, :, q] * bias)


def _hyena(proj, conv_w, conv_b, w1, b1, w2, b2, w3, freq, bias):
    bsz, l, _ = proj.shape
    nb = FFT_NB
    na = 2 * l // nb
    nah = na // 2
    nq = nb // 8
    c = D_GROUP
    qb = FFT_QB
    dc = _dft_consts(na)
    pc = _shortconv(proj, CB_HV, 3, conv_w, conv_b, act=False)

    h, ss = _hy_filters(l, w1, b1, w2, b2, w3, freq)
    ncf = HY_ORDER * c
    ssn = ss[:, :ncf] + ss[:, ncf:]
    h5 = h.reshape(1, nah, nq, 8, 2 * ncf)
    ka = pl.pallas_call(
        functools.partial(_hy_fdft1_kernel, qb=qb),
        out_shape=jax.ShapeDtypeStruct((1, 2, na, nq, 8, ncf), F32), grid=(HY_ORDER, nq // qb),
        in_specs=[pl.BlockSpec(dc["g_full"].shape, lambda j, q: (0, 0)),
                  pl.BlockSpec((1, nah, qb, 8, c), lambda j, q: (0, 0, q, 0, j)),
                  pl.BlockSpec((1, nah, qb, 8, c), lambda j, q: (0, 0, q, 0, HY_ORDER + j))],
        out_specs=pl.BlockSpec((1, 2, na, qb, 8, c), lambda j, q: (0, 0, 0, q, 0, j)),
        compiler_params=_cparams(("parallel", "parallel"), 48), name="hyena_filter_dft1",
    )(dc["g_full"], h5, h5).reshape(2, na, nb, ncf)
    kb = 4 if na % 4 == 0 else 1
    reps = c // LANES
    tw = pl.BlockSpec((kb, nb, LANES), lambda j, k: (k, 0, 0))
    mat = pl.BlockSpec((2 * nb, 2 * nb), lambda j, k: (0, 0))
    hspec = pl.pallas_call(
        functools.partial(_hy_spec_kernel, kb=kb, reps=reps),
        out_shape=jax.ShapeDtypeStruct((2, na, nb, ncf), F32), grid=(ncf // c, na // kb),
        in_specs=[pl.BlockSpec((2, kb, nb, c), lambda j, k: (0, k, 0, j)), tw, tw, mat,
                  pl.BlockSpec((1, c), lambda j, k: (0, j))],
        out_specs=pl.BlockSpec((2, kb, nb, c), lambda j, k: (0, k, 0, j)),
        compiler_params=_cparams(("parallel", "parallel"), 48), name="hyena_filter_dft2",
    )(ka, dc["twr"], dc["twi"], dc["fb"], ssn)

    pc5 = pc.reshape(bsz, nah, nq, 8, 3 * c)

    def long_conv_gate(z5, order, xcol):
        a = _hy_dft1(dc["g_half"], z5, 1, "hyena_dft1").reshape(bsz, 2, na, nb, c)
        zmid = pl.pallas_call(
            functools.partial(_hy_mid_kernel, kb=kb, reps=reps),
            out_shape=jax.ShapeDtypeStruct((bsz, 2, na, nb, c), F32), grid=(bsz, na // kb),
            in_specs=[pl.BlockSpec((1, 2, kb, nb, c), lambda b, k: (b, 0, k, 0, 0)),
                      pl.BlockSpec((2, kb, nb, c), lambda b, k: (0, k, 0, order)), tw, tw, mat, mat],
            out_specs=pl.BlockSpec((1, 2, kb, nb, c), lambda b, k: (b, 0, k, 0, 0)),
            compiler_params=_cparams(("parallel", "parallel"), 48), name="hyena_dft_mid",
        )(a, hspec, dc["twr"], dc["twi"], dc["fb"], dc["fbc"])
        zmid = zmid.reshape(bsz, 2, na, nq, 8, c)
        sig = lambda col: pl.BlockSpec((1, nah, qb, 8, c), lambda b, q: (b, 0, q, 0, col))
        return pl.pallas_call(
            functools.partial(_hy_out_kernel, qb=qb),
            out_shape=jax.ShapeDtypeStruct((bsz, nah, nq, 8, c), F32), grid=(bsz, nq // qb),
            in_specs=[pl.BlockSpec(dc["g_out"].shape, lambda b, q: (0, 0)),
                      pl.BlockSpec((1, 2, na, qb, 8, c), lambda b, q: (b, 0, 0, q, 0, 0)),
                      sig(xcol), sig(0), pl.BlockSpec((1, c), lambda b, q: (0, 0))],
            out_specs=sig(0),
            compiler_params=_cparams(("parallel", "parallel"), 48), name="hyena_idft_gate",
        )(dc["g_out"], zmid, pc5, z5, bias[order].astype(F32).reshape(1, c))

    z1 = long_conv_gate(pc5, 0, 1)
    z2 = long_conv_gate(z1, 1, 2)
    return z2.reshape(bsz, l, c)


def _ffn_kernel(x_ref, w1_ref, w3_ref, w2_ref, lw_ref, lb_ref, o_ref, xb_ref, acc_ref, *, nf):
    f = pl.program_id(1)

    @pl.when(f == 0)
    def _():
        xb_ref[...] = x_ref[...].astype(BF16)
        acc_ref[...] = jnp.zeros_like(acc_ref)

    xb = xb_ref[...]
    a = jnp.dot(xb, w1_ref[...], preferred_element_type=F32)
    b = jnp.dot(xb, w3_ref[...], preferred_element_type=F32)
    acc_ref[...] += jnp.dot((_silu(a) * b).astype(BF16), w2_ref[...], preferred_element_type=F32)

    @pl.when(f == nf - 1)
    def _():
        o_ref[...] = _ln_core(DN_ALPHA * x_ref[...] + acc_ref[...], lw_ref[...], lb_ref[...])


def _ffn_ln(x, w1, w3, w2, lw, lb):
    t, d = x.shape
    ff = w1.shape[1]
    tm = _tile(t, 1024)
    tf = 512 if ff % 512 == 0 else (256 if ff % 256 == 0 else ff)
    nf = ff // tf
    vec = pl.BlockSpec((1, d), lambda i, f: (0, 0))
    return pl.pallas_call(
        functools.partial(_ffn_kernel, nf=nf),
        out_shape=jax.ShapeDtypeStruct((t, d), F32), grid=(t // tm, nf),
        in_specs=[pl.BlockSpec((tm, d), lambda i, f: (i, 0)),
                  pl.BlockSpec((d, tf), lambda i, f: (0, f)),
                  pl.BlockSpec((d, tf), lambda i, f: (0, f)),
                  pl.BlockSpec((tf, d), lambda i, f: (f, 0)), vec, vec],
        out_specs=pl.BlockSpec((tm, d), lambda i, f: (i, 0)),
        scratch_shapes=[pltpu.VMEM((tm, d), BF16), pltpu.VMEM((tm, d), F32)],
        compiler_params=_cparams(("parallel", "arbitrary"), 52), name="swiglu_ffn_ln",
    )(x, w1, w3, w2, lw.reshape(1, d), lb.reshape(1, d))


MOE_TB = 1024
MOE_SUB = 288
MOE_SLOT = 384
MOE_CUM = 256


def _moe_kernel(cnt_ref, x_ref, cmb_ref, cmbt_ref, lt_ref, ut_ref, w1_ref, w3_ref, w2_ref, lw_ref, lb_ref, o_ref,
                xb_ref, xs_ref, ys_ref, gs_ref, posc_ref, posr_ref, *, nf):
    i = pl.program_id(0)
    e = pl.program_id(1)
    f = pl.program_id(2)
    tb = x_ref.shape[0]
    count = cnt_ref[i * N_EXPERTS + e]
    npass = (count + (MOE_SUB - 1)) // MOE_SUB

    @pl.when((e == 0) & (f == 0))
    def _():
        xb_ref[...] = x_ref[...].astype(BF16)
        o_ref[...] = jnp.zeros_like(o_ref)
        ys_ref[...] = jnp.zeros_like(ys_ref)
        carry_c = jnp.zeros((1, LANES), F32)
        carry_r = jnp.zeros((N_EXPERTS, 1), F32)
        for c in range(tb // MOE_CUM):
            rows = slice(c * MOE_CUM, (c + 1) * MOE_CUM)
            mc = (cmb_ref[rows] > 0.0).astype(F32)
            inc = jnp.dot(lt_ref[...], mc.astype(BF16), preferred_element_type=F32) + carry_c
            posc_ref[rows] = jnp.where(mc > 0.0, inc - 1.0, -1.0)
            carry_c = inc[MOE_CUM - 1:MOE_CUM]
            mr = (cmbt_ref[:, rows] > 0.0).astype(F32)
            incr = jnp.dot(mr.astype(BF16), ut_ref[...], preferred_element_type=F32) + carry_r
            posr_ref[:, rows] = jnp.where(mr > 0.0, incr - 1.0, -1.0)
            carry_r = incr[:, MOE_CUM - 1:MOE_CUM]

    @pl.when(f == 0)
    def _():
        pos_row = posr_ref[pl.ds(e, 1), :]
        gate_row = cmbt_ref[pl.ds(e, 1), :]
        slot = lax.broadcasted_iota(jnp.int32, (MOE_SUB, tb), 0).astype(F32)

        def gather(j, carry):
            base = pl.multiple_of(j * MOE_SLOT, LANES)
            hit = pos_row == slot + (j * MOE_SUB).astype(F32)
            xs_ref[pl.ds(base, MOE_SUB), :] = jnp.dot(
                jnp.where(hit, 1.0, 0.0).astype(BF16), xb_ref[...], preferred_element_type=F32).astype(BF16)
            g = jnp.sum(jnp.where(hit, gate_row, 0.0), axis=1, keepdims=True)
            gs_ref[pl.ds(base, MOE_SUB), :] = jnp.broadcast_to(g, (MOE_SUB, LANES))
            ys_ref[pl.ds(base, MOE_SUB), :] = jnp.zeros((MOE_SUB, ys_ref.shape[1]), F32)
            return carry

        lax.fori_loop(0, npass, gather, 0)

    def expert(j, carry):
        base = pl.multiple_of(j * MOE_SLOT, LANES)
        xs = xs_ref[pl.ds(base, MOE_SUB), :]
        a = jnp.dot(xs, w1_ref[0], preferred_element_type=F32)
        b = jnp.dot(xs, w3_ref[0], preferred_element_type=F32)
        hid = _silu(a) * b * gs_ref[pl.ds(base, MOE_SUB), 0:1]
        ys_ref[pl.ds(base, MOE_SUB), :] += jnp.dot(hid.astype(BF16), w2_ref[0], preferred_element_type=F32)
        return carry

    lax.fori_loop(0, npass, expert, 0)

    @pl.when(f == nf - 1)
    def _():
        lane = lax.broadcasted_iota(jnp.int32, (tb, LANES), 1)
        pos_col = jnp.sum(jnp.where(lane == e, posc_ref[...], 0.0), axis=1, keepdims=True)
        slot = lax.broadcasted_iota(jnp.int32, (tb, MOE_SLOT), 1)
        slot = jnp.where(slot < MOE_SUB, slot, -2 * tb).astype(F32)

        def scatter(j, carry):
            base = pl.multiple_of(j * MOE_SLOT, LANES)
            hit = pos_col == slot + (j * MOE_SUB).astype(F32)
            o_ref[...] += jnp.dot(jnp.where(hit, 1.0, 0.0).astype(BF16),
                                  ys_ref[pl.ds(base, MOE_SLOT), :].astype(BF16), preferred_element_type=F32)
            return carry

        lax.fori_loop(0, npass, scatter, 0)

        @pl.when(e == pl.num_programs(1) - 1)
        def _():
            o_ref[...] = _ln_core(DN_ALPHA * x_ref[...] + o_ref[...], lw_ref[...], lb_ref[...])


def _moe_ln(x, cmb, w1, w3, w2, lw, lb):
    t, d = x.shape
    ne, _, ff = w1.shape
    tb = _tile(t, MOE_TB)
    nb = t // tb
    tf = 896 if ff % 896 == 0 else ff
    nf = ff // tf
    max_pass = -(-tb // MOE_SUB)
    cmbt = jnp.transpose(cmb[:, :N_EXPERTS])
    counts = jnp.sum((cmb[:, :N_EXPERTS] > 0.0).reshape(nb, tb, N_EXPERTS), axis=1).astype(jnp.int32).reshape(-1)
    idx = np.arange(MOE_CUM)
    lt = jnp.asarray(idx[None, :] <= idx[:, None], BF16)
    grid_spec = pltpu.PrefetchScalarGridSpec(
        num_scalar_prefetch=1, grid=(nb, ne, nf),
        in_specs=[pl.BlockSpec((tb, d), lambda i, e, f, c: (i, 0)),
                  pl.BlockSpec((tb, LANES), lambda i, e, f, c: (i, 0)),
                  pl.BlockSpec((N_EXPERTS, tb), lambda i, e, f, c: (0, i)),
                  pl.BlockSpec((MOE_CUM, MOE_CUM), lambda i, e, f, c: (0, 0)),
                  pl.BlockSpec((MOE_CUM, MOE_CUM), lambda i, e, f, c: (0, 0)),
                  pl.BlockSpec((1, d, tf), lambda i, e, f, c: (e, 0, f)),
                  pl.BlockSpec((1, d, tf), lambda i, e, f, c: (e, 0, f)),
                  pl.BlockSpec((1, tf, d), lambda i, e, f, c: (e, f, 0)),
                  pl.BlockSpec((1, d), lambda i, e, f, c: (0, 0)),
                  pl.BlockSpec((1, d), lambda i, e, f, c: (0, 0))],
        out_specs=pl.BlockSpec((tb, d), lambda i, e, f, c: (i, 0)),
        scratch_shapes=[pltpu.VMEM((tb, d), BF16), pltpu.VMEM((max_pass * MOE_SLOT, d), BF16),
                        pltpu.VMEM((max_pass * MOE_SLOT, d), F32), pltpu.VMEM((max_pass * MOE_SLOT, LANES), F32),
                        pltpu.VMEM((tb, LANES), F32), pltpu.VMEM((N_EXPERTS, tb), F32)])
    return pl.pallas_call(
        functools.partial(_moe_kernel, nf=nf), out_shape=jax.ShapeDtypeStruct((t, d), F32), grid_spec=grid_spec,
        compiler_params=_cparams(("parallel", "arbitrary", "arbitrary"), 56), name="moe_routed",
    )(counts, x, cmb, cmbt, lt, jnp.transpose(lt), w1, w3, w2, lw.reshape(1, d), lb.reshape(1, d))


def _router_kernel(x_ref, rh_ref, rl_ref, o_ref):
    x = x_ref[...]
    xh = x.astype(BF16)
    xl = (x - xh.astype(F32)).astype(BF16)
    logits = (jnp.dot(xh, rh_ref[...], preferred_element_type=F32)
              + jnp.dot(xl, rh_ref[...], preferred_element_type=F32)
              + jnp.dot(xh, rl_ref[...], preferred_element_type=F32))
    lane = lax.broadcasted_iota(jnp.int32, logits.shape, 1).astype(F32)
    logits = jnp.where(lane < N_EXPERTS, logits, -jnp.inf)
    m1 = jnp.max(logits, axis=1, keepdims=True)
    i1 = jnp.min(jnp.where(logits == m1, lane, float(LANES)), axis=1, keepdims=True)
    rest = jnp.where(lane == i1, -jnp.inf, logits)
    m2 = jnp.max(rest, axis=1, keepdims=True)
    i2 = jnp.min(jnp.where(rest == m2, lane, float(LANES)), axis=1, keepdims=True)
    e2 = jnp.exp(m2 - m1)
    g1 = 1.0 / (1.0 + e2)
    g2 = e2 / (1.0 + e2)
    o_ref[...] = jnp.where(lane == i1, g1, 0.0) + jnp.where(lane == i2, g2, 0.0)


def _router(x, router):
    t, d = x.shape
    r = jnp.pad(router.astype(F32), ((0, 0), (0, LANES - N_EXPERTS)))
    rh = r.astype(BF16)
    rl = (r - rh.astype(F32)).astype(BF16)
    tm = _tile(t, 1024)
    return pl.pallas_call(
        _router_kernel, out_shape=jax.ShapeDtypeStruct((t, LANES), F32), grid=(t // tm,),
        in_specs=[pl.BlockSpec((tm, d), lambda i: (i, 0)), pl.BlockSpec((d, LANES), lambda i: (0, 0)),
                  pl.BlockSpec((d, LANES), lambda i: (0, 0))],
        out_specs=pl.BlockSpec((tm, LANES), lambda i: (i, 0)),
        compiler_params=_cparams(("parallel",)), name="moe_router")(x, rh, rl)


def _extended_w_in(w_in):
    w = w_in.astype(F32)
    scale = HEAD_DIM ** -0.5

    def rot_half(cols):
        c4 = cols.reshape(-1, N_HEADS, 2, HEAD_DIM // 2)
        return jnp.stack([-c4[:, :, 1], c4[:, :, 0]], axis=2).reshape(-1, D_GROUP)

    wq = w[:, 0:256]
    wk = w[:, 256:512] * scale
    main = jnp.concatenate([wq, wk, w[:, 512:3072]], axis=1)
    gates = jnp.pad(w[:, 3072:3088], ((0, 0), (0, LANES - 16)))
    ext = jnp.concatenate([main, rot_half(wq), rot_half(wk), gates], axis=1)
    return jnp.pad(ext, ((0, 0), (0, N_EXT - ext.shape[1]))).astype(BF16)


def kernel(x, ln_in_w, ln_in_b, w_in, w_out, ret_gn_w, s5_a_re, s5_a_im, s5_log_dt, s5_b_re, s5_b_im, s5_c_re, s5_c_im, s5_d, s5_w_glu, hy_conv_w, hy_conv_b, hy_w1, hy_b1, hy_w2, hy_b2, hy_w3, hy_freq, hy_bias, ml_conv_w, ml_conv_b, ml_gate_b, ml_gn_w, ln1_w, ln1_b, ln2_w, ln2_b, ffn_w1, ffn_w3, ffn_w2, moe_router, moe_w1, moe_w3, moe_w2):
    bsz, l, d = x.shape
    t = bsz * l
    cos_full, sin_full = _rope_tables(l)
    h = _layer_norm(x.reshape(t, d), ln_in_w, ln_in_b)
    for layer in range(DEPTH):
        proj = _mm(h, _extended_w_in(w_in[layer]), tm=1024, tn=1280, name="in_proj").reshape(bsz, l, N_EXT)
        y_ret = _retention(proj, ret_gn_w[layer], cos_full, sin_full)
        y_s5 = _s5(proj, s5_a_re[layer], s5_a_im[layer], s5_log_dt[layer], s5_b_re[layer], s5_b_im[layer],
                   s5_c_re[layer], s5_c_im[layer], s5_d[layer], s5_w_glu[layer])
        y_hy = _hyena(proj, hy_conv_w[layer], hy_conv_b[layer], hy_w1[layer], hy_b1[layer], hy_w2[layer],
                      hy_b2[layer], hy_w3[layer], hy_freq[layer], hy_bias[layer])
        qk = _shortconv(proj, CB_MQ, 2, ml_conv_w[layer], ml_conv_b[layer], act=True)
        gates_row = jnp.transpose(proj[:, :, GATE_COL128 * LANES:GATE_COL128 * LANES + 16], (0, 2, 1))
        y_ml = _mlstm(proj, qk, gates_row, ml_gate_b[layer], ml_gn_w[layer])
        ys = [y.reshape(t, D_GROUP) for y in (y_ret, y_s5, y_hy, y_ml)]
        h = _outproj_ln(ys, w_out[layer], h, ln1_w[layer], ln1_b[layer])
        j = layer // 2
        if layer % 2 == 0:
            h = _ffn_ln(h, ffn_w1[j].astype(BF16), ffn_w3[j].astype(BF16), ffn_w2[j].astype(BF16),
                        ln2_w[layer], ln2_b[layer])
        else:
            cmb = _router(h, moe_router[j])
            h = _moe_ln(h, cmb, moe_w1[j].astype(BF16), moe_w3[j].astype(BF16), moe_w2[j].astype(BF16),
                        ln2_w[layer], ln2_b[layer])
    return h.reshape(bsz, l, d)
```

```python
import functools
import math

import numpy as np
import jax
import jax.numpy as jnp
from jax import lax
from jax.experimental import pallas as pl
from jax.experimental.pallas import tpu as pltpu

F32 = jnp.float32
BF16 = jnp.bfloat16

D_MODEL = 1024
DEPTH = 2
D_GROUP = 256
HEAD_DIM = 64
N_HEADS = 4
CHUNK = 128
S5_CH = 16
S5_GROUPS = 16
S5_STATE = 64
HY_ORDER = 2
HY_EMB = 33
HY_BANDS = 16
HY_FFN = 64
HY_FAST_DECAY = 0.3
HY_SLOW_DECAY = 1.5
HY_TARGET = 1e-2
N_EXPERTS = 8
ROPE_BASE = 10000.0
EPS = 1e-5
DN_ALPHA = (2 * DEPTH) ** 0.25

LANES = 128
S5_TC = 32
FFT_NB = 256
FFT_QB = 2
N_EXT = 3840

CB_RQ, CB_RK, CB_RV, CB_RG, CB_S5, CB_HV, CB_HX1, CB_HX2 = 0, 1, 2, 3, 4, 5, 6, 7
CB_MQ, CB_MK, CB_MV, CB_MO, CB_RQR, CB_RKR = 8, 9, 10, 11, 12, 13
GATE_COL128 = 28


def _cparams(sem, vmem_mb=None):
    kw = dict(dimension_semantics=sem)
    if vmem_mb is not None:
        kw["vmem_limit_bytes"] = vmem_mb * 1024 * 1024
    return pltpu.CompilerParams(**kw)


def _tile(n, pref):
    return pref if n % pref == 0 else n


def _split_dot(x, m, parts=3):
    acc = None
    r = x
    for _ in range(parts):
        hi = r.astype(BF16)
        t = jnp.dot(hi, m, preferred_element_type=F32)
        acc = t if acc is None else acc + t
        r = r - hi.astype(F32)
    return acc


def _split_dot_left(m, x, parts=3):
    acc = None
    r = x
    for _ in range(parts):
        hi = r.astype(BF16)
        t = jnp.dot(m, hi, preferred_element_type=F32)
        acc = t if acc is None else acc + t
        r = r - hi.astype(F32)
    return acc


def _dot_nt(a, b):
    return lax.dot_general(a, b, (((1,), (1,)), ((), ())), preferred_element_type=F32)


def _dot_tn(a, b):
    return lax.dot_general(a, b, (((0,), (0,)), ((), ())), preferred_element_type=F32)


def _sigmoid(x):
    return 1.0 / (1.0 + jnp.exp(-x))


def _silu(x):
    return x * _sigmoid(x)


def _log_sigmoid(x):
    return jnp.minimum(x, 0.0) - jnp.log(1.0 + jnp.exp(-jnp.abs(x)))


def _head_masks(dtype):
    lane = lax.broadcasted_iota(jnp.int32, (1, D_GROUP), 1)
    return [((lane >= h * HEAD_DIM) & (lane < (h + 1) * HEAD_DIM)).astype(dtype) for h in range(N_HEADS)]


def _ln_core(x, w, b):
    mu = jnp.mean(x, -1, keepdims=True)
    xc = x - mu
    var = jnp.mean(xc * xc, -1, keepdims=True)
    return xc * lax.rsqrt(var + EPS) * w + b


def _ln_kernel(x_ref, w_ref, b_ref, o_ref):
    o_ref[...] = _ln_core(x_ref[...], w_ref[...], b_ref[...])


def _layer_norm(x, w, b):
    t, d = x.shape
    tm = _tile(t, 512)
    row = pl.BlockSpec((tm, d), lambda i: (i, 0))
    vec = pl.BlockSpec((1, d), lambda i: (0, 0))
    return pl.pallas_call(_ln_kernel, out_shape=jax.ShapeDtypeStruct((t, d), F32), grid=(t // tm,),
                          in_specs=[row, vec, vec], out_specs=row,
                          compiler_params=_cparams(("parallel",)), name="layer_norm")(x, w.reshape(1, d), b.reshape(1, d))


def _mm_kernel(a_ref, b_ref, o_ref):
    o_ref[...] = jnp.dot(a_ref[...].astype(BF16), b_ref[...], preferred_element_type=F32).astype(o_ref.dtype)


def _mm(a, b, tm=1024, tn=1024, out_dtype=F32, name="matmul"):
    m, k = a.shape
    n = b.shape[1]
    tm, tn = _tile(m, tm), _tile(n, tn)
    return pl.pallas_call(
        _mm_kernel, out_shape=jax.ShapeDtypeStruct((m, n), out_dtype), grid=(m // tm, n // tn),
        in_specs=[pl.BlockSpec((tm, k), lambda i, j: (i, 0)), pl.BlockSpec((k, tn), lambda i, j: (0, j))],
        out_specs=pl.BlockSpec((tm, tn), lambda i, j: (i, j)),
        compiler_params=_cparams(("parallel", "arbitrary"), 48), name=name)(a, b)


def _outproj_ln_kernel(y0_ref, y1_ref, y2_ref, y3_ref, w_ref, h_ref, lw_ref, lb_ref, o_ref):
    mix = None
    for g, y_ref in enumerate((y0_ref, y1_ref, y2_ref, y3_ref)):
        part = jnp.dot(y_ref[...].astype(BF16), w_ref[g * D_GROUP:(g + 1) * D_GROUP, :], preferred_element_type=F32)
        mix = part if mix is None else mix + part
    o_ref[...] = _ln_core(DN_ALPHA * h_ref[...] + mix, lw_ref[...], lb_ref[...])


def _outproj_ln(ys, w_out, h, lw, lb):
    t, d = h.shape
    tm = _tile(t, 1024)
    grp = pl.BlockSpec((tm, D_GROUP), lambda i: (i, 0))
    row = pl.BlockSpec((tm, d), lambda i: (i, 0))
    vec = pl.BlockSpec((1, d), lambda i: (0, 0))
    return pl.pallas_call(
        _outproj_ln_kernel, out_shape=jax.ShapeDtypeStruct((t, d), F32), grid=(t // tm,),
        in_specs=[grp, grp, grp, grp, pl.BlockSpec((d, d), lambda i: (0, 0)), row, vec, vec], out_specs=row,
        compiler_params=_cparams(("parallel",), 48), name="out_proj_ln",
    )(*ys, w_out.astype(BF16), h, lw.reshape(1, d), lb.reshape(1, d))


def _shortconv_kernel(x_ref, xp_ref, xn_ref, w_ref, b_ref, o_ref, *, nt, act):
    i = pl.program_id(1)
    x = x_ref[0]
    tl = x.shape[0]
    row = lax.broadcasted_iota(jnp.int32, x.shape, 0)
    prev_row = jnp.where(i == 0, 0.0, xp_ref[0, 7:8, :])
    next_row = jnp.where(i == nt - 1, 0.0, xn_ref[0, 0:1, :])
    x_prev = jnp.where(row == 0, prev_row, pltpu.roll(x, 1, 0))
    x_next = jnp.where(row == tl - 1, next_row, pltpu.roll(x, tl - 1, 0))
    w = w_ref[0]
    y = b_ref[0, 0:1] + x_prev * w[0:1] + x * w[1:2] + x_next * w[2:3]
    if act:
        y = _silu(y)
    o_ref[0] = y


def _shortconv(proj, col0, nblk, w, b, act):
    bsz, l, _ = proj.shape
    tl = _tile(l, 1024)
    nt = l // tl
    w3 = jnp.transpose(w.reshape(3, nblk, D_GROUP), (1, 0, 2))
    w3 = jnp.pad(w3, ((0, 0), (0, 5), (0, 0)))
    b3 = jnp.broadcast_to(b.reshape(nblk, 1, D_GROUP), (nblk, 8, D_GROUP))
    r8 = tl // 8
    return pl.pallas_call(
        functools.partial(_shortconv_kernel, nt=nt, act=act),
        out_shape=jax.ShapeDtypeStruct((bsz, l, nblk * D_GROUP), F32), grid=(bsz, nt, nblk),
        in_specs=[
            pl.BlockSpec((1, tl, D_GROUP), lambda bb, i, j: (bb, i, col0 + j)),
            pl.BlockSpec((1, 8, D_GROUP), lambda bb, i, j: (bb, jnp.maximum(i * r8 - 1, 0), col0 + j)),
            pl.BlockSpec((1, 8, D_GROUP), lambda bb, i, j: (bb, jnp.minimum((i + 1) * r8, l // 8 - 1), col0 + j)),
            pl.BlockSpec((1, 8, D_GROUP), lambda bb, i, j: (j, 0, 0)),
            pl.BlockSpec((1, 8, D_GROUP), lambda bb, i, j: (j, 0, 0)),
        ],
        out_specs=pl.BlockSpec((1, tl, D_GROUP), lambda bb, i, j: (bb, i, j)),
        compiler_params=_cparams(("parallel", "parallel", "parallel")), name="shortconv")(proj, proj, proj, w3, b3)


def _stack_heads(xb, masks):
    return jnp.concatenate([xb * masks[h] for h in range(N_HEADS)], axis=0)


def _compact(s):
    return s[0:64] + s[64:128] + s[128:192] + s[192:256]


def _expand(c, bd):
    return jnp.concatenate([c, c, c, c], axis=0) * bd


def _head_norm(o, avg, gn):
    mu = _split_dot(o, avg, parts=2)
    oc = o - mu
    var = _split_dot(oc * oc, avg, parts=2)
    return oc * lax.rsqrt(var + EPS) * gn


def _ret_kernel(q_ref, qr_ref, k_ref, kr_ref, v_ref, g_ref, cos_ref, sin_ref,
                dsym_ref, qdf_ref, qdb_ref, kdf_ref, kdb_ref, cdec_ref, bd_ref, avg_ref, gn_ref,
                o_ref, sfw_ref, sbw_ref, save_ref, *, cb, nblk):
    p = pl.program_id(0)
    i = pl.program_id(1)
    bsz = q_ref.shape[0]
    masks = _head_masks(BF16)
    bd = bd_ref[...]
    cdec = cdec_ref[...]

    def rope_k(b, rows):
        return k_ref[b, rows] * cos_ref[rows] + kr_ref[b, rows] * sin_ref[rows]

    def kv_update(s, k, decay, vb):
        kv = _dot_tn((k * decay).astype(BF16), vb)
        return s * cdec + kv * bd

    @pl.when(p == 0)
    def _():
        @pl.when(i == 0)
        def _():
            sbw_ref[...] = jnp.zeros_like(sbw_ref)

        blk = nblk - 1 - i
        for c in reversed(range(cb)):
            rows = slice(c * CHUNK, (c + 1) * CHUNK)
            for b in range(bsz):
                s = sbw_ref[b]
                save_ref[b, blk * cb + c] = _compact(s)
                sbw_ref[b] = kv_update(s, rope_k(b, rows), kdb_ref[...], v_ref[b, rows].astype(BF16))

    @pl.when(p == 1)
    def _():
        @pl.when(i == 0)
        def _():
            sfw_ref[...] = jnp.zeros_like(sfw_ref)

        for c in range(cb):
            rows = slice(c * CHUNK, (c + 1) * CHUNK)
            for b in range(bsz):
                q = q_ref[b, rows] * cos_ref[rows] + qr_ref[b, rows] * sin_ref[rows]
                k = rope_k(b, rows)
                qb, kb, vb = q.astype(BF16), k.astype(BF16), v_ref[b, rows].astype(BF16)
                s_all = _dot_nt(qb, _stack_heads(kb, masks))
                pmat = (s_all * dsym_ref[...]).astype(BF16)
                o = jnp.dot(pmat, _stack_heads(vb, masks), preferred_element_type=F32)
                sfw = sfw_ref[b]
                sbw = _expand(save_ref[b, i * cb + c], bd)
                o = o + jnp.dot(qb, sfw.astype(BF16), preferred_element_type=F32) * qdf_ref[...]
                o = o + jnp.dot(qb, sbw.astype(BF16), preferred_element_type=F32) * qdb_ref[...]
                y = _head_norm(o, avg_ref[...], gn_ref[...])
                o_ref[b, rows] = _silu(g_ref[b, rows]) * y
                sfw_ref[b] = kv_update(sfw, k, kdf_ref[...], vb)


def _ret_tables():
    lg = np.log(1.0 - 2.0 ** (-5.0 - np.arange(N_HEADS, dtype=np.float64)))
    pos = np.arange(CHUNK, dtype=np.float64)
    lag = np.abs(pos[:, None] - pos[None, :])
    dsym = np.concatenate([np.exp(lg[h] * lag) for h in range(N_HEADS)], axis=1)
    lane_lg = np.repeat(lg, HEAD_DIM)[None, :]
    qdf = np.exp(lane_lg * (pos[:, None] + 1.0))
    qdb = np.exp(lane_lg * (CHUNK - pos[:, None]))
    kdf = np.exp(lane_lg * (CHUNK - 1.0 - pos[:, None]))
    kdb = np.exp(lane_lg * pos[:, None])
    cdec = np.exp(lane_lg * CHUNK)
    return [jnp.asarray(t, F32) for t in (dsym, qdf, qdb, kdf, kdb, cdec)]


def _block_diag_mask():
    hid = np.arange(D_GROUP) // HEAD_DIM
    return (hid[:, None] == hid[None, :]).astype(np.float32)


def _rope_tables(l):
    half = HEAD_DIM // 2
    inv = ROPE_BASE ** (-np.arange(half, dtype=np.float64) / half)
    ang = np.arange(l, dtype=np.float64)[:, None] * inv[None, :]
    cos, sin = np.cos(ang), np.sin(ang)
    cos_full = np.tile(np.concatenate([cos, cos], -1), (1, N_HEADS))
    sin_full = np.tile(np.concatenate([sin, sin], -1), (1, N_HEADS))
    return jnp.asarray(cos_full, F32), jnp.asarray(sin_full, F32)


def _retention(proj, gn_w, cos_full, sin_full):
    bsz, l, _ = proj.shape
    nc = l // CHUNK
    cb = 4 if nc % 4 == 0 else 1
    nblk = nc // cb
    tl = cb * CHUNK
    dsym, qdf, qdb, kdf, kdb, cdec = _ret_tables()
    bd = jnp.asarray(_block_diag_mask())
    avg = jnp.asarray(_block_diag_mask() / HEAD_DIM, BF16)

    def both(col):
        return pl.BlockSpec((bsz, tl, D_GROUP), lambda p, i: (0, i + (1 - p) * (nblk - 1 - 2 * i), col))

    def fwd_only(col):
        return pl.BlockSpec((bsz, tl, D_GROUP), lambda p, i: (0, p * i, col))

    tab = pl.BlockSpec((tl, D_GROUP), lambda p, i: (i + (1 - p) * (nblk - 1 - 2 * i), 0))

    def const(shape):
        return pl.BlockSpec(shape, lambda p, i: (0,) * len(shape))

    return pl.pallas_call(
        functools.partial(_ret_kernel, cb=cb, nblk=nblk),
        out_shape=jax.ShapeDtypeStruct((bsz, l, D_GROUP), F32), grid=(2, nblk),
        in_specs=[fwd_only(CB_RQ), fwd_only(CB_RQR), both(CB_RK), both(CB_RKR), both(CB_RV), fwd_only(CB_RG),
                  tab, tab, const((CHUNK, 4 * CHUNK)), const((CHUNK, D_GROUP)), const((CHUNK, D_GROUP)),
                  const((CHUNK, D_GROUP)), const((CHUNK, D_GROUP)), const((1, D_GROUP)),
                  const((D_GROUP, D_GROUP)), const((D_GROUP, D_GROUP)), const((1, D_GROUP))],
        out_specs=pl.BlockSpec((bsz, tl, D_GROUP), lambda p, i: (0, p * i, 0)),
        scratch_shapes=[pltpu.VMEM((bsz, D_GROUP, D_GROUP), F32), pltpu.VMEM((bsz, D_GROUP, D_GROUP), F32),
                        pltpu.VMEM((bsz, nc, HEAD_DIM, D_GROUP), F32)],
        compiler_params=_cparams(("arbitrary", "arbitrary"), 48), name="retention",
    )(proj, proj, proj, proj, proj, proj, cos_full, sin_full, dsym, qdf, qdb, kdf, kdb, cdec, bd, avg,
      gn_w.reshape(1, D_GROUP))


def _mlstm_kernel(q_ref, k_ref, v_ref, og_ref, gc_ref, gr_ref, bc_ref, br_ref, ex_ref, lt_ref, ut_ref,
                  ones_ref, obd_ref, bd_ref, avg_ref, gn_ref,
                  o_ref, cfw_ref, cbw_ref, nmfw_ref, nmbw_ref, csave_ref, nmsave_ref, *, cb, nblk):
    p = pl.program_id(0)
    i = pl.program_id(1)
    bsz = q_ref.shape[0]
    masks = _head_masks(BF16)
    bd = bd_ref[...]
    lt = lt_ref[...]
    ut = ut_ref[...]
    ri = lax.broadcasted_iota(jnp.int32, (CHUNK, CHUNK), 0)
    ci = lax.broadcasted_iota(jnp.int32, (CHUNK, CHUNK), 1)
    lane = lax.broadcasted_iota(jnp.int32, (1, D_GROUP), 1)

    def gates_expanded(b, rows):
        return _split_dot(gc_ref[b, rows] + bc_ref[...], ex_ref[...])

    def state_update(c_ref, nm_ref, b, total, cum, i_x, k, vb):
        m_prev = nm_ref[b, 1:2]
        g = (total - cum) + i_x
        m_new = jnp.maximum(total + m_prev, jnp.max(g, axis=0, keepdims=True))
        wk = jnp.exp(g - m_new) * k
        decay = jnp.exp(total + m_prev - m_new)
        c_ref[b] = c_ref[b] * decay + _dot_tn(wk.astype(BF16), vb) * bd
        nm_ref[b, 0:1] = decay * nm_ref[b, 0:1] + jnp.sum(wk, axis=0, keepdims=True)
        nm_ref[b, 1:2] = m_new

    @pl.when(p == 0)
    def _():
        @pl.when(i == 0)
        def _():
            cbw_ref[...] = jnp.zeros_like(cbw_ref)
            nmbw_ref[...] = jnp.zeros_like(nmbw_ref)

        blk = nblk - 1 - i
        for c in reversed(range(cb)):
            rows = slice(c * CHUNK, (c + 1) * CHUNK)
            for b in range(bsz):
                csave_ref[b, blk * cb + c] = _compact(cbw_ref[b])
                nmsave_ref[b, blk * cb + c] = nmbw_ref[b]
                gx = gates_expanded(b, rows)
                cum = _split_dot_left(ut, _log_sigmoid(gx[:, 768:1024]))
                k = k_ref[b, rows] * (HEAD_DIM ** -0.5)
                state_update(cbw_ref, nmbw_ref, b, cum[0:1], cum, gx[:, 512:768], k, v_ref[b, rows].astype(BF16))

    def chunk_out(b, rows, cidx):
        q = q_ref[b, rows]
        k = k_ref[b, rows] * (HEAD_DIM ** -0.5)
        qb, kb, vb = q.astype(BF16), k.astype(BF16), v_ref[b, rows].astype(BF16)
        s_all = _dot_nt(qb, _stack_heads(kb, masks))
        vaug = jnp.concatenate([_stack_heads(vb, masks), ones_ref[...]], axis=1)
        gx = gates_expanded(b, rows)
        graw = gr_ref[b, :, rows] + br_ref[...]
        gls = _log_sigmoid(graw)
        cum_r_fw = _split_dot(gls, ut)
        cum_r_bw = _split_dot(gls, lt)
        ccomp = csave_ref[b, cidx]
        nmb = nmsave_ref[b, cidx]

        def direction(i_x, f_x, tri, cum_r, i_row0, f_row0, mask, c_state, n_vec, m_prev, total_row):
            cum = _split_dot_left(tri, _log_sigmoid(f_x))
            total = cum[total_row:total_row + 1]
            inter = cum + m_prev
            ps, rmax = [], []
            dms = []
            for h in range(N_HEADS):
                a_col = cum[:, h * HEAD_DIM:h * HEAD_DIM + 1]
                dm = a_col - cum_r[f_row0 + h:f_row0 + h + 1] + graw[i_row0 + h:i_row0 + h + 1]
                dm = jnp.where(mask, dm, -jnp.inf)
                dms.append(dm)
                rmax.append(jnp.max(dm, axis=-1, keepdims=True))
            rmax256 = jnp.where(lane < 64, rmax[0], jnp.where(lane < 128, rmax[1],
                                jnp.where(lane < 192, rmax[2], rmax[3])))
            m_row = jnp.maximum(inter, rmax256)
            for h in range(N_HEADS):
                m_h = m_row[:, h * HEAD_DIM:h * HEAD_DIM + 1]
                ps.append(s_all[:, h * CHUNK:(h + 1) * CHUNK] * jnp.exp(dms[h] - m_h))
            pmat = jnp.concatenate(ps, axis=1).astype(BF16)
            nd = jnp.dot(pmat, vaug, preferred_element_type=F32)
            w_inter = jnp.exp(inter - m_row)
            qc = jnp.dot(qb, c_state.astype(BF16), preferred_element_type=F32)
            qn = _split_dot(q * n_vec, obd_ref[...])
            num = nd[:, :D_GROUP] + w_inter * qc
            den = nd[:, D_GROUP:] + w_inter * qn
            hdir = num / jnp.maximum(jnp.abs(den), jnp.exp(-m_row))
            return hdir, total, cum

        h_fw, tot_fw, cum_fw = direction(gx[:, 0:256], gx[:, 256:512], lt, cum_r_fw, 0, 4, ri >= ci,
                                         cfw_ref[b], nmfw_ref[b, 0:1], nmfw_ref[b, 1:2], CHUNK - 1)
        h_bw, _, _ = direction(gx[:, 512:768], gx[:, 768:1024], ut, cum_r_bw, 8, 12, ci >= ri,
                               _expand(ccomp, bd), nmb[0:1], nmb[1:2], 0)
        y = _head_norm(h_fw + h_bw, avg_ref[...], gn_ref[...])
        o_ref[b, rows] = _sigmoid(og_ref[b, rows]) * y
        state_update(cfw_ref, nmfw_ref, b, tot_fw, cum_fw, gx[:, 0:256], k, vb)

    @pl.when(p == 1)
    def _():
        @pl.when(i == 0)
        def _():
            cfw_ref[...] = jnp.zeros_like(cfw_ref)
            nmfw_ref[...] = jnp.zeros_like(nmfw_ref)

        for c in range(cb):
            rows = slice(c * CHUNK, (c + 1) * CHUNK)
            for b in range(bsz):
                chunk_out(b, rows, i * cb + c)


def _mlstm(proj, qk, gates_row, gate_b, gn_w):
    bsz, l, _ = proj.shape
    nc = l // CHUNK
    cb = 2 if nc % 2 == 0 else 1
    nblk = nc // cb
    tl = cb * CHUNK
    bd_np = _block_diag_mask()
    bd = jnp.asarray(bd_np)
    avg = jnp.asarray(bd_np / HEAD_DIM, BF16)
    obd = jnp.asarray(bd_np, BF16)
    ex = np.zeros((LANES, 4 * D_GROUP), np.float32)
    for j in range(16):
        typ, h = divmod(j, N_HEADS)
        ex[j, typ * D_GROUP + h * HEAD_DIM: typ * D_GROUP + (h + 1) * HEAD_DIM] = 1.0
    idx = np.arange(CHUNK)
    lt = (idx[None, :] <= idx[:, None]).astype(np.float32)
    ones_st = np.repeat(np.repeat(np.eye(N_HEADS, dtype=np.float32), CHUNK, 0), HEAD_DIM, 1)
    gb = gate_b.astype(F32).reshape(16)
    bias_col = jnp.pad(gb, (0, LANES - 16)).reshape(1, LANES)
    bias_row = jnp.broadcast_to(gb.reshape(16, 1), (16, CHUNK))

    def both(arr_col, width=D_GROUP):
        return pl.BlockSpec((bsz, tl, width), lambda p, i: (0, i + (1 - p) * (nblk - 1 - 2 * i), arr_col))

    def fwd_only(arr_col):
        return pl.BlockSpec((bsz, tl, D_GROUP), lambda p, i: (0, p * i, arr_col))

    def const(shape):
        return pl.BlockSpec(shape, lambda p, i: (0,) * len(shape))

    return pl.pallas_call(
        functools.partial(_mlstm_kernel, cb=cb, nblk=nblk),
        out_shape=jax.ShapeDtypeStruct((bsz, l, D_GROUP), F32), grid=(2, nblk),
        in_specs=[fwd_only(0), both(1), both(CB_MV), fwd_only(CB_MO), both(GATE_COL128, LANES),
                  pl.BlockSpec((bsz, 16, tl), lambda p, i: (0, 0, p * i)),
                  const((1, LANES)), const((16, CHUNK)), const((LANES, 4 * D_GROUP)),
                  const((CHUNK, CHUNK)), const((CHUNK, CHUNK)), const((4 * CHUNK, D_GROUP)),
                  const((D_GROUP, D_GROUP)), const((D_GROUP, D_GROUP)), const((D_GROUP, D_GROUP)),
                  const((1, D_GROUP))],
        out_specs=pl.BlockSpec((bsz, tl, D_GROUP), lambda p, i: (0, p * i, 0)),
        scratch_shapes=[pltpu.VMEM((bsz, D_GROUP, D_GROUP), F32), pltpu.VMEM((bsz, D_GROUP, D_GROUP), F32),
                        pltpu.VMEM((bsz, 8, D_GROUP), F32), pltpu.VMEM((bsz, 8, D_GROUP), F32),
                        pltpu.VMEM((bsz, nc, HEAD_DIM, D_GROUP), F32), pltpu.VMEM((bsz, nc, 8, D_GROUP), F32)],
        compiler_params=_cparams(("arbitrary", "arbitrary"), 48), name="mlstm",
    )(qk, qk, proj, proj, proj, gates_row, bias_col, bias_row, jnp.asarray(ex, BF16), jnp.asarray(lt, BF16),
      jnp.asarray(lt.T, BF16), jnp.asarray(ones_st, BF16), obd, bd, avg, gn_w.reshape(1, D_GROUP))


def _s5_kernel(u_ref, mt_ref, bg_ref, cg_ref, pa_ref, pb_ref, o_ref, *, nsteps):
    ub = u_ref[0].astype(BF16)
    e = jnp.dot(ub, bg_ref[0], preferred_element_type=F32)
    r = e.shape[0]
    row = lax.broadcasted_iota(jnp.int32, (r, LANES), 0)
    xf, xb = e[:, :LANES], e[:, LANES:]
    pa, pb = pa_ref[0], pb_ref[0]
    for s in range(nsteps):
        sh = 1 << s
        a_f, b_f = pa[s:s + 1, :LANES], pb[s:s + 1, :LANES]
        a_b, b_b = pa[s:s + 1, LANES:], pb[s:s + 1, LANES:]
        yf = jnp.where(row >= sh, pltpu.roll(xf, sh, 0), 0.0)
        yb = jnp.where(row < r - sh, pltpu.roll(xb, r - sh, 0), 0.0)
        xf = xf + a_f * yf + b_f * pltpu.roll(yf, LANES // 2, 1)
        xb = xb + a_b * yb + b_b * pltpu.roll(yb, LANES // 2, 1)
    sprev = jnp.where(row >= 1, pltpu.roll(xf, 1, 0), 0.0)
    snext = jnp.where(row < r - 1, pltpu.roll(xb, r - 1, 0), 0.0)
    st = jnp.concatenate([sprev, snext], axis=1).astype(BF16)
    o_ref[0] = (jnp.dot(ub, mt_ref[0], preferred_element_type=F32)
                + jnp.dot(st, cg_ref[0], preferred_element_type=F32))


def _s5_tables(a_re, a_im, log_dt, b_re, b_im, c_re, c_im, d_skip, tc, nsteps):
    g, p, ch = S5_GROUPS, S5_STATE, S5_CH
    hp = lax.Precision.HIGHEST
    are, aim = a_re.astype(F32), a_im.astype(F32)
    delta = jnp.exp(log_dt.astype(F32))[..., None]
    lre, lim = are * delta, aim * delta

    class Cx:
        def __init__(self, re, im):
            self.re, self.im = re, im

        def __mul__(self, o):
            return Cx(self.re * o.re - self.im * o.im, self.re * o.im + self.im * o.re)

        def __getitem__(self, idx):
            return Cx(self.re[idx], self.im[idx])

    def apow(n):
        n = jnp.asarray(n, F32)[None, None, :, None]
        mag, ang = jnp.exp(lre[:, :, None, :] * n), lim[:, :, None, :] * n
        return Cx(mag * jnp.cos(ang), mag * jnp.sin(ang))

    abr, abi = jnp.exp(lre) * jnp.cos(lim), jnp.exp(lre) * jnp.sin(lim)
    den = are * are + aim * aim
    quo = Cx(((abr - 1.0) * are + abi * aim) / den, (abi * are - (abr - 1.0) * aim) / den)
    b_bar = quo[..., None] * Cx(b_re.astype(F32)[None], b_im.astype(F32)[None])
    c = Cx(c_re.astype(F32), c_im.astype(F32))
    taus = np.arange(tc)
    cp = c[:, :, None] * apow(taus)[:, :, :, None, :]
    kk = (jnp.einsum("dgtop,dgpi->dgtoi", cp.re, b_bar.re, precision=hp)
          - jnp.einsum("dgtop,dgpi->dgtoi", cp.im, b_bar.im, precision=hp))
    dsk = d_skip.astype(F32).reshape(g, ch)[:, :, None] * jnp.eye(ch, dtype=F32)[None]
    kdiag = kk[0][:, 0] + kk[1][:, 0] + dsk
    lags = jnp.concatenate([kk[1][:, :0:-1], kdiag[:, None], kk[0][:, 1:]], axis=1)
    diff = taus[None, :] - taus[:, None]
    sel = (diff[None] + (tc - 1) == np.arange(2 * tc - 1)[:, None, None]).astype(np.float32)
    mt = jnp.einsum("jst,gjoi->gsito", jnp.asarray(sel), lags, precision=hp).reshape(g, tc * ch, tc * ch)

    zf = apow(tc - 1 - taus)[0][..., None] * b_bar[0][:, None]
    zb = apow(taus)[1][..., None] * b_bar[1][:, None]

    def to_rows(z):
        return jnp.transpose(z, (0, 1, 3, 2)).reshape(g, tc * ch, p)

    bg = jnp.concatenate([to_rows(zf.re), to_rows(zf.im), to_rows(zb.re), to_rows(zb.im)], axis=-1)

    yf = c[0][:, None] * apow(taus + 1)[0][:, :, None, :]
    yb = c[1][:, None] * apow(tc - taus)[1][:, :, None, :]

    def to_cols(z):
        return jnp.transpose(z, (0, 3, 1, 2)).reshape(g, p, tc * ch)

    cg = jnp.concatenate([to_cols(yf.re), -to_cols(yf.im), to_cols(yb.re), -to_cols(yb.im)], axis=1)

    steps = tc * (2.0 ** np.arange(nsteps))
    pw = apow(steps)
    re0, im0, re1, im1 = pw.re[0], pw.im[0], pw.re[1], pw.im[1]
    pa = jnp.concatenate([re0, re0, re1, re1], axis=-1)
    pb = jnp.concatenate([-im0, im0, -im1, im1], axis=-1)
    pad = (-nsteps) % 8
    pa = jnp.pad(pa, ((0, 0), (0, pad), (0, 0)))
    pb = jnp.pad(pb, ((0, 0), (0, pad), (0, 0)))
    return mt.astype(BF16), bg.astype(BF16), cg.astype(BF16), pa, pb


def _s5_glu_kernel(y_ref, w_ref, o_ref):
    y = y_ref[...]
    z = 0.5 * y * (1.0 + jnp.tanh(math.sqrt(2.0 / math.pi) * (y + 0.044715 * (y * y * y))))
    o_ref[...] = z * _sigmoid(jnp.dot(z.astype(BF16), w_ref[...], preferred_element_type=F32))


def _s5(proj, a_re, a_im, log_dt, b_re, b_im, c_re, c_im, d_skip, w_glu):
    bsz, l, _ = proj.shape
    tc = S5_TC
    r = l // tc
    nsteps = max(1, int(math.ceil(math.log2(r))))
    w = tc * S5_CH
    mt, bg, cg, pa, pb = _s5_tables(a_re, a_im, log_dt, b_re, b_im, c_re, c_im, d_skip, tc, nsteps)
    u = proj[:, :, CB_S5 * D_GROUP:(CB_S5 + 1) * D_GROUP]
    ug = jnp.transpose(u.reshape(bsz, r, tc, S5_GROUPS, S5_CH), (3, 0, 1, 2, 4)).reshape(S5_GROUPS, bsz * r, w)
    ns8 = pa.shape[1]
    yg = pl.pallas_call(
        functools.partial(_s5_kernel, nsteps=nsteps),
        out_shape=jax.ShapeDtypeStruct((S5_GROUPS, bsz * r, w), F32), grid=(S5_GROUPS, bsz),
        in_specs=[pl.BlockSpec((1, r, w), lambda g, b: (g, b, 0)),
                  pl.BlockSpec((1, w, w), lambda g, b: (g, 0, 0)),
                  pl.BlockSpec((1, w, D_GROUP), lambda g, b: (g, 0, 0)),
                  pl.BlockSpec((1, D_GROUP, w), lambda g, b: (g, 0, 0)),
                  pl.BlockSpec((1, ns8, D_GROUP), lambda g, b: (g, 0, 0)),
                  pl.BlockSpec((1, ns8, D_GROUP), lambda g, b: (g, 0, 0))],
        out_specs=pl.BlockSpec((1, r, w), lambda g, b: (g, b, 0)),
        compiler_params=_cparams(("parallel", "parallel"), 48), name="s5_ssm")(ug, mt, bg, cg, pa, pb)
    y = jnp.transpose(yg.reshape(S5_GROUPS, bsz, r, tc, S5_CH), (1, 2, 3, 0, 4)).reshape(bsz * l, D_GROUP)
    t = bsz * l
    tm = _tile(t, 2048)
    out = pl.pallas_call(
        _s5_glu_kernel, out_shape=jax.ShapeDtypeStruct((t, D_GROUP), F32), grid=(t // tm,),
        in_specs=[pl.BlockSpec((tm, D_GROUP), lambda i: (i, 0)), pl.BlockSpec((D_GROUP, D_GROUP), lambda i: (0, 0))],
        out_specs=pl.BlockSpec((tm, D_GROUP), lambda i: (i, 0)),
        compiler_params=_cparams(("parallel",)), name="s5_glu")(y, w_glu.astype(BF16))
    return out.reshape(bsz, l, D_GROUP)


def _hy_filter_kernel(z_ref, w1_ref, b1_ref, w2_ref, b2_ref, w3_ref, fr_ref, dec_ref, h_ref, ss_ref):
    i = pl.program_id(0)
    hp = lax.Precision.HIGHEST
    fr = fr_ref[...]
    a = jnp.sin(fr * (jnp.dot(z_ref[...], w1_ref[...], precision=hp, preferred_element_type=F32) + b1_ref[...]))
    a = jnp.sin(fr * (jnp.dot(a, w2_ref[...], precision=hp, preferred_element_type=F32) + b2_ref[...]))
    h = jnp.dot(a, w3_ref[...], precision=hp, preferred_element_type=F32)
    df, db = dec_ref[:, :D_GROUP], dec_ref[:, D_GROUP:]
    h = h * jnp.concatenate([df, df, db, db], axis=1)

    @pl.when(i == 0)
    def _():
        ss_ref[...] = jnp.zeros_like(ss_ref)

    ss_ref[...] += jnp.sum(h * h, axis=0, keepdims=True)
    nhalf = h.shape[1] // 2
    row = lax.broadcasted_iota(jnp.int32, h.shape, 0)
    col = lax.broadcasted_iota(jnp.int32, h.shape, 1)
    h_ref[...] = jnp.where((row == 0) & (col >= nhalf) & (i == 0), 0.0, h)


def _hy_filters(l, w1, b1, w2, b2, w3, freq):
    t = np.linspace(0.0, 1.0, l)[:, None]
    w = 2.0 * np.pi * np.arange(l, dtype=np.float64)[:, None] / l
    bands = np.linspace(1e-4, HY_BANDS - 1, HY_BANDS)[None, :]
    z = np.concatenate([t, np.cos(bands * w), -np.sin(bands * w)], axis=-1)
    max_decay = math.log(HY_TARGET) / HY_FAST_DECAY
    min_decay = math.log(HY_TARGET) / HY_SLOW_DECAY
    rates = np.abs(np.linspace(min_decay, max_decay, D_GROUP))
    dec = np.exp(-t * rates)
    rev = np.concatenate([[0], np.arange(l - 1, 0, -1)])
    half = LANES // 2
    zz = np.zeros((l, LANES))
    zz[:, :HY_EMB] = z
    zz[:, half:half + HY_EMB] = z[rev]
    zz = jnp.asarray(zz, F32)
    dec2 = jnp.asarray(np.concatenate([dec, dec[rev]], axis=1), F32)

    def two(m):
        m = m.astype(F32)
        top = jnp.pad(m, ((0, half - m.shape[0]), (0, half - m.shape[1])))
        zero = jnp.zeros_like(top)
        return jnp.concatenate([jnp.concatenate([top, zero], 1), jnp.concatenate([zero, top], 1)], 0)

    def twice(v):
        v = jnp.pad(v.astype(F32), (0, half - v.shape[0]))
        return jnp.concatenate([v, v]).reshape(1, LANES)

    w3r = w3.astype(F32).reshape(HY_FFN, HY_ORDER, 2, D_GROUP)
    nhalf = HY_ORDER * D_GROUP
    w3f = jnp.pad(w3r[:, :, 0].reshape(HY_FFN, nhalf), ((0, half - HY_FFN), (0, 0)))
    w3b = jnp.pad(w3r[:, :, 1].reshape(HY_FFN, nhalf), ((0, half - HY_FFN), (0, 0)))
    zero = jnp.zeros_like(w3f)
    w3p = jnp.concatenate([jnp.concatenate([w3f, zero], 1), jnp.concatenate([zero, w3b], 1)], 0)
    nout = 2 * nhalf
    tl = _tile(l, 512)

    def const(shape):
        return pl.BlockSpec(shape, lambda i: (0, 0))

    return pl.pallas_call(
        _hy_filter_kernel,
        out_shape=(jax.ShapeDtypeStruct((l, nout), F32), jax.ShapeDtypeStruct((1, nout), F32)), grid=(l // tl,),
        in_specs=[pl.BlockSpec((tl, LANES), lambda i: (i, 0)), const((LANES, LANES)), const((1, LANES)),
                  const((LANES, LANES)), const((1, LANES)), const((LANES, nout)), const((1, LANES)),
                  pl.BlockSpec((tl, 2 * D_GROUP), lambda i: (i, 0))],
        out_specs=(pl.BlockSpec((tl, nout), lambda i: (i, 0)), const((1, nout))),
        compiler_params=_cparams(("arbitrary",)), name="hyena_filter_mlp",
    )(zz, two(w1), twice(b1), two(w2), twice(b2), w3p, twice(freq), dec2)


def _dft_consts(na):
    nb = FFT_NB
    n = na * nb
    ia = np.arange(na, dtype=np.float64)
    th = 2.0 * np.pi * np.outer(ia, ia) / na
    c1, s1 = np.cos(th), np.sin(th)
    eye8 = np.eye(8)
    fa_full = np.concatenate([c1, -s1], axis=0)
    g_full = np.kron(fa_full, eye8)
    g_half = np.kron(fa_full[:, : na // 2], eye8)
    g_out = np.kron(np.concatenate([c1[: na // 2], -s1[: na // 2]], axis=1) / n, eye8)
    ib = np.arange(nb, dtype=np.float64)
    ph = 2.0 * np.pi * np.outer(ib, ib) / nb
    c2, s2 = np.cos(ph), np.sin(ph)
    fb = np.block([[c2, s2], [-s2, c2]])
    fbc = np.block([[c2, -s2], [s2, c2]])
    ps = 2.0 * np.pi * np.outer(ia, ib) / n
    twr = np.broadcast_to(np.cos(ps)[:, :, None], (na, nb, LANES))
    twi = np.broadcast_to(-np.sin(ps)[:, :, None], (na, nb, LANES))
    as_bf = lambda x: jnp.asarray(x, BF16)
    return dict(g_full=as_bf(g_full), g_half=as_bf(g_half), g_out=as_bf(g_out), fb=as_bf(fb), fbc=as_bf(fbc),
                twr=jnp.asarray(twr, F32), twi=jnp.asarray(twi, F32))


def _lane_tile(x, reps):
    return x if reps == 1 else jnp.concatenate([x] * reps, axis=-1)


def _hy_spec_kernel(a_ref, twr_ref, twi_ref, fb_ref, ss_ref, o_ref, *, kb, reps):
    scale = lax.rsqrt(ss_ref[...])
    for j in range(kb):
        ar, ai = a_ref[0, j], a_ref[1, j]
        twr, twi = _lane_tile(twr_ref[j], reps), _lane_tile(twi_ref[j], reps)
        br = twr * ar - twi * ai
        bi = twr * ai + twi * ar
        x = jnp.dot(fb_ref[...], jnp.concatenate([br, bi], axis=0).astype(BF16), preferred_element_type=F32)
        o_ref[0, j] = x[:FFT_NB] * scale
        o_ref[1, j] = x[FFT_NB:] * scale


def _hy_mid_kernel(a_ref, h_ref, twr_ref, twi_ref, fb_ref, fbc_ref, o_ref, *, kb, reps):
    for j in range(kb):
        ar, ai = a_ref[0, 0, j], a_ref[0, 1, j]
        twr, twi = _lane_tile(twr_ref[j], reps), _lane_tile(twi_ref[j], reps)
        br = twr * ar - twi * ai
        bi = twr * ai + twi * ar
        x = jnp.dot(fb_ref[...], jnp.concatenate([br, bi], axis=0).astype(BF16), preferred_element_type=F32)
        xr, xi = x[:FFT_NB], x[FFT_NB:]
        hr, hi = h_ref[0, j], h_ref[1, j]
        yr = xr * hr - xi * hi
        yi = xr * hi + xi * hr
        z = jnp.dot(fbc_ref[...], jnp.concatenate([yr, yi], axis=0).astype(BF16), preferred_element_type=F32)
        zr, zi = z[:FFT_NB], z[FFT_NB:]
        o_ref[0, 0, j] = twr * zr + twi * zi
        o_ref[0, 1, j] = twr * zi - twi * zr


def _hy_dft1_kernel(g_ref, x_ref, o_ref, *, qb):
    na_in, c = x_ref.shape[1], x_ref.shape[4]
    na = o_ref.shape[2]
    for q in range(qb):
        x = x_ref[0, :, q].reshape(na_in * 8, c).astype(BF16)
        a = jnp.dot(g_ref[...], x, preferred_element_type=F32)
        o_ref[0, :, :, q] = a.reshape(2, na, 8, c)


def _hy_fdft1_kernel(g_ref, xt_ref, xb_ref, o_ref, *, qb):
    nah, c = xt_ref.shape[1], xt_ref.shape[4]
    na = o_ref.shape[2]
    kh = nah * 8
    for q in range(qb):
        xt = xt_ref[0, :, q].reshape(kh, c).astype(BF16)
        xb = xb_ref[0, :, q].reshape(kh, c).astype(BF16)
        a = (jnp.dot(g_ref[:, :kh], xt, preferred_element_type=F32)
             + jnp.dot(g_ref[:, kh:], xb, preferred_element_type=F32))
        o_ref[0, :, :, q] = a.reshape(2, na, 8, c)


def _hy_dft1(g, x5, ncol, name):
    bsz, na_in, nq = x5.shape[:3]
    na = g.shape[0] // 16
    c = D_GROUP
    qb = FFT_QB
    return pl.pallas_call(
        functools.partial(_hy_dft1_kernel, qb=qb),
        out_shape=jax.ShapeDtypeStruct((bsz, 2, na, nq, 8, ncol * c), F32), grid=(bsz, ncol, nq // qb),
        in_specs=[pl.BlockSpec(g.shape, lambda b, j, q: (0, 0)),
                  pl.BlockSpec((1, na_in, qb, 8, c), lambda b, j, q: (b, 0, q, 0, j))],
        out_specs=pl.BlockSpec((1, 2, na, qb, 8, c), lambda b, j, q: (b, 0, 0, q, 0, j)),
        compiler_params=_cparams(("parallel", "parallel", "parallel"), 48), name=name)(g, x5)


def _hy_out_kernel(g_ref, z_ref, x_ref, v_ref, b_ref, o_ref, *, qb):
    na2, c = z_ref.shape[1] * z_ref.shape[2], z_ref.shape[5]
    nah = o_ref.shape[1]
    bias = b_ref[...].reshape(1, 1, c)
    for q in range(qb):
        z = z_ref[0, :, :, q].reshape(na2 * 8, c).astype(BF16)
        y = jnp.dot(g_ref[...], z, preferred_element_type=F32).reshape(nah, 8, c)
        o_ref[0, :, q] = x_ref[0, :, q] * (y + v_ref[0, :, q] * bias)


def _hyena(proj, conv_w, conv_b, w1, b1, w2, b2, w3, freq, bias):
    bsz, l, _ = proj.shape
    nb = FFT_NB
    na = 2 * l // nb
    nah = na // 2
    nq = nb // 8
    c = D_GROUP
    qb = FFT_QB
    dc = _dft_consts(na)
    pc = _shortconv(proj, CB_HV, 3, conv_w, conv_b, act=False)

    h, ss = _hy_filters(l, w1, b1, w2, b2, w3, freq)
    ncf = HY_ORDER * c
    ssn = ss[:, :ncf] + ss[:, ncf:]
    h5 = h.reshape(1, nah, nq, 8, 2 * ncf)
    ka = pl.pallas_call(
        functools.partial(_hy_fdft1_kernel, qb=qb),
        out_shape=jax.ShapeDtypeStruct((1, 2, na, nq, 8, ncf), F32), grid=(HY_ORDER, nq // qb),
        in_specs=[pl.BlockSpec(dc["g_full"].shape, lambda j, q: (0, 0)),
                  pl.BlockSpec((1, nah, qb, 8, c), lambda j, q: (0, 0, q, 0, j)),
                  pl.BlockSpec((1, nah, qb, 8, c), lambda j, q: (0, 0, q, 0, HY_ORDER + j))],
        out_specs=pl.BlockSpec((1, 2, na, qb, 8, c), lambda j, q: (0, 0, 0, q, 0, j)),
        compiler_params=_cparams(("parallel", "parallel"), 48), name="hyena_filter_dft1",
    )(dc["g_full"], h5, h5).reshape(2, na, nb, ncf)
    kb = 4 if na % 4 == 0 else 1
    reps = c // LANES
    tw = pl.BlockSpec((kb, nb, LANES), lambda j, k: (k, 0, 0))
    mat = pl.BlockSpec((2 * nb, 2 * nb), lambda j, k: (0, 0))
    hspec = pl.pallas_call(
        functools.partial(_hy_spec_kernel, kb=kb, reps=reps),
        out_shape=jax.ShapeDtypeStruct((2, na, nb, ncf), F32), grid=(ncf // c, na // kb),
        in_specs=[pl.BlockSpec((2, kb, nb, c), lambda j, k: (0, k, 0, j)), tw, tw, mat,
                  pl.BlockSpec((1, c), lambda j, k: (0, j))],
        out_specs=pl.BlockSpec((2, kb, nb, c), lambda j, k: (0, k, 0, j)),
        compiler_params=_cparams(("parallel", "parallel"), 48), name="hyena_filter_dft2",
    )(ka, dc["twr"], dc["twi"], dc["fb"], ssn)

    pc5 = pc.reshape(bsz, nah, nq, 8, 3 * c)

    def long_conv_gate(z5, order, xcol):
        a = _hy_dft1(dc["g_half"], z5, 1, "hyena_dft1").reshape(bsz, 2, na, nb, c)
        zmid = pl.pallas_call(
            functools.partial(_hy_mid_kernel, kb=kb, reps=reps),
            out_shape=jax.ShapeDtypeStruct((bsz, 2, na, nb, c), F32), grid=(bsz, na // kb),
            in_specs=[pl.BlockSpec((1, 2, kb, nb, c), lambda b, k: (b, 0, k, 0, 0)),
                      pl.BlockSpec((2, kb, nb, c), lambda b, k: (0, k, 0, order)), tw, tw, mat, mat],
            out_specs=pl.BlockSpec((1, 2, kb, nb, c), lambda b, k: (b, 0, k, 0, 0)),
            compiler_params=_cparams(("parallel", "parallel"), 48), name="hyena_dft_mid",
        )(a, hspec, dc["twr"], dc["twi"], dc["fb"], dc["fbc"])
        zmid = zmid.reshape(bsz, 2, na, nq, 8, c)
        sig = lambda col: pl.BlockSpec((1, nah, qb, 8, c), lambda b, q: (b, 0, q, 0, col))
        return pl.pallas_call(
            functools.partial(_hy_out_kernel, qb=qb),
            out_shape=jax.ShapeDtypeStruct((bsz, nah, nq, 8, c), F32), grid=(bsz, nq // qb),
            in_specs=[pl.BlockSpec(dc["g_out"].shape, lambda b, q: (0, 0)),
                      pl.BlockSpec((1, 2, na, qb, 8, c), lambda b, q: (b, 0, 0, q, 0, 0)),
                      sig(xcol), sig(0), pl.BlockSpec((1, c), lambda b, q: (0, 0))],
            out_specs=sig(0),
            compiler_params=_cparams(("parallel", "parallel"), 48), name="hyena_idft_gate",
        )(dc["g_out"], zmid, pc5, z5, bias[order].astype(F32).reshape(1, c))

    z1 = long_conv_gate(pc5, 0, 1)
    z2 = long_conv_gate(z1, 1, 2)
    return z2.reshape(bsz, l, c)


def _ffn_kernel(x_ref, w1_ref, w3_ref, w2_ref, lw_ref, lb_ref, o_ref, xb_ref, acc_ref, *, nf):
    f = pl.program_id(1)

    @pl.when(f == 0)
    def _():
        xb_ref[...] = x_ref[...].astype(BF16)
        acc_ref[...] = jnp.zeros_like(acc_ref)

    xb = xb_ref[...]
    a = jnp.dot(xb, w1_ref[...], preferred_element_type=F32)
    b = jnp.dot(xb, w3_ref[...], preferred_element_type=F32)
    acc_ref[...] += jnp.dot((_silu(a) * b).astype(BF16), w2_ref[...], preferred_element_type=F32)

    @pl.when(f == nf - 1)
    def _():
        o_ref[...] = _ln_core(DN_ALPHA * x_ref[...] + acc_ref[...], lw_ref[...], lb_ref[...])


def _ffn_ln(x, w1, w3, w2, lw, lb):
    t, d = x.shape
    ff = w1.shape[1]
    tm = _tile(t, 1024)
    tf = 512 if ff % 512 == 0 else (256 if ff % 256 == 0 else ff)
    nf = ff // tf
    vec = pl.BlockSpec((1, d), lambda i, f: (0, 0))
    return pl.pallas_call(
        functools.partial(_ffn_kernel, nf=nf),
        out_shape=jax.ShapeDtypeStruct((t, d), F32), grid=(t // tm, nf),
        in_specs=[pl.BlockSpec((tm, d), lambda i, f: (i, 0)),
                  pl.BlockSpec((d, tf), lambda i, f: (0, f)),
                  pl.BlockSpec((d, tf), lambda i, f: (0, f)),
                  pl.BlockSpec((tf, d), lambda i, f: (f, 0)), vec, vec],
        out_specs=pl.BlockSpec((tm, d), lambda i, f: (i, 0)),
        scratch_shapes=[pltpu.VMEM((tm, d), BF16), pltpu.VMEM((tm, d), F32)],
        compiler_params=_cparams(("parallel", "arbitrary"), 52), name="swiglu_ffn_ln",
    )(x, w1, w3, w2, lw.reshape(1, d), lb.reshape(1, d))


MOE_TB = 896
MOE_SUB = 256
MOE_CUM = 128


def _moe_kernel(cnt_ref, x_ref, cmb_ref, lt_ref, w1_ref, w3_ref, w2_ref, lw_ref, lb_ref, o_ref,
                xb_ref, xs_ref, ys_ref, gs_ref, pos_ref, *, nf, t_total):
    i = pl.program_id(0)
    e = pl.program_id(1)
    f = pl.program_id(2)
    tb = x_ref.shape[0]
    count = cnt_ref[i * N_EXPERTS + e]
    npass = (count + (MOE_SUB - 1)) // MOE_SUB

    @pl.when((e == 0) & (f == 0))
    def _():
        valid = lax.broadcasted_iota(jnp.int32, (tb, 1), 0) < t_total - i * tb
        xb_ref[...] = jnp.where(valid, x_ref[...], 0.0).astype(BF16)
        o_ref[...] = jnp.zeros_like(o_ref)
        carry = jnp.zeros((1, LANES), F32)
        for c in range(tb // MOE_CUM):
            rows = slice(c * MOE_CUM, (c + 1) * MOE_CUM)
            m = jnp.where(valid[rows] & (cmb_ref[rows] > 0.0), 1.0, 0.0)
            inc = jnp.dot(lt_ref[...], m.astype(BF16), preferred_element_type=F32) + carry
            pos_ref[rows] = jnp.where(m > 0.0, inc - 1.0, -1.0)
            carry = inc[MOE_CUM - 1:MOE_CUM]

    lane = lax.broadcasted_iota(jnp.int32, (tb, LANES), 1)

    def one_hot(j):
        pos = jnp.sum(jnp.where(lane == e, pos_ref[...], 0.0), axis=1, keepdims=True)
        slot = lax.broadcasted_iota(jnp.int32, (tb, MOE_SUB), 1).astype(F32) + (j * MOE_SUB).astype(F32)
        return pos == slot

    @pl.when(f == 0)
    def _():
        gate = jnp.sum(jnp.where(lane == e, cmb_ref[...], 0.0), axis=1, keepdims=True)

        def gather(j, carry):
            hit = one_hot(j)
            xs_ref[j] = _dot_tn(xb_ref[...], jnp.where(hit, 1.0, 0.0).astype(BF16)).astype(BF16)
            g = jnp.sum(jnp.where(hit, gate, 0.0), axis=0, keepdims=True)
            gs_ref[j] = jnp.broadcast_to(g, (8, MOE_SUB))
            ys_ref[j] = jnp.zeros(ys_ref.shape[1:], F32)
            return carry

        lax.fori_loop(0, npass, gather, 0)

    def expert(j, carry):
        xs = xs_ref[j]
        a = _dot_tn(w1_ref[0], xs)
        b = _dot_tn(w3_ref[0], xs)
        hid = (_silu(a) * b * gs_ref[j][0:1]).astype(BF16)
        ys_ref[j] += _dot_tn(w2_ref[0], hid)
        return carry

    lax.fori_loop(0, npass, expert, 0)

    @pl.when(f == nf - 1)
    def _():
        def scatter(j, carry):
            hit = one_hot(j)
            o_ref[...] += _dot_nt(jnp.where(hit, 1.0, 0.0).astype(BF16), ys_ref[j].astype(BF16))
            return carry

        lax.fori_loop(0, npass, scatter, 0)

        @pl.when(e == pl.num_programs(1) - 1)
        def _():
            o_ref[...] = _ln_core(DN_ALPHA * x_ref[...] + o_ref[...], lw_ref[...], lb_ref[...])


def _moe_ln(x, cmb, w1, w3, w2, lw, lb):
    t, d = x.shape
    ne, _, ff = w1.shape
    tb = MOE_TB if t > MOE_TB else t
    nb = -(-t // tb)
    tf = 896 if ff % 896 == 0 else ff
    nf = ff // tf
    max_pass = -(-tb // MOE_SUB)
    cmb_p = jnp.pad(cmb, ((0, nb * tb - t), (0, 0)))
    counts = jnp.sum((cmb_p[:, :N_EXPERTS] > 0.0).reshape(nb, tb, N_EXPERTS), axis=1).astype(jnp.int32).reshape(-1)
    idx = np.arange(MOE_CUM)
    lt = jnp.asarray(idx[None, :] <= idx[:, None], BF16)
    grid_spec = pltpu.PrefetchScalarGridSpec(
        num_scalar_prefetch=1, grid=(nb, ne, nf),
        in_specs=[pl.BlockSpec((tb, d), lambda i, e, f, c: (i, 0)),
                  pl.BlockSpec((tb, LANES), lambda i, e, f, c: (i, 0)),
                  pl.BlockSpec((MOE_CUM, MOE_CUM), lambda i, e, f, c: (0, 0)),
                  pl.BlockSpec((1, d, tf), lambda i, e, f, c: (e, 0, f)),
                  pl.BlockSpec((1, d, tf), lambda i, e, f, c: (e, 0, f)),
                  pl.BlockSpec((1, tf, d), lambda i, e, f, c: (e, f, 0)),
                  pl.BlockSpec((1, d), lambda i, e, f, c: (0, 0)),
                  pl.BlockSpec((1, d), lambda i, e, f, c: (0, 0))],
        out_specs=pl.BlockSpec((tb, d), lambda i, e, f, c: (i, 0)),
        scratch_shapes=[pltpu.VMEM((tb, d), BF16), pltpu.VMEM((max_pass, d, MOE_SUB), BF16),
                        pltpu.VMEM((max_pass, d, MOE_SUB), F32), pltpu.VMEM((max_pass, 8, MOE_SUB), F32),
                        pltpu.VMEM((tb, LANES), F32)])
    return pl.pallas_call(
        functools.partial(_moe_kernel, nf=nf, t_total=t), out_shape=jax.ShapeDtypeStruct((t, d), F32),
        grid_spec=grid_spec,
        compiler_params=_cparams(("parallel", "arbitrary", "arbitrary"), 56), name="moe_routed",
    )(counts, x, cmb_p, lt, w1, w3, w2, lw.reshape(1, d), lb.reshape(1, d))


def _router_kernel(x_ref, rh_ref, rl_ref, o_ref):
    x = x_ref[...]
    xh = x.astype(BF16)
    xl = (x - xh.astype(F32)).astype(BF16)
    logits = (jnp.dot(xh, rh_ref[...], preferred_element_type=F32)
              + jnp.dot(xl, rh_ref[...], preferred_element_type=F32)
              + jnp.dot(xh, rl_ref[...], preferred_element_type=F32))
    lane = lax.broadcasted_iota(jnp.int32, logits.shape, 1).astype(F32)
    logits = jnp.where(lane < N_EXPERTS, logits, -jnp.inf)
    m1 = jnp.max(logits, axis=1, keepdims=True)
    i1 = jnp.min(jnp.where(logits == m1, lane, float(LANES)), axis=1, keepdims=True)
    rest = jnp.where(lane == i1, -jnp.inf, logits)
    m2 = jnp.max(rest, axis=1, keepdims=True)
    i2 = jnp.min(jnp.where(rest == m2, lane, float(LANES)), axis=1, keepdims=True)
    e2 = jnp.exp(m2 - m1)
    g1 = 1.0 / (1.0 + e2)
    g2 = e2 / (1.0 + e2)
    o_ref[...] = jnp.where(lane == i1, g1, 0.0) + jnp.where(lane == i2, g2, 0.0)


def _router(x, router):
    t, d = x.shape
    r = jnp.pad(router.astype(F32), ((0, 0), (0, LANES - N_EXPERTS)))
    rh = r.astype(BF16)
    rl = (r - rh.astype(F32)).astype(BF16)
    tm = _tile(t, 1024)
    return pl.pallas_call(
        _router_kernel, out_shape=jax.ShapeDtypeStruct((t, LANES), F32), grid=(t // tm,),
        in_specs=[pl.BlockSpec((tm, d), lambda i: (i, 0)), pl.BlockSpec((d, LANES), lambda i: (0, 0)),
                  pl.BlockSpec((d, LANES), lambda i: (0, 0))],
        out_specs=pl.BlockSpec((tm, LANES), lambda i: (i, 0)),
        compiler_params=_cparams(("parallel",)), name="moe_router")(x, rh, rl)


def _extended_w_in(w_in):
    w = w_in.astype(F32)
    scale = HEAD_DIM ** -0.5

    def rot_half(cols):
        c4 = cols.reshape(-1, N_HEADS, 2, HEAD_DIM // 2)
        return jnp.stack([-c4[:, :, 1], c4[:, :, 0]], axis=2).reshape(-1, D_GROUP)

    wq = w[:, 0:256]
    wk = w[:, 256:512] * scale
    main = jnp.concatenate([wq, wk, w[:, 512:3072]], axis=1)
    gates = jnp.pad(w[:, 3072:3088], ((0, 0), (0, LANES - 16)))
    ext = jnp.concatenate([main, rot_half(wq), rot_half(wk), gates], axis=1)
    return jnp.pad(ext, ((0, 0), (0, N_EXT - ext.shape[1]))).astype(BF16)


def kernel(x, ln_in_w, ln_in_b, w_in, w_out, ret_gn_w, s5_a_re, s5_a_im, s5_log_dt, s5_b_re, s5_b_im, s5_c_re, s5_c_im, s5_d, s5_w_glu, hy_conv_w, hy_conv_b, hy_w1, hy_b1, hy_w2, hy_b2, hy_w3, hy_freq, hy_bias, ml_conv_w, ml_conv_b, ml_gate_b, ml_gn_w, ln1_w, ln1_b, ln2_w, ln2_b, ffn_w1, ffn_w3, ffn_w2, moe_router, moe_w1, moe_w3, moe_w2):
    bsz, l, d = x.shape
    t = bsz * l
    cos_full, sin_full = _rope_tables(l)
    h = _layer_norm(x.reshape(t, d), ln_in_w, ln_in_b)
    for layer in range(DEPTH):
        proj = _mm(h, _extended_w_in(w_in[layer]), tm=1024, tn=1280, name="in_proj").reshape(bsz, l, N_EXT)
        y_ret = _retention(proj, ret_gn_w[layer], cos_full, sin_full)
        y_s5 = _s5(proj, s5_a_re[layer], s5_a_im[layer], s5_log_dt[layer], s5_b_re[layer], s5_b_im[layer],
                   s5_c_re[layer], s5_c_im[layer], s5_d[layer], s5_w_glu[layer])
        y_hy = _hyena(proj, hy_conv_w[layer], hy_conv_b[layer], hy_w1[layer], hy_b1[layer], hy_w2[layer],
                      hy_b2[layer], hy_w3[layer], hy_freq[layer], hy_bias[layer])
        qk = _shortconv(proj, CB_MQ, 2, ml_conv_w[layer], ml_conv_b[layer], act=True)
        gates_row = jnp.transpose(proj[:, :, GATE_COL128 * LANES:GATE_COL128 * LANES + 16], (0, 2, 1))
        y_ml = _mlstm(proj, qk, gates_row, ml_gate_b[layer], ml_gn_w[layer])
        ys = [y.reshape(t, D_GROUP) for y in (y_ret, y_s5, y_hy, y_ml)]
        h = _outproj_ln(ys, w_out[layer], h, ln1_w[layer], ln1_b[layer])
        j = layer // 2
        if layer % 2 == 0:
            h = _ffn_ln(h, ffn_w1[j].astype(BF16), ffn_w3[j].astype(BF16), ffn_w2[j].astype(BF16),
                        ln2_w[layer], ln2_b[layer])
        else:
            cmb = _router(h, moe_router[j])
            h = _moe_ln(h, cmb, moe_w1[j].astype(BF16), moe_w3[j].astype(BF16), moe_w2[j].astype(BF16),
                        ln2_w[layer], ln2_b[layer])
    return h.reshape(bsz, l, d)
```

```python
import functools
import math

import numpy as np
import jax
import jax.numpy as jnp
from jax import lax
from jax.experimental import pallas as pl
from jax.experimental.pallas import tpu as pltpu

F32 = jnp.float32
BF16 = jnp.bfloat16

D_MODEL = 1024
DEPTH = 2
D_GROUP = 256
HEAD_DIM = 64
N_HEADS = 4
CHUNK = 128
S5_CH = 16
S5_GROUPS = 16
S5_STATE = 64
HY_ORDER = 2
HY_EMB = 33
HY_BANDS = 16
HY_FFN = 64
HY_FAST_DECAY = 0.3
HY_SLOW_DECAY = 1.5
HY_TARGET = 1e-2
N_EXPERTS = 8
ROPE_BASE = 10000.0
EPS = 1e-5
DN_ALPHA = (2 * DEPTH) ** 0.25

LANES = 128
S5_TC = 32
FFT_NB = 256
FFT_QB = 2
N_EXT = 3840

CB_RQ, CB_RK, CB_RV, CB_RG, CB_S5, CB_HV, CB_HX1, CB_HX2 = 0, 1, 2, 3, 4, 5, 6, 7
CB_MQ, CB_MK, CB_MV, CB_MO, CB_RQR, CB_RKR = 8, 9, 10, 11, 12, 13
GATE_COL128 = 28


def _cparams(sem, vmem_mb=None):
    kw = dict(dimension_semantics=sem)
    if vmem_mb is not None:
        kw["vmem_limit_bytes"] = vmem_mb * 1024 * 1024
    return pltpu.CompilerParams(**kw)


def _tile(n, pref):
    return pref if n % pref == 0 else n


def _split_dot(x, m, parts=3):
    acc = None
    r = x
    for _ in range(parts):
        hi = r.astype(BF16)
        t = jnp.dot(hi, m, preferred_element_type=F32)
        acc = t if acc is None else acc + t
        r = r - hi.astype(F32)
    return acc


def _split_dot_left(m, x, parts=3):
    acc = None
    r = x
    for _ in range(parts):
        hi = r.astype(BF16)
        t = jnp.dot(m, hi, preferred_element_type=F32)
        acc = t if acc is None else acc + t
        r = r - hi.astype(F32)
    return acc


def _dot_nt(a, b):
    return lax.dot_general(a, b, (((1,), (1,)), ((), ())), preferred_element_type=F32)


def _dot_tn(a, b):
    return lax.dot_general(a, b, (((0,), (0,)), ((), ())), preferred_element_type=F32)


def _sigmoid(x):
    return 1.0 / (1.0 + jnp.exp(-x))


def _silu(x):
    return x * _sigmoid(x)


def _log_sigmoid(x):
    return jnp.minimum(x, 0.0) - jnp.log(1.0 + jnp.exp(-jnp.abs(x)))


def _head_masks(dtype):
    lane = lax.broadcasted_iota(jnp.int32, (1, D_GROUP), 1)
    return [((lane >= h * HEAD_DIM) & (lane < (h + 1) * HEAD_DIM)).astype(dtype) for h in range(N_HEADS)]


def _ln_core(x, w, b):
    mu = jnp.mean(x, -1, keepdims=True)
    xc = x - mu
    var = jnp.mean(xc * xc, -1, keepdims=True)
    return xc * lax.rsqrt(var + EPS) * w + b


def _ln_kernel(x_ref, w_ref, b_ref, o_ref):
    o_ref[...] = _ln_core(x_ref[...], w_ref[...], b_ref[...])


def _layer_norm(x, w, b):
    t, d = x.shape
    tm = _tile(t, 512)
    row = pl.BlockSpec((tm, d), lambda i: (i, 0))
    vec = pl.BlockSpec((1, d), lambda i: (0, 0))
    return pl.pallas_call(_ln_kernel, out_shape=jax.ShapeDtypeStruct((t, d), F32), grid=(t // tm,),
                          in_specs=[row, vec, vec], out_specs=row,
                          compiler_params=_cparams(("parallel",)), name="layer_norm")(x, w.reshape(1, d), b.reshape(1, d))


def _mm_kernel(a_ref, b_ref, o_ref):
    o_ref[...] = jnp.dot(a_ref[...].astype(BF16), b_ref[...], preferred_element_type=F32).astype(o_ref.dtype)


def _mm(a, b, tm=1024, tn=1024, out_dtype=F32, name="matmul"):
    m, k = a.shape
    n = b.shape[1]
    tm, tn = _tile(m, tm), _tile(n, tn)
    return pl.pallas_call(
        _mm_kernel, out_shape=jax.ShapeDtypeStruct((m, n), out_dtype), grid=(m // tm, n // tn),
        in_specs=[pl.BlockSpec((tm, k), lambda i, j: (i, 0)), pl.BlockSpec((k, tn), lambda i, j: (0, j))],
        out_specs=pl.BlockSpec((tm, tn), lambda i, j: (i, j)),
        compiler_params=_cparams(("parallel", "arbitrary"), 48), name=name)(a, b)


def _outproj_ln_kernel(y0_ref, y1_ref, y2_ref, y3_ref, w_ref, h_ref, lw_ref, lb_ref, o_ref):
    mix = None
    for g, y_ref in enumerate((y0_ref, y1_ref, y2_ref, y3_ref)):
        part = jnp.dot(y_ref[...].astype(BF16), w_ref[g * D_GROUP:(g + 1) * D_GROUP, :], preferred_element_type=F32)
        mix = part if mix is None else mix + part
    o_ref[...] = _ln_core(DN_ALPHA * h_ref[...] + mix, lw_ref[...], lb_ref[...])


def _outproj_ln(ys, w_out, h, lw, lb):
    t, d = h.shape
    tm = _tile(t, 1024)
    grp = pl.BlockSpec((tm, D_GROUP), lambda i: (i, 0))
    row = pl.BlockSpec((tm, d), lambda i: (i, 0))
    vec = pl.BlockSpec((1, d), lambda i: (0, 0))
    return pl.pallas_call(
        _outproj_ln_kernel, out_shape=jax.ShapeDtypeStruct((t, d), F32), grid=(t // tm,),
        in_specs=[grp, grp, grp, grp, pl.BlockSpec((d, d), lambda i: (0, 0)), row, vec, vec], out_specs=row,
        compiler_params=_cparams(("parallel",), 48), name="out_proj_ln",
    )(*ys, w_out.astype(BF16), h, lw.reshape(1, d), lb.reshape(1, d))


def _shortconv_kernel(x_ref, xp_ref, xn_ref, w_ref, b_ref, o_ref, *, nt, act):
    i = pl.program_id(1)
    x = x_ref[0]
    tl = x.shape[0]
    row = lax.broadcasted_iota(jnp.int32, x.shape, 0)
    prev_row = jnp.where(i == 0, 0.0, xp_ref[0, 7:8, :])
    next_row = jnp.where(i == nt - 1, 0.0, xn_ref[0, 0:1, :])
    x_prev = jnp.where(row == 0, prev_row, pltpu.roll(x, 1, 0))
    x_next = jnp.where(row == tl - 1, next_row, pltpu.roll(x, tl - 1, 0))
    w = w_ref[0]
    y = b_ref[0, 0:1] + x_prev * w[0:1] + x * w[1:2] + x_next * w[2:3]
    if act:
        y = _silu(y)
    o_ref[0] = y


def _shortconv(proj, col0, nblk, w, b, act):
    bsz, l, _ = proj.shape
    tl = _tile(l, 1024)
    nt = l // tl
    w3 = jnp.transpose(w.reshape(3, nblk, D_GROUP), (1, 0, 2))
    w3 = jnp.pad(w3, ((0, 0), (0, 5), (0, 0)))
    b3 = jnp.broadcast_to(b.reshape(nblk, 1, D_GROUP), (nblk, 8, D_GROUP))
    r8 = tl // 8
    return pl.pallas_call(
        functools.partial(_shortconv_kernel, nt=nt, act=act),
        out_shape=jax.ShapeDtypeStruct((bsz, l, nblk * D_GROUP), F32), grid=(bsz, nt, nblk),
        in_specs=[
            pl.BlockSpec((1, tl, D_GROUP), lambda bb, i, j: (bb, i, col0 + j)),
            pl.BlockSpec((1, 8, D_GROUP), lambda bb, i, j: (bb, jnp.maximum(i * r8 - 1, 0), col0 + j)),
            pl.BlockSpec((1, 8, D_GROUP), lambda bb, i, j: (bb, jnp.minimum((i + 1) * r8, l // 8 - 1), col0 + j)),
            pl.BlockSpec((1, 8, D_GROUP), lambda bb, i, j: (j, 0, 0)),
            pl.BlockSpec((1, 8, D_GROUP), lambda bb, i, j: (j, 0, 0)),
        ],
        out_specs=pl.BlockSpec((1, tl, D_GROUP), lambda bb, i, j: (bb, i, j)),
        compiler_params=_cparams(("parallel", "parallel", "parallel")), name="shortconv")(proj, proj, proj, w3, b3)


def _stack_heads(xb, masks):
    return jnp.concatenate([xb * masks[h] for h in range(N_HEADS)], axis=0)


def _compact(s):
    return s[0:64] + s[64:128] + s[128:192] + s[192:256]


def _expand(c, bd):
    return jnp.concatenate([c, c, c, c], axis=0) * bd


def _head_norm(o, avg, gn):
    mu = _split_dot(o, avg, parts=2)
    oc = o - mu
    var = _split_dot(oc * oc, avg, parts=2)
    return oc * lax.rsqrt(var + EPS) * gn


def _ret_kernel(q_ref, qr_ref, k_ref, kr_ref, v_ref, g_ref, cos_ref, sin_ref,
                dsym_ref, qdf_ref, qdb_ref, kdf_ref, kdb_ref, cdec_ref, bd_ref, avg_ref, gn_ref,
                o_ref, sfw_ref, sbw_ref, save_ref, *, cb, nblk):
    p = pl.program_id(0)
    i = pl.program_id(1)
    bsz = q_ref.shape[0]
    masks = _head_masks(BF16)
    bd = bd_ref[...]
    cdec = cdec_ref[...]

    def rope_k(b, rows):
        return k_ref[b, rows] * cos_ref[rows] + kr_ref[b, rows] * sin_ref[rows]

    def kv_update(s, k, decay, vb):
        kv = _dot_tn((k * decay).astype(BF16), vb)
        return s * cdec + kv * bd

    @pl.when(p == 0)
    def _():
        @pl.when(i == 0)
        def _():
            sbw_ref[...] = jnp.zeros_like(sbw_ref)

        blk = nblk - 1 - i
        for c in reversed(range(cb)):
            rows = slice(c * CHUNK, (c + 1) * CHUNK)
            for b in range(bsz):
                s = sbw_ref[b]
                save_ref[b, blk * cb + c] = _compact(s)
                sbw_ref[b] = kv_update(s, rope_k(b, rows), kdb_ref[...], v_ref[b, rows].astype(BF16))

    @pl.when(p == 1)
    def _():
        @pl.when(i == 0)
        def _():
            sfw_ref[...] = jnp.zeros_like(sfw_ref)

        for c in range(cb):
            rows = slice(c * CHUNK, (c + 1) * CHUNK)
            for b in range(bsz):
                q = q_ref[b, rows] * cos_ref[rows] + qr_ref[b, rows] * sin_ref[rows]
                k = rope_k(b, rows)
                qb, kb, vb = q.astype(BF16), k.astype(BF16), v_ref[b, rows].astype(BF16)
                s_all = _dot_nt(qb, _stack_heads(kb, masks))
                pmat = (s_all * dsym_ref[...]).astype(BF16)
                o = jnp.dot(pmat, _stack_heads(vb, masks), preferred_element_type=F32)
                sfw = sfw_ref[b]
                sbw = _expand(save_ref[b, i * cb + c], bd)
                o = o + jnp.dot(qb, sfw.astype(BF16), preferred_element_type=F32) * qdf_ref[...]
                o = o + jnp.dot(qb, sbw.astype(BF16), preferred_element_type=F32) * qdb_ref[...]
                y = _head_norm(o, avg_ref[...], gn_ref[...])
                o_ref[b, rows] = _silu(g_ref[b, rows]) * y
                sfw_ref[b] = kv_update(sfw, k, kdf_ref[...], vb)


def _ret_tables():
    lg = np.log(1.0 - 2.0 ** (-5.0 - np.arange(N_HEADS, dtype=np.float64)))
    pos = np.arange(CHUNK, dtype=np.float64)
    lag = np.abs(pos[:, None] - pos[None, :])
    dsym = np.concatenate([np.exp(lg[h] * lag) for h in range(N_HEADS)], axis=1)
    lane_lg = np.repeat(lg, HEAD_DIM)[None, :]
    qdf = np.exp(lane_lg * (pos[:, None] + 1.0))
    qdb = np.exp(lane_lg * (CHUNK - pos[:, None]))
    kdf = np.exp(lane_lg * (CHUNK - 1.0 - pos[:, None]))
    kdb = np.exp(lane_lg * pos[:, None])
    cdec = np.exp(lane_lg * CHUNK)
    return [jnp.asarray(t, F32) for t in (dsym, qdf, qdb, kdf, kdb, cdec)]


def _block_diag_mask():
    hid = np.arange(D_GROUP) // HEAD_DIM
    return (hid[:, None] == hid[None, :]).astype(np.float32)


def _rope_tables(l):
    half = HEAD_DIM // 2
    inv = ROPE_BASE ** (-np.arange(half, dtype=np.float64) / half)
    ang = np.arange(l, dtype=np.float64)[:, None] * inv[None, :]
    cos, sin = np.cos(ang), np.sin(ang)
    cos_full = np.tile(np.concatenate([cos, cos], -1), (1, N_HEADS))
    sin_full = np.tile(np.concatenate([sin, sin], -1), (1, N_HEADS))
    return jnp.asarray(cos_full, F32), jnp.asarray(sin_full, F32)


def _retention(proj, gn_w, cos_full, sin_full):
    bsz, l, _ = proj.shape
    nc = l // CHUNK
    cb = 4 if nc % 4 == 0 else 1
    nblk = nc // cb
    tl = cb * CHUNK
    dsym, qdf, qdb, kdf, kdb, cdec = _ret_tables()
    bd = jnp.asarray(_block_diag_mask())
    avg = jnp.asarray(_block_diag_mask() / HEAD_DIM, BF16)

    def both(col):
        return pl.BlockSpec((bsz, tl, D_GROUP), lambda p, i: (0, i + (1 - p) * (nblk - 1 - 2 * i), col))

    def fwd_only(col):
        return pl.BlockSpec((bsz, tl, D_GROUP), lambda p, i: (0, p * i, col))

    tab = pl.BlockSpec((tl, D_GROUP), lambda p, i: (i + (1 - p) * (nblk - 1 - 2 * i), 0))

    def const(shape):
        return pl.BlockSpec(shape, lambda p, i: (0,) * len(shape))

    return pl.pallas_call(
        functools.partial(_ret_kernel, cb=cb, nblk=nblk),
        out_shape=jax.ShapeDtypeStruct((bsz, l, D_GROUP), F32), grid=(2, nblk),
        in_specs=[fwd_only(CB_RQ), fwd_only(CB_RQR), both(CB_RK), both(CB_RKR), both(CB_RV), fwd_only(CB_RG),
                  tab, tab, const((CHUNK, 4 * CHUNK)), const((CHUNK, D_GROUP)), const((CHUNK, D_GROUP)),
                  const((CHUNK, D_GROUP)), const((CHUNK, D_GROUP)), const((1, D_GROUP)),
                  const((D_GROUP, D_GROUP)), const((D_GROUP, D_GROUP)), const((1, D_GROUP))],
        out_specs=pl.BlockSpec((bsz, tl, D_GROUP), lambda p, i: (0, p * i, 0)),
        scratch_shapes=[pltpu.VMEM((bsz, D_GROUP, D_GROUP), F32), pltpu.VMEM((bsz, D_GROUP, D_GROUP), F32),
                        pltpu.VMEM((bsz, nc, HEAD_DIM, D_GROUP), F32)],
        compiler_params=_cparams(("arbitrary", "arbitrary"), 48), name="retention",
    )(proj, proj, proj, proj, proj, proj, cos_full, sin_full, dsym, qdf, qdb, kdf, kdb, cdec, bd, avg,
      gn_w.reshape(1, D_GROUP))


def _mlstm_kernel(q_ref, k_ref, v_ref, og_ref, gc_ref, gr_ref, bc_ref, br_ref, ex_ref, lt_ref, ut_ref,
                  ones_ref, obd_ref, bd_ref, avg_ref, gn_ref,
                  o_ref, cfw_ref, cbw_ref, nmfw_ref, nmbw_ref, csave_ref, nmsave_ref, *, cb, nblk):
    p = pl.program_id(0)
    i = pl.program_id(1)
    bsz = q_ref.shape[0]
    masks = _head_masks(BF16)
    bd = bd_ref[...]
    lt = lt_ref[...]
    ut = ut_ref[...]
    ri = lax.broadcasted_iota(jnp.int32, (CHUNK, CHUNK), 0)
    ci = lax.broadcasted_iota(jnp.int32, (CHUNK, CHUNK), 1)
    lane = lax.broadcasted_iota(jnp.int32, (1, D_GROUP), 1)

    def gates_expanded(b, rows):
        return _split_dot(gc_ref[b, rows] + bc_ref[...], ex_ref[...])

    def state_update(c_ref, nm_ref, b, total, cum, i_x, k, vb):
        m_prev = nm_ref[b, 1:2]
        g = (total - cum) + i_x
        m_new = jnp.maximum(total + m_prev, jnp.max(g, axis=0, keepdims=True))
        wk = jnp.exp(g - m_new) * k
        decay = jnp.exp(total + m_prev - m_new)
        c_ref[b] = c_ref[b] * decay + _dot_tn(wk.astype(BF16), vb) * bd
        nm_ref[b, 0:1] = decay * nm_ref[b, 0:1] + jnp.sum(wk, axis=0, keepdims=True)
        nm_ref[b, 1:2] = m_new

    @pl.when(p == 0)
    def _():
        @pl.when(i == 0)
        def _():
            cbw_ref[...] = jnp.zeros_like(cbw_ref)
            nmbw_ref[...] = jnp.zeros_like(nmbw_ref)

        blk = nblk - 1 - i
        for c in reversed(range(cb)):
            rows = slice(c * CHUNK, (c + 1) * CHUNK)
            for b in range(bsz):
                csave_ref[b, blk * cb + c] = _compact(cbw_ref[b])
                nmsave_ref[b, blk * cb + c] = nmbw_ref[b]
                gx = gates_expanded(b, rows)
                cum = _split_dot_left(ut, _log_sigmoid(gx[:, 768:1024]))
                k = k_ref[b, rows] * (HEAD_DIM ** -0.5)
                state_update(cbw_ref, nmbw_ref, b, cum[0:1], cum, gx[:, 512:768], k, v_ref[b, rows].astype(BF16))

    def chunk_out(b, rows, cidx):
        q = q_ref[b, rows]
        k = k_ref[b, rows] * (HEAD_DIM ** -0.5)
        qb, kb, vb = q.astype(BF16), k.astype(BF16), v_ref[b, rows].astype(BF16)
        s_all = _dot_nt(qb, _stack_heads(kb, masks))
        vaug = jnp.concatenate([_stack_heads(vb, masks), ones_ref[...]], axis=1)
        gx = gates_expanded(b, rows)
        graw = gr_ref[b, :, rows] + br_ref[...]
        gls = _log_sigmoid(graw)
        cum_r_fw = _split_dot(gls, ut)
        cum_r_bw = _split_dot(gls, lt)
        ccomp = csave_ref[b, cidx]
        nmb = nmsave_ref[b, cidx]

        def direction(i_x, f_x, tri, cum_r, i_row0, f_row0, mask, c_state, n_vec, m_prev, total_row):
            cum = _split_dot_left(tri, _log_sigmoid(f_x))
            total = cum[total_row:total_row + 1]
            inter = cum + m_prev
            ps, rmax = [], []
            dms = []
            for h in range(N_HEADS):
                a_col = cum[:, h * HEAD_DIM:h * HEAD_DIM + 1]
                dm = a_col - cum_r[f_row0 + h:f_row0 + h + 1] + graw[i_row0 + h:i_row0 + h + 1]
                dm = jnp.where(mask, dm, -jnp.inf)
                dms.append(dm)
                rmax.append(jnp.max(dm, axis=-1, keepdims=True))
            rmax256 = jnp.where(lane < 64, rmax[0], jnp.where(lane < 128, rmax[1],
                                jnp.where(lane < 192, rmax[2], rmax[3])))
            m_row = jnp.maximum(inter, rmax256)
            for h in range(N_HEADS):
                m_h = m_row[:, h * HEAD_DIM:h * HEAD_DIM + 1]
                ps.append(s_all[:, h * CHUNK:(h + 1) * CHUNK] * jnp.exp(dms[h] - m_h))
            pmat = jnp.concatenate(ps, axis=1).astype(BF16)
            nd = jnp.dot(pmat, vaug, preferred_element_type=F32)
            w_inter = jnp.exp(inter - m_row)
            qc = jnp.dot(qb, c_state.astype(BF16), preferred_element_type=F32)
            qn = _split_dot(q * n_vec, obd_ref[...])
            num = nd[:, :D_GROUP] + w_inter * qc
            den = nd[:, D_GROUP:] + w_inter * qn
            hdir = num / jnp.maximum(jnp.abs(den), jnp.exp(-m_row))
            return hdir, total, cum

        h_fw, tot_fw, cum_fw = direction(gx[:, 0:256], gx[:, 256:512], lt, cum_r_fw, 0, 4, ri >= ci,
                                         cfw_ref[b], nmfw_ref[b, 0:1], nmfw_ref[b, 1:2], CHUNK - 1)
        h_bw, _, _ = direction(gx[:, 512:768], gx[:, 768:1024], ut, cum_r_bw, 8, 12, ci >= ri,
                               _expand(ccomp, bd), nmb[0:1], nmb[1:2], 0)
        y = _head_norm(h_fw + h_bw, avg_ref[...], gn_ref[...])
        o_ref[b, rows] = _sigmoid(og_ref[b, rows]) * y
        state_update(cfw_ref, nmfw_ref, b, tot_fw, cum_fw, gx[:, 0:256], k, vb)

    @pl.when(p == 1)
    def _():
        @pl.when(i == 0)
        def _():
            cfw_ref[...] = jnp.zeros_like(cfw_ref)
            nmfw_ref[...] = jnp.zeros_like(nmfw_ref)

        for c in range(cb):
            rows = slice(c * CHUNK, (c + 1) * CHUNK)
            for b in range(bsz):
                chunk_out(b, rows, i * cb + c)


def _mlstm(proj, qk, gates_row, gate_b, gn_w):
    bsz, l, _ = proj.shape
    nc = l // CHUNK
    cb = 2 if nc % 2 == 0 else 1
    nblk = nc // cb
    tl = cb * CHUNK
    bd_np = _block_diag_mask()
    bd = jnp.asarray(bd_np)
    avg = jnp.asarray(bd_np / HEAD_DIM, BF16)
    obd = jnp.asarray(bd_np, BF16)
    ex = np.zeros((LANES, 4 * D_GROUP), np.float32)
    for j in range(16):
        typ, h = divmod(j, N_HEADS)
        ex[j, typ * D_GROUP + h * HEAD_DIM: typ * D_GROUP + (h + 1) * HEAD_DIM] = 1.0
    idx = np.arange(CHUNK)
    lt = (idx[None, :] <= idx[:, None]).astype(np.float32)
    ones_st = np.repeat(np.repeat(np.eye(N_HEADS, dtype=np.float32), CHUNK, 0), HEAD_DIM, 1)
    gb = gate_b.astype(F32).reshape(16)
    bias_col = jnp.pad(gb, (0, LANES - 16)).reshape(1, LANES)
    bias_row = jnp.broadcast_to(gb.reshape(16, 1), (16, CHUNK))

    def both(arr_col, width=D_GROUP):
        return pl.BlockSpec((bsz, tl, width), lambda p, i: (0, i + (1 - p) * (nblk - 1 - 2 * i), arr_col))

    def fwd_only(arr_col):
        return pl.BlockSpec((bsz, tl, D_GROUP), lambda p, i: (0, p * i, arr_col))

    def const(shape):
        return pl.BlockSpec(shape, lambda p, i: (0,) * len(shape))

    return pl.pallas_call(
        functools.partial(_mlstm_kernel, cb=cb, nblk=nblk),
        out_shape=jax.ShapeDtypeStruct((bsz, l, D_GROUP), F32), grid=(2, nblk),
        in_specs=[fwd_only(0), both(1), both(CB_MV), fwd_only(CB_MO), both(GATE_COL128, LANES),
                  pl.BlockSpec((bsz, 16, tl), lambda p, i: (0, 0, p * i)),
                  const((1, LANES)), const((16, CHUNK)), const((LANES, 4 * D_GROUP)),
                  const((CHUNK, CHUNK)), const((CHUNK, CHUNK)), const((4 * CHUNK, D_GROUP)),
                  const((D_GROUP, D_GROUP)), const((D_GROUP, D_GROUP)), const((D_GROUP, D_GROUP)),
                  const((1, D_GROUP))],
        out_specs=pl.BlockSpec((bsz, tl, D_GROUP), lambda p, i: (0, p * i, 0)),
        scratch_shapes=[pltpu.VMEM((bsz, D_GROUP, D_GROUP), F32), pltpu.VMEM((bsz, D_GROUP, D_GROUP), F32),
                        pltpu.VMEM((bsz, 8, D_GROUP), F32), pltpu.VMEM((bsz, 8, D_GROUP), F32),
                        pltpu.VMEM((bsz, nc, HEAD_DIM, D_GROUP), F32), pltpu.VMEM((bsz, nc, 8, D_GROUP), F32)],
        compiler_params=_cparams(("arbitrary", "arbitrary"), 48), name="mlstm",
    )(qk, qk, proj, proj, proj, gates_row, bias_col, bias_row, jnp.asarray(ex, BF16), jnp.asarray(lt, BF16),
      jnp.asarray(lt.T, BF16), jnp.asarray(ones_st, BF16), obd, bd, avg, gn_w.reshape(1, D_GROUP))


def _s5_kernel(u_ref, mt_ref, bg_ref, cg_ref, pa_ref, pb_ref, o_ref, *, nsteps):
    ub = u_ref[0].astype(BF16)
    e = jnp.dot(ub, bg_ref[0], preferred_element_type=F32)
    r = e.shape[0]
    row = lax.broadcasted_iota(jnp.int32, (r, LANES), 0)
    xf, xb = e[:, :LANES], e[:, LANES:]
    pa, pb = pa_ref[0], pb_ref[0]
    for s in range(nsteps):
        sh = 1 << s
        a_f, b_f = pa[s:s + 1, :LANES], pb[s:s + 1, :LANES]
        a_b, b_b = pa[s:s + 1, LANES:], pb[s:s + 1, LANES:]
        yf = jnp.where(row >= sh, pltpu.roll(xf, sh, 0), 0.0)
        yb = jnp.where(row < r - sh, pltpu.roll(xb, r - sh, 0), 0.0)
        xf = xf + a_f * yf + b_f * pltpu.roll(yf, LANES // 2, 1)
        xb = xb + a_b * yb + b_b * pltpu.roll(yb, LANES // 2, 1)
    sprev = jnp.where(row >= 1, pltpu.roll(xf, 1, 0), 0.0)
    snext = jnp.where(row < r - 1, pltpu.roll(xb, r - 1, 0), 0.0)
    st = jnp.concatenate([sprev, snext], axis=1).astype(BF16)
    o_ref[0] = (jnp.dot(ub, mt_ref[0], preferred_element_type=F32)
                + jnp.dot(st, cg_ref[0], preferred_element_type=F32))


def _s5_tables(a_re, a_im, log_dt, b_re, b_im, c_re, c_im, d_skip, tc, nsteps):
    g, p, ch = S5_GROUPS, S5_STATE, S5_CH
    hp = lax.Precision.HIGHEST
    are, aim = a_re.astype(F32), a_im.astype(F32)
    delta = jnp.exp(log_dt.astype(F32))[..., None]
    lre, lim = are * delta, aim * delta

    class Cx:
        def __init__(self, re, im):
            self.re, self.im = re, im

        def __mul__(self, o):
            return Cx(self.re * o.re - self.im * o.im, self.re * o.im + self.im * o.re)

        def __getitem__(self, idx):
            return Cx(self.re[idx], self.im[idx])

    def apow(n):
        n = jnp.asarray(n, F32)[None, None, :, None]
        mag, ang = jnp.exp(lre[:, :, None, :] * n), lim[:, :, None, :] * n
        return Cx(mag * jnp.cos(ang), mag * jnp.sin(ang))

    abr, abi = jnp.exp(lre) * jnp.cos(lim), jnp.exp(lre) * jnp.sin(lim)
    den = are * are + aim * aim
    quo = Cx(((abr - 1.0) * are + abi * aim) / den, (abi * are - (abr - 1.0) * aim) / den)
    b_bar = quo[..., None] * Cx(b_re.astype(F32)[None], b_im.astype(F32)[None])
    c = Cx(c_re.astype(F32), c_im.astype(F32))
    taus = np.arange(tc)
    cp = c[:, :, None] * apow(taus)[:, :, :, None, :]
    kk = (jnp.einsum("dgtop,dgpi->dgtoi", cp.re, b_bar.re, precision=hp)
          - jnp.einsum("dgtop,dgpi->dgtoi", cp.im, b_bar.im, precision=hp))
    dsk = d_skip.astype(F32).reshape(g, ch)[:, :, None] * jnp.eye(ch, dtype=F32)[None]
    kdiag = kk[0][:, 0] + kk[1][:, 0] + dsk
    lags = jnp.concatenate([kk[1][:, :0:-1], kdiag[:, None], kk[0][:, 1:]], axis=1)
    diff = taus[None, :] - taus[:, None]
    sel = (diff[None] + (tc - 1) == np.arange(2 * tc - 1)[:, None, None]).astype(np.float32)
    mt = jnp.einsum("jst,gjoi->gsito", jnp.asarray(sel), lags, precision=hp).reshape(g, tc * ch, tc * ch)

    zf = apow(tc - 1 - taus)[0][..., None] * b_bar[0][:, None]
    zb = apow(taus)[1][..., None] * b_bar[1][:, None]

    def to_rows(z):
        return jnp.transpose(z, (0, 1, 3, 2)).reshape(g, tc * ch, p)

    bg = jnp.concatenate([to_rows(zf.re), to_rows(zf.im), to_rows(zb.re), to_rows(zb.im)], axis=-1)

    yf = c[0][:, None] * apow(taus + 1)[0][:, :, None, :]
    yb = c[1][:, None] * apow(tc - taus)[1][:, :, None, :]

    def to_cols(z):
        return jnp.transpose(z, (0, 3, 1, 2)).reshape(g, p, tc * ch)

    cg = jnp.concatenate([to_cols(yf.re), -to_cols(yf.im), to_cols(yb.re), -to_cols(yb.im)], axis=1)

    steps = tc * (2.0 ** np.arange(nsteps))
    pw = apow(steps)
    re0, im0, re1, im1 = pw.re[0], pw.im[0], pw.re[1], pw.im[1]
    pa = jnp.concatenate([re0, re0, re1, re1], axis=-1)
    pb = jnp.concatenate([-im0, im0, -im1, im1], axis=-1)
    pad = (-nsteps) % 8
    pa = jnp.pad(pa, ((0, 0), (0, pad), (0, 0)))
    pb = jnp.pad(pb, ((0, 0), (0, pad), (0, 0)))
    return mt.astype(BF16), bg.astype(BF16), cg.astype(BF16), pa, pb


def _s5_glu_kernel(y_ref, w_ref, o_ref):
    y = y_ref[...]
    z = 0.5 * y * (1.0 + jnp.tanh(math.sqrt(2.0 / math.pi) * (y + 0.044715 * (y * y * y))))
    o_ref[...] = z * _sigmoid(jnp.dot(z.astype(BF16), w_ref[...], preferred_element_type=F32))


def _s5(proj, a_re, a_im, log_dt, b_re, b_im, c_re, c_im, d_skip, w_glu):
    bsz, l, _ = proj.shape
    tc = S5_TC
    r = l // tc
    nsteps = max(1, int(math.ceil(math.log2(r))))
    w = tc * S5_CH
    mt, bg, cg, pa, pb = _s5_tables(a_re, a_im, log_dt, b_re, b_im, c_re, c_im, d_skip, tc, nsteps)
    u = proj[:, :, CB_S5 * D_GROUP:(CB_S5 + 1) * D_GROUP]
    ug = jnp.transpose(u.reshape(bsz, r, tc, S5_GROUPS, S5_CH), (3, 0, 1, 2, 4)).reshape(S5_GROUPS, bsz * r, w)
    ns8 = pa.shape[1]
    yg = pl.pallas_call(
        functools.partial(_s5_kernel, nsteps=nsteps),
        out_shape=jax.ShapeDtypeStruct((S5_GROUPS, bsz * r, w), F32), grid=(S5_GROUPS, bsz),
        in_specs=[pl.BlockSpec((1, r, w), lambda g, b: (g, b, 0)),
                  pl.BlockSpec((1, w, w), lambda g, b: (g, 0, 0)),
                  pl.BlockSpec((1, w, D_GROUP), lambda g, b: (g, 0, 0)),
                  pl.BlockSpec((1, D_GROUP, w), lambda g, b: (g, 0, 0)),
                  pl.BlockSpec((1, ns8, D_GROUP), lambda g, b: (g, 0, 0)),
                  pl.BlockSpec((1, ns8, D_GROUP), lambda g, b: (g, 0, 0))],
        out_specs=pl.BlockSpec((1, r, w), lambda g, b: (g, b, 0)),
        compiler_params=_cparams(("parallel", "parallel"), 48), name="s5_ssm")(ug, mt, bg, cg, pa, pb)
    y = jnp.transpose(yg.reshape(S5_GROUPS, bsz, r, tc, S5_CH), (1, 2, 3, 0, 4)).reshape(bsz * l, D_GROUP)
    t = bsz * l
    tm = _tile(t, 2048)
    out = pl.pallas_call(
        _s5_glu_kernel, out_shape=jax.ShapeDtypeStruct((t, D_GROUP), F32), grid=(t // tm,),
        in_specs=[pl.BlockSpec((tm, D_GROUP), lambda i: (i, 0)), pl.BlockSpec((D_GROUP, D_GROUP), lambda i: (0, 0))],
        out_specs=pl.BlockSpec((tm, D_GROUP), lambda i: (i, 0)),
        compiler_params=_cparams(("parallel",)), name="s5_glu")(y, w_glu.astype(BF16))
    return out.reshape(bsz, l, D_GROUP)


def _hy_filter_kernel(z_ref, w1_ref, b1_ref, w2_ref, b2_ref, w3_ref, fr_ref, dec_ref, h_ref, ss_ref):
    i = pl.program_id(0)
    hp = lax.Precision.HIGHEST
    fr = fr_ref[...]
    a = jnp.sin(fr * (jnp.dot(z_ref[...], w1_ref[...], precision=hp, preferred_element_type=F32) + b1_ref[...]))
    a = jnp.sin(fr * (jnp.dot(a, w2_ref[...], precision=hp, preferred_element_type=F32) + b2_ref[...]))
    h = jnp.dot(a, w3_ref[...], precision=hp, preferred_element_type=F32)
    df, db = dec_ref[:, :D_GROUP], dec_ref[:, D_GROUP:]
    h = h * jnp.concatenate([df, df, db, db], axis=1)

    @pl.when(i == 0)
    def _():
        ss_ref[...] = jnp.zeros_like(ss_ref)

    ss_ref[...] += jnp.sum(h * h, axis=0, keepdims=True)
    nhalf = h.shape[1] // 2
    row = lax.broadcasted_iota(jnp.int32, h.shape, 0)
    col = lax.broadcasted_iota(jnp.int32, h.shape, 1)
    h_ref[...] = jnp.where((row == 0) & (col >= nhalf) & (i == 0), 0.0, h)


def _hy_filters(l, w1, b1, w2, b2, w3, freq):
    t = np.linspace(0.0, 1.0, l)[:, None]
    w = 2.0 * np.pi * np.arange(l, dtype=np.float64)[:, None] / l
    bands = np.linspace(1e-4, HY_BANDS - 1, HY_BANDS)[None, :]
    z = np.concatenate([t, np.cos(bands * w), -np.sin(bands * w)], axis=-1)
    max_decay = math.log(HY_TARGET) / HY_FAST_DECAY
    min_decay = math.log(HY_TARGET) / HY_SLOW_DECAY
    rates = np.abs(np.linspace(min_decay, max_decay, D_GROUP))
    dec = np.exp(-t * rates)
    rev = np.concatenate([[0], np.arange(l - 1, 0, -1)])
    half = LANES // 2
    zz = np.zeros((l, LANES))
    zz[:, :HY_EMB] = z
    zz[:, half:half + HY_EMB] = z[rev]
    zz = jnp.asarray(zz, F32)
    dec2 = jnp.asarray(np.concatenate([dec, dec[rev]], axis=1), F32)

    def two(m):
        m = m.astype(F32)
        top = jnp.pad(m, ((0, half - m.shape[0]), (0, half - m.shape[1])))
        zero = jnp.zeros_like(top)
        return jnp.concatenate([jnp.concatenate([top, zero], 1), jnp.concatenate([zero, top], 1)], 0)

    def twice(v):
        v = jnp.pad(v.astype(F32), (0, half - v.shape[0]))
        return jnp.concatenate([v, v]).reshape(1, LANES)

    w3r = w3.astype(F32).reshape(HY_FFN, HY_ORDER, 2, D_GROUP)
    nhalf = HY_ORDER * D_GROUP
    w3f = jnp.pad(w3r[:, :, 0].reshape(HY_FFN, nhalf), ((0, half - HY_FFN), (0, 0)))
    w3b = jnp.pad(w3r[:, :, 1].reshape(HY_FFN, nhalf), ((0, half - HY_FFN), (0, 0)))
    zero = jnp.zeros_like(w3f)
    w3p = jnp.concatenate([jnp.concatenate([w3f, zero], 1), jnp.concatenate([zero, w3b], 1)], 0)
    nout = 2 * nhalf
    tl = _tile(l, 512)

    def const(shape):
        return pl.BlockSpec(shape, lambda i: (0, 0))

    return pl.pallas_call(
        _hy_filter_kernel,
        out_shape=(jax.ShapeDtypeStruct((l, nout), F32), jax.ShapeDtypeStruct((1, nout), F32)), grid=(l // tl,),
        in_specs=[pl.BlockSpec((tl, LANES), lambda i: (i, 0)), const((LANES, LANES)), const((1, LANES)),
                  const((LANES, LANES)), const((1, LANES)), const((LANES, nout)), const((1, LANES)),
                  pl.BlockSpec((tl, 2 * D_GROUP), lambda i: (i, 0))],
        out_specs=(pl.BlockSpec((tl, nout), lambda i: (i, 0)), const((1, nout))),
        compiler_params=_cparams(("arbitrary",)), name="hyena_filter_mlp",
    )(zz, two(w1), twice(b1), two(w2), twice(b2), w3p, twice(freq), dec2)


def _dft_consts(na):
    nb = FFT_NB
    n = na * nb
    ia = np.arange(na, dtype=np.float64)
    th = 2.0 * np.pi * np.outer(ia, ia) / na
    c1, s1 = np.cos(th), np.sin(th)
    eye8 = np.eye(8)
    fa_full = np.concatenate([c1, -s1], axis=0)
    g_full = np.kron(fa_full, eye8)
    g_half = np.kron(fa_full[:, : na // 2], eye8)
    g_out = np.kron(np.concatenate([c1[: na // 2], -s1[: na // 2]], axis=1) / n, eye8)
    ib = np.arange(nb, dtype=np.float64)
    ph = 2.0 * np.pi * np.outer(ib, ib) / nb
    c2, s2 = np.cos(ph), np.sin(ph)
    fb = np.block([[c2, s2], [-s2, c2]])
    fbc = np.block([[c2, -s2], [s2, c2]])
    ps = 2.0 * np.pi * np.outer(ia, ib) / n
    twr = np.broadcast_to(np.cos(ps)[:, :, None], (na, nb, LANES))
    twi = np.broadcast_to(-np.sin(ps)[:, :, None], (na, nb, LANES))
    as_bf = lambda x: jnp.asarray(x, BF16)
    return dict(g_full=as_bf(g_full), g_half=as_bf(g_half), g_out=as_bf(g_out), fb=as_bf(fb), fbc=as_bf(fbc),
                twr=jnp.asarray(twr, F32), twi=jnp.asarray(twi, F32))


def _lane_tile(x, reps):
    return x if reps == 1 else jnp.concatenate([x] * reps, axis=-1)


def _hy_spec_kernel(a_ref, twr_ref, twi_ref, fb_ref, ss_ref, o_ref, *, kb, reps):
    scale = lax.rsqrt(ss_ref[...])
    for j in range(kb):
        ar, ai = a_ref[0, j], a_ref[1, j]
        twr, twi = _lane_tile(twr_ref[j], reps), _lane_tile(twi_ref[j], reps)
        br = twr * ar - twi * ai
        bi = twr * ai + twi * ar
        x = jnp.dot(fb_ref[...], jnp.concatenate([br, bi], axis=0).astype(BF16), preferred_element_type=F32)
        o_ref[0, j] = x[:FFT_NB] * scale
        o_ref[1, j] = x[FFT_NB:] * scale


def _hy_mid_kernel(a_ref, h_ref, twr_ref, twi_ref, fb_ref, fbc_ref, o_ref, *, kb, reps):
    for j in range(kb):
        ar, ai = a_ref[0, 0, j], a_ref[0, 1, j]
        twr, twi = _lane_tile(twr_ref[j], reps), _lane_tile(twi_ref[j], reps)
        br = twr * ar - twi * ai
        bi = twr * ai + twi * ar
        x = jnp.dot(fb_ref[...], jnp.concatenate([br, bi], axis=0).astype(BF16), preferred_element_type=F32)
        xr, xi = x[:FFT_NB], x[FFT_NB:]
        hr, hi = h_ref[0, j], h_ref[1, j]
        yr = xr * hr - xi * hi
        yi = xr * hi + xi * hr
        z = jnp.dot(fbc_ref[...], jnp.concatenate([yr, yi], axis=0).astype(BF16), preferred_element_type=F32)
        zr, zi = z[:FFT_NB], z[FFT_NB:]
        o_ref[0, 0, j] = twr * zr + twi * zi
        o_ref[0, 1, j] = twr * zi - twi * zr


def _hy_dft1_kernel(g_ref, x_ref, o_ref, *, qb):
    na_in, c = x_ref.shape[1], x_ref.shape[4]
    na = o_ref.shape[2]
    for q in range(qb):
        x = x_ref[0, :, q].reshape(na_in * 8, c).astype(BF16)
        a = jnp.dot(g_ref[...], x, preferred_element_type=F32)
        o_ref[0, :, :, q] = a.reshape(2, na, 8, c)


def _hy_fdft1_kernel(g_ref, xt_ref, xb_ref, o_ref, *, qb):
    nah, c = xt_ref.shape[1], xt_ref.shape[4]
    na = o_ref.shape[2]
    kh = nah * 8
    for q in range(qb):
        xt = xt_ref[0, :, q].reshape(kh, c).astype(BF16)
        xb = xb_ref[0, :, q].reshape(kh, c).astype(BF16)
        a = (jnp.dot(g_ref[:, :kh], xt, preferred_element_type=F32)
             + jnp.dot(g_ref[:, kh:], xb, preferred_element_type=F32))
        o_ref[0, :, :, q] = a.reshape(2, na, 8, c)


def _hy_dft1(g, x5, ncol, name):
    bsz, na_in, nq = x5.shape[:3]
    na = g.shape[0] // 16
    c = D_GROUP
    qb = FFT_QB
    return pl.pallas_call(
        functools.partial(_hy_dft1_kernel, qb=qb),
        out_shape=jax.ShapeDtypeStruct((bsz, 2, na, nq, 8, ncol * c), F32), grid=(bsz, ncol, nq // qb),
        in_specs=[pl.BlockSpec(g.shape, lambda b, j, q: (0, 0)),
                  pl.BlockSpec((1, na_in, qb, 8, c), lambda b, j, q: (b, 0, q, 0, j))],
        out_specs=pl.BlockSpec((1, 2, na, qb, 8, c), lambda b, j, q: (b, 0, 0, q, 0, j)),
        compiler_params=_cparams(("parallel", "parallel", "parallel"), 48), name=name)(g, x5)


def _hy_out_kernel(g_ref, z_ref, x_ref, v_ref, b_ref, o_ref, *, qb):
    na2, c = z_ref.shape[1] * z_ref.shape[2], z_ref.shape[5]
    nah = o_ref.shape[1]
    bias = b_ref[...].reshape(1, 1, c)
    for q in range(qb):
        z = z_ref[0, :, :, q].reshape(na2 * 8, c).astype(BF16)
        y = jnp.dot(g_ref[...], z, preferred_element_type=F32).reshape(nah, 8, c)
        o_ref[0, :, q] = x_ref[0, :, q] * (y + v_ref[0, :, q] * bias)


def _hyena(proj, conv_w, conv_b, w1, b1, w2, b2, w3, freq, bias):
    bsz, l, _ = proj.shape
    nb = FFT_NB
    na = 2 * l // nb
    nah = na // 2
    nq = nb // 8
    c = D_GROUP
    qb = FFT_QB
    dc = _dft_consts(na)
    pc = _shortconv(proj, CB_HV, 3, conv_w, conv_b, act=False)

    h, ss = _hy_filters(l, w1, b1, w2, b2, w3, freq)
    ncf = HY_ORDER * c
    ssn = ss[:, :ncf] + ss[:, ncf:]
    h5 = h.reshape(1, nah, nq, 8, 2 * ncf)
    ka = pl.pallas_call(
        functools.partial(_hy_fdft1_kernel, qb=qb),
        out_shape=jax.ShapeDtypeStruct((1, 2, na, nq, 8, ncf), F32), grid=(HY_ORDER, nq // qb),
        in_specs=[pl.BlockSpec(dc["g_full"].shape, lambda j, q: (0, 0)),
                  pl.BlockSpec((1, nah, qb, 8, c), lambda j, q: (0, 0, q, 0, j)),
                  pl.BlockSpec((1, nah, qb, 8, c), lambda j, q: (0, 0, q, 0, HY_ORDER + j))],
        out_specs=pl.BlockSpec((1, 2, na, qb, 8, c), lambda j, q: (0, 0, 0, q, 0, j)),
        compiler_params=_cparams(("parallel", "parallel"), 48), name="hyena_filter_dft1",
    )(dc["g_full"], h5, h5).reshape(2, na, nb, ncf)
    kb = 4 if na % 4 == 0 else 1
    reps = c // LANES
    tw = pl.BlockSpec((kb, nb, LANES), lambda j, k: (k, 0, 0))
    mat = pl.BlockSpec((2 * nb, 2 * nb), lambda j, k: (0, 0))
    hspec = pl.pallas_call(
        functools.partial(_hy_spec_kernel, kb=kb, reps=reps),
        out_shape=jax.ShapeDtypeStruct((2, na, nb, ncf), F32), grid=(ncf // c, na // kb),
        in_specs=[pl.BlockSpec((2, kb, nb, c), lambda j, k: (0, k, 0, j)), tw, tw, mat,
                  pl.BlockSpec((1, c), lambda j, k: (0, j))],
        out_specs=pl.BlockSpec((2, kb, nb, c), lambda j, k: (0, k, 0, j)),
        compiler_params=_cparams(("parallel", "parallel"), 48), name="hyena_filter_dft2",
    )(ka, dc["twr"], dc["twi"], dc["fb"], ssn)

    pc5 = pc.reshape(bsz, nah, nq, 8, 3 * c)

    def long_conv_gate(z5, order, xcol):
        a = _hy_dft1(dc["g_half"], z5, 1, "hyena_dft1").reshape(bsz, 2, na, nb, c)
        zmid = pl.pallas_call(
            functools.partial(_hy_mid_kernel, kb=kb, reps=reps),
            out_shape=jax.ShapeDtypeStruct((bsz, 2, na, nb, c), F32), grid=(bsz, na // kb),
            in_specs=[pl.BlockSpec((1, 2, kb, nb, c), lambda b, k: (b, 0, k, 0, 0)),
                      pl.BlockSpec((2, kb, nb, c), lambda b, k: (0, k, 0, order)), tw, tw, mat, mat],
            out_specs=pl.BlockSpec((1, 2, kb, nb, c), lambda b, k: (b, 0, k, 0, 0)),
            compiler_params=_cparams(("parallel", "parallel"), 48), name="hyena_dft_mid",
        )(a, hspec, dc["twr"], dc["twi"], dc["fb"], dc["fbc"])
        zmid = zmid.reshape(bsz, 2, na, nq, 8, c)
        sig = lambda col: pl.BlockSpec((1, nah, qb, 8, c), lambda b, q: (b, 0, q, 0, col))
        return pl.pallas_call(
            functools.partial(_hy_out_kernel, qb=qb),
            out_shape=jax.ShapeDtypeStruct((bsz, nah, nq, 8, c), F32), grid=(bsz, nq // qb),
            in_specs=[pl.BlockSpec(dc["g_out"].shape, lambda b, q: (0, 0)),
                      pl.BlockSpec((1, 2, na, qb, 8, c), lambda b, q: (b, 0, 0, q, 0, 0)),
                      sig(xcol), sig(0), pl.BlockSpec((1, c), lambda b, q: (0, 0))],
            out_specs=sig(0),
            compiler_params=_cparams(("parallel", "parallel"), 48), name="hyena_idft_gate",
        )(dc["g_out"], zmid, pc5, z5, bias[order].astype(F32).reshape(1, c))

    z1 = long_conv_gate(pc5, 0, 1)
    z2 = long_conv_gate(z1, 1, 2)
    return z2.reshape(bsz, l, c)


def _ffn_kernel(x_ref, w1_ref, w3_ref, w2_ref, lw_ref, lb_ref, o_ref, xb_ref, acc_ref, *, nf):
    f = pl.program_id(1)

    @pl.when(f == 0)
    def _():
        xb_ref[...] = x_ref[...].astype(BF16)
        acc_ref[...] = jnp.zeros_like(acc_ref)

    xb = xb_ref[...]
    a = jnp.dot(xb, w1_ref[...], preferred_element_type=F32)
    b = jnp.dot(xb, w3_ref[...], preferred_element_type=F32)
    acc_ref[...] += jnp.dot((_silu(a) * b).astype(BF16), w2_ref[...], preferred_element_type=F32)

    @pl.when(f == nf - 1)
    def _():
        o_ref[...] = _ln_core(DN_ALPHA * x_ref[...] + acc_ref[...], lw_ref[...], lb_ref[...])


def _ffn_ln(x, w1, w3, w2, lw, lb):
    t, d = x.shape
    ff = w1.shape[1]
    tm = _tile(t, 1024)
    tf = 512 if ff % 512 == 0 else (256 if ff % 256 == 0 else ff)
    nf = ff // tf
    vec = pl.BlockSpec((1, d), lambda i, f: (0, 0))
    return pl.pallas_call(
        functools.partial(_ffn_kernel, nf=nf),
        out_shape=jax.ShapeDtypeStruct((t, d), F32), grid=(t // tm, nf),
        in_specs=[pl.BlockSpec((tm, d), lambda i, f: (i, 0)),
                  pl.BlockSpec((d, tf), lambda i, f: (0, f)),
                  pl.BlockSpec((d, tf), lambda i, f: (0, f)),
                  pl.BlockSpec((tf, d), lambda i, f: (f, 0)), vec, vec],
        out_specs=pl.BlockSpec((tm, d), lambda i, f: (i, 0)),
        scratch_shapes=[pltpu.VMEM((tm, d), BF16), pltpu.VMEM((tm, d), F32)],
        compiler_params=_cparams(("parallel", "arbitrary"), 52), name="swiglu_ffn_ln",
    )(x, w1, w3, w2, lw.reshape(1, d), lb.reshape(1, d))


MOE_SB = 896
MOE_NSB = 2
MOE_SUB = 256
MOE_CUM = 128
MOE_MAXP = -(-MOE_SB // MOE_SUB)


def _moe_kernel(cnt_ref, x_ref, cmb_ref, lt_ref, w1_ref, w3_ref, w2_ref, lw_ref, lb_ref, o_ref,
                xb_ref, xs_ref, ys_ref, gs_ref, pos_ref, *, nf, nsb, t_total):
    i = pl.program_id(0)
    e = pl.program_id(1)
    f = pl.program_id(2)
    sb = x_ref.shape[0] // nsb
    npass = [(cnt_ref[(i * nsb + s) * N_EXPERTS + e] + (MOE_SUB - 1)) // MOE_SUB for s in range(nsb)]
    sub = [slice(s * sb, (s + 1) * sb) for s in range(nsb)]

    @pl.when((e == 0) & (f == 0))
    def _():
        o_ref[...] = jnp.zeros_like(o_ref)
        for s in range(nsb):
            valid = lax.broadcasted_iota(jnp.int32, (sb, 1), 0) < t_total - (i * nsb + s) * sb
            xb_ref[sub[s]] = jnp.where(valid, x_ref[sub[s]], 0.0).astype(BF16)
            carry = jnp.zeros((1, LANES), F32)
            for c in range(sb // MOE_CUM):
                rows = slice(s * sb + c * MOE_CUM, s * sb + (c + 1) * MOE_CUM)
                vrows = slice(c * MOE_CUM, (c + 1) * MOE_CUM)
                m = jnp.where(valid[vrows] & (cmb_ref[rows] > 0.0), 1.0, 0.0)
                inc = jnp.dot(lt_ref[...], m.astype(BF16), preferred_element_type=F32) + carry
                pos_ref[rows] = jnp.where(m > 0.0, inc - 1.0, -1.0)
                carry = inc[MOE_CUM - 1:MOE_CUM]

    lane = lax.broadcasted_iota(jnp.int32, (sb, LANES), 1)

    def one_hot(s, j):
        pos = jnp.sum(jnp.where(lane == e, pos_ref[sub[s]], 0.0), axis=1, keepdims=True)
        slot = lax.broadcasted_iota(jnp.int32, (sb, MOE_SUB), 1).astype(F32) + (j * MOE_SUB).astype(F32)
        return pos == slot

    @pl.when(f == 0)
    def _():
        for s in range(nsb):
            gate = jnp.sum(jnp.where(lane == e, cmb_ref[sub[s]], 0.0), axis=1, keepdims=True)

            def gather(j, carry, s=s, gate=gate):
                hit = one_hot(s, j)
                k = s * MOE_MAXP + j
                xs_ref[k] = _dot_tn(xb_ref[sub[s]], jnp.where(hit, 1.0, 0.0).astype(BF16)).astype(BF16)
                g = jnp.sum(jnp.where(hit, gate, 0.0), axis=0, keepdims=True)
                gs_ref[k] = jnp.broadcast_to(g, (8, MOE_SUB))
                ys_ref[k] = jnp.zeros(ys_ref.shape[1:], F32)
                return carry

            lax.fori_loop(0, npass[s], gather, 0)

    for s in range(nsb):
        def expert(j, carry, s=s):
            k = s * MOE_MAXP + j
            xs = xs_ref[k]
            a = _dot_tn(w1_ref[0], xs)
            b = _dot_tn(w3_ref[0], xs)
            hid = (_silu(a) * b * gs_ref[k][0:1]).astype(BF16)
            ys_ref[k] += _dot_tn(w2_ref[0], hid)
            return carry

        lax.fori_loop(0, npass[s], expert, 0)

    @pl.when(f == nf - 1)
    def _():
        for s in range(nsb):
            def scatter(j, carry, s=s):
                hit = one_hot(s, j)
                o_ref[sub[s]] += _dot_nt(jnp.where(hit, 1.0, 0.0).astype(BF16),
                                         ys_ref[s * MOE_MAXP + j].astype(BF16))
                return carry

            lax.fori_loop(0, npass[s], scatter, 0)

        @pl.when(e == pl.num_programs(1) - 1)
        def _():
            o_ref[...] = _ln_core(DN_ALPHA * x_ref[...] + o_ref[...], lw_ref[...], lb_ref[...])


def _moe_ln(x, cmb, w1, w3, w2, lw, lb):
    t, d = x.shape
    ne, _, ff = w1.shape
    nsb = MOE_NSB
    tb = nsb * MOE_SB
    nb = -(-t // tb)
    tf = 896 if ff % 896 == 0 else ff
    nf = ff // tf
    max_pass = nsb * MOE_MAXP
    cmb_p = jnp.pad(cmb, ((0, nb * tb - t), (0, 0)))
    counts = jnp.sum((cmb_p[:, :N_EXPERTS] > 0.0).reshape(nb * nsb, MOE_SB, N_EXPERTS), axis=1)
    counts = counts.astype(jnp.int32).reshape(-1)
    idx = np.arange(MOE_CUM)
    lt = jnp.asarray(idx[None, :] <= idx[:, None], BF16)
    once = pl.Buffered(1)
    grid_spec = pltpu.PrefetchScalarGridSpec(
        num_scalar_prefetch=1, grid=(nb, ne, nf),
        in_specs=[pl.BlockSpec((tb, d), lambda i, e, f, c: (i, 0), pipeline_mode=once),
                  pl.BlockSpec((tb, LANES), lambda i, e, f, c: (i, 0), pipeline_mode=once),
                  pl.BlockSpec((MOE_CUM, MOE_CUM), lambda i, e, f, c: (0, 0)),
                  pl.BlockSpec((1, d, tf), lambda i, e, f, c: (e, 0, f)),
                  pl.BlockSpec((1, d, tf), lambda i, e, f, c: (e, 0, f)),
                  pl.BlockSpec((1, tf, d), lambda i, e, f, c: (e, f, 0)),
                  pl.BlockSpec((1, d), lambda i, e, f, c: (0, 0)),
                  pl.BlockSpec((1, d), lambda i, e, f, c: (0, 0))],
        out_specs=pl.BlockSpec((tb, d), lambda i, e, f, c: (i, 0)),
        scratch_shapes=[pltpu.VMEM((tb, d), BF16), pltpu.VMEM((max_pass, d, MOE_SUB), BF16),
                        pltpu.VMEM((max_pass, d, MOE_SUB), F32), pltpu.VMEM((max_pass, 8, MOE_SUB), F32),
                        pltpu.VMEM((tb, LANES), F32)])
    return pl.pallas_call(
        functools.partial(_moe_kernel, nf=nf, nsb=nsb, t_total=t), out_shape=jax.ShapeDtypeStruct((t, d), F32),
        grid_spec=grid_spec,
        compiler_params=_cparams(("parallel", "arbitrary", "arbitrary"), 58), name="moe_routed",
    )(counts, x, cmb_p, lt, w1, w3, w2, lw.reshape(1, d), lb.reshape(1, d))


def _router_kernel(x_ref, rh_ref, rl_ref, o_ref):
    x = x_ref[...]
    xh = x.astype(BF16)
    xl = (x - xh.astype(F32)).astype(BF16)
    logits = (jnp.dot(xh, rh_ref[...], preferred_element_type=F32)
              + jnp.dot(xl, rh_ref[...], preferred_element_type=F32)
              + jnp.dot(xh, rl_ref[...], preferred_element_type=F32))
    lane = lax.broadcasted_iota(jnp.int32, logits.shape, 1).astype(F32)
    logits = jnp.where(lane < N_EXPERTS, logits, -jnp.inf)
    m1 = jnp.max(logits, axis=1, keepdims=True)
    i1 = jnp.min(jnp.where(logits == m1, lane, float(LANES)), axis=1, keepdims=True)
    rest = jnp.where(lane == i1, -jnp.inf, logits)
    m2 = jnp.max(rest, axis=1, keepdims=True)
    i2 = jnp.min(jnp.where(rest == m2, lane, float(LANES)), axis=1, keepdims=True)
    e2 = jnp.exp(m2 - m1)
    g1 = 1.0 / (1.0 + e2)
    g2 = e2 / (1.0 + e2)
    o_ref[...] = jnp.where(lane == i1, g1, 0.0) + jnp.where(lane == i2, g2, 0.0)


def _router(x, router):
    t, d = x.shape
    r = jnp.pad(router.astype(F32), ((0, 0), (0, LANES - N_EXPERTS)))
    rh = r.astype(BF16)
    rl = (r - rh.astype(F32)).astype(BF16)
    tm = _tile(t, 1024)
    return pl.pallas_call(
        _router_kernel, out_shape=jax.ShapeDtypeStruct((t, LANES), F32), grid=(t // tm,),
        in_specs=[pl.BlockSpec((tm, d), lambda i: (i, 0)), pl.BlockSpec((d, LANES), lambda i: (0, 0)),
                  pl.BlockSpec((d, LANES), lambda i: (0, 0))],
        out_specs=pl.BlockSpec((tm, LANES), lambda i: (i, 0)),
        compiler_params=_cparams(("parallel",)), name="moe_router")(x, rh, rl)


def _extended_w_in(w_in):
    w = w_in.astype(F32)
    scale = HEAD_DIM ** -0.5

    def rot_half(cols):
        c4 = cols.reshape(-1, N_HEADS, 2, HEAD_DIM // 2)
        return jnp.stack([-c4[:, :, 1], c4[:, :, 0]], axis=2).reshape(-1, D_GROUP)

    wq = w[:, 0:256]
    wk = w[:, 256:512] * scale
    main = jnp.concatenate([wq, wk, w[:, 512:3072]], axis=1)
    gates = jnp.pad(w[:, 3072:3088], ((0, 0), (0, LANES - 16)))
    ext = jnp.concatenate([main, rot_half(wq), rot_half(wk), gates], axis=1)
    return jnp.pad(ext, ((0, 0), (0, N_EXT - ext.shape[1]))).astype(BF16)


def kernel(x, ln_in_w, ln_in_b, w_in, w_out, ret_gn_w, s5_a_re, s5_a_im, s5_log_dt, s5_b_re, s5_b_im, s5_c_re, s5_c_im, s5_d, s5_w_glu, hy_conv_w, hy_conv_b, hy_w1, hy_b1, hy_w2, hy_b2, hy_w3, hy_freq, hy_bias, ml_conv_w, ml_conv_b, ml_gate_b, ml_gn_w, ln1_w, ln1_b, ln2_w, ln2_b, ffn_w1, ffn_w3, ffn_w2, moe_router, moe_w1, moe_w3, moe_w2):
    bsz, l, d = x.shape
    t = bsz * l
    cos_full, sin_full = _rope_tables(l)
    h = _layer_norm(x.reshape(t, d), ln_in_w, ln_in_b)
    for layer in range(DEPTH):
        proj = _mm(h, _extended_w_in(w_in[layer]), tm=1024, tn=1280, name="in_proj").reshape(bsz, l, N_EXT)
        y_ret = _retention(proj, ret_gn_w[layer], cos_full, sin_full)
        y_s5 = _s5(proj, s5_a_re[layer], s5_a_im[layer], s5_log_dt[layer], s5_b_re[layer], s5_b_im[layer],
                   s5_c_re[layer], s5_c_im[layer], s5_d[layer], s5_w_glu[layer])
        y_hy = _hyena(proj, hy_conv_w[layer], hy_conv_b[layer], hy_w1[layer], hy_b1[layer], hy_w2[layer],
                      hy_b2[layer], hy_w3[layer], hy_freq[layer], hy_bias[layer])
        qk = _shortconv(proj, CB_MQ, 2, ml_conv_w[layer], ml_conv_b[layer], act=True)
        gates_row = jnp.transpose(proj[:, :, GATE_COL128 * LANES:GATE_COL128 * LANES + 16], (0, 2, 1))
        y_ml = _mlstm(proj, qk, gates_row, ml_gate_b[layer], ml_gn_w[layer])
        ys = [y.reshape(t, D_GROUP) for y in (y_ret, y_s5, y_hy, y_ml)]
        h = _outproj_ln(ys, w_out[layer], h, ln1_w[layer], ln1_b[layer])
        j = layer // 2
        if layer % 2 == 0:
            h = _ffn_ln(h, ffn_w1[j].astype(BF16), ffn_w3[j].astype(BF16), ffn_w2[j].astype(BF16),
                        ln2_w[layer], ln2_b[layer])
        else:
            cmb = _router(h, moe_router[j])
            h = _moe_ln(h, cmb, moe_w1[j].astype(BF16), moe_w3[j].astype(BF16), moe_w2[j].astype(BF16),
                        ln2_w[layer], ln2_b[layer])
    return h.reshape(bsz, l, d)
```

```python
import functools
import math

import numpy as np
import jax
import jax.numpy as jnp
from jax import lax
from jax.experimental import pallas as pl
from jax.experimental.pallas import tpu as pltpu

F32 = jnp.float32
BF16 = jnp.bfloat16

D_MODEL = 1024
DEPTH = 2
D_GROUP = 256
HEAD_DIM = 64
N_HEADS = 4
CHUNK = 128
S5_CH = 16
S5_GROUPS = 16
S5_STATE = 64
HY_ORDER = 2
HY_EMB = 33
HY_BANDS = 16
HY_FFN = 64
HY_FAST_DECAY = 0.3
HY_SLOW_DECAY = 1.5
HY_TARGET = 1e-2
N_EXPERTS = 8
ROPE_BASE = 10000.0
EPS = 1e-5
DN_ALPHA = (2 * DEPTH) ** 0.25

LANES = 128
HALO = 16
S5_TC = 32
FFT_NB = 256
FFT_QB = 2
N_EXT = 3840

CB_RQ, CB_RK, CB_RV, CB_RG, CB_S5, CB_HV, CB_HX1, CB_HX2 = 0, 1, 2, 3, 4, 5, 6, 7
CB_MQ, CB_MK, CB_MV, CB_MO, CB_RQR, CB_RKR = 8, 9, 10, 11, 12, 13
GATE_COL128 = 28


def _cparams(sem, vmem_mb=None):
    kw = dict(dimension_semantics=sem)
    if vmem_mb is not None:
        kw["vmem_limit_bytes"] = vmem_mb * 1024 * 1024
    return pltpu.CompilerParams(**kw)


def _tile(n, pref):
    return pref if n % pref == 0 else n


def _split_dot(x, m, parts=3):
    acc = None
    r = x
    for _ in range(parts):
        hi = r.astype(BF16)
        t = jnp.dot(hi, m, preferred_element_type=F32)
        acc = t if acc is None else acc + t
        r = r - hi.astype(F32)
    return acc


def _split_dot_left(m, x, parts=3):
    acc = None
    r = x
    for _ in range(parts):
        hi = r.astype(BF16)
        t = jnp.dot(m, hi, preferred_element_type=F32)
        acc = t if acc is None else acc + t
        r = r - hi.astype(F32)
    return acc


def _dot_nt(a, b):
    return lax.dot_general(a, b, (((1,), (1,)), ((), ())), preferred_element_type=F32)


def _dot_tn(a, b):
    return lax.dot_general(a, b, (((0,), (0,)), ((), ())), preferred_element_type=F32)


def _sigmoid(x):
    return 1.0 / (1.0 + jnp.exp(-x))


def _silu(x):
    return x * _sigmoid(x)


def _log_sigmoid(x):
    return jnp.minimum(x, 0.0) - jnp.log(1.0 + jnp.exp(-jnp.abs(x)))


def _head_masks(dtype):
    lane = lax.broadcasted_iota(jnp.int32, (1, D_GROUP), 1)
    return [((lane >= h * HEAD_DIM) & (lane < (h + 1) * HEAD_DIM)).astype(dtype) for h in range(N_HEADS)]


def _ln_core(x, w, b):
    mu = jnp.mean(x, -1, keepdims=True)
    xc = x - mu
    var = jnp.mean(xc * xc, -1, keepdims=True)
    return xc * lax.rsqrt(var + EPS) * w + b


def _ln_kernel(x_ref, w_ref, b_ref, o_ref):
    o_ref[...] = _ln_core(x_ref[...], w_ref[...], b_ref[...])


def _layer_norm(x, w, b):
    t, d = x.shape
    tm = _tile(t, 512)
    row = pl.BlockSpec((tm, d), lambda i: (i, 0))
    vec = pl.BlockSpec((1, d), lambda i: (0, 0))
    return pl.pallas_call(_ln_kernel, out_shape=jax.ShapeDtypeStruct((t, d), F32), grid=(t // tm,),
                          in_specs=[row, vec, vec], out_specs=row,
                          compiler_params=_cparams(("parallel",)), name="layer_norm")(x, w.reshape(1, d), b.reshape(1, d))


IN_PROJ_TN = 1280


def _in_proj_kernel(a_ref, b_ref, o_ref, g_ref, *, nj, gate_off):
    acc = jnp.dot(a_ref[...].astype(BF16), b_ref[...], preferred_element_type=F32)
    o_ref[...] = acc.astype(o_ref.dtype)

    @pl.when(pl.program_id(1) == nj - 1)
    def _():
        g_ref[...] = acc[:, gate_off:gate_off + LANES]


def _in_proj(h, w_ext):
    m, k = h.shape
    n = w_ext.shape[1]
    tm, tn = _tile(m, 1024), IN_PROJ_TN
    nj = n // tn
    gate_off = GATE_COL128 * LANES - (nj - 1) * tn
    return pl.pallas_call(
        functools.partial(_in_proj_kernel, nj=nj, gate_off=gate_off),
        out_shape=(jax.ShapeDtypeStruct((m, n), BF16), jax.ShapeDtypeStruct((m, LANES), F32)), grid=(m // tm, nj),
        in_specs=[pl.BlockSpec((tm, k), lambda i, j: (i, 0)), pl.BlockSpec((k, tn), lambda i, j: (0, j))],
        out_specs=(pl.BlockSpec((tm, tn), lambda i, j: (i, j)), pl.BlockSpec((tm, LANES), lambda i, j: (i, 0))),
        compiler_params=_cparams(("parallel", "arbitrary"), 48), name="in_proj")(h, w_ext)


def _outproj_ln_kernel(y0_ref, y1_ref, y2_ref, y3_ref, w_ref, h_ref, lw_ref, lb_ref, o_ref):
    mix = None
    for g, y_ref in enumerate((y0_ref, y1_ref, y2_ref, y3_ref)):
        part = jnp.dot(y_ref[...].astype(BF16), w_ref[g * D_GROUP:(g + 1) * D_GROUP, :], preferred_element_type=F32)
        mix = part if mix is None else mix + part
    o_ref[...] = _ln_core(DN_ALPHA * h_ref[...] + mix, lw_ref[...], lb_ref[...])


def _outproj_ln(ys, w_out, h, lw, lb):
    t, d = h.shape
    tm = _tile(t, 1024)
    grp = pl.BlockSpec((tm, D_GROUP), lambda i: (i, 0))
    row = pl.BlockSpec((tm, d), lambda i: (i, 0))
    vec = pl.BlockSpec((1, d), lambda i: (0, 0))
    return pl.pallas_call(
        _outproj_ln_kernel, out_shape=jax.ShapeDtypeStruct((t, d), F32), grid=(t // tm,),
        in_specs=[grp, grp, grp, grp, pl.BlockSpec((d, d), lambda i: (0, 0)), row, vec, vec], out_specs=row,
        compiler_params=_cparams(("parallel",), 48), name="out_proj_ln",
    )(*ys, w_out.astype(BF16), h, lw.reshape(1, d), lb.reshape(1, d))


def _shortconv_kernel(x_ref, xp_ref, xn_ref, w_ref, b_ref, o_ref, *, nt, act):
    i = pl.program_id(1)
    x = x_ref[0].astype(F32)
    tl = x.shape[0]
    row = lax.broadcasted_iota(jnp.int32, x.shape, 0)
    prev_row = jnp.where(i == 0, 0.0, xp_ref[0, HALO - 1:HALO, :].astype(F32))
    next_row = jnp.where(i == nt - 1, 0.0, xn_ref[0, 0:1, :].astype(F32))
    x_prev = jnp.where(row == 0, prev_row, pltpu.roll(x, 1, 0))
    x_next = jnp.where(row == tl - 1, next_row, pltpu.roll(x, tl - 1, 0))
    w = w_ref[0]
    y = b_ref[0, 0:1] + x_prev * w[0:1] + x * w[1:2] + x_next * w[2:3]
    if act:
        y = _silu(y)
    o_ref[0] = y


def _shortconv(proj, col0, nblk, w, b, act):
    bsz, l, _ = proj.shape
    tl = _tile(l, 1024)
    nt = l // tl
    w3 = jnp.transpose(w.reshape(3, nblk, D_GROUP), (1, 0, 2))
    w3 = jnp.pad(w3, ((0, 0), (0, 5), (0, 0)))
    b3 = jnp.broadcast_to(b.reshape(nblk, 1, D_GROUP), (nblk, 8, D_GROUP))
    rh = tl // HALO
    return pl.pallas_call(
        functools.partial(_shortconv_kernel, nt=nt, act=act),
        out_shape=jax.ShapeDtypeStruct((bsz, l, nblk * D_GROUP), F32), grid=(bsz, nt, nblk),
        in_specs=[
            pl.BlockSpec((1, tl, D_GROUP), lambda bb, i, j: (bb, i, col0 + j)),
            pl.BlockSpec((1, HALO, D_GROUP), lambda bb, i, j: (bb, jnp.maximum(i * rh - 1, 0), col0 + j)),
            pl.BlockSpec((1, HALO, D_GROUP), lambda bb, i, j: (bb, jnp.minimum((i + 1) * rh, l // HALO - 1), col0 + j)),
            pl.BlockSpec((1, 8, D_GROUP), lambda bb, i, j: (j, 0, 0)),
            pl.BlockSpec((1, 8, D_GROUP), lambda bb, i, j: (j, 0, 0)),
        ],
        out_specs=pl.BlockSpec((1, tl, D_GROUP), lambda bb, i, j: (bb, i, j)),
        compiler_params=_cparams(("parallel", "parallel", "parallel")), name="shortconv")(proj, proj, proj, w3, b3)


def _stack_heads(xb, masks):
    return jnp.concatenate([xb * masks[h] for h in range(N_HEADS)], axis=0)


def _compact(s):
    return s[0:64] + s[64:128] + s[128:192] + s[192:256]


def _expand(c, bd):
    return jnp.concatenate([c, c, c, c], axis=0) * bd


def _head_norm(o, avg, gn):
    mu = _split_dot(o, avg, parts=2)
    oc = o - mu
    var = _split_dot(oc * oc, avg, parts=2)
    return oc * lax.rsqrt(var + EPS) * gn


def _ret_kernel(q_ref, qr_ref, k_ref, kr_ref, v_ref, g_ref, cos_ref, sin_ref,
                dsym_ref, qdf_ref, qdb_ref, kdf_ref, kdb_ref, cdec_ref, bd_ref, avg_ref, gn_ref,
                o_ref, sfw_ref, sbw_ref, save_ref, *, cb, nblk):
    p = pl.program_id(0)
    i = pl.program_id(1)
    bsz = q_ref.shape[0]
    masks = _head_masks(BF16)
    bd = bd_ref[...]
    cdec = cdec_ref[...]

    def rope_k(b, rows):
        return k_ref[b, rows] * cos_ref[rows] + kr_ref[b, rows] * sin_ref[rows]

    def kv_update(s, k, decay, vb):
        kv = _dot_tn((k * decay).astype(BF16), vb)
        return s * cdec + kv * bd

    @pl.when(p == 0)
    def _():
        @pl.when(i == 0)
        def _():
            sbw_ref[...] = jnp.zeros_like(sbw_ref)

        blk = nblk - 1 - i
        for c in reversed(range(cb)):
            rows = slice(c * CHUNK, (c + 1) * CHUNK)
            for b in range(bsz):
                s = sbw_ref[b]
                save_ref[b, blk * cb + c] = _compact(s)
                sbw_ref[b] = kv_update(s, rope_k(b, rows), kdb_ref[...], v_ref[b, rows].astype(BF16))

    @pl.when(p == 1)
    def _():
        @pl.when(i == 0)
        def _():
            sfw_ref[...] = jnp.zeros_like(sfw_ref)

        for c in range(cb):
            rows = slice(c * CHUNK, (c + 1) * CHUNK)
            for b in range(bsz):
                q = q_ref[b, rows] * cos_ref[rows] + qr_ref[b, rows] * sin_ref[rows]
                k = rope_k(b, rows)
                qb, kb, vb = q.astype(BF16), k.astype(BF16), v_ref[b, rows].astype(BF16)
                s_all = _dot_nt(qb, _stack_heads(kb, masks))
                pmat = (s_all * dsym_ref[...]).astype(BF16)
                o = jnp.dot(pmat, _stack_heads(vb, masks), preferred_element_type=F32)
                sfw = sfw_ref[b]
                sbw = _expand(save_ref[b, i * cb + c], bd)
                o = o + jnp.dot(qb, sfw.astype(BF16), preferred_element_type=F32) * qdf_ref[...]
                o = o + jnp.dot(qb, sbw.astype(BF16), preferred_element_type=F32) * qdb_ref[...]
                y = _head_norm(o, avg_ref[...], gn_ref[...])
                o_ref[b, rows] = _silu(g_ref[b, rows].astype(F32)) * y
                sfw_ref[b] = kv_update(sfw, k, kdf_ref[...], vb)


def _ret_tables():
    lg = np.log(1.0 - 2.0 ** (-5.0 - np.arange(N_HEADS, dtype=np.float64)))
    pos = np.arange(CHUNK, dtype=np.float64)
    lag = np.abs(pos[:, None] - pos[None, :])
    dsym = np.concatenate([np.exp(lg[h] * lag) for h in range(N_HEADS)], axis=1)
    lane_lg = np.repeat(lg, HEAD_DIM)[None, :]
    qdf = np.exp(lane_lg * (pos[:, None] + 1.0))
    qdb = np.exp(lane_lg * (CHUNK - pos[:, None]))
    kdf = np.exp(lane_lg * (CHUNK - 1.0 - pos[:, None]))
    kdb = np.exp(lane_lg * pos[:, None])
    cdec = np.exp(lane_lg * CHUNK)
    return [jnp.asarray(t, F32) for t in (dsym, qdf, qdb, kdf, kdb, cdec)]


def _block_diag_mask():
    hid = np.arange(D_GROUP) // HEAD_DIM
    return (hid[:, None] == hid[None, :]).astype(np.float32)


def _rope_tables(l):
    half = HEAD_DIM // 2
    inv = ROPE_BASE ** (-np.arange(half, dtype=np.float64) / half)
    ang = np.arange(l, dtype=np.float64)[:, None] * inv[None, :]
    cos, sin = np.cos(ang), np.sin(ang)
    cos_full = np.tile(np.concatenate([cos, cos], -1), (1, N_HEADS))
    sin_full = np.tile(np.concatenate([sin, sin], -1), (1, N_HEADS))
    return jnp.asarray(cos_full, F32), jnp.asarray(sin_full, F32)


def _retention(proj, gn_w, cos_full, sin_full):
    bsz, l, _ = proj.shape
    nc = l // CHUNK
    cb = 4 if nc % 4 == 0 else 1
    nblk = nc // cb
    tl = cb * CHUNK
    dsym, qdf, qdb, kdf, kdb, cdec = _ret_tables()
    bd = jnp.asarray(_block_diag_mask())
    avg = jnp.asarray(_block_diag_mask() / HEAD_DIM, BF16)

    def both(col):
        return pl.BlockSpec((bsz, tl, D_GROUP), lambda p, i: (0, i + (1 - p) * (nblk - 1 - 2 * i), col))

    def fwd_only(col):
        return pl.BlockSpec((bsz, tl, D_GROUP), lambda p, i: (0, p * i, col))

    tab = pl.BlockSpec((tl, D_GROUP), lambda p, i: (i + (1 - p) * (nblk - 1 - 2 * i), 0))

    def const(shape):
        return pl.BlockSpec(shape, lambda p, i: (0,) * len(shape))

    return pl.pallas_call(
        functools.partial(_ret_kernel, cb=cb, nblk=nblk),
        out_shape=jax.ShapeDtypeStruct((bsz, l, D_GROUP), F32), grid=(2, nblk),
        in_specs=[fwd_only(CB_RQ), fwd_only(CB_RQR), both(CB_RK), both(CB_RKR), both(CB_RV), fwd_only(CB_RG),
                  tab, tab, const((CHUNK, 4 * CHUNK)), const((CHUNK, D_GROUP)), const((CHUNK, D_GROUP)),
                  const((CHUNK, D_GROUP)), const((CHUNK, D_GROUP)), const((1, D_GROUP)),
                  const((D_GROUP, D_GROUP)), const((D_GROUP, D_GROUP)), const((1, D_GROUP))],
        out_specs=pl.BlockSpec((bsz, tl, D_GROUP), lambda p, i: (0, p * i, 0)),
        scratch_shapes=[pltpu.VMEM((bsz, D_GROUP, D_GROUP), F32), pltpu.VMEM((bsz, D_GROUP, D_GROUP), F32),
                        pltpu.VMEM((bsz, nc, HEAD_DIM, D_GROUP), F32)],
        compiler_params=_cparams(("arbitrary", "arbitrary"), 48), name="retention",
    )(proj, proj, proj, proj, proj, proj, cos_full, sin_full, dsym, qdf, qdb, kdf, kdb, cdec, bd, avg,
      gn_w.reshape(1, D_GROUP))


def _mlstm_kernel(q_ref, k_ref, v_ref, og_ref, gc_ref, gr_ref, bc_ref, br_ref, ex_ref, lt_ref, ut_ref,
                  ones_ref, obd_ref, bd_ref, avg_ref, gn_ref,
                  o_ref, cfw_ref, cbw_ref, nmfw_ref, nmbw_ref, csave_ref, nmsave_ref, *, cb, nblk):
    p = pl.program_id(0)
    i = pl.program_id(1)
    bsz = q_ref.shape[0]
    masks = _head_masks(BF16)
    bd = bd_ref[...]
    lt = lt_ref[...]
    ut = ut_ref[...]
    ri = lax.broadcasted_iota(jnp.int32, (CHUNK, CHUNK), 0)
    ci = lax.broadcasted_iota(jnp.int32, (CHUNK, CHUNK), 1)
    lane = lax.broadcasted_iota(jnp.int32, (1, D_GROUP), 1)

    def gates_expanded(b, rows):
        return _split_dot(gc_ref[b, rows] + bc_ref[...], ex_ref[...])

    def state_update(c_ref, nm_ref, b, total, cum, i_x, k, vb):
        m_prev = nm_ref[b, 1:2]
        g = (total - cum) + i_x
        m_new = jnp.maximum(total + m_prev, jnp.max(g, axis=0, keepdims=True))
        wk = jnp.exp(g - m_new) * k
        decay = jnp.exp(total + m_prev - m_new)
        c_ref[b] = c_ref[b] * decay + _dot_tn(wk.astype(BF16), vb) * bd
        nm_ref[b, 0:1] = decay * nm_ref[b, 0:1] + jnp.sum(wk, axis=0, keepdims=True)
        nm_ref[b, 1:2] = m_new

    @pl.when(p == 0)
    def _():
        @pl.when(i == 0)
        def _():
            cbw_ref[...] = jnp.zeros_like(cbw_ref)
            nmbw_ref[...] = jnp.zeros_like(nmbw_ref)

        blk = nblk - 1 - i
        for c in reversed(range(cb)):
            rows = slice(c * CHUNK, (c + 1) * CHUNK)
            for b in range(bsz):
                csave_ref[b, blk * cb + c] = _compact(cbw_ref[b])
                nmsave_ref[b, blk * cb + c] = nmbw_ref[b]
                gx = gates_expanded(b, rows)
                cum = _split_dot_left(ut, _log_sigmoid(gx[:, 768:1024]))
                k = k_ref[b, rows] * (HEAD_DIM ** -0.5)
                state_update(cbw_ref, nmbw_ref, b, cum[0:1], cum, gx[:, 512:768], k, v_ref[b, rows].astype(BF16))

    def chunk_out(b, rows, cidx):
        q = q_ref[b, rows]
        k = k_ref[b, rows] * (HEAD_DIM ** -0.5)
        qb, kb, vb = q.astype(BF16), k.astype(BF16), v_ref[b, rows].astype(BF16)
        s_all = _dot_nt(qb, _stack_heads(kb, masks))
        vaug = jnp.concatenate([_stack_heads(vb, masks), ones_ref[...]], axis=1)
        gx = gates_expanded(b, rows)
        graw = gr_ref[b, :, rows] + br_ref[...]
        gls = _log_sigmoid(graw)
        cum_r_fw = _split_dot(gls, ut)
        cum_r_bw = _split_dot(gls, lt)
        ccomp = csave_ref[b, cidx]
        nmb = nmsave_ref[b, cidx]

        def direction(i_x, f_x, tri, cum_r, i_row0, f_row0, mask, c_state, n_vec, m_prev, total_row):
            cum = _split_dot_left(tri, _log_sigmoid(f_x))
            total = cum[total_row:total_row + 1]
            inter = cum + m_prev
            ps, rmax = [], []
            dms = []
            for h in range(N_HEADS):
                a_col = cum[:, h * HEAD_DIM:h * HEAD_DIM + 1]
                dm = a_col - cum_r[f_row0 + h:f_row0 + h + 1] + graw[i_row0 + h:i_row0 + h + 1]
                dm = jnp.where(mask, dm, -jnp.inf)
                dms.append(dm)
                rmax.append(jnp.max(dm, axis=-1, keepdims=True))
            rmax256 = jnp.where(lane < 64, rmax[0], jnp.where(lane < 128, rmax[1],
                                jnp.where(lane < 192, rmax[2], rmax[3])))
            m_row = jnp.maximum(inter, rmax256)
            for h in range(N_HEADS):
                m_h = m_row[:, h * HEAD_DIM:h * HEAD_DIM + 1]
                ps.append(s_all[:, h * CHUNK:(h + 1) * CHUNK] * jnp.exp(dms[h] - m_h))
            pmat = jnp.concatenate(ps, axis=1).astype(BF16)
            nd = jnp.dot(pmat, vaug, preferred_element_type=F32)
            w_inter = jnp.exp(inter - m_row)
            qc = jnp.dot(qb, c_state.astype(BF16), preferred_element_type=F32)
            qn = _split_dot(q * n_vec, obd_ref[...])
            num = nd[:, :D_GROUP] + w_inter * qc
            den = nd[:, D_GROUP:] + w_inter * qn
            hdir = num / jnp.maximum(jnp.abs(den), jnp.exp(-m_row))
            return hdir, total, cum

        h_fw, tot_fw, cum_fw = direction(gx[:, 0:256], gx[:, 256:512], lt, cum_r_fw, 0, 4, ri >= ci,
                                         cfw_ref[b], nmfw_ref[b, 0:1], nmfw_ref[b, 1:2], CHUNK - 1)
        h_bw, _, _ = direction(gx[:, 512:768], gx[:, 768:1024], ut, cum_r_bw, 8, 12, ci >= ri,
                               _expand(ccomp, bd), nmb[0:1], nmb[1:2], 0)
        y = _head_norm(h_fw + h_bw, avg_ref[...], gn_ref[...])
        o_ref[b, rows] = _sigmoid(og_ref[b, rows].astype(F32)) * y
        state_update(cfw_ref, nmfw_ref, b, tot_fw, cum_fw, gx[:, 0:256], k, vb)

    @pl.when(p == 1)
    def _():
        @pl.when(i == 0)
        def _():
            cfw_ref[...] = jnp.zeros_like(cfw_ref)
            nmfw_ref[...] = jnp.zeros_like(nmfw_ref)

        for c in range(cb):
            rows = slice(c * CHUNK, (c + 1) * CHUNK)
            for b in range(bsz):
                chunk_out(b, rows, i * cb + c)


def _mlstm(proj, qk, gates, gate_b, gn_w):
    gates_row = jnp.transpose(gates[:, :, :16], (0, 2, 1))
    bsz, l, _ = proj.shape
    nc = l // CHUNK
    cb = 2 if nc % 2 == 0 else 1
    nblk = nc // cb
    tl = cb * CHUNK
    bd_np = _block_diag_mask()
    bd = jnp.asarray(bd_np)
    avg = jnp.asarray(bd_np / HEAD_DIM, BF16)
    obd = jnp.asarray(bd_np, BF16)
    ex = np.zeros((LANES, 4 * D_GROUP), np.float32)
    for j in range(16):
        typ, h = divmod(j, N_HEADS)
        ex[j, typ * D_GROUP + h * HEAD_DIM: typ * D_GROUP + (h + 1) * HEAD_DIM] = 1.0
    idx = np.arange(CHUNK)
    lt = (idx[None, :] <= idx[:, None]).astype(np.float32)
    ones_st = np.repeat(np.repeat(np.eye(N_HEADS, dtype=np.float32), CHUNK, 0), HEAD_DIM, 1)
    gb = gate_b.astype(F32).reshape(16)
    bias_col = jnp.pad(gb, (0, LANES - 16)).reshape(1, LANES)
    bias_row = jnp.broadcast_to(gb.reshape(16, 1), (16, CHUNK))

    def both(arr_col, width=D_GROUP):
        return pl.BlockSpec((bsz, tl, width), lambda p, i: (0, i + (1 - p) * (nblk - 1 - 2 * i), arr_col))

    def fwd_only(arr_col):
        return pl.BlockSpec((bsz, tl, D_GROUP), lambda p, i: (0, p * i, arr_col))

    def const(shape):
        return pl.BlockSpec(shape, lambda p, i: (0,) * len(shape))

    return pl.pallas_call(
        functools.partial(_mlstm_kernel, cb=cb, nblk=nblk),
        out_shape=jax.ShapeDtypeStruct((bsz, l, D_GROUP), F32), grid=(2, nblk),
        in_specs=[fwd_only(0), both(1), both(CB_MV), fwd_only(CB_MO), both(0, LANES),
                  pl.BlockSpec((bsz, 16, tl), lambda p, i: (0, 0, p * i)),
                  const((1, LANES)), const((16, CHUNK)), const((LANES, 4 * D_GROUP)),
                  const((CHUNK, CHUNK)), const((CHUNK, CHUNK)), const((4 * CHUNK, D_GROUP)),
                  const((D_GROUP, D_GROUP)), const((D_GROUP, D_GROUP)), const((D_GROUP, D_GROUP)),
                  const((1, D_GROUP))],
        out_specs=pl.BlockSpec((bsz, tl, D_GROUP), lambda p, i: (0, p * i, 0)),
        scratch_shapes=[pltpu.VMEM((bsz, D_GROUP, D_GROUP), F32), pltpu.VMEM((bsz, D_GROUP, D_GROUP), F32),
                        pltpu.VMEM((bsz, 8, D_GROUP), F32), pltpu.VMEM((bsz, 8, D_GROUP), F32),
                        pltpu.VMEM((bsz, nc, HEAD_DIM, D_GROUP), F32), pltpu.VMEM((bsz, nc, 8, D_GROUP), F32)],
        compiler_params=_cparams(("arbitrary", "arbitrary"), 48), name="mlstm",
    )(qk, qk, proj, proj, gates, gates_row, bias_col, bias_row, jnp.asarray(ex, BF16), jnp.asarray(lt, BF16),
      jnp.asarray(lt.T, BF16), jnp.asarray(ones_st, BF16), obd, bd, avg, gn_w.reshape(1, D_GROUP))


def _s5_kernel(u_ref, mt_ref, bg_ref, cg_ref, pa_ref, pb_ref, o_ref, *, nsteps):
    ub = u_ref[0].astype(BF16)
    e = jnp.dot(ub, bg_ref[0], preferred_element_type=F32)
    r = e.shape[0]
    row = lax.broadcasted_iota(jnp.int32, (r, LANES), 0)
    xf, xb = e[:, :LANES], e[:, LANES:]
    pa, pb = pa_ref[0], pb_ref[0]
    for s in range(nsteps):
        sh = 1 << s
        a_f, b_f = pa[s:s + 1, :LANES], pb[s:s + 1, :LANES]
        a_b, b_b = pa[s:s + 1, LANES:], pb[s:s + 1, LANES:]
        yf = jnp.where(row >= sh, pltpu.roll(xf, sh, 0), 0.0)
        yb = jnp.where(row < r - sh, pltpu.roll(xb, r - sh, 0), 0.0)
        xf = xf + a_f * yf + b_f * pltpu.roll(yf, LANES // 2, 1)
        xb = xb + a_b * yb + b_b * pltpu.roll(yb, LANES // 2, 1)
    sprev = jnp.where(row >= 1, pltpu.roll(xf, 1, 0), 0.0)
    snext = jnp.where(row < r - 1, pltpu.roll(xb, r - 1, 0), 0.0)
    st = jnp.concatenate([sprev, snext], axis=1).astype(BF16)
    o_ref[0] = (jnp.dot(ub, mt_ref[0], preferred_element_type=F32)
                + jnp.dot(st, cg_ref[0], preferred_element_type=F32))


def _s5_tables(a_re, a_im, log_dt, b_re, b_im, c_re, c_im, d_skip, tc, nsteps):
    g, p, ch = S5_GROUPS, S5_STATE, S5_CH
    hp = lax.Precision.HIGHEST
    are, aim = a_re.astype(F32), a_im.astype(F32)
    delta = jnp.exp(log_dt.astype(F32))[..., None]
    lre, lim = are * delta, aim * delta

    class Cx:
        def __init__(self, re, im):
            self.re, self.im = re, im

        def __mul__(self, o):
            return Cx(self.re * o.re - self.im * o.im, self.re * o.im + self.im * o.re)

        def __getitem__(self, idx):
            return Cx(self.re[idx], self.im[idx])

    def apow(n):
        n = jnp.asarray(n, F32)[None, None, :, None]
        mag, ang = jnp.exp(lre[:, :, None, :] * n), lim[:, :, None, :] * n
        return Cx(mag * jnp.cos(ang), mag * jnp.sin(ang))

    abr, abi = jnp.exp(lre) * jnp.cos(lim), jnp.exp(lre) * jnp.sin(lim)
    den = are * are + aim * aim
    quo = Cx(((abr - 1.0) * are + abi * aim) / den, (abi * are - (abr - 1.0) * aim) / den)
    b_bar = quo[..., None] * Cx(b_re.astype(F32)[None], b_im.astype(F32)[None])
    c = Cx(c_re.astype(F32), c_im.astype(F32))
    taus = np.arange(tc)
    cp = c[:, :, None] * apow(taus)[:, :, :, None, :]
    kk = (jnp.einsum("dgtop,dgpi->dgtoi", cp.re, b_bar.re, precision=hp)
          - jnp.einsum("dgtop,dgpi->dgtoi", cp.im, b_bar.im, precision=hp))
    dsk = d_skip.astype(F32).reshape(g, ch)[:, :, None] * jnp.eye(ch, dtype=F32)[None]
    kdiag = kk[0][:, 0] + kk[1][:, 0] + dsk
    lags = jnp.concatenate([kk[1][:, :0:-1], kdiag[:, None], kk[0][:, 1:]], axis=1)
    diff = taus[None, :] - taus[:, None]
    sel = (diff[None] + (tc - 1) == np.arange(2 * tc - 1)[:, None, None]).astype(np.float32)
    mt = jnp.einsum("jst,gjoi->gsito", jnp.asarray(sel), lags, precision=hp).reshape(g, tc * ch, tc * ch)

    zf = apow(tc - 1 - taus)[0][..., None] * b_bar[0][:, None]
    zb = apow(taus)[1][..., None] * b_bar[1][:, None]

    def to_rows(z):
        return jnp.transpose(z, (0, 1, 3, 2)).reshape(g, tc * ch, p)

    bg = jnp.concatenate([to_rows(zf.re), to_rows(zf.im), to_rows(zb.re), to_rows(zb.im)], axis=-1)

    yf = c[0][:, None] * apow(taus + 1)[0][:, :, None, :]
    yb = c[1][:, None] * apow(tc - taus)[1][:, :, None, :]

    def to_cols(z):
        return jnp.transpose(z, (0, 3, 1, 2)).reshape(g, p, tc * ch)

    cg = jnp.concatenate([to_cols(yf.re), -to_cols(yf.im), to_cols(yb.re), -to_cols(yb.im)], axis=1)

    steps = tc * (2.0 ** np.arange(nsteps))
    pw = apow(steps)
    re0, im0, re1, im1 = pw.re[0], pw.im[0], pw.re[1], pw.im[1]
    pa = jnp.concatenate([re0, re0, re1, re1], axis=-1)
    pb = jnp.concatenate([-im0, im0, -im1, im1], axis=-1)
    pad = (-nsteps) % 8
    pa = jnp.pad(pa, ((0, 0), (0, pad), (0, 0)))
    pb = jnp.pad(pb, ((0, 0), (0, pad), (0, 0)))
    return mt.astype(BF16), bg.astype(BF16), cg.astype(BF16), pa, pb


def _s5_glu_kernel(y_ref, w_ref, o_ref):
    y = y_ref[...]
    z = 0.5 * y * (1.0 + jnp.tanh(math.sqrt(2.0 / math.pi) * (y + 0.044715 * (y * y * y))))
    o_ref[...] = z * _sigmoid(jnp.dot(z.astype(BF16), w_ref[...], preferred_element_type=F32))


def _s5(proj, a_re, a_im, log_dt, b_re, b_im, c_re, c_im, d_skip, w_glu):
    bsz, l, _ = proj.shape
    tc = S5_TC
    r = l // tc
    nsteps = max(1, int(math.ceil(math.log2(r))))
    w = tc * S5_CH
    mt, bg, cg, pa, pb = _s5_tables(a_re, a_im, log_dt, b_re, b_im, c_re, c_im, d_skip, tc, nsteps)
    u = proj[:, :, CB_S5 * D_GROUP:(CB_S5 + 1) * D_GROUP].astype(F32)
    ug = jnp.transpose(u.reshape(bsz, r, tc, S5_GROUPS, S5_CH), (3, 0, 1, 2, 4)).reshape(S5_GROUPS, bsz * r, w)
    ns8 = pa.shape[1]
    yg = pl.pallas_call(
        functools.partial(_s5_kernel, nsteps=nsteps),
        out_shape=jax.ShapeDtypeStruct((S5_GROUPS, bsz * r, w), F32), grid=(S5_GROUPS, bsz),
        in_specs=[pl.BlockSpec((1, r, w), lambda g, b: (g, b, 0)),
                  pl.BlockSpec((1, w, w), lambda g, b: (g, 0, 0)),
                  pl.BlockSpec((1, w, D_GROUP), lambda g, b: (g, 0, 0)),
                  pl.BlockSpec((1, D_GROUP, w), lambda g, b: (g, 0, 0)),
                  pl.BlockSpec((1, ns8, D_GROUP), lambda g, b: (g, 0, 0)),
                  pl.BlockSpec((1, ns8, D_GROUP), lambda g, b: (g, 0, 0))],
        out_specs=pl.BlockSpec((1, r, w), lambda g, b: (g, b, 0)),
        compiler_params=_cparams(("parallel", "parallel"), 48), name="s5_ssm")(ug, mt, bg, cg, pa, pb)
    y = jnp.transpose(yg.reshape(S5_GROUPS, bsz, r, tc, S5_CH), (1, 2, 3, 0, 4)).reshape(bsz * l, D_GROUP)
    t = bsz * l
    tm = _tile(t, 2048)
    out = pl.pallas_call(
        _s5_glu_kernel, out_shape=jax.ShapeDtypeStruct((t, D_GROUP), F32), grid=(t // tm,),
        in_specs=[pl.BlockSpec((tm, D_GROUP), lambda i: (i, 0)), pl.BlockSpec((D_GROUP, D_GROUP), lambda i: (0, 0))],
        out_specs=pl.BlockSpec((tm, D_GROUP), lambda i: (i, 0)),
        compiler_params=_cparams(("parallel",)), name="s5_glu")(y, w_glu.astype(BF16))
    return out.reshape(bsz, l, D_GROUP)


def _hy_filter_kernel(z_ref, w1_ref, b1_ref, w2_ref, b2_ref, w3_ref, fr_ref, dec_ref, h_ref, ss_ref):
    i = pl.program_id(0)
    hp = lax.Precision.HIGHEST
    fr = fr_ref[...]
    a = jnp.sin(fr * (jnp.dot(z_ref[...], w1_ref[...], precision=hp, preferred_element_type=F32) + b1_ref[...]))
    a = jnp.sin(fr * (jnp.dot(a, w2_ref[...], precision=hp, preferred_element_type=F32) + b2_ref[...]))
    h = jnp.dot(a, w3_ref[...], precision=hp, preferred_element_type=F32)
    df, db = dec_ref[:, :D_GROUP], dec_ref[:, D_GROUP:]
    h = h * jnp.concatenate([df, df, db, db], axis=1)

    @pl.when(i == 0)
    def _():
        ss_ref[...] = jnp.zeros_like(ss_ref)

    ss_ref[...] += jnp.sum(h * h, axis=0, keepdims=True)
    nhalf = h.shape[1] // 2
    row = lax.broadcasted_iota(jnp.int32, h.shape, 0)
    col = lax.broadcasted_iota(jnp.int32, h.shape, 1)
    h_ref[...] = jnp.where((row == 0) & (col >= nhalf) & (i == 0), 0.0, h)


def _hy_filters(l, w1, b1, w2, b2, w3, freq):
    t = np.linspace(0.0, 1.0, l)[:, None]
    w = 2.0 * np.pi * np.arange(l, dtype=np.float64)[:, None] / l
    bands = np.linspace(1e-4, HY_BANDS - 1, HY_BANDS)[None, :]
    z = np.concatenate([t, np.cos(bands * w), -np.sin(bands * w)], axis=-1)
    max_decay = math.log(HY_TARGET) / HY_FAST_DECAY
    min_decay = math.log(HY_TARGET) / HY_SLOW_DECAY
    rates = np.abs(np.linspace(min_decay, max_decay, D_GROUP))
    dec = np.exp(-t * rates)
    rev = np.concatenate([[0], np.arange(l - 1, 0, -1)])
    half = LANES // 2
    zz = np.zeros((l, LANES))
    zz[:, :HY_EMB] = z
    zz[:, half:half + HY_EMB] = z[rev]
    zz = jnp.asarray(zz, F32)
    dec2 = jnp.asarray(np.concatenate([dec, dec[rev]], axis=1), F32)

    def two(m):
        m = m.astype(F32)
        top = jnp.pad(m, ((0, half - m.shape[0]), (0, half - m.shape[1])))
        zero = jnp.zeros_like(top)
        return jnp.concatenate([jnp.concatenate([top, zero], 1), jnp.concatenate([zero, top], 1)], 0)

    def twice(v):
        v = jnp.pad(v.astype(F32), (0, half - v.shape[0]))
        return jnp.concatenate([v, v]).reshape(1, LANES)

    w3r = w3.astype(F32).reshape(HY_FFN, HY_ORDER, 2, D_GROUP)
    nhalf = HY_ORDER * D_GROUP
    w3f = jnp.pad(w3r[:, :, 0].reshape(HY_FFN, nhalf), ((0, half - HY_FFN), (0, 0)))
    w3b = jnp.pad(w3r[:, :, 1].reshape(HY_FFN, nhalf), ((0, half - HY_FFN), (0, 0)))
    zero = jnp.zeros_like(w3f)
    w3p = jnp.concatenate([jnp.concatenate([w3f, zero], 1), jnp.concatenate([zero, w3b], 1)], 0)
    nout = 2 * nhalf
    tl = _tile(l, 512)

    def const(shape):
        return pl.BlockSpec(shape, lambda i: (0, 0))

    return pl.pallas_call(
        _hy_filter_kernel,
        out_shape=(jax.ShapeDtypeStruct((l, nout), F32), jax.ShapeDtypeStruct((1, nout), F32)), grid=(l // tl,),
        in_specs=[pl.BlockSpec((tl, LANES), lambda i: (i, 0)), const((LANES, LANES)), const((1, LANES)),
                  const((LANES, LANES)), const((1, LANES)), const((LANES, nout)), const((1, LANES)),
                  pl.BlockSpec((tl, 2 * D_GROUP), lambda i: (i, 0))],
        out_specs=(pl.BlockSpec((tl, nout), lambda i: (i, 0)), const((1, nout))),
        compiler_params=_cparams(("arbitrary",)), name="hyena_filter_mlp",
    )(zz, two(w1), twice(b1), two(w2), twice(b2), w3p, twice(freq), dec2)


def _dft_consts(na):
    nb = FFT_NB
    n = na * nb
    ia = np.arange(na, dtype=np.float64)
    th = 2.0 * np.pi * np.outer(ia, ia) / na
    c1, s1 = np.cos(th), np.sin(th)
    eye8 = np.eye(8)
    fa_full = np.concatenate([c1, -s1], axis=0)
    g_full = np.kron(fa_full, eye8)
    g_half = np.kron(fa_full[:, : na // 2], eye8)
    g_out = np.kron(np.concatenate([c1[: na // 2], -s1[: na // 2]], axis=1) / n, eye8)
    ib = np.arange(nb, dtype=np.float64)
    ph = 2.0 * np.pi * np.outer(ib, ib) / nb
    c2, s2 = np.cos(ph), np.sin(ph)
    fb = np.block([[c2, s2], [-s2, c2]])
    fbc = np.block([[c2, -s2], [s2, c2]])
    ps = 2.0 * np.pi * np.outer(ia, ib) / n
    twr = np.broadcast_to(np.cos(ps)[:, :, None], (na, nb, LANES))
    twi = np.broadcast_to(-np.sin(ps)[:, :, None], (na, nb, LANES))
    as_bf = lambda x: jnp.asarray(x, BF16)
    return dict(g_full=as_bf(g_full), g_half=as_bf(g_half), g_out=as_bf(g_out), fb=as_bf(fb), fbc=as_bf(fbc),
                twr=jnp.asarray(twr, F32), twi=jnp.asarray(twi, F32))


def _lane_tile(x, reps):
    return x if reps == 1 else jnp.concatenate([x] * reps, axis=-1)


def _hy_spec_kernel(a_ref, twr_ref, twi_ref, fb_ref, ss_ref, o_ref, *, kb, reps):
    scale = lax.rsqrt(ss_ref[...])
    for j in range(kb):
        ar, ai = a_ref[0, j], a_ref[1, j]
        twr, twi = _lane_tile(twr_ref[j], reps), _lane_tile(twi_ref[j], reps)
        br = twr * ar - twi * ai
        bi = twr * ai + twi * ar
        x = jnp.dot(fb_ref[...], jnp.concatenate([br, bi], axis=0).astype(BF16), preferred_element_type=F32)
        o_ref[0, j] = x[:FFT_NB] * scale
        o_ref[1, j] = x[FFT_NB:] * scale


def _hy_mid_kernel(a_ref, h_ref, twr_ref, twi_ref, fb_ref, fbc_ref, o_ref, *, kb, reps):
    for j in range(kb):
        ar, ai = a_ref[0, 0, j], a_ref[0, 1, j]
        twr, twi = _lane_tile(twr_ref[j], reps), _lane_tile(twi_ref[j], reps)
        br = twr * ar - twi * ai
        bi = twr * ai + twi * ar
        x = jnp.dot(fb_ref[...], jnp.concatenate([br, bi], axis=0).astype(BF16), preferred_element_type=F32)
        xr, xi = x[:FFT_NB], x[FFT_NB:]
        hr, hi = h_ref[0, j], h_ref[1, j]
        yr = xr * hr - xi * hi
        yi = xr * hi + xi * hr
        z = jnp.dot(fbc_ref[...], jnp.concatenate([yr, yi], axis=0).astype(BF16), preferred_element_type=F32)
        zr, zi = z[:FFT_NB], z[FFT_NB:]
        o_ref[0, 0, j] = twr * zr + twi * zi
        o_ref[0, 1, j] = twr * zi - twi * zr


def _hy_dft1_kernel(g_ref, x_ref, o_ref, *, qb):
    na_in, c = x_ref.shape[1], x_ref.shape[4]
    na = o_ref.shape[2]
    for q in range(qb):
        x = x_ref[0, :, q].reshape(na_in * 8, c).astype(BF16)
        a = jnp.dot(g_ref[...], x, preferred_element_type=F32)
        o_ref[0, :, :, q] = a.reshape(2, na, 8, c)


def _hy_fdft1_kernel(g_ref, xt_ref, xb_ref, o_ref, *, qb):
    nah, c = xt_ref.shape[1], xt_ref.shape[4]
    na = o_ref.shape[2]
    kh = nah * 8
    for q in range(qb):
        xt = xt_ref[0, :, q].reshape(kh, c).astype(BF16)
        xb = xb_ref[0, :, q].reshape(kh, c).astype(BF16)
        a = (jnp.dot(g_ref[:, :kh], xt, preferred_element_type=F32)
             + jnp.dot(g_ref[:, kh:], xb, preferred_element_type=F32))
        o_ref[0, :, :, q] = a.reshape(2, na, 8, c)


def _hy_dft1(g, x5, ncol, name):
    bsz, na_in, nq = x5.shape[:3]
    na = g.shape[0] // 16
    c = D_GROUP
    qb = FFT_QB
    return pl.pallas_call(
        functools.partial(_hy_dft1_kernel, qb=qb),
        out_shape=jax.ShapeDtypeStruct((bsz, 2, na, nq, 8, ncol * c), F32), grid=(bsz, ncol, nq // qb),
        in_specs=[pl.BlockSpec(g.shape, lambda b, j, q: (0, 0)),
                  pl.BlockSpec((1, na_in, qb, 8, c), lambda b, j, q: (b, 0, q, 0, j))],
        out_specs=pl.BlockSpec((1, 2, na, qb, 8, c), lambda b, j, q: (b, 0, 0, q, 0, j)),
        compiler_params=_cparams(("parallel", "parallel", "parallel"), 48), name=name)(g, x5)


def _hy_out_kernel(g_ref, z_ref, x_ref, v_ref, b_ref, o_ref, *, qb):
    na2, c = z_ref.shape[1] * z_ref.shape[2], z_ref.shape[5]
    nah = o_ref.shape[1]
    bias = b_ref[...].reshape(1, 1, c)
    for q in range(qb):
        z = z_ref[0, :, :, q].reshape(na2 * 8, c).astype(BF16)
        y = jnp.dot(g_ref[...], z, preferred_element_type=F32).reshape(nah, 8, c)
        o_ref[0, :, q] = x_ref[0, :, q] * (y + v_ref[0, :, q] * bias)


def _hyena(proj, conv_w, conv_b, w1, b1, w2, b2, w3, freq, bias):
    bsz, l, _ = proj.shape
    nb = FFT_NB
    na = 2 * l // nb
    nah = na // 2
    nq = nb // 8
    c = D_GROUP
    qb = FFT_QB
    dc = _dft_consts(na)
    pc = _shortconv(proj, CB_HV, 3, conv_w, conv_b, act=False)

    h, ss = _hy_filters(l, w1, b1, w2, b2, w3, freq)
    ncf = HY_ORDER * c
    ssn = ss[:, :ncf] + ss[:, ncf:]
    h5 = h.reshape(1, nah, nq, 8, 2 * ncf)
    ka = pl.pallas_call(
        functools.partial(_hy_fdft1_kernel, qb=qb),
        out_shape=jax.ShapeDtypeStruct((1, 2, na, nq, 8, ncf), F32), grid=(HY_ORDER, nq // qb),
        in_specs=[pl.BlockSpec(dc["g_full"].shape, lambda j, q: (0, 0)),
                  pl.BlockSpec((1, nah, qb, 8, c), lambda j, q: (0, 0, q, 0, j)),
                  pl.BlockSpec((1, nah, qb, 8, c), lambda j, q: (0, 0, q, 0, HY_ORDER + j))],
        out_specs=pl.BlockSpec((1, 2, na, qb, 8, c), lambda j, q: (0, 0, 0, q, 0, j)),
        compiler_params=_cparams(("parallel", "parallel"), 48), name="hyena_filter_dft1",
    )(dc["g_full"], h5, h5).reshape(2, na, nb, ncf)
    kb = 4 if na % 4 == 0 else 1
    reps = c // LANES
    tw = pl.BlockSpec((kb, nb, LANES), lambda j, k: (k, 0, 0))
    mat = pl.BlockSpec((2 * nb, 2 * nb), lambda j, k: (0, 0))
    hspec = pl.pallas_call(
        functools.partial(_hy_spec_kernel, kb=kb, reps=reps),
        out_shape=jax.ShapeDtypeStruct((2, na, nb, ncf), F32), grid=(ncf // c, na // kb),
        in_specs=[pl.BlockSpec((2, kb, nb, c), lambda j, k: (0, k, 0, j)), tw, tw, mat,
                  pl.BlockSpec((1, c), lambda j, k: (0, j))],
        out_specs=pl.BlockSpec((2, kb, nb, c), lambda j, k: (0, k, 0, j)),
        compiler_params=_cparams(("parallel", "parallel"), 48), name="hyena_filter_dft2",
    )(ka, dc["twr"], dc["twi"], dc["fb"], ssn)

    pc5 = pc.reshape(bsz, nah, nq, 8, 3 * c)

    def long_conv_gate(z5, order, xcol):
        a = _hy_dft1(dc["g_half"], z5, 1, "hyena_dft1").reshape(bsz, 2, na, nb, c)
        zmid = pl.pallas_call(
            functools.partial(_hy_mid_kernel, kb=kb, reps=reps),
            out_shape=jax.ShapeDtypeStruct((bsz, 2, na, nb, c), F32), grid=(bsz, na // kb),
            in_specs=[pl.BlockSpec((1, 2, kb, nb, c), lambda b, k: (b, 0, k, 0, 0)),
                      pl.BlockSpec((2, kb, nb, c), lambda b, k: (0, k, 0, order)), tw, tw, mat, mat],
            out_specs=pl.BlockSpec((1, 2, kb, nb, c), lambda b, k: (b, 0, k, 0, 0)),
            compiler_params=_cparams(("parallel", "parallel"), 48), name="hyena_dft_mid",
        )(a, hspec, dc["twr"], dc["twi"], dc["fb"], dc["fbc"])
        zmid = zmid.reshape(bsz, 2, na, nq, 8, c)
        sig = lambda col: pl.BlockSpec((1, nah, qb, 8, c), lambda b, q: (b, 0, q, 0, col))
        return pl.pallas_call(
            functools.partial(_hy_out_kernel, qb=qb),
            out_shape=jax.ShapeDtypeStruct((bsz, nah, nq, 8, c), F32), grid=(bsz, nq // qb),
            in_specs=[pl.BlockSpec(dc["g_out"].shape, lambda b, q: (0, 0)),
                      pl.BlockSpec((1, 2, na, qb, 8, c), lambda b, q: (b, 0, 0, q, 0, 0)),
                      sig(xcol), sig(0), pl.BlockSpec((1, c), lambda b, q: (0, 0))],
            out_specs=sig(0),
            compiler_params=_cparams(("parallel", "parallel"), 48), name="hyena_idft_gate",
        )(dc["g_out"], zmid, pc5, z5, bias[order].astype(F32).reshape(1, c))

    z1 = long_conv_gate(pc5, 0, 1)
    z2 = long_conv_gate(z1, 1, 2)
    return z2.reshape(bsz, l, c)


def _ffn_kernel(x_ref, w1_ref, w3_ref, w2_ref, lw_ref, lb_ref, o_ref, xb_ref, acc_ref, *, nf):
    f = pl.program_id(1)

    @pl.when(f == 0)
    def _():
        xb_ref[...] = x_ref[...].astype(BF16)
        acc_ref[...] = jnp.zeros_like(acc_ref)

    xb = xb_ref[...]
    a = jnp.dot(xb, w1_ref[...], preferred_element_type=F32)
    b = jnp.dot(xb, w3_ref[...], preferred_element_type=F32)
    acc_ref[...] += jnp.dot((_silu(a) * b).astype(BF16), w2_ref[...], preferred_element_type=F32)

    @pl.when(f == nf - 1)
    def _():
        o_ref[...] = _ln_core(DN_ALPHA * x_ref[...] + acc_ref[...], lw_ref[...], lb_ref[...])


def _ffn_ln(x, w1, w3, w2, lw, lb):
    t, d = x.shape
    ff = w1.shape[1]
    tm = _tile(t, 1024)
    tf = 512 if ff % 512 == 0 else (256 if ff % 256 == 0 else ff)
    nf = ff // tf
    vec = pl.BlockSpec((1, d), lambda i, f: (0, 0))
    return pl.pallas_call(
        functools.partial(_ffn_kernel, nf=nf),
        out_shape=jax.ShapeDtypeStruct((t, d), F32), grid=(t // tm, nf),
        in_specs=[pl.BlockSpec((tm, d), lambda i, f: (i, 0)),
                  pl.BlockSpec((d, tf), lambda i, f: (0, f)),
                  pl.BlockSpec((d, tf), lambda i, f: (0, f)),
                  pl.BlockSpec((tf, d), lambda i, f: (f, 0)), vec, vec],
        out_specs=pl.BlockSpec((tm, d), lambda i, f: (i, 0)),
        scratch_shapes=[pltpu.VMEM((tm, d), BF16), pltpu.VMEM((tm, d), F32)],
        compiler_params=_cparams(("parallel", "arbitrary"), 52), name="swiglu_ffn_ln",
    )(x, w1, w3, w2, lw.reshape(1, d), lb.reshape(1, d))


MOE_SB = 832
MOE_NSB = 2
MOE_SUB = 256
MOE_CUM = 64
MOE_MAXP = -(-MOE_SB // MOE_SUB)


def _moe_kernel(cnt_ref, x_ref, cmb_ref, lt_ref, w1_ref, w3_ref, w2_ref, lw_ref, lb_ref, o_ref,
                xb_ref, xs_ref, ys_ref, gs_ref, pos_ref, *, nf, nsb, t_total):
    i = pl.program_id(0)
    e = pl.program_id(1)
    f = pl.program_id(2)
    sb = x_ref.shape[0] // nsb
    npass = [(cnt_ref[(i * nsb + s) * N_EXPERTS + e] + (MOE_SUB - 1)) // MOE_SUB for s in range(nsb)]
    sub = [slice(s * sb, (s + 1) * sb) for s in range(nsb)]

    @pl.when((e == 0) & (f == 0))
    def _():
        o_ref[...] = jnp.zeros_like(o_ref)
        for s in range(nsb):
            valid = lax.broadcasted_iota(jnp.int32, (sb, 1), 0) < t_total - (i * nsb + s) * sb
            xb_ref[sub[s]] = jnp.where(valid, x_ref[sub[s]], 0.0).astype(BF16)
            carry = jnp.zeros((1, LANES), F32)
            for c in range(sb // MOE_CUM):
                rows = slice(s * sb + c * MOE_CUM, s * sb + (c + 1) * MOE_CUM)
                vrows = slice(c * MOE_CUM, (c + 1) * MOE_CUM)
                m = jnp.where(valid[vrows] & (cmb_ref[rows] > 0.0), 1.0, 0.0)
                inc = jnp.dot(lt_ref[...], m, preferred_element_type=F32) + carry
                pos_ref[rows] = jnp.where(m > 0.0, inc - 1.0, -1.0)
                carry = inc[MOE_CUM - 1:MOE_CUM]

    lane = lax.broadcasted_iota(jnp.int32, (sb, LANES), 1)

    def one_hot(s, j):
        pos = jnp.sum(jnp.where(lane == e, pos_ref[sub[s]], 0.0), axis=1, keepdims=True)
        slot = lax.broadcasted_iota(jnp.int32, (sb, MOE_SUB), 1).astype(F32) + (j * MOE_SUB).astype(F32)
        return pos == slot

    @pl.when(f == 0)
    def _():
        for s in range(nsb):
            gate = jnp.sum(jnp.where(lane == e, cmb_ref[sub[s]], 0.0), axis=1, keepdims=True)

            def gather(j, carry, s=s, gate=gate):
                hit = one_hot(s, j)
                k = s * MOE_MAXP + j
                xs_ref[k] = _dot_tn(xb_ref[sub[s]], jnp.where(hit, 1.0, 0.0).astype(BF16)).astype(BF16)
                g = jnp.sum(jnp.where(hit, gate, 0.0), axis=0, keepdims=True)
                gs_ref[k] = jnp.broadcast_to(g, (8, MOE_SUB))
                ys_ref[k] = jnp.zeros(ys_ref.shape[1:], F32)
                return carry

            lax.fori_loop(0, npass[s], gather, 0)

    for s in range(nsb):
        def expert(j, carry, s=s):
            k = s * MOE_MAXP + j
            xs = xs_ref[k]
            a = _dot_tn(w1_ref[0], xs)
            b = _dot_tn(w3_ref[0], xs)
            hid = (_silu(a) * b * gs_ref[k][0:1]).astype(BF16)
            ys_ref[k] += _dot_tn(w2_ref[0], hid)
            return carry

        lax.fori_loop(0, npass[s], expert, 0)

    @pl.when(f == nf - 1)
    def _():
        for s in range(nsb):
            def scatter(j, carry, s=s):
                hit = one_hot(s, j)
                o_ref[sub[s]] += _dot_nt(jnp.where(hit, 1.0, 0.0).astype(BF16),
                                         ys_ref[s * MOE_MAXP + j].astype(BF16))
                return carry

            lax.fori_loop(0, npass[s], scatter, 0)

        @pl.when(e == pl.num_programs(1) - 1)
        def _():
            o_ref[...] = _ln_core(DN_ALPHA * x_ref[...] + o_ref[...], lw_ref[...], lb_ref[...])


def _moe_ln(x, cmb, w1, w3, w2, lw, lb):
    t, d = x.shape
    ne, _, ff = w1.shape
    nsb = MOE_NSB
    tb = nsb * MOE_SB
    nb = -(-t // tb)
    tf = 896 if ff % 896 == 0 else ff
    nf = ff // tf
    max_pass = nsb * MOE_MAXP
    cmb_p = jnp.pad(cmb, ((0, nb * tb - t), (0, 0)))
    counts = jnp.sum((cmb_p[:, :N_EXPERTS] > 0.0).reshape(nb * nsb, MOE_SB, N_EXPERTS), axis=1)
    counts = counts.astype(jnp.int32).reshape(-1)
    idx = np.arange(MOE_CUM)
    lt = jnp.asarray(idx[None, :] <= idx[:, None], F32)
    once = pl.Buffered(1)
    grid_spec = pltpu.PrefetchScalarGridSpec(
        num_scalar_prefetch=1, grid=(nb, ne, nf),
        in_specs=[pl.BlockSpec((tb, d), lambda i, e, f, c: (i, 0), pipeline_mode=once),
                  pl.BlockSpec((tb, LANES), lambda i, e, f, c: (i, 0), pipeline_mode=once),
                  pl.BlockSpec((MOE_CUM, MOE_CUM), lambda i, e, f, c: (0, 0)),
                  pl.BlockSpec((1, d, tf), lambda i, e, f, c: (e, 0, f)),
                  pl.BlockSpec((1, d, tf), lambda i, e, f, c: (e, 0, f)),
                  pl.BlockSpec((1, tf, d), lambda i, e, f, c: (e, f, 0)),
                  pl.BlockSpec((1, d), lambda i, e, f, c: (0, 0)),
                  pl.BlockSpec((1, d), lambda i, e, f, c: (0, 0))],
        out_specs=pl.BlockSpec((tb, d), lambda i, e, f, c: (i, 0)),
        scratch_shapes=[pltpu.VMEM((tb, d), BF16), pltpu.VMEM((max_pass, d, MOE_SUB), BF16),
                        pltpu.VMEM((max_pass, d, MOE_SUB), F32), pltpu.VMEM((max_pass, 8, MOE_SUB), F32),
                        pltpu.VMEM((tb, LANES), F32)])
    return pl.pallas_call(
        functools.partial(_moe_kernel, nf=nf, nsb=nsb, t_total=t), out_shape=jax.ShapeDtypeStruct((t, d), F32),
        grid_spec=grid_spec,
        compiler_params=_cparams(("parallel", "arbitrary", "arbitrary"), 58), name="moe_routed",
    )(counts, x, cmb_p, lt, w1, w3, w2, lw.reshape(1, d), lb.reshape(1, d))


def _router_kernel(x_ref, rh_ref, rl_ref, o_ref):
    x = x_ref[...]
    xh = x.astype(BF16)
    xl = (x - xh.astype(F32)).astype(BF16)
    logits = (jnp.dot(xh, rh_ref[...], preferred_element_type=F32)
              + jnp.dot(xl, rh_ref[...], preferred_element_type=F32)
              + jnp.dot(xh, rl_ref[...], preferred_element_type=F32))
    lane = lax.broadcasted_iota(jnp.int32, logits.shape, 1).astype(F32)
    logits = jnp.where(lane < N_EXPERTS, logits, -jnp.inf)
    m1 = jnp.max(logits, axis=1, keepdims=True)
    i1 = jnp.min(jnp.where(logits == m1, lane, float(LANES)), axis=1, keepdims=True)
    rest = jnp.where(lane == i1, -jnp.inf, logits)
    m2 = jnp.max(rest, axis=1, keepdims=True)
    i2 = jnp.min(jnp.where(rest == m2, lane, float(LANES)), axis=1, keepdims=True)
    e2 = jnp.exp(m2 - m1)
    g1 = 1.0 / (1.0 + e2)
    g2 = e2 / (1.0 + e2)
    o_ref[...] = jnp.where(lane == i1, g1, 0.0) + jnp.where(lane == i2, g2, 0.0)


def _router(x, router):
    t, d = x.shape
    r = jnp.pad(router.astype(F32), ((0, 0), (0, LANES - N_EXPERTS)))
    rh = r.astype(BF16)
    rl = (r - rh.astype(F32)).astype(BF16)
    tm = _tile(t, 1024)
    return pl.pallas_call(
        _router_kernel, out_shape=jax.ShapeDtypeStruct((t, LANES), F32), grid=(t // tm,),
        in_specs=[pl.BlockSpec((tm, d), lambda i: (i, 0)), pl.BlockSpec((d, LANES), lambda i: (0, 0)),
                  pl.BlockSpec((d, LANES), lambda i: (0, 0))],
        out_specs=pl.BlockSpec((tm, LANES), lambda i: (i, 0)),
        compiler_params=_cparams(("parallel",)), name="moe_router")(x, rh, rl)


def _extended_w_in(w_in):
    w = w_in.astype(F32)
    scale = HEAD_DIM ** -0.5

    def rot_half(cols):
        c4 = cols.reshape(-1, N_HEADS, 2, HEAD_DIM // 2)
        return jnp.stack([-c4[:, :, 1], c4[:, :, 0]], axis=2).reshape(-1, D_GROUP)

    wq = w[:, 0:256]
    wk = w[:, 256:512] * scale
    main = jnp.concatenate([wq, wk, w[:, 512:3072]], axis=1)
    gates = jnp.pad(w[:, 3072:3088], ((0, 0), (0, LANES - 16)))
    ext = jnp.concatenate([main, rot_half(wq), rot_half(wk), gates], axis=1)
    return jnp.pad(ext, ((0, 0), (0, N_EXT - ext.shape[1]))).astype(BF16)


def kernel(x, ln_in_w, ln_in_b, w_in, w_out, ret_gn_w, s5_a_re, s5_a_im, s5_log_dt, s5_b_re, s5_b_im, s5_c_re, s5_c_im, s5_d, s5_w_glu, hy_conv_w, hy_conv_b, hy_w1, hy_b1, hy_w2, hy_b2, hy_w3, hy_freq, hy_bias, ml_conv_w, ml_conv_b, ml_gate_b, ml_gn_w, ln1_w, ln1_b, ln2_w, ln2_b, ffn_w1, ffn_w3, ffn_w2, moe_router, moe_w1, moe_w3, moe_w2):
    bsz, l, d = x.shape
    t = bsz * l
    cos_full, sin_full = _rope_tables(l)
    h = _layer_norm(x.reshape(t, d), ln_in_w, ln_in_b)
    for layer in range(DEPTH):
        proj, gates = _in_proj(h, _extended_w_in(w_in[layer]))
        proj, gates = proj.reshape(bsz, l, N_EXT), gates.reshape(bsz, l, LANES)
        y_ret = _retention(proj, ret_gn_w[layer], cos_full, sin_full)
        y_s5 = _s5(proj, s5_a_re[layer], s5_a_im[layer], s5_log_dt[layer], s5_b_re[layer], s5_b_im[layer],
                   s5_c_re[layer], s5_c_im[layer], s5_d[layer], s5_w_glu[layer])
        y_hy = _hyena(proj, hy_conv_w[layer], hy_conv_b[layer], hy_w1[layer], hy_b1[layer], hy_w2[layer],
                      hy_b2[layer], hy_w3[layer], hy_freq[layer], hy_bias[layer])
        qk = _shortconv(proj, CB_MQ, 2, ml_conv_w[layer], ml_conv_b[layer], act=True)
        y_ml = _mlstm(proj, qk, gates, ml_gate_b[layer], ml_gn_w[layer])
        ys = [y.reshape(t, D_GROUP) for y in (y_ret, y_s5, y_hy, y_ml)]
        h = _outproj_ln(ys, w_out[layer], h, ln1_w[layer], ln1_b[layer])
        j = layer // 2
        if layer % 2 == 0:
            h = _ffn_ln(h, ffn_w1[j].astype(BF16), ffn_w3[j].astype(BF16), ffn_w2[j].astype(BF16),
                        ln2_w[layer], ln2_b[layer])
        else:
            cmb = _router(h, moe_router[j])
            h = _moe_ln(h, cmb, moe_w1[j].astype(BF16), moe_w3[j].astype(BF16), moe_w2[j].astype(BF16),
                        ln2_w[layer], ln2_b[layer])
    return h.reshape(bsz, l, d)
```

```python
import functools
import math

import numpy as np
import jax
import jax.numpy as jnp
from jax import lax
from jax.experimental import pallas as pl
from jax.experimental.pallas import tpu as pltpu

F32 = jnp.float32
BF16 = jnp.bfloat16

D_MODEL = 1024
DEPTH = 2
D_GROUP = 256
HEAD_DIM = 64
N_HEADS = 4
CHUNK = 128
S5_CH = 16
S5_GROUPS = 16
S5_STATE = 64
HY_ORDER = 2
HY_EMB = 33
HY_BANDS = 16
HY_FFN = 64
HY_FAST_DECAY = 0.3
HY_SLOW_DECAY = 1.5
HY_TARGET = 1e-2
N_EXPERTS = 8
ROPE_BASE = 10000.0
EPS = 1e-5
DN_ALPHA = (2 * DEPTH) ** 0.25

LANES = 128
HALO = 16
S5_TC = 32
FFT_NB = 256
FFT_QB = 2
N_EXT = 3840

CB_RQ, CB_RK, CB_RV, CB_RG, CB_S5, CB_HV, CB_HX1, CB_HX2 = 0, 1, 2, 3, 4, 5, 6, 7
CB_MQ, CB_MK, CB_MV, CB_MO, CB_RQR, CB_RKR = 8, 9, 10, 11, 12, 13
GATE_COL128 = 28


def _cparams(sem, vmem_mb=None):
    kw = dict(dimension_semantics=sem)
    if vmem_mb is not None:
        kw["vmem_limit_bytes"] = vmem_mb * 1024 * 1024
    return pltpu.CompilerParams(**kw)


def _tile(n, pref):
    return pref if n % pref == 0 else n


def _split_dot(x, m, parts=2):
    acc = None
    r = x
    for _ in range(parts):
        hi = r.astype(BF16)
        t = jnp.dot(hi, m, preferred_element_type=F32)
        acc = t if acc is None else acc + t
        r = r - hi.astype(F32)
    return acc


def _split_dot_left(m, x, parts=2):
    acc = None
    r = x
    for _ in range(parts):
        hi = r.astype(BF16)
        t = jnp.dot(m, hi, preferred_element_type=F32)
        acc = t if acc is None else acc + t
        r = r - hi.astype(F32)
    return acc


def _dot_nt(a, b):
    return lax.dot_general(a, b, (((1,), (1,)), ((), ())), preferred_element_type=F32)


def _dot_tn(a, b):
    return lax.dot_general(a, b, (((0,), (0,)), ((), ())), preferred_element_type=F32)


def _sigmoid(x):
    return 1.0 / (1.0 + jnp.exp(-x))


def _silu(x):
    return x * _sigmoid(x)


def _log_sigmoid(x):
    return jnp.minimum(x, 0.0) - jnp.log(1.0 + jnp.exp(-jnp.abs(x)))


def _head_masks(dtype):
    lane = lax.broadcasted_iota(jnp.int32, (1, D_GROUP), 1)
    return [((lane >= h * HEAD_DIM) & (lane < (h + 1) * HEAD_DIM)).astype(dtype) for h in range(N_HEADS)]


def _ln_core(x, w, b):
    mu = jnp.mean(x, -1, keepdims=True)
    xc = x - mu
    var = jnp.mean(xc * xc, -1, keepdims=True)
    return xc * lax.rsqrt(var + EPS) * w + b


def _ln_kernel(x_ref, w_ref, b_ref, o_ref):
    o_ref[...] = _ln_core(x_ref[...], w_ref[...], b_ref[...])


def _layer_norm(x, w, b):
    t, d = x.shape
    tm = _tile(t, 512)
    row = pl.BlockSpec((tm, d), lambda i: (i, 0))
    vec = pl.BlockSpec((1, d), lambda i: (0, 0))
    return pl.pallas_call(_ln_kernel, out_shape=jax.ShapeDtypeStruct((t, d), F32), grid=(t // tm,),
                          in_specs=[row, vec, vec], out_specs=row,
                          compiler_params=_cparams(("parallel",)), name="layer_norm")(x, w.reshape(1, d), b.reshape(1, d))


IN_PROJ_TN = 1280


def _in_proj_kernel(a_ref, b_ref, o_ref, g_ref, *, nj, gate_off):
    acc = jnp.dot(a_ref[...].astype(BF16), b_ref[...], preferred_element_type=F32)
    o_ref[...] = acc.astype(o_ref.dtype)

    @pl.when(pl.program_id(1) == nj - 1)
    def _():
        g_ref[...] = acc[:, gate_off:gate_off + LANES]


def _in_proj(h, w_ext):
    m, k = h.shape
    n = w_ext.shape[1]
    tm, tn = _tile(m, 1024), IN_PROJ_TN
    nj = n // tn
    gate_off = GATE_COL128 * LANES - (nj - 1) * tn
    return pl.pallas_call(
        functools.partial(_in_proj_kernel, nj=nj, gate_off=gate_off),
        out_shape=(jax.ShapeDtypeStruct((m, n), BF16), jax.ShapeDtypeStruct((m, LANES), F32)), grid=(m // tm, nj),
        in_specs=[pl.BlockSpec((tm, k), lambda i, j: (i, 0)), pl.BlockSpec((k, tn), lambda i, j: (0, j))],
        out_specs=(pl.BlockSpec((tm, tn), lambda i, j: (i, j)), pl.BlockSpec((tm, LANES), lambda i, j: (i, 0))),
        compiler_params=_cparams(("parallel", "arbitrary"), 48), name="in_proj")(h, w_ext)


def _outproj_ln_kernel(y0_ref, y1_ref, y2_ref, y3_ref, w_ref, h_ref, lw_ref, lb_ref, o_ref):
    mix = None
    for g, y_ref in enumerate((y0_ref, y1_ref, y2_ref, y3_ref)):
        part = jnp.dot(y_ref[...].astype(BF16), w_ref[g * D_GROUP:(g + 1) * D_GROUP, :], preferred_element_type=F32)
        mix = part if mix is None else mix + part
    o_ref[...] = _ln_core(DN_ALPHA * h_ref[...] + mix, lw_ref[...], lb_ref[...])


def _outproj_ln(ys, w_out, h, lw, lb):
    t, d = h.shape
    tm = _tile(t, 1024)
    grp = pl.BlockSpec((tm, D_GROUP), lambda i: (i, 0))
    row = pl.BlockSpec((tm, d), lambda i: (i, 0))
    vec = pl.BlockSpec((1, d), lambda i: (0, 0))
    return pl.pallas_call(
        _outproj_ln_kernel, out_shape=jax.ShapeDtypeStruct((t, d), F32), grid=(t // tm,),
        in_specs=[grp, grp, grp, grp, pl.BlockSpec((d, d), lambda i: (0, 0)), row, vec, vec], out_specs=row,
        compiler_params=_cparams(("parallel",), 48), name="out_proj_ln",
    )(*ys, w_out.astype(BF16), h, lw.reshape(1, d), lb.reshape(1, d))


def _shortconv_kernel(x_ref, xp_ref, xn_ref, w_ref, b_ref, o_ref, *, nt, act):
    i = pl.program_id(1)
    x = x_ref[0].astype(F32)
    tl = x.shape[0]
    row = lax.broadcasted_iota(jnp.int32, x.shape, 0)
    prev_row = jnp.where(i == 0, 0.0, xp_ref[0, HALO - 1:HALO, :].astype(F32))
    next_row = jnp.where(i == nt - 1, 0.0, xn_ref[0, 0:1, :].astype(F32))
    x_prev = jnp.where(row == 0, prev_row, pltpu.roll(x, 1, 0))
    x_next = jnp.where(row == tl - 1, next_row, pltpu.roll(x, tl - 1, 0))
    w = w_ref[0]
    y = b_ref[0, 0:1] + x_prev * w[0:1] + x * w[1:2] + x_next * w[2:3]
    if act:
        y = _silu(y)
    o_ref[0] = y


def _shortconv(proj, col0, nblk, w, b, act):
    bsz, l, _ = proj.shape
    tl = _tile(l, 1024)
    nt = l // tl
    w3 = jnp.transpose(w.reshape(3, nblk, D_GROUP), (1, 0, 2))
    w3 = jnp.pad(w3, ((0, 0), (0, 5), (0, 0)))
    b3 = jnp.broadcast_to(b.reshape(nblk, 1, D_GROUP), (nblk, 8, D_GROUP))
    rh = tl // HALO
    return pl.pallas_call(
        functools.partial(_shortconv_kernel, nt=nt, act=act),
        out_shape=jax.ShapeDtypeStruct((bsz, l, nblk * D_GROUP), F32), grid=(bsz, nt, nblk),
        in_specs=[
            pl.BlockSpec((1, tl, D_GROUP), lambda bb, i, j: (bb, i, col0 + j)),
            pl.BlockSpec((1, HALO, D_GROUP), lambda bb, i, j: (bb, jnp.maximum(i * rh - 1, 0), col0 + j)),
            pl.BlockSpec((1, HALO, D_GROUP), lambda bb, i, j: (bb, jnp.minimum((i + 1) * rh, l // HALO - 1), col0 + j)),
            pl.BlockSpec((1, 8, D_GROUP), lambda bb, i, j: (j, 0, 0)),
            pl.BlockSpec((1, 8, D_GROUP), lambda bb, i, j: (j, 0, 0)),
        ],
        out_specs=pl.BlockSpec((1, tl, D_GROUP), lambda bb, i, j: (bb, i, j)),
        compiler_params=_cparams(("parallel", "parallel", "parallel")), name="shortconv")(proj, proj, proj, w3, b3)


def _stack_heads(xb, masks):
    return jnp.concatenate([xb * masks[h] for h in range(N_HEADS)], axis=0)


def _compact(s):
    return s[0:64] + s[64:128] + s[128:192] + s[192:256]


def _expand(c, bd):
    return jnp.concatenate([c, c, c, c], axis=0) * bd


def _head_norm(o, avg, gn):
    mu = _split_dot(o, avg, parts=2)
    oc = o - mu
    var = _split_dot(oc * oc, avg, parts=2)
    return oc * lax.rsqrt(var + EPS) * gn


def _ret_kernel(q_ref, qr_ref, k_ref, kr_ref, v_ref, g_ref, cos_ref, sin_ref,
                dsym_ref, qdf_ref, qdb_ref, kdf_ref, kdb_ref, cdec_ref, bd_ref, avg_ref, gn_ref,
                o_ref, sfw_ref, sbw_ref, save_ref, *, cb, nblk):
    p = pl.program_id(0)
    i = pl.program_id(1)
    bsz = q_ref.shape[0]
    masks = _head_masks(BF16)
    bd = bd_ref[...]
    cdec = cdec_ref[...]

    def rope_k(b, rows):
        return k_ref[b, rows] * cos_ref[rows] + kr_ref[b, rows] * sin_ref[rows]

    def kv_update(s, k, decay, vb):
        kv = _dot_tn((k * decay).astype(BF16), vb)
        return s * cdec + kv * bd

    @pl.when(p == 0)
    def _():
        @pl.when(i == 0)
        def _():
            sbw_ref[...] = jnp.zeros_like(sbw_ref)

        blk = nblk - 1 - i
        for c in reversed(range(cb)):
            rows = slice(c * CHUNK, (c + 1) * CHUNK)
            for b in range(bsz):
                s = sbw_ref[b]
                save_ref[b, blk * cb + c] = _compact(s)
                sbw_ref[b] = kv_update(s, rope_k(b, rows), kdb_ref[...], v_ref[b, rows].astype(BF16))

    @pl.when(p == 1)
    def _():
        @pl.when(i == 0)
        def _():
            sfw_ref[...] = jnp.zeros_like(sfw_ref)

        for c in range(cb):
            rows = slice(c * CHUNK, (c + 1) * CHUNK)
            for b in range(bsz):
                q = q_ref[b, rows] * cos_ref[rows] + qr_ref[b, rows] * sin_ref[rows]
                k = rope_k(b, rows)
                qb, kb, vb = q.astype(BF16), k.astype(BF16), v_ref[b, rows].astype(BF16)
                s_all = _dot_nt(qb, _stack_heads(kb, masks))
                pmat = (s_all * dsym_ref[...]).astype(BF16)
                o = jnp.dot(pmat, _stack_heads(vb, masks), preferred_element_type=F32)
                sfw = sfw_ref[b]
                sbw = _expand(save_ref[b, i * cb + c], bd)
                o = o + jnp.dot(qb, sfw.astype(BF16), preferred_element_type=F32) * qdf_ref[...]
                o = o + jnp.dot(qb, sbw.astype(BF16), preferred_element_type=F32) * qdb_ref[...]
                y = _head_norm(o, avg_ref[...], gn_ref[...])
                o_ref[b, rows] = _silu(g_ref[b, rows].astype(F32)) * y
                sfw_ref[b] = kv_update(sfw, k, kdf_ref[...], vb)


def _ret_tables():
    lg = np.log(1.0 - 2.0 ** (-5.0 - np.arange(N_HEADS, dtype=np.float64)))
    pos = np.arange(CHUNK, dtype=np.float64)
    lag = np.abs(pos[:, None] - pos[None, :])
    dsym = np.concatenate([np.exp(lg[h] * lag) for h in range(N_HEADS)], axis=1)
    lane_lg = np.repeat(lg, HEAD_DIM)[None, :]
    qdf = np.exp(lane_lg * (pos[:, None] + 1.0))
    qdb = np.exp(lane_lg * (CHUNK - pos[:, None]))
    kdf = np.exp(lane_lg * (CHUNK - 1.0 - pos[:, None]))
    kdb = np.exp(lane_lg * pos[:, None])
    cdec = np.exp(lane_lg * CHUNK)
    return [jnp.asarray(t, F32) for t in (dsym, qdf, qdb, kdf, kdb, cdec)]


def _block_diag_mask():
    hid = np.arange(D_GROUP) // HEAD_DIM
    return (hid[:, None] == hid[None, :]).astype(np.float32)


def _rope_tables(l):
    half = HEAD_DIM // 2
    inv = ROPE_BASE ** (-np.arange(half, dtype=np.float64) / half)
    ang = np.arange(l, dtype=np.float64)[:, None] * inv[None, :]
    cos, sin = np.cos(ang), np.sin(ang)
    cos_full = np.tile(np.concatenate([cos, cos], -1), (1, N_HEADS))
    sin_full = np.tile(np.concatenate([sin, sin], -1), (1, N_HEADS))
    return jnp.asarray(cos_full, F32), jnp.asarray(sin_full, F32)


def _retention(proj, gn_w, cos_full, sin_full):
    bsz, l, _ = proj.shape
    nc = l // CHUNK
    cb = 4 if nc % 4 == 0 else 1
    nblk = nc // cb
    tl = cb * CHUNK
    dsym, qdf, qdb, kdf, kdb, cdec = _ret_tables()
    bd = jnp.asarray(_block_diag_mask())
    avg = jnp.asarray(_block_diag_mask() / HEAD_DIM, BF16)

    def both(col):
        return pl.BlockSpec((bsz, tl, D_GROUP), lambda p, i: (0, i + (1 - p) * (nblk - 1 - 2 * i), col))

    def fwd_only(col):
        return pl.BlockSpec((bsz, tl, D_GROUP), lambda p, i: (0, p * i, col))

    tab = pl.BlockSpec((tl, D_GROUP), lambda p, i: (i + (1 - p) * (nblk - 1 - 2 * i), 0))

    def const(shape):
        return pl.BlockSpec(shape, lambda p, i: (0,) * len(shape))

    return pl.pallas_call(
        functools.partial(_ret_kernel, cb=cb, nblk=nblk),
        out_shape=jax.ShapeDtypeStruct((bsz, l, D_GROUP), F32), grid=(2, nblk),
        in_specs=[fwd_only(CB_RQ), fwd_only(CB_RQR), both(CB_RK), both(CB_RKR), both(CB_RV), fwd_only(CB_RG),
                  tab, tab, const((CHUNK, 4 * CHUNK)), const((CHUNK, D_GROUP)), const((CHUNK, D_GROUP)),
                  const((CHUNK, D_GROUP)), const((CHUNK, D_GROUP)), const((1, D_GROUP)),
                  const((D_GROUP, D_GROUP)), const((D_GROUP, D_GROUP)), const((1, D_GROUP))],
        out_specs=pl.BlockSpec((bsz, tl, D_GROUP), lambda p, i: (0, p * i, 0)),
        scratch_shapes=[pltpu.VMEM((bsz, D_GROUP, D_GROUP), F32), pltpu.VMEM((bsz, D_GROUP, D_GROUP), F32),
                        pltpu.VMEM((bsz, nc, HEAD_DIM, D_GROUP), F32)],
        compiler_params=_cparams(("arbitrary", "arbitrary"), 48), name="retention",
    )(proj, proj, proj, proj, proj, proj, cos_full, sin_full, dsym, qdf, qdb, kdf, kdb, cdec, bd, avg,
      gn_w.reshape(1, D_GROUP))


def _mlstm_kernel(q_ref, k_ref, v_ref, og_ref, gc_ref, gr_ref, bc_ref, br_ref, ex_ref, lt_ref, ut_ref,
                  ones_ref, obd_ref, bd_ref, avg_ref, gn_ref,
                  o_ref, cfw_ref, cbw_ref, nmfw_ref, nmbw_ref, csave_ref, nmsave_ref, *, cb, nblk):
    p = pl.program_id(0)
    i = pl.program_id(1)
    bsz = q_ref.shape[0]
    masks = _head_masks(BF16)
    bd = bd_ref[...]
    lt = lt_ref[...]
    ut = ut_ref[...]
    ri = lax.broadcasted_iota(jnp.int32, (CHUNK, CHUNK), 0)
    ci = lax.broadcasted_iota(jnp.int32, (CHUNK, CHUNK), 1)
    lane = lax.broadcasted_iota(jnp.int32, (1, D_GROUP), 1)

    def gates_expanded(b, rows):
        return _split_dot(gc_ref[b, rows] + bc_ref[...], ex_ref[...])

    def state_update(c_ref, nm_ref, b, total, cum, i_x, k, vb):
        m_prev = nm_ref[b, 1:2]
        g = (total - cum) + i_x
        m_new = jnp.maximum(total + m_prev, jnp.max(g, axis=0, keepdims=True))
        wk = jnp.exp(g - m_new) * k
        decay = jnp.exp(total + m_prev - m_new)
        c_ref[b] = c_ref[b] * decay + _dot_tn(wk.astype(BF16), vb) * bd
        nm_ref[b, 0:1] = decay * nm_ref[b, 0:1] + jnp.sum(wk, axis=0, keepdims=True)
        nm_ref[b, 1:2] = m_new

    @pl.when(p == 0)
    def _():
        @pl.when(i == 0)
        def _():
            cbw_ref[...] = jnp.zeros_like(cbw_ref)
            nmbw_ref[...] = jnp.zeros_like(nmbw_ref)

        blk = nblk - 1 - i
        for c in reversed(range(cb)):
            rows = slice(c * CHUNK, (c + 1) * CHUNK)
            for b in range(bsz):
                csave_ref[b, blk * cb + c] = _compact(cbw_ref[b])
                nmsave_ref[b, blk * cb + c] = nmbw_ref[b]
                gx = gates_expanded(b, rows)
                cum = _split_dot_left(ut, _log_sigmoid(gx[:, 768:1024]))
                k = k_ref[b, rows] * (HEAD_DIM ** -0.5)
                state_update(cbw_ref, nmbw_ref, b, cum[0:1], cum, gx[:, 512:768], k, v_ref[b, rows].astype(BF16))

    def chunk_out(b, rows, cidx):
        q = q_ref[b, rows]
        k = k_ref[b, rows] * (HEAD_DIM ** -0.5)
        qb, kb, vb = q.astype(BF16), k.astype(BF16), v_ref[b, rows].astype(BF16)
        s_all = _dot_nt(qb, _stack_heads(kb, masks))
        vaug = jnp.concatenate([_stack_heads(vb, masks), ones_ref[...]], axis=1)
        gx = gates_expanded(b, rows)
        graw = gr_ref[b, :, rows] + br_ref[...]
        gls = _log_sigmoid(graw)
        cum_r_fw = _split_dot(gls, ut)
        cum_r_bw = _split_dot(gls, lt)
        ccomp = csave_ref[b, cidx]
        nmb = nmsave_ref[b, cidx]

        def direction(i_x, f_x, tri, cum_r, i_row0, f_row0, mask, c_state, n_vec, m_prev, total_row):
            cum = _split_dot_left(tri, _log_sigmoid(f_x))
            total = cum[total_row:total_row + 1]
            inter = cum + m_prev
            ps, rmax = [], []
            dms = []
            for h in range(N_HEADS):
                a_col = cum[:, h * HEAD_DIM:h * HEAD_DIM + 1]
                dm = a_col - cum_r[f_row0 + h:f_row0 + h + 1] + graw[i_row0 + h:i_row0 + h + 1]
                dm = jnp.where(mask, dm, -jnp.inf)
                dms.append(dm)
                rmax.append(jnp.max(dm, axis=-1, keepdims=True))
            rmax256 = jnp.where(lane < 64, rmax[0], jnp.where(lane < 128, rmax[1],
                                jnp.where(lane < 192, rmax[2], rmax[3])))
            m_row = jnp.maximum(inter, rmax256)
            for h in range(N_HEADS):
                m_h = m_row[:, h * HEAD_DIM:h * HEAD_DIM + 1]
                ps.append(s_all[:, h * CHUNK:(h + 1) * CHUNK] * jnp.exp(dms[h] - m_h))
            pmat = jnp.concatenate(ps, axis=1).astype(BF16)
            nd = jnp.dot(pmat, vaug, preferred_element_type=F32)
            w_inter = jnp.exp(inter - m_row)
            qc = jnp.dot(qb, c_state.astype(BF16), preferred_element_type=F32)
            qn = _split_dot(q * n_vec, obd_ref[...])
            num = nd[:, :D_GROUP] + w_inter * qc
            den = nd[:, D_GROUP:] + w_inter * qn
            hdir = num / jnp.maximum(jnp.abs(den), jnp.exp(-m_row))
            return hdir, total, cum

        h_fw, tot_fw, cum_fw = direction(gx[:, 0:256], gx[:, 256:512], lt, cum_r_fw, 0, 4, ri >= ci,
                                         cfw_ref[b], nmfw_ref[b, 0:1], nmfw_ref[b, 1:2], CHUNK - 1)
        h_bw, _, _ = direction(gx[:, 512:768], gx[:, 768:1024], ut, cum_r_bw, 8, 12, ci >= ri,
                               _expand(ccomp, bd), nmb[0:1], nmb[1:2], 0)
        y = _head_norm(h_fw + h_bw, avg_ref[...], gn_ref[...])
        o_ref[b, rows] = _sigmoid(og_ref[b, rows].astype(F32)) * y
        state_update(cfw_ref, nmfw_ref, b, tot_fw, cum_fw, gx[:, 0:256], k, vb)

    @pl.when(p == 1)
    def _():
        @pl.when(i == 0)
        def _():
            cfw_ref[...] = jnp.zeros_like(cfw_ref)
            nmfw_ref[...] = jnp.zeros_like(nmfw_ref)

        for c in range(cb):
            rows = slice(c * CHUNK, (c + 1) * CHUNK)
            for b in range(bsz):
                chunk_out(b, rows, i * cb + c)


def _mlstm(proj, qk, gates, gate_b, gn_w):
    gates_row = jnp.transpose(gates[:, :, :16], (0, 2, 1))
    bsz, l, _ = proj.shape
    nc = l // CHUNK
    cb = 2 if nc % 2 == 0 else 1
    nblk = nc // cb
    tl = cb * CHUNK
    bd_np = _block_diag_mask()
    bd = jnp.asarray(bd_np)
    avg = jnp.asarray(bd_np / HEAD_DIM, BF16)
    obd = jnp.asarray(bd_np, BF16)
    ex = np.zeros((LANES, 4 * D_GROUP), np.float32)
    for j in range(16):
        typ, h = divmod(j, N_HEADS)
        ex[j, typ * D_GROUP + h * HEAD_DIM: typ * D_GROUP + (h + 1) * HEAD_DIM] = 1.0
    idx = np.arange(CHUNK)
    lt = (idx[None, :] <= idx[:, None]).astype(np.float32)
    ones_st = np.repeat(np.repeat(np.eye(N_HEADS, dtype=np.float32), CHUNK, 0), HEAD_DIM, 1)
    gb = gate_b.astype(F32).reshape(16)
    bias_col = jnp.pad(gb, (0, LANES - 16)).reshape(1, LANES)
    bias_row = jnp.broadcast_to(gb.reshape(16, 1), (16, CHUNK))

    def both(arr_col, width=D_GROUP):
        return pl.BlockSpec((bsz, tl, width), lambda p, i: (0, i + (1 - p) * (nblk - 1 - 2 * i), arr_col))

    def fwd_only(arr_col):
        return pl.BlockSpec((bsz, tl, D_GROUP), lambda p, i: (0, p * i, arr_col))

    def const(shape):
        return pl.BlockSpec(shape, lambda p, i: (0,) * len(shape))

    return pl.pallas_call(
        functools.partial(_mlstm_kernel, cb=cb, nblk=nblk),
        out_shape=jax.ShapeDtypeStruct((bsz, l, D_GROUP), F32), grid=(2, nblk),
        in_specs=[fwd_only(0), both(1), both(CB_MV), fwd_only(CB_MO), both(0, LANES),
                  pl.BlockSpec((bsz, 16, tl), lambda p, i: (0, 0, p * i)),
                  const((1, LANES)), const((16, CHUNK)), const((LANES, 4 * D_GROUP)),
                  const((CHUNK, CHUNK)), const((CHUNK, CHUNK)), const((4 * CHUNK, D_GROUP)),
                  const((D_GROUP, D_GROUP)), const((D_GROUP, D_GROUP)), const((D_GROUP, D_GROUP)),
                  const((1, D_GROUP))],
        out_specs=pl.BlockSpec((bsz, tl, D_GROUP), lambda p, i: (0, p * i, 0)),
        scratch_shapes=[pltpu.VMEM((bsz, D_GROUP, D_GROUP), F32), pltpu.VMEM((bsz, D_GROUP, D_GROUP), F32),
                        pltpu.VMEM((bsz, 8, D_GROUP), F32), pltpu.VMEM((bsz, 8, D_GROUP), F32),
                        pltpu.VMEM((bsz, nc, HEAD_DIM, D_GROUP), F32), pltpu.VMEM((bsz, nc, 8, D_GROUP), F32)],
        compiler_params=_cparams(("arbitrary", "arbitrary"), 48), name="mlstm",
    )(qk, qk, proj, proj, gates, gates_row, bias_col, bias_row, jnp.asarray(ex, BF16), jnp.asarray(lt, BF16),
      jnp.asarray(lt.T, BF16), jnp.asarray(ones_st, BF16), obd, bd, avg, gn_w.reshape(1, D_GROUP))


def _s5_kernel(u_ref, mt_ref, bg_ref, cg_ref, pa_ref, pb_ref, o_ref, *, nsteps):
    ub = u_ref[0].astype(BF16)
    e = jnp.dot(ub, bg_ref[0], preferred_element_type=F32)
    r = e.shape[0]
    row = lax.broadcasted_iota(jnp.int32, (r, LANES), 0)
    xf, xb = e[:, :LANES], e[:, LANES:]
    pa, pb = pa_ref[0], pb_ref[0]
    for s in range(nsteps):
        sh = 1 << s
        a_f, b_f = pa[s:s + 1, :LANES], pb[s:s + 1, :LANES]
        a_b, b_b = pa[s:s + 1, LANES:], pb[s:s + 1, LANES:]
        yf = jnp.where(row >= sh, pltpu.roll(xf, sh, 0), 0.0)
        yb = jnp.where(row < r - sh, pltpu.roll(xb, r - sh, 0), 0.0)
        xf = xf + a_f * yf + b_f * pltpu.roll(yf, LANES // 2, 1)
        xb = xb + a_b * yb + b_b * pltpu.roll(yb, LANES // 2, 1)
    sprev = jnp.where(row >= 1, pltpu.roll(xf, 1, 0), 0.0)
    snext = jnp.where(row < r - 1, pltpu.roll(xb, r - 1, 0), 0.0)
    st = jnp.concatenate([sprev, snext], axis=1).astype(BF16)
    o_ref[0] = (jnp.dot(ub, mt_ref[0], preferred_element_type=F32)
                + jnp.dot(st, cg_ref[0], preferred_element_type=F32))


def _s5_tables(a_re, a_im, log_dt, b_re, b_im, c_re, c_im, d_skip, tc, nsteps):
    g, p, ch = S5_GROUPS, S5_STATE, S5_CH
    hp = lax.Precision.HIGHEST
    are, aim = a_re.astype(F32), a_im.astype(F32)
    delta = jnp.exp(log_dt.astype(F32))[..., None]
    lre, lim = are * delta, aim * delta

    class Cx:
        def __init__(self, re, im):
            self.re, self.im = re, im

        def __mul__(self, o):
            return Cx(self.re * o.re - self.im * o.im, self.re * o.im + self.im * o.re)

        def __getitem__(self, idx):
            return Cx(self.re[idx], self.im[idx])

    def apow(n):
        n = jnp.asarray(n, F32)[None, None, :, None]
        mag, ang = jnp.exp(lre[:, :, None, :] * n), lim[:, :, None, :] * n
        return Cx(mag * jnp.cos(ang), mag * jnp.sin(ang))

    abr, abi = jnp.exp(lre) * jnp.cos(lim), jnp.exp(lre) * jnp.sin(lim)
    den = are * are + aim * aim
    quo = Cx(((abr - 1.0) * are + abi * aim) / den, (abi * are - (abr - 1.0) * aim) / den)
    b_bar = quo[..., None] * Cx(b_re.astype(F32)[None], b_im.astype(F32)[None])
    c = Cx(c_re.astype(F32), c_im.astype(F32))
    taus = np.arange(tc)
    cp = c[:, :, None] * apow(taus)[:, :, :, None, :]
    kk = (jnp.einsum("dgtop,dgpi->dgtoi", cp.re, b_bar.re, precision=hp)
          - jnp.einsum("dgtop,dgpi->dgtoi", cp.im, b_bar.im, precision=hp))
    dsk = d_skip.astype(F32).reshape(g, ch)[:, :, None] * jnp.eye(ch, dtype=F32)[None]
    kdiag = kk[0][:, 0] + kk[1][:, 0] + dsk
    lags = jnp.concatenate([kk[1][:, :0:-1], kdiag[:, None], kk[0][:, 1:]], axis=1)
    diff = taus[None, :] - taus[:, None]
    sel = (diff[None] + (tc - 1) == np.arange(2 * tc - 1)[:, None, None]).astype(np.float32)
    mt = jnp.einsum("jst,gjoi->gsito", jnp.asarray(sel), lags, precision=hp).reshape(g, tc * ch, tc * ch)

    zf = apow(tc - 1 - taus)[0][..., None] * b_bar[0][:, None]
    zb = apow(taus)[1][..., None] * b_bar[1][:, None]

    def to_rows(z):
        return jnp.transpose(z, (0, 1, 3, 2)).reshape(g, tc * ch, p)

    bg = jnp.concatenate([to_rows(zf.re), to_rows(zf.im), to_rows(zb.re), to_rows(zb.im)], axis=-1)

    yf = c[0][:, None] * apow(taus + 1)[0][:, :, None, :]
    yb = c[1][:, None] * apow(tc - taus)[1][:, :, None, :]

    def to_cols(z):
        return jnp.transpose(z, (0, 3, 1, 2)).reshape(g, p, tc * ch)

    cg = jnp.concatenate([to_cols(yf.re), -to_cols(yf.im), to_cols(yb.re), -to_cols(yb.im)], axis=1)

    steps = tc * (2.0 ** np.arange(nsteps))
    pw = apow(steps)
    re0, im0, re1, im1 = pw.re[0], pw.im[0], pw.re[1], pw.im[1]
    pa = jnp.concatenate([re0, re0, re1, re1], axis=-1)
    pb = jnp.concatenate([-im0, im0, -im1, im1], axis=-1)
    pad = (-nsteps) % 8
    pa = jnp.pad(pa, ((0, 0), (0, pad), (0, 0)))
    pb = jnp.pad(pb, ((0, 0), (0, pad), (0, 0)))
    return mt.astype(BF16), bg.astype(BF16), cg.astype(BF16), pa, pb


def _s5_glu_kernel(y_ref, w_ref, o_ref):
    y = y_ref[...]
    z = 0.5 * y * (1.0 + jnp.tanh(math.sqrt(2.0 / math.pi) * (y + 0.044715 * (y * y * y))))
    o_ref[...] = z * _sigmoid(jnp.dot(z.astype(BF16), w_ref[...], preferred_element_type=F32))


def _s5(proj, a_re, a_im, log_dt, b_re, b_im, c_re, c_im, d_skip, w_glu):
    bsz, l, _ = proj.shape
    tc = S5_TC
    r = l // tc
    nsteps = max(1, int(math.ceil(math.log2(r))))
    w = tc * S5_CH
    mt, bg, cg, pa, pb = _s5_tables(a_re, a_im, log_dt, b_re, b_im, c_re, c_im, d_skip, tc, nsteps)
    u = proj[:, :, CB_S5 * D_GROUP:(CB_S5 + 1) * D_GROUP].astype(F32)
    ug = jnp.transpose(u.reshape(bsz, r, tc, S5_GROUPS, S5_CH), (3, 0, 1, 2, 4)).reshape(S5_GROUPS, bsz * r, w)
    ns8 = pa.shape[1]
    yg = pl.pallas_call(
        functools.partial(_s5_kernel, nsteps=nsteps),
        out_shape=jax.ShapeDtypeStruct((S5_GROUPS, bsz * r, w), F32), grid=(S5_GROUPS, bsz),
        in_specs=[pl.BlockSpec((1, r, w), lambda g, b: (g, b, 0)),
                  pl.BlockSpec((1, w, w), lambda g, b: (g, 0, 0)),
                  pl.BlockSpec((1, w, D_GROUP), lambda g, b: (g, 0, 0)),
                  pl.BlockSpec((1, D_GROUP, w), lambda g, b: (g, 0, 0)),
                  pl.BlockSpec((1, ns8, D_GROUP), lambda g, b: (g, 0, 0)),
                  pl.BlockSpec((1, ns8, D_GROUP), lambda g, b: (g, 0, 0))],
        out_specs=pl.BlockSpec((1, r, w), lambda g, b: (g, b, 0)),
        compiler_params=_cparams(("parallel", "parallel"), 48), name="s5_ssm")(ug, mt, bg, cg, pa, pb)
    y = jnp.transpose(yg.reshape(S5_GROUPS, bsz, r, tc, S5_CH), (1, 2, 3, 0, 4)).reshape(bsz * l, D_GROUP)
    t = bsz * l
    tm = _tile(t, 2048)
    out = pl.pallas_call(
        _s5_glu_kernel, out_shape=jax.ShapeDtypeStruct((t, D_GROUP), F32), grid=(t // tm,),
        in_specs=[pl.BlockSpec((tm, D_GROUP), lambda i: (i, 0)), pl.BlockSpec((D_GROUP, D_GROUP), lambda i: (0, 0))],
        out_specs=pl.BlockSpec((tm, D_GROUP), lambda i: (i, 0)),
        compiler_params=_cparams(("parallel",)), name="s5_glu")(y, w_glu.astype(BF16))
    return out.reshape(bsz, l, D_GROUP)


def _hy_filter_kernel(z_ref, w1_ref, b1_ref, w2_ref, b2_ref, w3_ref, fr_ref, dec_ref, h_ref, ss_ref):
    i = pl.program_id(0)
    hp = lax.Precision.HIGHEST
    fr = fr_ref[...]
    a = jnp.sin(fr * (jnp.dot(z_ref[...], w1_ref[...], precision=hp, preferred_element_type=F32) + b1_ref[...]))
    a = jnp.sin(fr * (jnp.dot(a, w2_ref[...], precision=hp, preferred_element_type=F32) + b2_ref[...]))
    h = jnp.dot(a, w3_ref[...], precision=hp, preferred_element_type=F32)
    df, db = dec_ref[:, :D_GROUP], dec_ref[:, D_GROUP:]
    h = h * jnp.concatenate([df, df, db, db], axis=1)

    @pl.when(i == 0)
    def _():
        ss_ref[...] = jnp.zeros_like(ss_ref)

    ss_ref[...] += jnp.sum(h * h, axis=0, keepdims=True)
    nhalf = h.shape[1] // 2
    row = lax.broadcasted_iota(jnp.int32, h.shape, 0)
    col = lax.broadcasted_iota(jnp.int32, h.shape, 1)
    h_ref[...] = jnp.where((row == 0) & (col >= nhalf) & (i == 0), 0.0, h)


def _hy_filters(l, w1, b1, w2, b2, w3, freq):
    t = np.linspace(0.0, 1.0, l)[:, None]
    w = 2.0 * np.pi * np.arange(l, dtype=np.float64)[:, None] / l
    bands = np.linspace(1e-4, HY_BANDS - 1, HY_BANDS)[None, :]
    z = np.concatenate([t, np.cos(bands * w), -np.sin(bands * w)], axis=-1)
    max_decay = math.log(HY_TARGET) / HY_FAST_DECAY
    min_decay = math.log(HY_TARGET) / HY_SLOW_DECAY
    rates = np.abs(np.linspace(min_decay, max_decay, D_GROUP))
    dec = np.exp(-t * rates)
    rev = np.concatenate([[0], np.arange(l - 1, 0, -1)])
    half = LANES // 2
    zz = np.zeros((l, LANES))
    zz[:, :HY_EMB] = z
    zz[:, half:half + HY_EMB] = z[rev]
    zz = jnp.asarray(zz, F32)
    dec2 = jnp.asarray(np.concatenate([dec, dec[rev]], axis=1), F32)

    def two(m):
        m = m.astype(F32)
        top = jnp.pad(m, ((0, half - m.shape[0]), (0, half - m.shape[1])))
        zero = jnp.zeros_like(top)
        return jnp.concatenate([jnp.concatenate([top, zero], 1), jnp.concatenate([zero, top], 1)], 0)

    def twice(v):
        v = jnp.pad(v.astype(F32), (0, half - v.shape[0]))
        return jnp.concatenate([v, v]).reshape(1, LANES)

    w3r = w3.astype(F32).reshape(HY_FFN, HY_ORDER, 2, D_GROUP)
    nhalf = HY_ORDER * D_GROUP
    w3f = jnp.pad(w3r[:, :, 0].reshape(HY_FFN, nhalf), ((0, half - HY_FFN), (0, 0)))
    w3b = jnp.pad(w3r[:, :, 1].reshape(HY_FFN, nhalf), ((0, half - HY_FFN), (0, 0)))
    zero = jnp.zeros_like(w3f)
    w3p = jnp.concatenate([jnp.concatenate([w3f, zero], 1), jnp.concatenate([zero, w3b], 1)], 0)
    nout = 2 * nhalf
    tl = _tile(l, 512)

    def const(shape):
        return pl.BlockSpec(shape, lambda i: (0, 0))

    return pl.pallas_call(
        _hy_filter_kernel,
        out_shape=(jax.ShapeDtypeStruct((l, nout), F32), jax.ShapeDtypeStruct((1, nout), F32)), grid=(l // tl,),
        in_specs=[pl.BlockSpec((tl, LANES), lambda i: (i, 0)), const((LANES, LANES)), const((1, LANES)),
                  const((LANES, LANES)), const((1, LANES)), const((LANES, nout)), const((1, LANES)),
                  pl.BlockSpec((tl, 2 * D_GROUP), lambda i: (i, 0))],
        out_specs=(pl.BlockSpec((tl, nout), lambda i: (i, 0)), const((1, nout))),
        compiler_params=_cparams(("arbitrary",)), name="hyena_filter_mlp",
    )(zz, two(w1), twice(b1), two(w2), twice(b2), w3p, twice(freq), dec2)


def _dft_consts(na):
    nb = FFT_NB
    n = na * nb
    ia = np.arange(na, dtype=np.float64)
    th = 2.0 * np.pi * np.outer(ia, ia) / na
    c1, s1 = np.cos(th), np.sin(th)
    eye8 = np.eye(8)
    fa_full = np.concatenate([c1, -s1], axis=0)
    g_full = np.kron(fa_full, eye8)
    g_half = np.kron(fa_full[:, : na // 2], eye8)
    g_out = np.kron(np.concatenate([c1[: na // 2], -s1[: na // 2]], axis=1) / n, eye8)
    ib = np.arange(nb, dtype=np.float64)
    ph = 2.0 * np.pi * np.outer(ib, ib) / nb
    c2, s2 = np.cos(ph), np.sin(ph)
    fb = np.block([[c2, s2], [-s2, c2]])
    fbc = np.block([[c2, -s2], [s2, c2]])
    ps = 2.0 * np.pi * np.outer(ia, ib) / n
    twr = np.broadcast_to(np.cos(ps)[:, :, None], (na, nb, LANES))
    twi = np.broadcast_to(-np.sin(ps)[:, :, None], (na, nb, LANES))
    as_bf = lambda x: jnp.asarray(x, BF16)
    return dict(g_full=as_bf(g_full), g_half=as_bf(g_half), g_out=as_bf(g_out), fb=as_bf(fb), fbc=as_bf(fbc),
                twr=jnp.asarray(twr, F32), twi=jnp.asarray(twi, F32))


def _lane_tile(x, reps):
    return x if reps == 1 else jnp.concatenate([x] * reps, axis=-1)


def _hy_spec_kernel(a_ref, twr_ref, twi_ref, fb_ref, ss_ref, o_ref, *, kb, reps):
    scale = lax.rsqrt(ss_ref[...])
    for j in range(kb):
        ar, ai = a_ref[0, j], a_ref[1, j]
        twr, twi = _lane_tile(twr_ref[j], reps), _lane_tile(twi_ref[j], reps)
        br = twr * ar - twi * ai
        bi = twr * ai + twi * ar
        x = jnp.dot(fb_ref[...], jnp.concatenate([br, bi], axis=0).astype(BF16), preferred_element_type=F32)
        o_ref[0, j] = x[:FFT_NB] * scale
        o_ref[1, j] = x[FFT_NB:] * scale


def _hy_mid_kernel(a_ref, h_ref, twr_ref, twi_ref, fb_ref, fbc_ref, o_ref, *, kb, reps):
    for j in range(kb):
        ar, ai = a_ref[0, 0, j], a_ref[0, 1, j]
        twr, twi = _lane_tile(twr_ref[j], reps), _lane_tile(twi_ref[j], reps)
        br = twr * ar - twi * ai
        bi = twr * ai + twi * ar
        x = jnp.dot(fb_ref[...], jnp.concatenate([br, bi], axis=0).astype(BF16), preferred_element_type=F32)
        xr, xi = x[:FFT_NB], x[FFT_NB:]
        hr, hi = h_ref[0, j], h_ref[1, j]
        yr = xr * hr - xi * hi
        yi = xr * hi + xi * hr
        z = jnp.dot(fbc_ref[...], jnp.concatenate([yr, yi], axis=0).astype(BF16), preferred_element_type=F32)
        zr, zi = z[:FFT_NB], z[FFT_NB:]
        o_ref[0, 0, j] = twr * zr + twi * zi
        o_ref[0, 1, j] = twr * zi - twi * zr


def _hy_dft1_kernel(g_ref, x_ref, o_ref, *, qb):
    na_in, c = x_ref.shape[1], x_ref.shape[4]
    na = o_ref.shape[2]
    for q in range(qb):
        x = x_ref[0, :, q].reshape(na_in * 8, c).astype(BF16)
        a = jnp.dot(g_ref[...], x, preferred_element_type=F32)
        o_ref[0, :, :, q] = a.reshape(2, na, 8, c)


def _hy_fdft1_kernel(g_ref, xt_ref, xb_ref, o_ref, *, qb):
    nah, c = xt_ref.shape[1], xt_ref.shape[4]
    na = o_ref.shape[2]
    kh = nah * 8
    for q in range(qb):
        xt = xt_ref[0, :, q].reshape(kh, c).astype(BF16)
        xb = xb_ref[0, :, q].reshape(kh, c).astype(BF16)
        a = (jnp.dot(g_ref[:, :kh], xt, preferred_element_type=F32)
             + jnp.dot(g_ref[:, kh:], xb, preferred_element_type=F32))
        o_ref[0, :, :, q] = a.reshape(2, na, 8, c)


def _hy_dft1(g, x5, ncol, name):
    bsz, na_in, nq = x5.shape[:3]
    na = g.shape[0] // 16
    c = D_GROUP
    qb = FFT_QB
    return pl.pallas_call(
        functools.partial(_hy_dft1_kernel, qb=qb),
        out_shape=jax.ShapeDtypeStruct((bsz, 2, na, nq, 8, ncol * c), F32), grid=(bsz, ncol, nq // qb),
        in_specs=[pl.BlockSpec(g.shape, lambda b, j, q: (0, 0)),
                  pl.BlockSpec((1, na_in, qb, 8, c), lambda b, j, q: (b, 0, q, 0, j))],
        out_specs=pl.BlockSpec((1, 2, na, qb, 8, c), lambda b, j, q: (b, 0, 0, q, 0, j)),
        compiler_params=_cparams(("parallel", "parallel", "parallel"), 48), name=name)(g, x5)


def _hy_out_kernel(g_ref, z_ref, x_ref, v_ref, b_ref, o_ref, *, qb):
    na2, c = z_ref.shape[1] * z_ref.shape[2], z_ref.shape[5]
    nah = o_ref.shape[1]
    bias = b_ref[...].reshape(1, 1, c)
    for q in range(qb):
        z = z_ref[0, :, :, q].reshape(na2 * 8, c).astype(BF16)
        y = jnp.dot(g_ref[...], z, preferred_element_type=F32).reshape(nah, 8, c)
        o_ref[0, :, q] = x_ref[0, :, q] * (y + v_ref[0, :, q] * bias)


def _hyena(proj, conv_w, conv_b, w1, b1, w2, b2, w3, freq, bias):
    bsz, l, _ = proj.shape
    nb = FFT_NB
    na = 2 * l // nb
    nah = na // 2
    nq = nb // 8
    c = D_GROUP
    qb = FFT_QB
    dc = _dft_consts(na)
    pc = _shortconv(proj, CB_HV, 3, conv_w, conv_b, act=False)

    h, ss = _hy_filters(l, w1, b1, w2, b2, w3, freq)
    ncf = HY_ORDER * c
    ssn = ss[:, :ncf] + ss[:, ncf:]
    h5 = h.reshape(1, nah, nq, 8, 2 * ncf)
    ka = pl.pallas_call(
        functools.partial(_hy_fdft1_kernel, qb=qb),
        out_shape=jax.ShapeDtypeStruct((1, 2, na, nq, 8, ncf), F32), grid=(HY_ORDER, nq // qb),
        in_specs=[pl.BlockSpec(dc["g_full"].shape, lambda j, q: (0, 0)),
                  pl.BlockSpec((1, nah, qb, 8, c), lambda j, q: (0, 0, q, 0, j)),
                  pl.BlockSpec((1, nah, qb, 8, c), lambda j, q: (0, 0, q, 0, HY_ORDER + j))],
        out_specs=pl.BlockSpec((1, 2, na, qb, 8, c), lambda j, q: (0, 0, 0, q, 0, j)),
        compiler_params=_cparams(("parallel", "parallel"), 48), name="hyena_filter_dft1",
    )(dc["g_full"], h5, h5).reshape(2, na, nb, ncf)
    kb = 4 if na % 4 == 0 else 1
    reps = c // LANES
    tw = pl.BlockSpec((kb, nb, LANES), lambda j, k: (k, 0, 0))
    mat = pl.BlockSpec((2 * nb, 2 * nb), lambda j, k: (0, 0))
    hspec = pl.pallas_call(
        functools.partial(_hy_spec_kernel, kb=kb, reps=reps),
        out_shape=jax.ShapeDtypeStruct((2, na, nb, ncf), F32), grid=(ncf // c, na // kb),
        in_specs=[pl.BlockSpec((2, kb, nb, c), lambda j, k: (0, k, 0, j)), tw, tw, mat,
                  pl.BlockSpec((1, c), lambda j, k: (0, j))],
        out_specs=pl.BlockSpec((2, kb, nb, c), lambda j, k: (0, k, 0, j)),
        compiler_params=_cparams(("parallel", "parallel"), 48), name="hyena_filter_dft2",
    )(ka, dc["twr"], dc["twi"], dc["fb"], ssn)

    pc5 = pc.reshape(bsz, nah, nq, 8, 3 * c)

    def long_conv_gate(z5, order, xcol):
        a = _hy_dft1(dc["g_half"], z5, 1, "hyena_dft1").reshape(bsz, 2, na, nb, c)
        zmid = pl.pallas_call(
            functools.partial(_hy_mid_kernel, kb=kb, reps=reps),
            out_shape=jax.ShapeDtypeStruct((bsz, 2, na, nb, c), F32), grid=(bsz, na // kb),
            in_specs=[pl.BlockSpec((1, 2, kb, nb, c), lambda b, k: (b, 0, k, 0, 0)),
                      pl.BlockSpec((2, kb, nb, c), lambda b, k: (0, k, 0, order)), tw, tw, mat, mat],
            out_specs=pl.BlockSpec((1, 2, kb, nb, c), lambda b, k: (b, 0, k, 0, 0)),
            compiler_params=_cparams(("parallel", "parallel"), 48), name="hyena_dft_mid",
        )(a, hspec, dc["twr"], dc["twi"], dc["fb"], dc["fbc"])
        zmid = zmid.reshape(bsz, 2, na, nq, 8, c)
        sig = lambda col: pl.BlockSpec((1, nah, qb, 8, c), lambda b, q: (b, 0, q, 0, col))
        return pl.pallas_call(
            functools.partial(_hy_out_kernel, qb=qb),
            out_shape=jax.ShapeDtypeStruct((bsz, nah, nq, 8, c), F32), grid=(bsz, nq // qb),
            in_specs=[pl.BlockSpec(dc["g_out"].shape, lambda b, q: (0, 0)),
                      pl.BlockSpec((1, 2, na, qb, 8, c), lambda b, q: (b, 0, 0, q, 0, 0)),
                      sig(xcol), sig(0), pl.BlockSpec((1, c), lambda b, q: (0, 0))],
            out_specs=sig(0),
            compiler_params=_cparams(("parallel", "parallel"), 48), name="hyena_idft_gate",
        )(dc["g_out"], zmid, pc5, z5, bias[order].astype(F32).reshape(1, c))

    z1 = long_conv_gate(pc5, 0, 1)
    z2 = long_conv_gate(z1, 1, 2)
    return z2.reshape(bsz, l, c)


def _ffn_kernel(x_ref, w1_ref, w3_ref, w2_ref, lw_ref, lb_ref, o_ref, xb_ref, acc_ref, *, nf):
    f = pl.program_id(1)

    @pl.when(f == 0)
    def _():
        xb_ref[...] = x_ref[...].astype(BF16)
        acc_ref[...] = jnp.zeros_like(acc_ref)

    xb = xb_ref[...]
    a = jnp.dot(xb, w1_ref[...], preferred_element_type=F32)
    b = jnp.dot(xb, w3_ref[...], preferred_element_type=F32)
    acc_ref[...] += jnp.dot((_silu(a) * b).astype(BF16), w2_ref[...], preferred_element_type=F32)

    @pl.when(f == nf - 1)
    def _():
        o_ref[...] = _ln_core(DN_ALPHA * x_ref[...] + acc_ref[...], lw_ref[...], lb_ref[...])


def _ffn_ln(x, w1, w3, w2, lw, lb):
    t, d = x.shape
    ff = w1.shape[1]
    tm = _tile(t, 1024)
    tf = 512 if ff % 512 == 0 else (256 if ff % 256 == 0 else ff)
    nf = ff // tf
    vec = pl.BlockSpec((1, d), lambda i, f: (0, 0))
    return pl.pallas_call(
        functools.partial(_ffn_kernel, nf=nf),
        out_shape=jax.ShapeDtypeStruct((t, d), F32), grid=(t // tm, nf),
        in_specs=[pl.BlockSpec((tm, d), lambda i, f: (i, 0)),
                  pl.BlockSpec((d, tf), lambda i, f: (0, f)),
                  pl.BlockSpec((d, tf), lambda i, f: (0, f)),
                  pl.BlockSpec((tf, d), lambda i, f: (f, 0)), vec, vec],
        out_specs=pl.BlockSpec((tm, d), lambda i, f: (i, 0)),
        scratch_shapes=[pltpu.VMEM((tm, d), BF16), pltpu.VMEM((tm, d), F32)],
        compiler_params=_cparams(("parallel", "arbitrary"), 52), name="swiglu_ffn_ln",
    )(x, w1, w3, w2, lw.reshape(1, d), lb.reshape(1, d))


MOE_SB = 832
MOE_NSB = 2
MOE_SUB = 256
MOE_CUM = 64
MOE_MAXP = -(-MOE_SB // MOE_SUB)


def _moe_kernel(cnt_ref, x_ref, cmb_ref, lt_ref, w1_ref, w3_ref, w2_ref, lw_ref, lb_ref, o_ref,
                xb_ref, xs_ref, ys_ref, gs_ref, pos_ref, w1t_ref, w3t_ref, w2t_ref, *, nf, nsb, t_total):
    i = pl.program_id(0)
    e = pl.program_id(1)
    f = pl.program_id(2)
    sb = x_ref.shape[0] // nsb
    npass = [(cnt_ref[(i * nsb + s) * N_EXPERTS + e] + (MOE_SUB - 1)) // MOE_SUB for s in range(nsb)]
    sub = [slice(s * sb, (s + 1) * sb) for s in range(nsb)]

    @pl.when((e == 0) & (f == 0))
    def _():
        o_ref[...] = jnp.zeros_like(o_ref)
        for s in range(nsb):
            valid = lax.broadcasted_iota(jnp.int32, (sb, 1), 0) < t_total - (i * nsb + s) * sb
            xb_ref[sub[s]] = jnp.where(valid, x_ref[sub[s]], 0.0).astype(BF16)
            carry = jnp.zeros((1, LANES), F32)
            for c in range(sb // MOE_CUM):
                rows = slice(s * sb + c * MOE_CUM, s * sb + (c + 1) * MOE_CUM)
                vrows = slice(c * MOE_CUM, (c + 1) * MOE_CUM)
                m = jnp.where(valid[vrows] & (cmb_ref[rows] > 0.0), 1.0, 0.0)
                inc = jnp.dot(lt_ref[...], m, preferred_element_type=F32) + carry
                pos_ref[rows] = jnp.where(m > 0.0, inc - 1.0, -1.0)
                carry = inc[MOE_CUM - 1:MOE_CUM]

    lane = lax.broadcasted_iota(jnp.int32, (sb, LANES), 1)

    def one_hot(s, j):
        pos = jnp.sum(jnp.where(lane == e, pos_ref[sub[s]], 0.0), axis=1, keepdims=True)
        slot = lax.broadcasted_iota(jnp.int32, (sb, MOE_SUB), 1).astype(F32) + (j * MOE_SUB).astype(F32)
        return pos == slot

    @pl.when(f == 0)
    def _():
        for s in range(nsb):
            gate = jnp.sum(jnp.where(lane == e, cmb_ref[sub[s]], 0.0), axis=1, keepdims=True)

            def gather(j, carry, s=s, gate=gate):
                hit = one_hot(s, j)
                k = s * MOE_MAXP + j
                xs_ref[k] = _dot_tn(xb_ref[sub[s]], jnp.where(hit, 1.0, 0.0).astype(BF16)).astype(BF16)
                g = jnp.sum(jnp.where(hit, gate, 0.0), axis=0, keepdims=True)
                gs_ref[k] = jnp.broadcast_to(g, (8, MOE_SUB))
                ys_ref[k] = jnp.zeros(ys_ref.shape[1:], F32)
                return carry

            lax.fori_loop(0, npass[s], gather, 0)

    w1t_ref[...] = w1_ref[0].T
    w3t_ref[...] = w3_ref[0].T
    w2t_ref[...] = w2_ref[0].T

    for s in range(nsb):
        def expert(j, carry, s=s):
            k = s * MOE_MAXP + j
            xs = xs_ref[k]
            a = jnp.dot(w1t_ref[...], xs, preferred_element_type=F32)
            b = jnp.dot(w3t_ref[...], xs, preferred_element_type=F32)
            hid = (_silu(a) * b * gs_ref[k][0:1]).astype(BF16)
            ys_ref[k] += jnp.dot(w2t_ref[...], hid, preferred_element_type=F32)
            return carry

        lax.fori_loop(0, npass[s], expert, 0)

    @pl.when(f == nf - 1)
    def _():
        for s in range(nsb):
            def scatter(j, carry, s=s):
                hit = one_hot(s, j)
                o_ref[sub[s]] += _dot_nt(jnp.where(hit, 1.0, 0.0).astype(BF16),
                                         ys_ref[s * MOE_MAXP + j].astype(BF16))
                return carry

            lax.fori_loop(0, npass[s], scatter, 0)

        @pl.when(e == pl.num_programs(1) - 1)
        def _():
            o_ref[...] = _ln_core(DN_ALPHA * x_ref[...] + o_ref[...], lw_ref[...], lb_ref[...])


def _moe_ln(x, cmb, w1, w3, w2, lw, lb):
    t, d = x.shape
    ne, _, ff = w1.shape
    nsb = MOE_NSB
    tb = nsb * MOE_SB
    nb = -(-t // tb)
    tf = 896 if ff % 896 == 0 else ff
    nf = ff // tf
    max_pass = nsb * MOE_MAXP
    cmb_p = jnp.pad(cmb, ((0, nb * tb - t), (0, 0)))
    counts = jnp.sum((cmb_p[:, :N_EXPERTS] > 0.0).reshape(nb * nsb, MOE_SB, N_EXPERTS), axis=1)
    counts = counts.astype(jnp.int32).reshape(-1)
    idx = np.arange(MOE_CUM)
    lt = jnp.asarray(idx[None, :] <= idx[:, None], F32)
    once = pl.Buffered(1)
    grid_spec = pltpu.PrefetchScalarGridSpec(
        num_scalar_prefetch=1, grid=(nb, ne, nf),
        in_specs=[pl.BlockSpec((tb, d), lambda i, e, f, c: (i, 0), pipeline_mode=once),
                  pl.BlockSpec((tb, LANES), lambda i, e, f, c: (i, 0), pipeline_mode=once),
                  pl.BlockSpec((MOE_CUM, MOE_CUM), lambda i, e, f, c: (0, 0)),
                  pl.BlockSpec((1, d, tf), lambda i, e, f, c: (e, 0, f)),
                  pl.BlockSpec((1, d, tf), lambda i, e, f, c: (e, 0, f)),
                  pl.BlockSpec((1, tf, d), lambda i, e, f, c: (e, f, 0)),
                  pl.BlockSpec((1, d), lambda i, e, f, c: (0, 0)),
                  pl.BlockSpec((1, d), lambda i, e, f, c: (0, 0))],
        out_specs=pl.BlockSpec((tb, d), lambda i, e, f, c: (i, 0), pipeline_mode=once),
        scratch_shapes=[pltpu.VMEM((tb, d), BF16), pltpu.VMEM((max_pass, d, MOE_SUB), BF16),
                        pltpu.VMEM((max_pass, d, MOE_SUB), F32), pltpu.VMEM((max_pass, 8, MOE_SUB), F32),
                        pltpu.VMEM((tb, LANES), F32), pltpu.VMEM((tf, d), BF16), pltpu.VMEM((tf, d), BF16),
                        pltpu.VMEM((d, tf), BF16)])
    return pl.pallas_call(
        functools.partial(_moe_kernel, nf=nf, nsb=nsb, t_total=t), out_shape=jax.ShapeDtypeStruct((t, d), F32),
        grid_spec=grid_spec,
        compiler_params=_cparams(("parallel", "arbitrary", "arbitrary"), 58), name="moe_routed",
    )(counts, x, cmb_p, lt, w1, w3, w2, lw.reshape(1, d), lb.reshape(1, d))


def _router_kernel(x_ref, rh_ref, rl_ref, o_ref):
    x = x_ref[...]
    xh = x.astype(BF16)
    xl = (x - xh.astype(F32)).astype(BF16)
    logits = (jnp.dot(xh, rh_ref[...], preferred_element_type=F32)
              + jnp.dot(xl, rh_ref[...], preferred_element_type=F32)
              + jnp.dot(xh, rl_ref[...], preferred_element_type=F32))
    lane = lax.broadcasted_iota(jnp.int32, logits.shape, 1).astype(F32)
    logits = jnp.where(lane < N_EXPERTS, logits, -jnp.inf)
    m1 = jnp.max(logits, axis=1, keepdims=True)
    i1 = jnp.min(jnp.where(logits == m1, lane, float(LANES)), axis=1, keepdims=True)
    rest = jnp.where(lane == i1, -jnp.inf, logits)
    m2 = jnp.max(rest, axis=1, keepdims=True)
    i2 = jnp.min(jnp.where(rest == m2, lane, float(LANES)), axis=1, keepdims=True)
    e2 = jnp.exp(m2 - m1)
    g1 = 1.0 / (1.0 + e2)
    g2 = e2 / (1.0 + e2)
    o_ref[...] = jnp.where(lane == i1, g1, 0.0) + jnp.where(lane == i2, g2, 0.0)


def _router(x, router):
    t, d = x.shape
    r = jnp.pad(router.astype(F32), ((0, 0), (0, LANES - N_EXPERTS)))
    rh = r.astype(BF16)
    rl = (r - rh.astype(F32)).astype(BF16)
    tm = _tile(t, 1024)
    return pl.pallas_call(
        _router_kernel, out_shape=jax.ShapeDtypeStruct((t, LANES), F32), grid=(t // tm,),
        in_specs=[pl.BlockSpec((tm, d), lambda i: (i, 0)), pl.BlockSpec((d, LANES), lambda i: (0, 0)),
                  pl.BlockSpec((d, LANES), lambda i: (0, 0))],
        out_specs=pl.BlockSpec((tm, LANES), lambda i: (i, 0)),
        compiler_params=_cparams(("parallel",)), name="moe_router")(x, rh, rl)


def _extended_w_in(w_in):
    w = w_in.astype(F32)
    scale = HEAD_DIM ** -0.5

    def rot_half(cols):
        c4 = cols.reshape(-1, N_HEADS, 2, HEAD_DIM // 2)
        return jnp.stack([-c4[:, :, 1], c4[:, :, 0]], axis=2).reshape(-1, D_GROUP)

    wq = w[:, 0:256]
    wk = w[:, 256:512] * scale
    main = jnp.concatenate([wq, wk, w[:, 512:3072]], axis=1)
    gates = jnp.pad(w[:, 3072:3088], ((0, 0), (0, LANES - 16)))
    ext = jnp.concatenate([main, rot_half(wq), rot_half(wk), gates], axis=1)
    return jnp.pad(ext, ((0, 0), (0, N_EXT - ext.shape[1]))).astype(BF16)


def kernel(x, ln_in_w, ln_in_b, w_in, w_out, ret_gn_w, s5_a_re, s5_a_im, s5_log_dt, s5_b_re, s5_b_im, s5_c_re, s5_c_im, s5_d, s5_w_glu, hy_conv_w, hy_conv_b, hy_w1, hy_b1, hy_w2, hy_b2, hy_w3, hy_freq, hy_bias, ml_conv_w, ml_conv_b, ml_gate_b, ml_gn_w, ln1_w, ln1_b, ln2_w, ln2_b, ffn_w1, ffn_w3, ffn_w2, moe_router, moe_w1, moe_w3, moe_w2):
    bsz, l, d = x.shape
    t = bsz * l
    cos_full, sin_full = _rope_tables(l)
    h = _layer_norm(x.reshape(t, d), ln_in_w, ln_in_b)
    for layer in range(DEPTH):
        proj, gates = _in_proj(h, _extended_w_in(w_in[layer]))
        proj, gates = proj.reshape(bsz, l, N_EXT), gates.reshape(bsz, l, LANES)
        y_ret = _retention(proj, ret_gn_w[layer], cos_full, sin_full)
        y_s5 = _s5(proj, s5_a_re[layer], s5_a_im[layer], s5_log_dt[layer], s5_b_re[layer], s5_b_im[layer],
                   s5_c_re[layer], s5_c_im[layer], s5_d[layer], s5_w_glu[layer])
        y_hy = _hyena(proj, hy_conv_w[layer], hy_conv_b[layer], hy_w1[layer], hy_b1[layer], hy_w2[layer],
                      hy_b2[layer], hy_w3[layer], hy_freq[layer], hy_bias[layer])
        qk = _shortconv(proj, CB_MQ, 2, ml_conv_w[layer], ml_conv_b[layer], act=True)
        y_ml = _mlstm(proj, qk, gates, ml_gate_b[layer], ml_gn_w[layer])
        ys = [y.reshape(t, D_GROUP) for y in (y_ret, y_s5, y_hy, y_ml)]
        h = _outproj_ln(ys, w_out[layer], h, ln1_w[layer], ln1_b[layer])
        j = layer // 2
        if layer % 2 == 0:
            h = _ffn_ln(h, ffn_w1[j].astype(BF16), ffn_w3[j].astype(BF16), ffn_w2[j].astype(BF16),
                        ln2_w[layer], ln2_b[layer])
        else:
            cmb = _router(h, moe_router[j])
            h = _moe_ln(h, cmb, moe_w1[j].astype(BF16), moe_w3[j].astype(BF16), moe_w2[j].astype(BF16),
                        ln2_w[layer], ln2_b[layer])
    return h.reshape(bsz, l, d)
```

```python
import functools
import math

import numpy as np
import jax
import jax.numpy as jnp
from jax import lax
from jax.experimental import pallas as pl
from jax.experimental.pallas import tpu as pltpu

F32 = jnp.float32
BF16 = jnp.bfloat16

D_MODEL = 1024
DEPTH = 2
D_GROUP = 256
HEAD_DIM = 64
N_HEADS = 4
CHUNK = 128
S5_CH = 16
S5_GROUPS = 16
S5_STATE = 64
HY_ORDER = 2
HY_EMB = 33
HY_BANDS = 16
HY_FFN = 64
HY_FAST_DECAY = 0.3
HY_SLOW_DECAY = 1.5
HY_TARGET = 1e-2
N_EXPERTS = 8
ROPE_BASE = 10000.0
EPS = 1e-5
DN_ALPHA = (2 * DEPTH) ** 0.25

LANES = 128
HALO = 16
S5_TC = 32
FFT_NB = 256
FFT_QB = 2
N_EXT = 3840

CB_RQ, CB_RK, CB_RV, CB_RG, CB_S5, CB_HV, CB_HX1, CB_HX2 = 0, 1, 2, 3, 4, 5, 6, 7
CB_MQ, CB_MK, CB_MV, CB_MO, CB_RQR, CB_RKR = 8, 9, 10, 11, 12, 13
GATE_COL128 = 28


def _cparams(sem, vmem_mb=None):
    kw = dict(dimension_semantics=sem)
    if vmem_mb is not None:
        kw["vmem_limit_bytes"] = vmem_mb * 1024 * 1024
    return pltpu.CompilerParams(**kw)


def _tile(n, pref):
    return pref if n % pref == 0 else n


def _split_dot(x, m, parts=2):
    acc = None
    r = x
    for _ in range(parts):
        hi = r.astype(BF16)
        t = jnp.dot(hi, m, preferred_element_type=F32)
        acc = t if acc is None else acc + t
        r = r - hi.astype(F32)
    return acc


def _split_dot_left(m, x, parts=2):
    acc = None
    r = x
    for _ in range(parts):
        hi = r.astype(BF16)
        t = jnp.dot(m, hi, preferred_element_type=F32)
        acc = t if acc is None else acc + t
        r = r - hi.astype(F32)
    return acc


def _dot_nt(a, b):
    return lax.dot_general(a, b, (((1,), (1,)), ((), ())), preferred_element_type=F32)


def _dot_tn(a, b):
    return lax.dot_general(a, b, (((0,), (0,)), ((), ())), preferred_element_type=F32)


def _sigmoid(x):
    return 1.0 / (1.0 + jnp.exp(-x))


def _silu(x):
    return x * _sigmoid(x)


def _log_sigmoid(x):
    return jnp.minimum(x, 0.0) - jnp.log(1.0 + jnp.exp(-jnp.abs(x)))


def _head_masks(dtype):
    lane = lax.broadcasted_iota(jnp.int32, (1, D_GROUP), 1)
    return [((lane >= h * HEAD_DIM) & (lane < (h + 1) * HEAD_DIM)).astype(dtype) for h in range(N_HEADS)]


def _ln_core(x, w, b):
    mu = jnp.mean(x, -1, keepdims=True)
    xc = x - mu
    var = jnp.mean(xc * xc, -1, keepdims=True)
    return xc * lax.rsqrt(var + EPS) * w + b


def _ln_kernel(x_ref, w_ref, b_ref, o_ref):
    o_ref[...] = _ln_core(x_ref[...], w_ref[...], b_ref[...])


def _layer_norm(x, w, b):
    t, d = x.shape
    tm = _tile(t, 512)
    row = pl.BlockSpec((tm, d), lambda i: (i, 0))
    vec = pl.BlockSpec((1, d), lambda i: (0, 0))
    return pl.pallas_call(_ln_kernel, out_shape=jax.ShapeDtypeStruct((t, d), F32), grid=(t // tm,),
                          in_specs=[row, vec, vec], out_specs=row,
                          compiler_params=_cparams(("parallel",)), name="layer_norm")(x, w.reshape(1, d), b.reshape(1, d))


IN_PROJ_TN = 1280


def _in_proj_kernel(a_ref, b_ref, o_ref, g_ref, *, nj, gate_off):
    acc = jnp.dot(a_ref[...].astype(BF16), b_ref[...], preferred_element_type=F32)
    o_ref[...] = acc.astype(o_ref.dtype)

    @pl.when(pl.program_id(1) == nj - 1)
    def _():
        g_ref[...] = acc[:, gate_off:gate_off + LANES]


def _in_proj(h, w_ext):
    m, k = h.shape
    n = w_ext.shape[1]
    tm, tn = _tile(m, 1024), IN_PROJ_TN
    nj = n // tn
    gate_off = GATE_COL128 * LANES - (nj - 1) * tn
    return pl.pallas_call(
        functools.partial(_in_proj_kernel, nj=nj, gate_off=gate_off),
        out_shape=(jax.ShapeDtypeStruct((m, n), BF16), jax.ShapeDtypeStruct((m, LANES), F32)), grid=(m // tm, nj),
        in_specs=[pl.BlockSpec((tm, k), lambda i, j: (i, 0)), pl.BlockSpec((k, tn), lambda i, j: (0, j))],
        out_specs=(pl.BlockSpec((tm, tn), lambda i, j: (i, j)), pl.BlockSpec((tm, LANES), lambda i, j: (i, 0))),
        compiler_params=_cparams(("parallel", "arbitrary"), 48), name="in_proj")(h, w_ext)


def _outproj_ln_kernel(y0_ref, y1_ref, y2_ref, y3_ref, w_ref, h_ref, lw_ref, lb_ref, o_ref):
    mix = None
    for g, y_ref in enumerate((y0_ref, y1_ref, y2_ref, y3_ref)):
        part = jnp.dot(y_ref[...].astype(BF16), w_ref[g * D_GROUP:(g + 1) * D_GROUP, :], preferred_element_type=F32)
        mix = part if mix is None else mix + part
    o_ref[...] = _ln_core(DN_ALPHA * h_ref[...] + mix, lw_ref[...], lb_ref[...])


def _outproj_ln(ys, w_out, h, lw, lb):
    t, d = h.shape
    tm = _tile(t, 1024)
    grp = pl.BlockSpec((tm, D_GROUP), lambda i: (i, 0))
    row = pl.BlockSpec((tm, d), lambda i: (i, 0))
    vec = pl.BlockSpec((1, d), lambda i: (0, 0))
    return pl.pallas_call(
        _outproj_ln_kernel, out_shape=jax.ShapeDtypeStruct((t, d), F32), grid=(t // tm,),
        in_specs=[grp, grp, grp, grp, pl.BlockSpec((d, d), lambda i: (0, 0)), row, vec, vec], out_specs=row,
        compiler_params=_cparams(("parallel",), 48), name="out_proj_ln",
    )(*ys, w_out.astype(BF16), h, lw.reshape(1, d), lb.reshape(1, d))


def _shortconv_kernel(x_ref, xp_ref, xn_ref, w_ref, b_ref, o_ref, *, nt, act):
    i = pl.program_id(1)
    x = x_ref[0].astype(F32)
    tl = x.shape[0]
    row = lax.broadcasted_iota(jnp.int32, x.shape, 0)
    prev_row = jnp.where(i == 0, 0.0, xp_ref[0, HALO - 1:HALO, :].astype(F32))
    next_row = jnp.where(i == nt - 1, 0.0, xn_ref[0, 0:1, :].astype(F32))
    x_prev = jnp.where(row == 0, prev_row, pltpu.roll(x, 1, 0))
    x_next = jnp.where(row == tl - 1, next_row, pltpu.roll(x, tl - 1, 0))
    w = w_ref[0]
    y = b_ref[0, 0:1] + x_prev * w[0:1] + x * w[1:2] + x_next * w[2:3]
    if act:
        y = _silu(y)
    o_ref[0] = y


def _shortconv(proj, col0, nblk, w, b, act):
    bsz, l, _ = proj.shape
    tl = _tile(l, 1024)
    nt = l // tl
    w3 = jnp.transpose(w.reshape(3, nblk, D_GROUP), (1, 0, 2))
    w3 = jnp.pad(w3, ((0, 0), (0, 5), (0, 0)))
    b3 = jnp.broadcast_to(b.reshape(nblk, 1, D_GROUP), (nblk, 8, D_GROUP))
    rh = tl // HALO
    return pl.pallas_call(
        functools.partial(_shortconv_kernel, nt=nt, act=act),
        out_shape=jax.ShapeDtypeStruct((bsz, l, nblk * D_GROUP), F32), grid=(bsz, nt, nblk),
        in_specs=[
            pl.BlockSpec((1, tl, D_GROUP), lambda bb, i, j: (bb, i, col0 + j)),
            pl.BlockSpec((1, HALO, D_GROUP), lambda bb, i, j: (bb, jnp.maximum(i * rh - 1, 0), col0 + j)),
            pl.BlockSpec((1, HALO, D_GROUP), lambda bb, i, j: (bb, jnp.minimum((i + 1) * rh, l // HALO - 1), col0 + j)),
            pl.BlockSpec((1, 8, D_GROUP), lambda bb, i, j: (j, 0, 0)),
            pl.BlockSpec((1, 8, D_GROUP), lambda bb, i, j: (j, 0, 0)),
        ],
        out_specs=pl.BlockSpec((1, tl, D_GROUP), lambda bb, i, j: (bb, i, j)),
        compiler_params=_cparams(("parallel", "parallel", "parallel")), name="shortconv")(proj, proj, proj, w3, b3)


def _stack_heads(xb, masks):
    return jnp.concatenate([xb * masks[h] for h in range(N_HEADS)], axis=0)


def _compact(s):
    return s[0:64] + s[64:128] + s[128:192] + s[192:256]


def _expand(c, bd):
    return jnp.concatenate([c, c, c, c], axis=0) * bd


def _head_norm(o, avg, gn):
    mu = _split_dot(o, avg, parts=2)
    oc = o - mu
    var = _split_dot(oc * oc, avg, parts=2)
    return oc * lax.rsqrt(var + EPS) * gn


def _ret_kernel(q_ref, qr_ref, k_ref, kr_ref, v_ref, g_ref, cos_ref, sin_ref,
                dsym_ref, qdf_ref, qdb_ref, kdf_ref, kdb_ref, cdec_ref, bd_ref, avg_ref, gn_ref,
                o_ref, sfw_ref, sbw_ref, save_ref, *, cb, nblk):
    p = pl.program_id(0)
    i = pl.program_id(1)
    bsz = q_ref.shape[0]
    masks = _head_masks(BF16)
    bd = bd_ref[...]
    cdec = cdec_ref[...]

    def rope_k(b, rows):
        return k_ref[b, rows] * cos_ref[rows] + kr_ref[b, rows] * sin_ref[rows]

    def kv_update(s, k, decay, vb):
        kv = _dot_tn((k * decay).astype(BF16), vb)
        return s * cdec + kv * bd

    @pl.when(p == 0)
    def _():
        @pl.when(i == 0)
        def _():
            sbw_ref[...] = jnp.zeros_like(sbw_ref)

        blk = nblk - 1 - i
        for c in reversed(range(cb)):
            rows = slice(c * CHUNK, (c + 1) * CHUNK)
            for b in range(bsz):
                s = sbw_ref[b]
                save_ref[b, blk * cb + c] = _compact(s)
                sbw_ref[b] = kv_update(s, rope_k(b, rows), kdb_ref[...], v_ref[b, rows].astype(BF16))

    @pl.when(p == 1)
    def _():
        @pl.when(i == 0)
        def _():
            sfw_ref[...] = jnp.zeros_like(sfw_ref)

        for c in range(cb):
            rows = slice(c * CHUNK, (c + 1) * CHUNK)
            for b in range(bsz):
                q = q_ref[b, rows] * cos_ref[rows] + qr_ref[b, rows] * sin_ref[rows]
                k = rope_k(b, rows)
                qb, kb, vb = q.astype(BF16), k.astype(BF16), v_ref[b, rows].astype(BF16)
                s_all = _dot_nt(qb, _stack_heads(kb, masks))
                pmat = (s_all * dsym_ref[...]).astype(BF16)
                o = jnp.dot(pmat, _stack_heads(vb, masks), preferred_element_type=F32)
                sfw = sfw_ref[b]
                sbw = _expand(save_ref[b, i * cb + c], bd)
                o = o + jnp.dot(qb, sfw.astype(BF16), preferred_element_type=F32) * qdf_ref[...]
                o = o + jnp.dot(qb, sbw.astype(BF16), preferred_element_type=F32) * qdb_ref[...]
                y = _head_norm(o, avg_ref[...], gn_ref[...])
                o_ref[b, rows] = _silu(g_ref[b, rows].astype(F32)) * y
                sfw_ref[b] = kv_update(sfw, k, kdf_ref[...], vb)


def _ret_tables():
    lg = np.log(1.0 - 2.0 ** (-5.0 - np.arange(N_HEADS, dtype=np.float64)))
    pos = np.arange(CHUNK, dtype=np.float64)
    lag = np.abs(pos[:, None] - pos[None, :])
    dsym = np.concatenate([np.exp(lg[h] * lag) for h in range(N_HEADS)], axis=1)
    lane_lg = np.repeat(lg, HEAD_DIM)[None, :]
    qdf = np.exp(lane_lg * (pos[:, None] + 1.0))
    qdb = np.exp(lane_lg * (CHUNK - pos[:, None]))
    kdf = np.exp(lane_lg * (CHUNK - 1.0 - pos[:, None]))
    kdb = np.exp(lane_lg * pos[:, None])
    cdec = np.exp(lane_lg * CHUNK)
    return [jnp.asarray(t, F32) for t in (dsym, qdf, qdb, kdf, kdb, cdec)]


def _block_diag_mask():
    hid = np.arange(D_GROUP) // HEAD_DIM
    return (hid[:, None] == hid[None, :]).astype(np.float32)


def _rope_tables(l):
    half = HEAD_DIM // 2
    inv = ROPE_BASE ** (-np.arange(half, dtype=np.float64) / half)
    ang = np.arange(l, dtype=np.float64)[:, None] * inv[None, :]
    cos, sin = np.cos(ang), np.sin(ang)
    cos_full = np.tile(np.concatenate([cos, cos], -1), (1, N_HEADS))
    sin_full = np.tile(np.concatenate([sin, sin], -1), (1, N_HEADS))
    return jnp.asarray(cos_full, F32), jnp.asarray(sin_full, F32)


def _retention(proj, gn_w, cos_full, sin_full):
    bsz, l, _ = proj.shape
    nc = l // CHUNK
    cb = 4 if nc % 4 == 0 else 1
    nblk = nc // cb
    tl = cb * CHUNK
    dsym, qdf, qdb, kdf, kdb, cdec = _ret_tables()
    bd = jnp.asarray(_block_diag_mask())
    avg = jnp.asarray(_block_diag_mask() / HEAD_DIM, BF16)

    def both(col):
        return pl.BlockSpec((bsz, tl, D_GROUP), lambda p, i: (0, i + (1 - p) * (nblk - 1 - 2 * i), col))

    def fwd_only(col):
        return pl.BlockSpec((bsz, tl, D_GROUP), lambda p, i: (0, p * i, col))

    tab = pl.BlockSpec((tl, D_GROUP), lambda p, i: (i + (1 - p) * (nblk - 1 - 2 * i), 0))

    def const(shape):
        return pl.BlockSpec(shape, lambda p, i: (0,) * len(shape))

    return pl.pallas_call(
        functools.partial(_ret_kernel, cb=cb, nblk=nblk),
        out_shape=jax.ShapeDtypeStruct((bsz, l, D_GROUP), F32), grid=(2, nblk),
        in_specs=[fwd_only(CB_RQ), fwd_only(CB_RQR), both(CB_RK), both(CB_RKR), both(CB_RV), fwd_only(CB_RG),
                  tab, tab, const((CHUNK, 4 * CHUNK)), const((CHUNK, D_GROUP)), const((CHUNK, D_GROUP)),
                  const((CHUNK, D_GROUP)), const((CHUNK, D_GROUP)), const((1, D_GROUP)),
                  const((D_GROUP, D_GROUP)), const((D_GROUP, D_GROUP)), const((1, D_GROUP))],
        out_specs=pl.BlockSpec((bsz, tl, D_GROUP), lambda p, i: (0, p * i, 0)),
        scratch_shapes=[pltpu.VMEM((bsz, D_GROUP, D_GROUP), F32), pltpu.VMEM((bsz, D_GROUP, D_GROUP), F32),
                        pltpu.VMEM((bsz, nc, HEAD_DIM, D_GROUP), F32)],
        compiler_params=_cparams(("arbitrary", "arbitrary"), 48), name="retention",
    )(proj, proj, proj, proj, proj, proj, cos_full, sin_full, dsym, qdf, qdb, kdf, kdb, cdec, bd, avg,
      gn_w.reshape(1, D_GROUP))


def _mlstm_kernel(q_ref, k_ref, v_ref, og_ref, gc_ref, gr_ref, bc_ref, br_ref, ex_ref, lt_ref, ut_ref,
                  ones_ref, obd_ref, bd_ref, avg_ref, gn_ref,
                  o_ref, cfw_ref, cbw_ref, nmfw_ref, nmbw_ref, csave_ref, nmsave_ref, *, cb, nblk):
    p = pl.program_id(0)
    i = pl.program_id(1)
    bsz = q_ref.shape[0]
    masks = _head_masks(BF16)
    bd = bd_ref[...]
    lt = lt_ref[...]
    ut = ut_ref[...]
    ri = lax.broadcasted_iota(jnp.int32, (CHUNK, CHUNK), 0)
    ci = lax.broadcasted_iota(jnp.int32, (CHUNK, CHUNK), 1)
    lane = lax.broadcasted_iota(jnp.int32, (1, D_GROUP), 1)

    def gates_expanded(b, rows):
        return _split_dot(gc_ref[b, rows] + bc_ref[...], ex_ref[...])

    def state_update(c_ref, nm_ref, b, total, cum, i_x, k, vb):
        m_prev = nm_ref[b, 1:2]
        g = (total - cum) + i_x
        m_new = jnp.maximum(total + m_prev, jnp.max(g, axis=0, keepdims=True))
        wk = jnp.exp(g - m_new) * k
        decay = jnp.exp(total + m_prev - m_new)
        c_ref[b] = c_ref[b] * decay + _dot_tn(wk.astype(BF16), vb) * bd
        nm_ref[b, 0:1] = decay * nm_ref[b, 0:1] + jnp.sum(wk, axis=0, keepdims=True)
        nm_ref[b, 1:2] = m_new

    @pl.when(p == 0)
    def _():
        @pl.when(i == 0)
        def _():
            cbw_ref[...] = jnp.zeros_like(cbw_ref)
            nmbw_ref[...] = jnp.zeros_like(nmbw_ref)

        blk = nblk - 1 - i
        for c in reversed(range(cb)):
            rows = slice(c * CHUNK, (c + 1) * CHUNK)
            for b in range(bsz):
                csave_ref[b, blk * cb + c] = _compact(cbw_ref[b])
                nmsave_ref[b, blk * cb + c] = nmbw_ref[b]
                gx = gates_expanded(b, rows)
                cum = _split_dot_left(ut, _log_sigmoid(gx[:, 768:1024]))
                k = k_ref[b, rows] * (HEAD_DIM ** -0.5)
                state_update(cbw_ref, nmbw_ref, b, cum[0:1], cum, gx[:, 512:768], k, v_ref[b, rows].astype(BF16))

    def chunk_out(b, rows, cidx):
        q = q_ref[b, rows]
        k = k_ref[b, rows] * (HEAD_DIM ** -0.5)
        qb, kb, vb = q.astype(BF16), k.astype(BF16), v_ref[b, rows].astype(BF16)
        s_all = _dot_nt(qb, _stack_heads(kb, masks))
        vaug = jnp.concatenate([_stack_heads(vb, masks), ones_ref[...]], axis=1)
        gx = gates_expanded(b, rows)
        graw = gr_ref[b, :, rows] + br_ref[...]
        gls = _log_sigmoid(graw)
        cum_r_fw = _split_dot(gls, ut)
        cum_r_bw = _split_dot(gls, lt)
        ccomp = csave_ref[b, cidx]
        nmb = nmsave_ref[b, cidx]

        def direction(i_x, f_x, tri, cum_r, i_row0, f_row0, mask, c_state, n_vec, m_prev, total_row):
            cum = _split_dot_left(tri, _log_sigmoid(f_x))
            total = cum[total_row:total_row + 1]
            inter = cum + m_prev
            ps, rmax = [], []
            dms = []
            for h in range(N_HEADS):
                a_col = cum[:, h * HEAD_DIM:h * HEAD_DIM + 1]
                dm = a_col - cum_r[f_row0 + h:f_row0 + h + 1] + graw[i_row0 + h:i_row0 + h + 1]
                dm = jnp.where(mask, dm, -jnp.inf)
                dms.append(dm)
                rmax.append(jnp.max(dm, axis=-1, keepdims=True))
            rmax256 = jnp.where(lane < 64, rmax[0], jnp.where(lane < 128, rmax[1],
                                jnp.where(lane < 192, rmax[2], rmax[3])))
            m_row = jnp.maximum(inter, rmax256)
            for h in range(N_HEADS):
                m_h = m_row[:, h * HEAD_DIM:h * HEAD_DIM + 1]
                ps.append(s_all[:, h * CHUNK:(h + 1) * CHUNK] * jnp.exp(dms[h] - m_h))
            pmat = jnp.concatenate(ps, axis=1).astype(BF16)
            nd = jnp.dot(pmat, vaug, preferred_element_type=F32)
            w_inter = jnp.exp(inter - m_row)
            qc = jnp.dot(qb, c_state.astype(BF16), preferred_element_type=F32)
            qn = _split_dot(q * n_vec, obd_ref[...])
            num = nd[:, :D_GROUP] + w_inter * qc
            den = nd[:, D_GROUP:] + w_inter * qn
            hdir = num / jnp.maximum(jnp.abs(den), jnp.exp(-m_row))
            return hdir, total, cum

        h_fw, tot_fw, cum_fw = direction(gx[:, 0:256], gx[:, 256:512], lt, cum_r_fw, 0, 4, ri >= ci,
                                         cfw_ref[b], nmfw_ref[b, 0:1], nmfw_ref[b, 1:2], CHUNK - 1)
        h_bw, _, _ = direction(gx[:, 512:768], gx[:, 768:1024], ut, cum_r_bw, 8, 12, ci >= ri,
                               _expand(ccomp, bd), nmb[0:1], nmb[1:2], 0)
        y = _head_norm(h_fw + h_bw, avg_ref[...], gn_ref[...])
        o_ref[b, rows] = _sigmoid(og_ref[b, rows].astype(F32)) * y
        state_update(cfw_ref, nmfw_ref, b, tot_fw, cum_fw, gx[:, 0:256], k, vb)

    @pl.when(p == 1)
    def _():
        @pl.when(i == 0)
        def _():
            cfw_ref[...] = jnp.zeros_like(cfw_ref)
            nmfw_ref[...] = jnp.zeros_like(nmfw_ref)

        for c in range(cb):
            rows = slice(c * CHUNK, (c + 1) * CHUNK)
            for b in range(bsz):
                chunk_out(b, rows, i * cb + c)


def _mlstm(proj, qk, gates, gate_b, gn_w):
    gates_row = jnp.transpose(gates[:, :, :16], (0, 2, 1))
    bsz, l, _ = proj.shape
    nc = l // CHUNK
    cb = 2 if nc % 2 == 0 else 1
    nblk = nc // cb
    tl = cb * CHUNK
    bd_np = _block_diag_mask()
    bd = jnp.asarray(bd_np)
    avg = jnp.asarray(bd_np / HEAD_DIM, BF16)
    obd = jnp.asarray(bd_np, BF16)
    ex = np.zeros((LANES, 4 * D_GROUP), np.float32)
    for j in range(16):
        typ, h = divmod(j, N_HEADS)
        ex[j, typ * D_GROUP + h * HEAD_DIM: typ * D_GROUP + (h + 1) * HEAD_DIM] = 1.0
    idx = np.arange(CHUNK)
    lt = (idx[None, :] <= idx[:, None]).astype(np.float32)
    ones_st = np.repeat(np.repeat(np.eye(N_HEADS, dtype=np.float32), CHUNK, 0), HEAD_DIM, 1)
    gb = gate_b.astype(F32).reshape(16)
    bias_col = jnp.pad(gb, (0, LANES - 16)).reshape(1, LANES)
    bias_row = jnp.broadcast_to(gb.reshape(16, 1), (16, CHUNK))

    def both(arr_col, width=D_GROUP):
        return pl.BlockSpec((bsz, tl, width), lambda p, i: (0, i + (1 - p) * (nblk - 1 - 2 * i), arr_col))

    def fwd_only(arr_col):
        return pl.BlockSpec((bsz, tl, D_GROUP), lambda p, i: (0, p * i, arr_col))

    def const(shape):
        return pl.BlockSpec(shape, lambda p, i: (0,) * len(shape))

    return pl.pallas_call(
        functools.partial(_mlstm_kernel, cb=cb, nblk=nblk),
        out_shape=jax.ShapeDtypeStruct((bsz, l, D_GROUP), F32), grid=(2, nblk),
        in_specs=[fwd_only(0), both(1), both(CB_MV), fwd_only(CB_MO), both(0, LANES),
                  pl.BlockSpec((bsz, 16, tl), lambda p, i: (0, 0, p * i)),
                  const((1, LANES)), const((16, CHUNK)), const((LANES, 4 * D_GROUP)),
                  const((CHUNK, CHUNK)), const((CHUNK, CHUNK)), const((4 * CHUNK, D_GROUP)),
                  const((D_GROUP, D_GROUP)), const((D_GROUP, D_GROUP)), const((D_GROUP, D_GROUP)),
                  const((1, D_GROUP))],
        out_specs=pl.BlockSpec((bsz, tl, D_GROUP), lambda p, i: (0, p * i, 0)),
        scratch_shapes=[pltpu.VMEM((bsz, D_GROUP, D_GROUP), F32), pltpu.VMEM((bsz, D_GROUP, D_GROUP), F32),
                        pltpu.VMEM((bsz, 8, D_GROUP), F32), pltpu.VMEM((bsz, 8, D_GROUP), F32),
                        pltpu.VMEM((bsz, nc, HEAD_DIM, D_GROUP), F32), pltpu.VMEM((bsz, nc, 8, D_GROUP), F32)],
        compiler_params=_cparams(("arbitrary", "arbitrary"), 48), name="mlstm",
    )(qk, qk, proj, proj, gates, gates_row, bias_col, bias_row, jnp.asarray(ex, BF16), jnp.asarray(lt, BF16),
      jnp.asarray(lt.T, BF16), jnp.asarray(ones_st, BF16), obd, bd, avg, gn_w.reshape(1, D_GROUP))


def _s5_kernel(u_ref, mt_ref, bg_ref, cg_ref, pa_ref, pb_ref, o_ref, *, nsteps):
    ub = u_ref[0].astype(BF16)
    e = jnp.dot(ub, bg_ref[0], preferred_element_type=F32)
    r = e.shape[0]
    row = lax.broadcasted_iota(jnp.int32, (r, LANES), 0)
    xf, xb = e[:, :LANES], e[:, LANES:]
    pa, pb = pa_ref[0], pb_ref[0]
    for s in range(nsteps):
        sh = 1 << s
        a_f, b_f = pa[s:s + 1, :LANES], pb[s:s + 1, :LANES]
        a_b, b_b = pa[s:s + 1, LANES:], pb[s:s + 1, LANES:]
        yf = jnp.where(row >= sh, pltpu.roll(xf, sh, 0), 0.0)
        yb = jnp.where(row < r - sh, pltpu.roll(xb, r - sh, 0), 0.0)
        xf = xf + a_f * yf + b_f * pltpu.roll(yf, LANES // 2, 1)
        xb = xb + a_b * yb + b_b * pltpu.roll(yb, LANES // 2, 1)
    sprev = jnp.where(row >= 1, pltpu.roll(xf, 1, 0), 0.0)
    snext = jnp.where(row < r - 1, pltpu.roll(xb, r - 1, 0), 0.0)
    st = jnp.concatenate([sprev, snext], axis=1).astype(BF16)
    o_ref[0] = (jnp.dot(ub, mt_ref[0], preferred_element_type=F32)
                + jnp.dot(st, cg_ref[0], preferred_element_type=F32))


def _s5_tables(a_re, a_im, log_dt, b_re, b_im, c_re, c_im, d_skip, tc, nsteps):
    g, p, ch = S5_GROUPS, S5_STATE, S5_CH
    hp = lax.Precision.HIGHEST
    are, aim = a_re.astype(F32), a_im.astype(F32)
    delta = jnp.exp(log_dt.astype(F32))[..., None]
    lre, lim = are * delta, aim * delta

    class Cx:
        def __init__(self, re, im):
            self.re, self.im = re, im

        def __mul__(self, o):
            return Cx(self.re * o.re - self.im * o.im, self.re * o.im + self.im * o.re)

        def __getitem__(self, idx):
            return Cx(self.re[idx], self.im[idx])

    def apow(n):
        n = jnp.asarray(n, F32)[None, None, :, None]
        mag, ang = jnp.exp(lre[:, :, None, :] * n), lim[:, :, None, :] * n
        return Cx(mag * jnp.cos(ang), mag * jnp.sin(ang))

    abr, abi = jnp.exp(lre) * jnp.cos(lim), jnp.exp(lre) * jnp.sin(lim)
    den = are * are + aim * aim
    quo = Cx(((abr - 1.0) * are + abi * aim) / den, (abi * are - (abr - 1.0) * aim) / den)
    b_bar = quo[..., None] * Cx(b_re.astype(F32)[None], b_im.astype(F32)[None])
    c = Cx(c_re.astype(F32), c_im.astype(F32))
    taus = np.arange(tc)
    cp = c[:, :, None] * apow(taus)[:, :, :, None, :]
    kk = (jnp.einsum("dgtop,dgpi->dgtoi", cp.re, b_bar.re, precision=hp)
          - jnp.einsum("dgtop,dgpi->dgtoi", cp.im, b_bar.im, precision=hp))
    dsk = d_skip.astype(F32).reshape(g, ch)[:, :, None] * jnp.eye(ch, dtype=F32)[None]
    kdiag = kk[0][:, 0] + kk[1][:, 0] + dsk
    lags = jnp.concatenate([kk[1][:, :0:-1], kdiag[:, None], kk[0][:, 1:]], axis=1)
    diff = taus[None, :] - taus[:, None]
    sel = (diff[None] + (tc - 1) == np.arange(2 * tc - 1)[:, None, None]).astype(np.float32)
    mt = jnp.einsum("jst,gjoi->gsito", jnp.asarray(sel), lags, precision=hp).reshape(g, tc * ch, tc * ch)

    zf = apow(tc - 1 - taus)[0][..., None] * b_bar[0][:, None]
    zb = apow(taus)[1][..., None] * b_bar[1][:, None]

    def to_rows(z):
        return jnp.transpose(z, (0, 1, 3, 2)).reshape(g, tc * ch, p)

    bg = jnp.concatenate([to_rows(zf.re), to_rows(zf.im), to_rows(zb.re), to_rows(zb.im)], axis=-1)

    yf = c[0][:, None] * apow(taus + 1)[0][:, :, None, :]
    yb = c[1][:, None] * apow(tc - taus)[1][:, :, None, :]

    def to_cols(z):
        return jnp.transpose(z, (0, 3, 1, 2)).reshape(g, p, tc * ch)

    cg = jnp.concatenate([to_cols(yf.re), -to_cols(yf.im), to_cols(yb.re), -to_cols(yb.im)], axis=1)

    steps = tc * (2.0 ** np.arange(nsteps))
    pw = apow(steps)
    re0, im0, re1, im1 = pw.re[0], pw.im[0], pw.re[1], pw.im[1]
    pa = jnp.concatenate([re0, re0, re1, re1], axis=-1)
    pb = jnp.concatenate([-im0, im0, -im1, im1], axis=-1)
    pad = (-nsteps) % 8
    pa = jnp.pad(pa, ((0, 0), (0, pad), (0, 0)))
    pb = jnp.pad(pb, ((0, 0), (0, pad), (0, 0)))
    return mt.astype(BF16), bg.astype(BF16), cg.astype(BF16), pa, pb


def _s5_glu_kernel(y_ref, w_ref, o_ref):
    y = y_ref[...]
    z = 0.5 * y * (1.0 + jnp.tanh(math.sqrt(2.0 / math.pi) * (y + 0.044715 * (y * y * y))))
    o_ref[...] = z * _sigmoid(jnp.dot(z.astype(BF16), w_ref[...], preferred_element_type=F32))


def _s5(proj, a_re, a_im, log_dt, b_re, b_im, c_re, c_im, d_skip, w_glu):
    bsz, l, _ = proj.shape
    tc = S5_TC
    r = l // tc
    nsteps = max(1, int(math.ceil(math.log2(r))))
    w = tc * S5_CH
    mt, bg, cg, pa, pb = _s5_tables(a_re, a_im, log_dt, b_re, b_im, c_re, c_im, d_skip, tc, nsteps)
    u = proj[:, :, CB_S5 * D_GROUP:(CB_S5 + 1) * D_GROUP].astype(F32)
    ug = jnp.transpose(u.reshape(bsz, r, tc, S5_GROUPS, S5_CH), (3, 0, 1, 2, 4)).reshape(S5_GROUPS, bsz * r, w)
    ns8 = pa.shape[1]
    yg = pl.pallas_call(
        functools.partial(_s5_kernel, nsteps=nsteps),
        out_shape=jax.ShapeDtypeStruct((S5_GROUPS, bsz * r, w), F32), grid=(S5_GROUPS, bsz),
        in_specs=[pl.BlockSpec((1, r, w), lambda g, b: (g, b, 0)),
                  pl.BlockSpec((1, w, w), lambda g, b: (g, 0, 0)),
                  pl.BlockSpec((1, w, D_GROUP), lambda g, b: (g, 0, 0)),
                  pl.BlockSpec((1, D_GROUP, w), lambda g, b: (g, 0, 0)),
                  pl.BlockSpec((1, ns8, D_GROUP), lambda g, b: (g, 0, 0)),
                  pl.BlockSpec((1, ns8, D_GROUP), lambda g, b: (g, 0, 0))],
        out_specs=pl.BlockSpec((1, r, w), lambda g, b: (g, b, 0)),
        compiler_params=_cparams(("parallel", "parallel"), 48), name="s5_ssm")(ug, mt, bg, cg, pa, pb)
    y = jnp.transpose(yg.reshape(S5_GROUPS, bsz, r, tc, S5_CH), (1, 2, 3, 0, 4)).reshape(bsz * l, D_GROUP)
    t = bsz * l
    tm = _tile(t, 2048)
    out = pl.pallas_call(
        _s5_glu_kernel, out_shape=jax.ShapeDtypeStruct((t, D_GROUP), F32), grid=(t // tm,),
        in_specs=[pl.BlockSpec((tm, D_GROUP), lambda i: (i, 0)), pl.BlockSpec((D_GROUP, D_GROUP), lambda i: (0, 0))],
        out_specs=pl.BlockSpec((tm, D_GROUP), lambda i: (i, 0)),
        compiler_params=_cparams(("parallel",)), name="s5_glu")(y, w_glu.astype(BF16))
    return out.reshape(bsz, l, D_GROUP)


def _hy_filter_kernel(z_ref, w1_ref, b1_ref, w2_ref, b2_ref, w3_ref, fr_ref, dec_ref, h_ref, ss_ref):
    i = pl.program_id(0)
    hp = lax.Precision.HIGHEST
    fr = fr_ref[...]
    a = jnp.sin(fr * (jnp.dot(z_ref[...], w1_ref[...], precision=hp, preferred_element_type=F32) + b1_ref[...]))
    a = jnp.sin(fr * (jnp.dot(a, w2_ref[...], precision=hp, preferred_element_type=F32) + b2_ref[...]))
    h = jnp.dot(a, w3_ref[...], precision=hp, preferred_element_type=F32)
    df, db = dec_ref[:, :D_GROUP], dec_ref[:, D_GROUP:]
    h = h * jnp.concatenate([df, df, db, db], axis=1)

    @pl.when(i == 0)
    def _():
        ss_ref[...] = jnp.zeros_like(ss_ref)

    ss_ref[...] += jnp.sum(h * h, axis=0, keepdims=True)
    nhalf = h.shape[1] // 2
    row = lax.broadcasted_iota(jnp.int32, h.shape, 0)
    col = lax.broadcasted_iota(jnp.int32, h.shape, 1)
    h_ref[...] = jnp.where((row == 0) & (col >= nhalf) & (i == 0), 0.0, h)


def _hy_filters(l, w1, b1, w2, b2, w3, freq):
    t = np.linspace(0.0, 1.0, l)[:, None]
    w = 2.0 * np.pi * np.arange(l, dtype=np.float64)[:, None] / l
    bands = np.linspace(1e-4, HY_BANDS - 1, HY_BANDS)[None, :]
    z = np.concatenate([t, np.cos(bands * w), -np.sin(bands * w)], axis=-1)
    max_decay = math.log(HY_TARGET) / HY_FAST_DECAY
    min_decay = math.log(HY_TARGET) / HY_SLOW_DECAY
    rates = np.abs(np.linspace(min_decay, max_decay, D_GROUP))
    dec = np.exp(-t * rates)
    rev = np.concatenate([[0], np.arange(l - 1, 0, -1)])
    half = LANES // 2
    zz = np.zeros((l, LANES))
    zz[:, :HY_EMB] = z
    zz[:, half:half + HY_EMB] = z[rev]
    zz = jnp.asarray(zz, F32)
    dec2 = jnp.asarray(np.concatenate([dec, dec[rev]], axis=1), F32)

    def two(m):
        m = m.astype(F32)
        top = jnp.pad(m, ((0, half - m.shape[0]), (0, half - m.shape[1])))
        zero = jnp.zeros_like(top)
        return jnp.concatenate([jnp.concatenate([top, zero], 1), jnp.concatenate([zero, top], 1)], 0)

    def twice(v):
        v = jnp.pad(v.astype(F32), (0, half - v.shape[0]))
        return jnp.concatenate([v, v]).reshape(1, LANES)

    w3r = w3.astype(F32).reshape(HY_FFN, HY_ORDER, 2, D_GROUP)
    nhalf = HY_ORDER * D_GROUP
    w3f = jnp.pad(w3r[:, :, 0].reshape(HY_FFN, nhalf), ((0, half - HY_FFN), (0, 0)))
    w3b = jnp.pad(w3r[:, :, 1].reshape(HY_FFN, nhalf), ((0, half - HY_FFN), (0, 0)))
    zero = jnp.zeros_like(w3f)
    w3p = jnp.concatenate([jnp.concatenate([w3f, zero], 1), jnp.concatenate([zero, w3b], 1)], 0)
    nout = 2 * nhalf
    tl = _tile(l, 512)

    def const(shape):
        return pl.BlockSpec(shape, lambda i: (0, 0))

    return pl.pallas_call(
        _hy_filter_kernel,
        out_shape=(jax.ShapeDtypeStruct((l, nout), F32), jax.ShapeDtypeStruct((1, nout), F32)), grid=(l // tl,),
        in_specs=[pl.BlockSpec((tl, LANES), lambda i: (i, 0)), const((LANES, LANES)), const((1, LANES)),
                  const((LANES, LANES)), const((1, LANES)), const((LANES, nout)), const((1, LANES)),
                  pl.BlockSpec((tl, 2 * D_GROUP), lambda i: (i, 0))],
        out_specs=(pl.BlockSpec((tl, nout), lambda i: (i, 0)), const((1, nout))),
        compiler_params=_cparams(("arbitrary",)), name="hyena_filter_mlp",
    )(zz, two(w1), twice(b1), two(w2), twice(b2), w3p, twice(freq), dec2)


def _dft_consts(na):
    nb = FFT_NB
    n = na * nb
    ia = np.arange(na, dtype=np.float64)
    th = 2.0 * np.pi * np.outer(ia, ia) / na
    c1, s1 = np.cos(th), np.sin(th)
    eye8 = np.eye(8)
    fa_full = np.concatenate([c1, -s1], axis=0)
    g_full = np.kron(fa_full, eye8)
    g_half = np.kron(fa_full[:, : na // 2], eye8)
    g_out = np.kron(np.concatenate([c1[: na // 2], -s1[: na // 2]], axis=1) / n, eye8)
    ib = np.arange(nb, dtype=np.float64)
    ph = 2.0 * np.pi * np.outer(ib, ib) / nb
    c2, s2 = np.cos(ph), np.sin(ph)
    fb = np.block([[c2, s2], [-s2, c2]])
    fbc = np.block([[c2, -s2], [s2, c2]])
    ps = 2.0 * np.pi * np.outer(ia, ib) / n
    twr = np.broadcast_to(np.cos(ps)[:, :, None], (na, nb, LANES))
    twi = np.broadcast_to(-np.sin(ps)[:, :, None], (na, nb, LANES))
    as_bf = lambda x: jnp.asarray(x, BF16)
    return dict(g_full=as_bf(g_full), g_half=as_bf(g_half), g_out=as_bf(g_out), fb=as_bf(fb), fbc=as_bf(fbc),
                twr=jnp.asarray(twr, F32), twi=jnp.asarray(twi, F32))


def _lane_tile(x, reps):
    return x if reps == 1 else jnp.concatenate([x] * reps, axis=-1)


def _hy_spec_kernel(a_ref, twr_ref, twi_ref, fb_ref, ss_ref, o_ref, *, kb, reps):
    scale = lax.rsqrt(ss_ref[...])
    for j in range(kb):
        ar, ai = a_ref[0, j], a_ref[1, j]
        twr, twi = _lane_tile(twr_ref[j], reps), _lane_tile(twi_ref[j], reps)
        br = twr * ar - twi * ai
        bi = twr * ai + twi * ar
        x = jnp.dot(fb_ref[...], jnp.concatenate([br, bi], axis=0).astype(BF16), preferred_element_type=F32)
        o_ref[0, j] = x[:FFT_NB] * scale
        o_ref[1, j] = x[FFT_NB:] * scale


def _hy_mid_kernel(a_ref, h_ref, twr_ref, twi_ref, fb_ref, fbc_ref, o_ref, *, kb, reps):
    for j in range(kb):
        ar, ai = a_ref[0, 0, j], a_ref[0, 1, j]
        twr, twi = _lane_tile(twr_ref[j], reps), _lane_tile(twi_ref[j], reps)
        br = twr * ar - twi * ai
        bi = twr * ai + twi * ar
        x = jnp.dot(fb_ref[...], jnp.concatenate([br, bi], axis=0).astype(BF16), preferred_element_type=F32)
        xr, xi = x[:FFT_NB], x[FFT_NB:]
        hr, hi = h_ref[0, j], h_ref[1, j]
        yr = xr * hr - xi * hi
        yi = xr * hi + xi * hr
        z = jnp.dot(fbc_ref[...], jnp.concatenate([yr, yi], axis=0).astype(BF16), preferred_element_type=F32)
        zr, zi = z[:FFT_NB], z[FFT_NB:]
        o_ref[0, 0, j] = twr * zr + twi * zi
        o_ref[0, 1, j] = twr * zi - twi * zr


def _hy_dft1_kernel(g_ref, x_ref, o_ref, *, qb):
    na_in, c = x_ref.shape[1], x_ref.shape[4]
    na = o_ref.shape[2]
    for q in range(qb):
        x = x_ref[0, :, q].reshape(na_in * 8, c).astype(BF16)
        a = jnp.dot(g_ref[...], x, preferred_element_type=F32)
        o_ref[0, :, :, q] = a.reshape(2, na, 8, c)


def _hy_fdft1_kernel(g_ref, xt_ref, xb_ref, o_ref, *, qb):
    nah, c = xt_ref.shape[1], xt_ref.shape[4]
    na = o_ref.shape[2]
    kh = nah * 8
    for q in range(qb):
        xt = xt_ref[0, :, q].reshape(kh, c).astype(BF16)
        xb = xb_ref[0, :, q].reshape(kh, c).astype(BF16)
        a = (jnp.dot(g_ref[:, :kh], xt, preferred_element_type=F32)
             + jnp.dot(g_ref[:, kh:], xb, preferred_element_type=F32))
        o_ref[0, :, :, q] = a.reshape(2, na, 8, c)


def _hy_dft1(g, x5, ncol, name):
    bsz, na_in, nq = x5.shape[:3]
    na = g.shape[0] // 16
    c = D_GROUP
    qb = FFT_QB
    return pl.pallas_call(
        functools.partial(_hy_dft1_kernel, qb=qb),
        out_shape=jax.ShapeDtypeStruct((bsz, 2, na, nq, 8, ncol * c), F32), grid=(bsz, ncol, nq // qb),
        in_specs=[pl.BlockSpec(g.shape, lambda b, j, q: (0, 0)),
                  pl.BlockSpec((1, na_in, qb, 8, c), lambda b, j, q: (b, 0, q, 0, j))],
        out_specs=pl.BlockSpec((1, 2, na, qb, 8, c), lambda b, j, q: (b, 0, 0, q, 0, j)),
        compiler_params=_cparams(("parallel", "parallel", "parallel"), 48), name=name)(g, x5)


def _hy_out_kernel(g_ref, z_ref, x_ref, v_ref, b_ref, o_ref, *, qb):
    na2, c = z_ref.shape[1] * z_ref.shape[2], z_ref.shape[5]
    nah = o_ref.shape[1]
    bias = b_ref[...].reshape(1, 1, c)
    for q in range(qb):
        z = z_ref[0, :, :, q].reshape(na2 * 8, c).astype(BF16)
        y = jnp.dot(g_ref[...], z, preferred_element_type=F32).reshape(nah, 8, c)
        o_ref[0, :, q] = x_ref[0, :, q] * (y + v_ref[0, :, q] * bias)


def _hyena(proj, conv_w, conv_b, w1, b1, w2, b2, w3, freq, bias):
    bsz, l, _ = proj.shape
    nb = FFT_NB
    na = 2 * l // nb
    nah = na // 2
    nq = nb // 8
    c = D_GROUP
    qb = FFT_QB
    dc = _dft_consts(na)
    pc = _shortconv(proj, CB_HV, 3, conv_w, conv_b, act=False)

    h, ss = _hy_filters(l, w1, b1, w2, b2, w3, freq)
    ncf = HY_ORDER * c
    ssn = ss[:, :ncf] + ss[:, ncf:]
    h5 = h.reshape(1, nah, nq, 8, 2 * ncf)
    ka = pl.pallas_call(
        functools.partial(_hy_fdft1_kernel, qb=qb),
        out_shape=jax.ShapeDtypeStruct((1, 2, na, nq, 8, ncf), F32), grid=(HY_ORDER, nq // qb),
        in_specs=[pl.BlockSpec(dc["g_full"].shape, lambda j, q: (0, 0)),
                  pl.BlockSpec((1, nah, qb, 8, c), lambda j, q: (0, 0, q, 0, j)),
                  pl.BlockSpec((1, nah, qb, 8, c), lambda j, q: (0, 0, q, 0, HY_ORDER + j))],
        out_specs=pl.BlockSpec((1, 2, na, qb, 8, c), lambda j, q: (0, 0, 0, q, 0, j)),
        compiler_params=_cparams(("parallel", "parallel"), 48), name="hyena_filter_dft1",
    )(dc["g_full"], h5, h5).reshape(2, na, nb, ncf)
    kb = 4 if na % 4 == 0 else 1
    reps = c // LANES
    tw = pl.BlockSpec((kb, nb, LANES), lambda j, k: (k, 0, 0))
    mat = pl.BlockSpec((2 * nb, 2 * nb), lambda j, k: (0, 0))
    hspec = pl.pallas_call(
        functools.partial(_hy_spec_kernel, kb=kb, reps=reps),
        out_shape=jax.ShapeDtypeStruct((2, na, nb, ncf), F32), grid=(ncf // c, na // kb),
        in_specs=[pl.BlockSpec((2, kb, nb, c), lambda j, k: (0, k, 0, j)), tw, tw, mat,
                  pl.BlockSpec((1, c), lambda j, k: (0, j))],
        out_specs=pl.BlockSpec((2, kb, nb, c), lambda j, k: (0, k, 0, j)),
        compiler_params=_cparams(("parallel", "parallel"), 48), name="hyena_filter_dft2",
    )(ka, dc["twr"], dc["twi"], dc["fb"], ssn)

    pc5 = pc.reshape(bsz, nah, nq, 8, 3 * c)

    def long_conv_gate(z5, order, xcol):
        a = _hy_dft1(dc["g_half"], z5, 1, "hyena_dft1").reshape(bsz, 2, na, nb, c)
        zmid = pl.pallas_call(
            functools.partial(_hy_mid_kernel, kb=kb, reps=reps),
            out_shape=jax.ShapeDtypeStruct((bsz, 2, na, nb, c), F32), grid=(bsz, na // kb),
            in_specs=[pl.BlockSpec((1, 2, kb, nb, c), lambda b, k: (b, 0, k, 0, 0)),
                      pl.BlockSpec((2, kb, nb, c), lambda b, k: (0, k, 0, order)), tw, tw, mat, mat],
            out_specs=pl.BlockSpec((1, 2, kb, nb, c), lambda b, k: (b, 0, k, 0, 0)),
            compiler_params=_cparams(("parallel", "parallel"), 48), name="hyena_dft_mid",
        )(a, hspec, dc["twr"], dc["twi"], dc["fb"], dc["fbc"])
        zmid = zmid.reshape(bsz, 2, na, nq, 8, c)
        sig = lambda col: pl.BlockSpec((1, nah, qb, 8, c), lambda b, q: (b, 0, q, 0, col))
        return pl.pallas_call(
            functools.partial(_hy_out_kernel, qb=qb),
            out_shape=jax.ShapeDtypeStruct((bsz, nah, nq, 8, c), F32), grid=(bsz, nq // qb),
            in_specs=[pl.BlockSpec(dc["g_out"].shape, lambda b, q: (0, 0)),
                      pl.BlockSpec((1, 2, na, qb, 8, c), lambda b, q: (b, 0, 0, q, 0, 0)),
                      sig(xcol), sig(0), pl.BlockSpec((1, c), lambda b, q: (0, 0))],
            out_specs=sig(0),
            compiler_params=_cparams(("parallel", "parallel"), 48), name="hyena_idft_gate",
        )(dc["g_out"], zmid, pc5, z5, bias[order].astype(F32).reshape(1, c))

    z1 = long_conv_gate(pc5, 0, 1)
    z2 = long_conv_gate(z1, 1, 2)
    return z2.reshape(bsz, l, c)


def _ffn_kernel(x_ref, w1_ref, w3_ref, w2_ref, lw_ref, lb_ref, o_ref, xb_ref, acc_ref, *, nf):
    f = pl.program_id(1)

    @pl.when(f == 0)
    def _():
        xb_ref[...] = x_ref[...].astype(BF16)
        acc_ref[...] = jnp.zeros_like(acc_ref)

    xb = xb_ref[...]
    a = jnp.dot(xb, w1_ref[...], preferred_element_type=F32)
    b = jnp.dot(xb, w3_ref[...], preferred_element_type=F32)
    acc_ref[...] += jnp.dot((_silu(a) * b).astype(BF16), w2_ref[...], preferred_element_type=F32)

    @pl.when(f == nf - 1)
    def _():
        o_ref[...] = _ln_core(DN_ALPHA * x_ref[...] + acc_ref[...], lw_ref[...], lb_ref[...])


def _ffn_ln(x, w1, w3, w2, lw, lb):
    t, d = x.shape
    ff = w1.shape[1]
    tm = _tile(t, 1024)
    tf = 512 if ff % 512 == 0 else (256 if ff % 256 == 0 else ff)
    nf = ff // tf
    vec = pl.BlockSpec((1, d), lambda i, f: (0, 0))
    return pl.pallas_call(
        functools.partial(_ffn_kernel, nf=nf),
        out_shape=jax.ShapeDtypeStruct((t, d), F32), grid=(t // tm, nf),
        in_specs=[pl.BlockSpec((tm, d), lambda i, f: (i, 0)),
                  pl.BlockSpec((d, tf), lambda i, f: (0, f)),
                  pl.BlockSpec((d, tf), lambda i, f: (0, f)),
                  pl.BlockSpec((tf, d), lambda i, f: (f, 0)), vec, vec],
        out_specs=pl.BlockSpec((tm, d), lambda i, f: (i, 0)),
        scratch_shapes=[pltpu.VMEM((tm, d), BF16), pltpu.VMEM((tm, d), F32)],
        compiler_params=_cparams(("parallel", "arbitrary"), 52), name="swiglu_ffn_ln",
    )(x, w1, w3, w2, lw.reshape(1, d), lb.reshape(1, d))


MOE_SB = 832
MOE_NSB = 2
MOE_SUB = 256
MOE_CUM = 64
MOE_MAXP = -(-MOE_SB // MOE_SUB)


def _moe_kernel(cnt_ref, x_ref, cmb_ref, lt_ref, w1_ref, w3_ref, w2_ref, lw_ref, lb_ref, o_ref,
                xb_ref, xs_ref, ys_ref, gs_ref, pos_ref, *, nf, nsb, t_total):
    i = pl.program_id(0)
    e = pl.program_id(1)
    f = pl.program_id(2)
    sb = x_ref.shape[0] // nsb
    npass = [(cnt_ref[(i * nsb + s) * N_EXPERTS + e] + (MOE_SUB - 1)) // MOE_SUB for s in range(nsb)]
    sub = [slice(s * sb, (s + 1) * sb) for s in range(nsb)]

    @pl.when((e == 0) & (f == 0))
    def _():
        o_ref[...] = jnp.zeros_like(o_ref)
        for s in range(nsb):
            valid = lax.broadcasted_iota(jnp.int32, (sb, 1), 0) < t_total - (i * nsb + s) * sb
            xb_ref[sub[s]] = jnp.where(valid, x_ref[sub[s]], 0.0).astype(BF16)
            carry = jnp.zeros((1, LANES), F32)
            for c in range(sb // MOE_CUM):
                rows = slice(s * sb + c * MOE_CUM, s * sb + (c + 1) * MOE_CUM)
                vrows = slice(c * MOE_CUM, (c + 1) * MOE_CUM)
                m = jnp.where(valid[vrows] & (cmb_ref[rows] > 0.0), 1.0, 0.0)
                inc = jnp.dot(lt_ref[...], m, preferred_element_type=F32) + carry
                pos_ref[rows] = jnp.where(m > 0.0, inc - 1.0, -1.0)
                carry = inc[MOE_CUM - 1:MOE_CUM]

    lane = lax.broadcasted_iota(jnp.int32, (sb, LANES), 1)

    def one_hot(s, j):
        pos = jnp.sum(jnp.where(lane == e, pos_ref[sub[s]], 0.0), axis=1, keepdims=True)
        slot = lax.broadcasted_iota(jnp.int32, (sb, MOE_SUB), 1).astype(F32) + jnp.asarray(j * MOE_SUB, F32)
        return pos == slot

    single = npass[0] == 1
    for s in range(1, nsb):
        single = single & (npass[s] == 1)

    def for_passes(body):
        @pl.when(single)
        def _():
            for s in range(nsb):
                body(s, 0)

        @pl.when(jnp.logical_not(single))
        def _():
            for s in range(nsb):
                def step(j, carry, s=s):
                    body(s, j)
                    return carry

                lax.fori_loop(0, npass[s], step, 0)

    def gather(s, j):
        gate = jnp.sum(jnp.where(lane == e, cmb_ref[sub[s]], 0.0), axis=1, keepdims=True)
        hit = one_hot(s, j)
        k = s * MOE_MAXP + j
        xs_ref[k] = _dot_tn(xb_ref[sub[s]], jnp.where(hit, 1.0, 0.0).astype(BF16)).astype(BF16)
        g = jnp.sum(jnp.where(hit, gate, 0.0), axis=0, keepdims=True)
        gs_ref[k] = jnp.broadcast_to(g, (8, MOE_SUB))
        ys_ref[k] = jnp.zeros(ys_ref.shape[1:], F32)

    def expert(s, j):
        k = s * MOE_MAXP + j
        xs = xs_ref[k]
        a = jnp.dot(w1_ref[0], xs, preferred_element_type=F32)
        b = jnp.dot(w3_ref[0], xs, preferred_element_type=F32)
        hid = (_silu(a) * b * gs_ref[k][0:1]).astype(BF16)
        ys_ref[k] += jnp.dot(w2_ref[0], hid, preferred_element_type=F32)

    def scatter(s, j):
        hit = one_hot(s, j)
        o_ref[sub[s]] += _dot_nt(jnp.where(hit, 1.0, 0.0).astype(BF16),
                                 ys_ref[s * MOE_MAXP + j].astype(BF16))

    @pl.when(f == 0)
    def _():
        for_passes(gather)

    for_passes(expert)

    @pl.when(f == nf - 1)
    def _():
        for_passes(scatter)

        @pl.when(e == pl.num_programs(1) - 1)
        def _():
            o_ref[...] = _ln_core(DN_ALPHA * x_ref[...] + o_ref[...], lw_ref[...], lb_ref[...])


def _moe_ln(x, cmb, w1, w3, w2, lw, lb):
    t, d = x.shape
    ne, ff, _ = w1.shape
    nsb = MOE_NSB
    tb = nsb * MOE_SB
    nb = -(-t // tb)
    tf = 896 if ff % 896 == 0 else ff
    nf = ff // tf
    max_pass = nsb * MOE_MAXP
    cmb_p = jnp.pad(cmb, ((0, nb * tb - t), (0, 0)))
    counts = jnp.sum((cmb_p[:, :N_EXPERTS] > 0.0).reshape(nb * nsb, MOE_SB, N_EXPERTS), axis=1)
    counts = counts.astype(jnp.int32).reshape(-1)
    idx = np.arange(MOE_CUM)
    lt = jnp.asarray(idx[None, :] <= idx[:, None], F32)
    once = pl.Buffered(1)
    grid_spec = pltpu.PrefetchScalarGridSpec(
        num_scalar_prefetch=1, grid=(nb, ne, nf),
        in_specs=[pl.BlockSpec((tb, d), lambda i, e, f, c: (i, 0), pipeline_mode=once),
                  pl.BlockSpec((tb, LANES), lambda i, e, f, c: (i, 0), pipeline_mode=once),
                  pl.BlockSpec((MOE_CUM, MOE_CUM), lambda i, e, f, c: (0, 0)),
                  pl.BlockSpec((1, tf, d), lambda i, e, f, c: (e, f, 0)),
                  pl.BlockSpec((1, tf, d), lambda i, e, f, c: (e, f, 0)),
                  pl.BlockSpec((1, d, tf), lambda i, e, f, c: (e, 0, f)),
                  pl.BlockSpec((1, d), lambda i, e, f, c: (0, 0)),
                  pl.BlockSpec((1, d), lambda i, e, f, c: (0, 0))],
        out_specs=pl.BlockSpec((tb, d), lambda i, e, f, c: (i, 0), pipeline_mode=once),
        scratch_shapes=[pltpu.VMEM((tb, d), BF16), pltpu.VMEM((max_pass, d, MOE_SUB), BF16),
                        pltpu.VMEM((max_pass, d, MOE_SUB), F32), pltpu.VMEM((max_pass, 8, MOE_SUB), F32),
                        pltpu.VMEM((tb, LANES), F32)])
    return pl.pallas_call(
        functools.partial(_moe_kernel, nf=nf, nsb=nsb, t_total=t), out_shape=jax.ShapeDtypeStruct((t, d), F32),
        grid_spec=grid_spec,
        compiler_params=_cparams(("parallel", "arbitrary", "arbitrary"), 58), name="moe_routed",
    )(counts, x, cmb_p, lt, w1, w3, w2, lw.reshape(1, d), lb.reshape(1, d))


def _router_kernel(x_ref, rh_ref, rl_ref, o_ref):
    x = x_ref[...]
    xh = x.astype(BF16)
    xl = (x - xh.astype(F32)).astype(BF16)
    logits = (jnp.dot(xh, rh_ref[...], preferred_element_type=F32)
              + jnp.dot(xl, rh_ref[...], preferred_element_type=F32)
              + jnp.dot(xh, rl_ref[...], preferred_element_type=F32))
    lane = lax.broadcasted_iota(jnp.int32, logits.shape, 1).astype(F32)
    logits = jnp.where(lane < N_EXPERTS, logits, -jnp.inf)
    m1 = jnp.max(logits, axis=1, keepdims=True)
    i1 = jnp.min(jnp.where(logits == m1, lane, float(LANES)), axis=1, keepdims=True)
    rest = jnp.where(lane == i1, -jnp.inf, logits)
    m2 = jnp.max(rest, axis=1, keepdims=True)
    i2 = jnp.min(jnp.where(rest == m2, lane, float(LANES)), axis=1, keepdims=True)
    e2 = jnp.exp(m2 - m1)
    g1 = 1.0 / (1.0 + e2)
    g2 = e2 / (1.0 + e2)
    o_ref[...] = jnp.where(lane == i1, g1, 0.0) + jnp.where(lane == i2, g2, 0.0)


def _router(x, router):
    t, d = x.shape
    r = jnp.pad(router.astype(F32), ((0, 0), (0, LANES - N_EXPERTS)))
    rh = r.astype(BF16)
    rl = (r - rh.astype(F32)).astype(BF16)
    tm = _tile(t, 1024)
    return pl.pallas_call(
        _router_kernel, out_shape=jax.ShapeDtypeStruct((t, LANES), F32), grid=(t // tm,),
        in_specs=[pl.BlockSpec((tm, d), lambda i: (i, 0)), pl.BlockSpec((d, LANES), lambda i: (0, 0)),
                  pl.BlockSpec((d, LANES), lambda i: (0, 0))],
        out_specs=pl.BlockSpec((tm, LANES), lambda i: (i, 0)),
        compiler_params=_cparams(("parallel",)), name="moe_router")(x, rh, rl)


def _extended_w_in(w_in):
    w = w_in.astype(F32)
    scale = HEAD_DIM ** -0.5

    def rot_half(cols):
        c4 = cols.reshape(-1, N_HEADS, 2, HEAD_DIM // 2)
        return jnp.stack([-c4[:, :, 1], c4[:, :, 0]], axis=2).reshape(-1, D_GROUP)

    wq = w[:, 0:256]
    wk = w[:, 256:512] * scale
    main = jnp.concatenate([wq, wk, w[:, 512:3072]], axis=1)
    gates = jnp.pad(w[:, 3072:3088], ((0, 0), (0, LANES - 16)))
    ext = jnp.concatenate([main, rot_half(wq), rot_half(wk), gates], axis=1)
    return jnp.pad(ext, ((0, 0), (0, N_EXT - ext.shape[1]))).astype(BF16)


def kernel(x, ln_in_w, ln_in_b, w_in, w_out, ret_gn_w, s5_a_re, s5_a_im, s5_log_dt, s5_b_re, s5_b_im, s5_c_re, s5_c_im, s5_d, s5_w_glu, hy_conv_w, hy_conv_b, hy_w1, hy_b1, hy_w2, hy_b2, hy_w3, hy_freq, hy_bias, ml_conv_w, ml_conv_b, ml_gate_b, ml_gn_w, ln1_w, ln1_b, ln2_w, ln2_b, ffn_w1, ffn_w3, ffn_w2, moe_router, moe_w1, moe_w3, moe_w2):
    bsz, l, d = x.shape
    t = bsz * l
    cos_full, sin_full = _rope_tables(l)
    h = _layer_norm(x.reshape(t, d), ln_in_w, ln_in_b)
    for layer in range(DEPTH):
        proj, gates = _in_proj(h, _extended_w_in(w_in[layer]))
        proj, gates = proj.reshape(bsz, l, N_EXT), gates.reshape(bsz, l, LANES)
        y_ret = _retention(proj, ret_gn_w[layer], cos_full, sin_full)
        y_s5 = _s5(proj, s5_a_re[layer], s5_a_im[layer], s5_log_dt[layer], s5_b_re[layer], s5_b_im[layer],
                   s5_c_re[layer], s5_c_im[layer], s5_d[layer], s5_w_glu[layer])
        y_hy = _hyena(proj, hy_conv_w[layer], hy_conv_b[layer], hy_w1[layer], hy_b1[layer], hy_w2[layer],
                      hy_b2[layer], hy_w3[layer], hy_freq[layer], hy_bias[layer])
        qk = _shortconv(proj, CB_MQ, 2, ml_conv_w[layer], ml_conv_b[layer], act=True)
        y_ml = _mlstm(proj, qk, gates, ml_gate_b[layer], ml_gn_w[layer])
        ys = [y.reshape(t, D_GROUP) for y in (y_ret, y_s5, y_hy, y_ml)]
        h = _outproj_ln(ys, w_out[layer], h, ln1_w[layer], ln1_b[layer])
        j = layer // 2
        if layer % 2 == 0:
            h = _ffn_ln(h, ffn_w1[j].astype(BF16), ffn_w3[j].astype(BF16), ffn_w2[j].astype(BF16),
                        ln2_w[layer], ln2_b[layer])
        else:
            cmb = _router(h, moe_router[j])
            wt = [jnp.swapaxes(w[j], 1, 2).astype(BF16) for w in (moe_w1, moe_w3, moe_w2)]
            h = _moe_ln(h, cmb, *wt, ln2_w[layer], ln2_b[layer])
    return h.reshape(bsz, l, d)
```

```python
import functools
import math

import numpy as np
import jax
import jax.numpy as jnp
from jax import lax
from jax.experimental import pallas as pl
from jax.experimental.pallas import tpu as pltpu

F32 = jnp.float32
BF16 = jnp.bfloat16

D_MODEL = 1024
DEPTH = 2
D_GROUP = 256
HEAD_DIM = 64
N_HEADS = 4
CHUNK = 128
S5_CH = 16
S5_GROUPS = 16
S5_STATE = 64
HY_ORDER = 2
HY_EMB = 33
HY_BANDS = 16
HY_FFN = 64
HY_FAST_DECAY = 0.3
HY_SLOW_DECAY = 1.5
HY_TARGET = 1e-2
N_EXPERTS = 8
ROPE_BASE = 10000.0
EPS = 1e-5
DN_ALPHA = (2 * DEPTH) ** 0.25

LANES = 128
HALO = 16
S5_TC = 32
FFT_NB = 256
FFT_QB = 2
N_EXT = 3840

CB_RQ, CB_RK, CB_RV, CB_RG, CB_S5, CB_HV, CB_HX1, CB_HX2 = 0, 1, 2, 3, 4, 5, 6, 7
CB_MQ, CB_MK, CB_MV, CB_MO, CB_RQR, CB_RKR = 8, 9, 10, 11, 12, 13
GATE_COL128 = 28


def _cparams(sem, vmem_mb=None):
    kw = dict(dimension_semantics=sem)
    if vmem_mb is not None:
        kw["vmem_limit_bytes"] = vmem_mb * 1024 * 1024
    return pltpu.CompilerParams(**kw)


def _tile(n, pref):
    return pref if n % pref == 0 else n


def _split_dot(x, m, parts=2):
    acc = None
    r = x
    for _ in range(parts):
        hi = r.astype(BF16)
        t = jnp.dot(hi, m, preferred_element_type=F32)
        acc = t if acc is None else acc + t
        r = r - hi.astype(F32)
    return acc


def _split_dot_left(m, x, parts=2):
    acc = None
    r = x
    for _ in range(parts):
        hi = r.astype(BF16)
        t = jnp.dot(m, hi, preferred_element_type=F32)
        acc = t if acc is None else acc + t
        r = r - hi.astype(F32)
    return acc


def _dot_nt(a, b):
    return lax.dot_general(a, b, (((1,), (1,)), ((), ())), preferred_element_type=F32)


def _dot_tn(a, b):
    return lax.dot_general(a, b, (((0,), (0,)), ((), ())), preferred_element_type=F32)


def _sigmoid(x):
    return 1.0 / (1.0 + jnp.exp(-x))


def _silu(x):
    return x * _sigmoid(x)


def _log_sigmoid(x):
    return jnp.minimum(x, 0.0) - jnp.log(1.0 + jnp.exp(-jnp.abs(x)))


def _head_masks(dtype):
    lane = lax.broadcasted_iota(jnp.int32, (1, D_GROUP), 1)
    return [((lane >= h * HEAD_DIM) & (lane < (h + 1) * HEAD_DIM)).astype(dtype) for h in range(N_HEADS)]


def _ln_core(x, w, b):
    mu = jnp.mean(x, -1, keepdims=True)
    xc = x - mu
    var = jnp.mean(xc * xc, -1, keepdims=True)
    return xc * lax.rsqrt(var + EPS) * w + b


def _ln_kernel(x_ref, w_ref, b_ref, o_ref):
    o_ref[...] = _ln_core(x_ref[...], w_ref[...], b_ref[...])


def _layer_norm(x, w, b):
    t, d = x.shape
    tm = _tile(t, 512)
    row = pl.BlockSpec((tm, d), lambda i: (i, 0))
    vec = pl.BlockSpec((1, d), lambda i: (0, 0))
    return pl.pallas_call(_ln_kernel, out_shape=jax.ShapeDtypeStruct((t, d), F32), grid=(t // tm,),
                          in_specs=[row, vec, vec], out_specs=row,
                          compiler_params=_cparams(("parallel",)), name="layer_norm")(x, w.reshape(1, d), b.reshape(1, d))


IN_PROJ_TN = 1280


def _in_proj_kernel(a_ref, b_ref, o_ref, g_ref, *, nj, gate_off):
    acc = jnp.dot(a_ref[...].astype(BF16), b_ref[...], preferred_element_type=F32)
    o_ref[...] = acc.astype(o_ref.dtype)

    @pl.when(pl.program_id(1) == nj - 1)
    def _():
        g_ref[...] = acc[:, gate_off:gate_off + LANES]


def _in_proj(h, w_ext):
    m, k = h.shape
    n = w_ext.shape[1]
    tm, tn = _tile(m, 1024), IN_PROJ_TN
    nj = n // tn
    gate_off = GATE_COL128 * LANES - (nj - 1) * tn
    return pl.pallas_call(
        functools.partial(_in_proj_kernel, nj=nj, gate_off=gate_off),
        out_shape=(jax.ShapeDtypeStruct((m, n), BF16), jax.ShapeDtypeStruct((m, LANES), F32)), grid=(m // tm, nj),
        in_specs=[pl.BlockSpec((tm, k), lambda i, j: (i, 0)), pl.BlockSpec((k, tn), lambda i, j: (0, j))],
        out_specs=(pl.BlockSpec((tm, tn), lambda i, j: (i, j)), pl.BlockSpec((tm, LANES), lambda i, j: (i, 0))),
        compiler_params=_cparams(("parallel", "arbitrary"), 48), name="in_proj")(h, w_ext)


def _outproj_ln_kernel(y0_ref, y1_ref, y2_ref, y3_ref, w_ref, h_ref, lw_ref, lb_ref, o_ref):
    mix = None
    for g, y_ref in enumerate((y0_ref, y1_ref, y2_ref, y3_ref)):
        part = jnp.dot(y_ref[...].astype(BF16), w_ref[g * D_GROUP:(g + 1) * D_GROUP, :], preferred_element_type=F32)
        mix = part if mix is None else mix + part
    o_ref[...] = _ln_core(DN_ALPHA * h_ref[...] + mix, lw_ref[...], lb_ref[...])


def _outproj_ln(ys, w_out, h, lw, lb):
    t, d = h.shape
    tm = _tile(t, 1024)
    grp = pl.BlockSpec((tm, D_GROUP), lambda i: (i, 0))
    row = pl.BlockSpec((tm, d), lambda i: (i, 0))
    vec = pl.BlockSpec((1, d), lambda i: (0, 0))
    return pl.pallas_call(
        _outproj_ln_kernel, out_shape=jax.ShapeDtypeStruct((t, d), F32), grid=(t // tm,),
        in_specs=[grp, grp, grp, grp, pl.BlockSpec((d, d), lambda i: (0, 0)), row, vec, vec], out_specs=row,
        compiler_params=_cparams(("parallel",), 48), name="out_proj_ln",
    )(*ys, w_out.astype(BF16), h, lw.reshape(1, d), lb.reshape(1, d))


def _shortconv_kernel(x_ref, xp_ref, xn_ref, w_ref, b_ref, o_ref, *, nt, act):
    i = pl.program_id(1)
    x = x_ref[0].astype(F32)
    tl = x.shape[0]
    row = lax.broadcasted_iota(jnp.int32, x.shape, 0)
    prev_row = jnp.where(i == 0, 0.0, xp_ref[0, HALO - 1:HALO, :].astype(F32))
    next_row = jnp.where(i == nt - 1, 0.0, xn_ref[0, 0:1, :].astype(F32))
    x_prev = jnp.where(row == 0, prev_row, pltpu.roll(x, 1, 0))
    x_next = jnp.where(row == tl - 1, next_row, pltpu.roll(x, tl - 1, 0))
    w = w_ref[0]
    y = b_ref[0, 0:1] + x_prev * w[0:1] + x * w[1:2] + x_next * w[2:3]
    if act:
        y = _silu(y)
    o_ref[0] = y


def _shortconv(proj, col0, nblk, w, b, act):
    bsz, l, _ = proj.shape
    tl = _tile(l, 1024)
    nt = l // tl
    w3 = jnp.transpose(w.reshape(3, nblk, D_GROUP), (1, 0, 2))
    w3 = jnp.pad(w3, ((0, 0), (0, 5), (0, 0)))
    b3 = jnp.broadcast_to(b.reshape(nblk, 1, D_GROUP), (nblk, 8, D_GROUP))
    rh = tl // HALO
    return pl.pallas_call(
        functools.partial(_shortconv_kernel, nt=nt, act=act),
        out_shape=jax.ShapeDtypeStruct((bsz, l, nblk * D_GROUP), F32), grid=(bsz, nt, nblk),
        in_specs=[
            pl.BlockSpec((1, tl, D_GROUP), lambda bb, i, j: (bb, i, col0 + j)),
            pl.BlockSpec((1, HALO, D_GROUP), lambda bb, i, j: (bb, jnp.maximum(i * rh - 1, 0), col0 + j)),
            pl.BlockSpec((1, HALO, D_GROUP), lambda bb, i, j: (bb, jnp.minimum((i + 1) * rh, l // HALO - 1), col0 + j)),
            pl.BlockSpec((1, 8, D_GROUP), lambda bb, i, j: (j, 0, 0)),
            pl.BlockSpec((1, 8, D_GROUP), lambda bb, i, j: (j, 0, 0)),
        ],
        out_specs=pl.BlockSpec((1, tl, D_GROUP), lambda bb, i, j: (bb, i, j)),
        compiler_params=_cparams(("parallel", "parallel", "parallel")), name="shortconv")(proj, proj, proj, w3, b3)


def _stack_heads(xb, masks):
    return jnp.concatenate([xb * masks[h] for h in range(N_HEADS)], axis=0)


def _compact(s):
    return s[0:64] + s[64:128] + s[128:192] + s[192:256]


def _expand(c, bd):
    return jnp.concatenate([c, c, c, c], axis=0) * bd


def _head_norm(o, avg, gn):
    mu = _split_dot(o, avg, parts=2)
    oc = o - mu
    var = _split_dot(oc * oc, avg, parts=2)
    return oc * lax.rsqrt(var + EPS) * gn


def _ret_kernel(q_ref, qr_ref, k_ref, kr_ref, v_ref, g_ref, cos_ref, sin_ref,
                dsym_ref, qdf_ref, qdb_ref, kdf_ref, kdb_ref, cdec_ref, bd_ref, avg_ref, gn_ref,
                o_ref, sfw_ref, sbw_ref, save_ref, *, cb, nblk):
    p = pl.program_id(0)
    i = pl.program_id(1)
    bsz = q_ref.shape[0]
    masks = _head_masks(BF16)
    bd = bd_ref[...]
    cdec = cdec_ref[...]

    def rope_k(b, rows):
        return k_ref[b, rows] * cos_ref[rows] + kr_ref[b, rows] * sin_ref[rows]

    def kv_update(s, k, decay, vb):
        kv = _dot_tn((k * decay).astype(BF16), vb)
        return s * cdec + kv * bd

    @pl.when(p == 0)
    def _():
        @pl.when(i == 0)
        def _():
            sbw_ref[...] = jnp.zeros_like(sbw_ref)

        blk = nblk - 1 - i
        for c in reversed(range(cb)):
            rows = slice(c * CHUNK, (c + 1) * CHUNK)
            for b in range(bsz):
                s = sbw_ref[b]
                save_ref[b, blk * cb + c] = _compact(s)
                sbw_ref[b] = kv_update(s, rope_k(b, rows), kdb_ref[...], v_ref[b, rows].astype(BF16))

    @pl.when(p == 1)
    def _():
        @pl.when(i == 0)
        def _():
            sfw_ref[...] = jnp.zeros_like(sfw_ref)

        for c in range(cb):
            rows = slice(c * CHUNK, (c + 1) * CHUNK)
            for b in range(bsz):
                q = q_ref[b, rows] * cos_ref[rows] + qr_ref[b, rows] * sin_ref[rows]
                k = rope_k(b, rows)
                qb, kb, vb = q.astype(BF16), k.astype(BF16), v_ref[b, rows].astype(BF16)
                s_all = _dot_nt(qb, _stack_heads(kb, masks))
                pmat = (s_all * dsym_ref[...]).astype(BF16)
                o = jnp.dot(pmat, _stack_heads(vb, masks), preferred_element_type=F32)
                sfw = sfw_ref[b]
                sbw = _expand(save_ref[b, i * cb + c], bd)
                o = o + jnp.dot(qb, sfw.astype(BF16), preferred_element_type=F32) * qdf_ref[...]
                o = o + jnp.dot(qb, sbw.astype(BF16), preferred_element_type=F32) * qdb_ref[...]
                y = _head_norm(o, avg_ref[...], gn_ref[...])
                o_ref[b, rows] = _silu(g_ref[b, rows].astype(F32)) * y
                sfw_ref[b] = kv_update(sfw, k, kdf_ref[...], vb)


def _ret_tables():
    lg = np.log(1.0 - 2.0 ** (-5.0 - np.arange(N_HEADS, dtype=np.float64)))
    pos = np.arange(CHUNK, dtype=np.float64)
    lag = np.abs(pos[:, None] - pos[None, :])
    dsym = np.concatenate([np.exp(lg[h] * lag) for h in range(N_HEADS)], axis=1)
    lane_lg = np.repeat(lg, HEAD_DIM)[None, :]
    qdf = np.exp(lane_lg * (pos[:, None] + 1.0))
    qdb = np.exp(lane_lg * (CHUNK - pos[:, None]))
    kdf = np.exp(lane_lg * (CHUNK - 1.0 - pos[:, None]))
    kdb = np.exp(lane_lg * pos[:, None])
    cdec = np.exp(lane_lg * CHUNK)
    return [jnp.asarray(t, F32) for t in (dsym, qdf, qdb, kdf, kdb, cdec)]


def _block_diag_mask():
    hid = np.arange(D_GROUP) // HEAD_DIM
    return (hid[:, None] == hid[None, :]).astype(np.float32)


def _rope_tables(l):
    half = HEAD_DIM // 2
    inv = ROPE_BASE ** (-np.arange(half, dtype=np.float64) / half)
    ang = np.arange(l, dtype=np.float64)[:, None] * inv[None, :]
    cos, sin = np.cos(ang), np.sin(ang)
    cos_full = np.tile(np.concatenate([cos, cos], -1), (1, N_HEADS))
    sin_full = np.tile(np.concatenate([sin, sin], -1), (1, N_HEADS))
    return jnp.asarray(cos_full, F32), jnp.asarray(sin_full, F32)


def _retention(proj, gn_w, cos_full, sin_full):
    bsz, l, _ = proj.shape
    nc = l // CHUNK
    cb = 4 if nc % 4 == 0 else 1
    nblk = nc // cb
    tl = cb * CHUNK
    dsym, qdf, qdb, kdf, kdb, cdec = _ret_tables()
    bd = jnp.asarray(_block_diag_mask())
    avg = jnp.asarray(_block_diag_mask() / HEAD_DIM, BF16)

    def both(col):
        return pl.BlockSpec((bsz, tl, D_GROUP), lambda p, i: (0, i + (1 - p) * (nblk - 1 - 2 * i), col))

    def fwd_only(col):
        return pl.BlockSpec((bsz, tl, D_GROUP), lambda p, i: (0, p * i, col))

    tab = pl.BlockSpec((tl, D_GROUP), lambda p, i: (i + (1 - p) * (nblk - 1 - 2 * i), 0))

    def const(shape):
        return pl.BlockSpec(shape, lambda p, i: (0,) * len(shape))

    return pl.pallas_call(
        functools.partial(_ret_kernel, cb=cb, nblk=nblk),
        out_shape=jax.ShapeDtypeStruct((bsz, l, D_GROUP), F32), grid=(2, nblk),
        in_specs=[fwd_only(CB_RQ), fwd_only(CB_RQR), both(CB_RK), both(CB_RKR), both(CB_RV), fwd_only(CB_RG),
                  tab, tab, const((CHUNK, 4 * CHUNK)), const((CHUNK, D_GROUP)), const((CHUNK, D_GROUP)),
                  const((CHUNK, D_GROUP)), const((CHUNK, D_GROUP)), const((1, D_GROUP)),
                  const((D_GROUP, D_GROUP)), const((D_GROUP, D_GROUP)), const((1, D_GROUP))],
        out_specs=pl.BlockSpec((bsz, tl, D_GROUP), lambda p, i: (0, p * i, 0)),
        scratch_shapes=[pltpu.VMEM((bsz, D_GROUP, D_GROUP), F32), pltpu.VMEM((bsz, D_GROUP, D_GROUP), F32),
                        pltpu.VMEM((bsz, nc, HEAD_DIM, D_GROUP), F32)],
        compiler_params=_cparams(("arbitrary", "arbitrary"), 48), name="retention",
    )(proj, proj, proj, proj, proj, proj, cos_full, sin_full, dsym, qdf, qdb, kdf, kdb, cdec, bd, avg,
      gn_w.reshape(1, D_GROUP))


def _mlstm_kernel(q_ref, k_ref, v_ref, og_ref, gc_ref, gr_ref, bc_ref, br_ref, ex_ref, lt_ref, ut_ref,
                  ones_ref, obd_ref, bd_ref, avg_ref, gn_ref,
                  o_ref, cfw_ref, cbw_ref, nmfw_ref, nmbw_ref, csave_ref, nmsave_ref, *, cb, nblk):
    p = pl.program_id(0)
    i = pl.program_id(1)
    bsz = q_ref.shape[0]
    masks = _head_masks(BF16)
    bd = bd_ref[...]
    lt = lt_ref[...]
    ut = ut_ref[...]
    ri = lax.broadcasted_iota(jnp.int32, (CHUNK, CHUNK), 0)
    ci = lax.broadcasted_iota(jnp.int32, (CHUNK, CHUNK), 1)
    lane = lax.broadcasted_iota(jnp.int32, (1, D_GROUP), 1)

    def gates_expanded(b, rows):
        return _split_dot(gc_ref[b, rows] + bc_ref[...], ex_ref[...])

    def state_update(c_ref, nm_ref, b, total, cum, i_x, k, vb):
        m_prev = nm_ref[b, 1:2]
        g = (total - cum) + i_x
        m_new = jnp.maximum(total + m_prev, jnp.max(g, axis=0, keepdims=True))
        wk = jnp.exp(g - m_new) * k
        decay = jnp.exp(total + m_prev - m_new)
        c_ref[b] = c_ref[b] * decay + _dot_tn(wk.astype(BF16), vb) * bd
        nm_ref[b, 0:1] = decay * nm_ref[b, 0:1] + jnp.sum(wk, axis=0, keepdims=True)
        nm_ref[b, 1:2] = m_new

    @pl.when(p == 0)
    def _():
        @pl.when(i == 0)
        def _():
            cbw_ref[...] = jnp.zeros_like(cbw_ref)
            nmbw_ref[...] = jnp.zeros_like(nmbw_ref)

        blk = nblk - 1 - i
        for c in reversed(range(cb)):
            rows = slice(c * CHUNK, (c + 1) * CHUNK)
            for b in range(bsz):
                csave_ref[b, blk * cb + c] = _compact(cbw_ref[b])
                nmsave_ref[b, blk * cb + c] = nmbw_ref[b]
                gx = gates_expanded(b, rows)
                cum = _split_dot_left(ut, _log_sigmoid(gx[:, 768:1024]))
                k = k_ref[b, rows] * (HEAD_DIM ** -0.5)
                state_update(cbw_ref, nmbw_ref, b, cum[0:1], cum, gx[:, 512:768], k, v_ref[b, rows].astype(BF16))

    def chunk_out(b, rows, cidx):
        q = q_ref[b, rows]
        k = k_ref[b, rows] * (HEAD_DIM ** -0.5)
        qb, kb, vb = q.astype(BF16), k.astype(BF16), v_ref[b, rows].astype(BF16)
        s_all = _dot_nt(qb, _stack_heads(kb, masks))
        vaug = jnp.concatenate([_stack_heads(vb, masks), ones_ref[...]], axis=1)
        gx = gates_expanded(b, rows)
        graw = gr_ref[b, :, rows] + br_ref[...]
        gls = _log_sigmoid(graw)
        cum_r_fw = _split_dot(gls, ut)
        cum_r_bw = _split_dot(gls, lt)
        ccomp = csave_ref[b, cidx]
        nmb = nmsave_ref[b, cidx]

        def direction(i_x, f_x, tri, cum_r, i_row0, f_row0, mask, c_state, n_vec, m_prev, total_row):
            cum = _split_dot_left(tri, _log_sigmoid(f_x))
            total = cum[total_row:total_row + 1]
            inter = cum + m_prev
            ps, rmax = [], []
            dms = []
            for h in range(N_HEADS):
                a_col = cum[:, h * HEAD_DIM:h * HEAD_DIM + 1]
                dm = a_col - cum_r[f_row0 + h:f_row0 + h + 1] + graw[i_row0 + h:i_row0 + h + 1]
                dm = jnp.where(mask, dm, -jnp.inf)
                dms.append(dm)
                rmax.append(jnp.max(dm, axis=-1, keepdims=True))
            rmax256 = jnp.where(lane < 64, rmax[0], jnp.where(lane < 128, rmax[1],
                                jnp.where(lane < 192, rmax[2], rmax[3])))
            m_row = jnp.maximum(inter, rmax256)
            for h in range(N_HEADS):
                m_h = m_row[:, h * HEAD_DIM:h * HEAD_DIM + 1]
                ps.append(s_all[:, h * CHUNK:(h + 1) * CHUNK] * jnp.exp(dms[h] - m_h))
            pmat = jnp.concatenate(ps, axis=1).astype(BF16)
            nd = jnp.dot(pmat, vaug, preferred_element_type=F32)
            w_inter = jnp.exp(inter - m_row)
            qc = jnp.dot(qb, c_state.astype(BF16), preferred_element_type=F32)
            qn = _split_dot(q * n_vec, obd_ref[...])
            num = nd[:, :D_GROUP] + w_inter * qc
            den = nd[:, D_GROUP:] + w_inter * qn
            hdir = num / jnp.maximum(jnp.abs(den), jnp.exp(-m_row))
            return hdir, total, cum

        h_fw, tot_fw, cum_fw = direction(gx[:, 0:256], gx[:, 256:512], lt, cum_r_fw, 0, 4, ri >= ci,
                                         cfw_ref[b], nmfw_ref[b, 0:1], nmfw_ref[b, 1:2], CHUNK - 1)
        h_bw, _, _ = direction(gx[:, 512:768], gx[:, 768:1024], ut, cum_r_bw, 8, 12, ci >= ri,
                               _expand(ccomp, bd), nmb[0:1], nmb[1:2], 0)
        y = _head_norm(h_fw + h_bw, avg_ref[...], gn_ref[...])
        o_ref[b, rows] = _sigmoid(og_ref[b, rows].astype(F32)) * y
        state_update(cfw_ref, nmfw_ref, b, tot_fw, cum_fw, gx[:, 0:256], k, vb)

    @pl.when(p == 1)
    def _():
        @pl.when(i == 0)
        def _():
            cfw_ref[...] = jnp.zeros_like(cfw_ref)
            nmfw_ref[...] = jnp.zeros_like(nmfw_ref)

        for c in range(cb):
            rows = slice(c * CHUNK, (c + 1) * CHUNK)
            for b in range(bsz):
                chunk_out(b, rows, i * cb + c)


def _mlstm(proj, qk, gates, gate_b, gn_w):
    gates_row = jnp.transpose(gates[:, :, :16], (0, 2, 1))
    bsz, l, _ = proj.shape
    nc = l // CHUNK
    cb = 2 if nc % 2 == 0 else 1
    nblk = nc // cb
    tl = cb * CHUNK
    bd_np = _block_diag_mask()
    bd = jnp.asarray(bd_np)
    avg = jnp.asarray(bd_np / HEAD_DIM, BF16)
    obd = jnp.asarray(bd_np, BF16)
    ex = np.zeros((LANES, 4 * D_GROUP), np.float32)
    for j in range(16):
        typ, h = divmod(j, N_HEADS)
        ex[j, typ * D_GROUP + h * HEAD_DIM: typ * D_GROUP + (h + 1) * HEAD_DIM] = 1.0
    idx = np.arange(CHUNK)
    lt = (idx[None, :] <= idx[:, None]).astype(np.float32)
    ones_st = np.repeat(np.repeat(np.eye(N_HEADS, dtype=np.float32), CHUNK, 0), HEAD_DIM, 1)
    gb = gate_b.astype(F32).reshape(16)
    bias_col = jnp.pad(gb, (0, LANES - 16)).reshape(1, LANES)
    bias_row = jnp.broadcast_to(gb.reshape(16, 1), (16, CHUNK))

    def both(arr_col, width=D_GROUP):
        return pl.BlockSpec((bsz, tl, width), lambda p, i: (0, i + (1 - p) * (nblk - 1 - 2 * i), arr_col))

    def fwd_only(arr_col):
        return pl.BlockSpec((bsz, tl, D_GROUP), lambda p, i: (0, p * i, arr_col))

    def const(shape):
        return pl.BlockSpec(shape, lambda p, i: (0,) * len(shape))

    return pl.pallas_call(
        functools.partial(_mlstm_kernel, cb=cb, nblk=nblk),
        out_shape=jax.ShapeDtypeStruct((bsz, l, D_GROUP), F32), grid=(2, nblk),
        in_specs=[fwd_only(0), both(1), both(CB_MV), fwd_only(CB_MO), both(0, LANES),
                  pl.BlockSpec((bsz, 16, tl), lambda p, i: (0, 0, p * i)),
                  const((1, LANES)), const((16, CHUNK)), const((LANES, 4 * D_GROUP)),
                  const((CHUNK, CHUNK)), const((CHUNK, CHUNK)), const((4 * CHUNK, D_GROUP)),
                  const((D_GROUP, D_GROUP)), const((D_GROUP, D_GROUP)), const((D_GROUP, D_GROUP)),
                  const((1, D_GROUP))],
        out_specs=pl.BlockSpec((bsz, tl, D_GROUP), lambda p, i: (0, p * i, 0)),
        scratch_shapes=[pltpu.VMEM((bsz, D_GROUP, D_GROUP), F32), pltpu.VMEM((bsz, D_GROUP, D_GROUP), F32),
                        pltpu.VMEM((bsz, 8, D_GROUP), F32), pltpu.VMEM((bsz, 8, D_GROUP), F32),
                        pltpu.VMEM((bsz, nc, HEAD_DIM, D_GROUP), F32), pltpu.VMEM((bsz, nc, 8, D_GROUP), F32)],
        compiler_params=_cparams(("arbitrary", "arbitrary"), 48), name="mlstm",
    )(qk, qk, proj, proj, gates, gates_row, bias_col, bias_row, jnp.asarray(ex, BF16), jnp.asarray(lt, BF16),
      jnp.asarray(lt.T, BF16), jnp.asarray(ones_st, BF16), obd, bd, avg, gn_w.reshape(1, D_GROUP))


def _s5_kernel(u_ref, mt_ref, bg_ref, cg_ref, pa_ref, pb_ref, o_ref, *, nsteps):
    ub = u_ref[0].astype(BF16)
    e = jnp.dot(ub, bg_ref[0], preferred_element_type=F32)
    r = e.shape[0]
    row = lax.broadcasted_iota(jnp.int32, (r, LANES), 0)
    xf, xb = e[:, :LANES], e[:, LANES:]
    pa, pb = pa_ref[0], pb_ref[0]
    for s in range(nsteps):
        sh = 1 << s
        a_f, b_f = pa[s:s + 1, :LANES], pb[s:s + 1, :LANES]
        a_b, b_b = pa[s:s + 1, LANES:], pb[s:s + 1, LANES:]
        yf = jnp.where(row >= sh, pltpu.roll(xf, sh, 0), 0.0)
        yb = jnp.where(row < r - sh, pltpu.roll(xb, r - sh, 0), 0.0)
        xf = xf + a_f * yf + b_f * pltpu.roll(yf, LANES // 2, 1)
        xb = xb + a_b * yb + b_b * pltpu.roll(yb, LANES // 2, 1)
    sprev = jnp.where(row >= 1, pltpu.roll(xf, 1, 0), 0.0)
    snext = jnp.where(row < r - 1, pltpu.roll(xb, r - 1, 0), 0.0)
    st = jnp.concatenate([sprev, snext], axis=1).astype(BF16)
    o_ref[0] = (jnp.dot(ub, mt_ref[0], preferred_element_type=F32)
                + jnp.dot(st, cg_ref[0], preferred_element_type=F32))


def _s5_tables(a_re, a_im, log_dt, b_re, b_im, c_re, c_im, d_skip, tc, nsteps):
    g, p, ch = S5_GROUPS, S5_STATE, S5_CH
    hp = lax.Precision.HIGHEST
    are, aim = a_re.astype(F32), a_im.astype(F32)
    delta = jnp.exp(log_dt.astype(F32))[..., None]
    lre, lim = are * delta, aim * delta

    class Cx:
        def __init__(self, re, im):
            self.re, self.im = re, im

        def __mul__(self, o):
            return Cx(self.re * o.re - self.im * o.im, self.re * o.im + self.im * o.re)

        def __getitem__(self, idx):
            return Cx(self.re[idx], self.im[idx])

    def apow(n):
        n = jnp.asarray(n, F32)[None, None, :, None]
        mag, ang = jnp.exp(lre[:, :, None, :] * n), lim[:, :, None, :] * n
        return Cx(mag * jnp.cos(ang), mag * jnp.sin(ang))

    abr, abi = jnp.exp(lre) * jnp.cos(lim), jnp.exp(lre) * jnp.sin(lim)
    den = are * are + aim * aim
    quo = Cx(((abr - 1.0) * are + abi * aim) / den, (abi * are - (abr - 1.0) * aim) / den)
    b_bar = quo[..., None] * Cx(b_re.astype(F32)[None], b_im.astype(F32)[None])
    c = Cx(c_re.astype(F32), c_im.astype(F32))
    taus = np.arange(tc)
    cp = c[:, :, None] * apow(taus)[:, :, :, None, :]
    kk = (jnp.einsum("dgtop,dgpi->dgtoi", cp.re, b_bar.re, precision=hp)
          - jnp.einsum("dgtop,dgpi->dgtoi", cp.im, b_bar.im, precision=hp))
    dsk = d_skip.astype(F32).reshape(g, ch)[:, :, None] * jnp.eye(ch, dtype=F32)[None]
    kdiag = kk[0][:, 0] + kk[1][:, 0] + dsk
    lags = jnp.concatenate([kk[1][:, :0:-1], kdiag[:, None], kk[0][:, 1:]], axis=1)
    diff = taus[None, :] - taus[:, None]
    sel = (diff[None] + (tc - 1) == np.arange(2 * tc - 1)[:, None, None]).astype(np.float32)
    mt = jnp.einsum("jst,gjoi->gsito", jnp.asarray(sel), lags, precision=hp).reshape(g, tc * ch, tc * ch)

    zf = apow(tc - 1 - taus)[0][..., None] * b_bar[0][:, None]
    zb = apow(taus)[1][..., None] * b_bar[1][:, None]

    def to_rows(z):
        return jnp.transpose(z, (0, 1, 3, 2)).reshape(g, tc * ch, p)

    bg = jnp.concatenate([to_rows(zf.re), to_rows(zf.im), to_rows(zb.re), to_rows(zb.im)], axis=-1)

    yf = c[0][:, None] * apow(taus + 1)[0][:, :, None, :]
    yb = c[1][:, None] * apow(tc - taus)[1][:, :, None, :]

    def to_cols(z):
        return jnp.transpose(z, (0, 3, 1, 2)).reshape(g, p, tc * ch)

    cg = jnp.concatenate([to_cols(yf.re), -to_cols(yf.im), to_cols(yb.re), -to_cols(yb.im)], axis=1)

    steps = tc * (2.0 ** np.arange(nsteps))
    pw = apow(steps)
    re0, im0, re1, im1 = pw.re[0], pw.im[0], pw.re[1], pw.im[1]
    pa = jnp.concatenate([re0, re0, re1, re1], axis=-1)
    pb = jnp.concatenate([-im0, im0, -im1, im1], axis=-1)
    pad = (-nsteps) % 8
    pa = jnp.pad(pa, ((0, 0), (0, pad), (0, 0)))
    pb = jnp.pad(pb, ((0, 0), (0, pad), (0, 0)))
    return mt.astype(BF16), bg.astype(BF16), cg.astype(BF16), pa, pb


def _s5_glu_kernel(y_ref, w_ref, o_ref):
    y = y_ref[...]
    z = 0.5 * y * (1.0 + jnp.tanh(math.sqrt(2.0 / math.pi) * (y + 0.044715 * (y * y * y))))
    o_ref[...] = z * _sigmoid(jnp.dot(z.astype(BF16), w_ref[...], preferred_element_type=F32))


def _s5(proj, a_re, a_im, log_dt, b_re, b_im, c_re, c_im, d_skip, w_glu):
    bsz, l, _ = proj.shape
    tc = S5_TC
    r = l // tc
    nsteps = max(1, int(math.ceil(math.log2(r))))
    w = tc * S5_CH
    mt, bg, cg, pa, pb = _s5_tables(a_re, a_im, log_dt, b_re, b_im, c_re, c_im, d_skip, tc, nsteps)
    u = proj[:, :, CB_S5 * D_GROUP:(CB_S5 + 1) * D_GROUP].astype(F32)
    ug = jnp.transpose(u.reshape(bsz, r, tc, S5_GROUPS, S5_CH), (3, 0, 1, 2, 4)).reshape(S5_GROUPS, bsz * r, w)
    ns8 = pa.shape[1]
    yg = pl.pallas_call(
        functools.partial(_s5_kernel, nsteps=nsteps),
        out_shape=jax.ShapeDtypeStruct((S5_GROUPS, bsz * r, w), F32), grid=(S5_GROUPS, bsz),
        in_specs=[pl.BlockSpec((1, r, w), lambda g, b: (g, b, 0)),
                  pl.BlockSpec((1, w, w), lambda g, b: (g, 0, 0)),
                  pl.BlockSpec((1, w, D_GROUP), lambda g, b: (g, 0, 0)),
                  pl.BlockSpec((1, D_GROUP, w), lambda g, b: (g, 0, 0)),
                  pl.BlockSpec((1, ns8, D_GROUP), lambda g, b: (g, 0, 0)),
                  pl.BlockSpec((1, ns8, D_GROUP), lambda g, b: (g, 0, 0))],
        out_specs=pl.BlockSpec((1, r, w), lambda g, b: (g, b, 0)),
        compiler_params=_cparams(("parallel", "parallel"), 48), name="s5_ssm")(ug, mt, bg, cg, pa, pb)
    y = jnp.transpose(yg.reshape(S5_GROUPS, bsz, r, tc, S5_CH), (1, 2, 3, 0, 4)).reshape(bsz * l, D_GROUP)
    t = bsz * l
    tm = _tile(t, 2048)
    out = pl.pallas_call(
        _s5_glu_kernel, out_shape=jax.ShapeDtypeStruct((t, D_GROUP), F32), grid=(t // tm,),
        in_specs=[pl.BlockSpec((tm, D_GROUP), lambda i: (i, 0)), pl.BlockSpec((D_GROUP, D_GROUP), lambda i: (0, 0))],
        out_specs=pl.BlockSpec((tm, D_GROUP), lambda i: (i, 0)),
        compiler_params=_cparams(("parallel",)), name="s5_glu")(y, w_glu.astype(BF16))
    return out.reshape(bsz, l, D_GROUP)


def _hy_filter_kernel(z_ref, w1_ref, b1_ref, w2_ref, b2_ref, w3_ref, fr_ref, dec_ref, h_ref, ss_ref):
    i = pl.program_id(0)
    hp = lax.Precision.HIGHEST
    fr = fr_ref[...]
    a = jnp.sin(fr * (jnp.dot(z_ref[...], w1_ref[...], precision=hp, preferred_element_type=F32) + b1_ref[...]))
    a = jnp.sin(fr * (jnp.dot(a, w2_ref[...], precision=hp, preferred_element_type=F32) + b2_ref[...]))
    h = jnp.dot(a, w3_ref[...], precision=hp, preferred_element_type=F32)
    df, db = dec_ref[:, :D_GROUP], dec_ref[:, D_GROUP:]
    h = h * jnp.concatenate([df, df, db, db], axis=1)

    @pl.when(i == 0)
    def _():
        ss_ref[...] = jnp.zeros_like(ss_ref)

    ss_ref[...] += jnp.sum(h * h, axis=0, keepdims=True)
    nhalf = h.shape[1] // 2
    row = lax.broadcasted_iota(jnp.int32, h.shape, 0)
    col = lax.broadcasted_iota(jnp.int32, h.shape, 1)
    h_ref[...] = jnp.where((row == 0) & (col >= nhalf) & (i == 0), 0.0, h)


def _hy_filters(l, w1, b1, w2, b2, w3, freq):
    t = np.linspace(0.0, 1.0, l)[:, None]
    w = 2.0 * np.pi * np.arange(l, dtype=np.float64)[:, None] / l
    bands = np.linspace(1e-4, HY_BANDS - 1, HY_BANDS)[None, :]
    z = np.concatenate([t, np.cos(bands * w), -np.sin(bands * w)], axis=-1)
    max_decay = math.log(HY_TARGET) / HY_FAST_DECAY
    min_decay = math.log(HY_TARGET) / HY_SLOW_DECAY
    rates = np.abs(np.linspace(min_decay, max_decay, D_GROUP))
    dec = np.exp(-t * rates)
    rev = np.concatenate([[0], np.arange(l - 1, 0, -1)])
    half = LANES // 2
    zz = np.zeros((l, LANES))
    zz[:, :HY_EMB] = z
    zz[:, half:half + HY_EMB] = z[rev]
    zz = jnp.asarray(zz, F32)
    dec2 = jnp.asarray(np.concatenate([dec, dec[rev]], axis=1), F32)

    def two(m):
        m = m.astype(F32)
        top = jnp.pad(m, ((0, half - m.shape[0]), (0, half - m.shape[1])))
        zero = jnp.zeros_like(top)
        return jnp.concatenate([jnp.concatenate([top, zero], 1), jnp.concatenate([zero, top], 1)], 0)

    def twice(v):
        v = jnp.pad(v.astype(F32), (0, half - v.shape[0]))
        return jnp.concatenate([v, v]).reshape(1, LANES)

    w3r = w3.astype(F32).reshape(HY_FFN, HY_ORDER, 2, D_GROUP)
    nhalf = HY_ORDER * D_GROUP
    w3f = jnp.pad(w3r[:, :, 0].reshape(HY_FFN, nhalf), ((0, half - HY_FFN), (0, 0)))
    w3b = jnp.pad(w3r[:, :, 1].reshape(HY_FFN, nhalf), ((0, half - HY_FFN), (0, 0)))
    zero = jnp.zeros_like(w3f)
    w3p = jnp.concatenate([jnp.concatenate([w3f, zero], 1), jnp.concatenate([zero, w3b], 1)], 0)
    nout = 2 * nhalf
    tl = _tile(l, 512)

    def const(shape):
        return pl.BlockSpec(shape, lambda i: (0, 0))

    return pl.pallas_call(
        _hy_filter_kernel,
        out_shape=(jax.ShapeDtypeStruct((l, nout), F32), jax.ShapeDtypeStruct((1, nout), F32)), grid=(l // tl,),
        in_specs=[pl.BlockSpec((tl, LANES), lambda i: (i, 0)), const((LANES, LANES)), const((1, LANES)),
                  const((LANES, LANES)), const((1, LANES)), const((LANES, nout)), const((1, LANES)),
                  pl.BlockSpec((tl, 2 * D_GROUP), lambda i: (i, 0))],
        out_specs=(pl.BlockSpec((tl, nout), lambda i: (i, 0)), const((1, nout))),
        compiler_params=_cparams(("arbitrary",)), name="hyena_filter_mlp",
    )(zz, two(w1), twice(b1), two(w2), twice(b2), w3p, twice(freq), dec2)


def _dft_consts(na):
    nb = FFT_NB
    n = na * nb
    ia = np.arange(na, dtype=np.float64)
    th = 2.0 * np.pi * np.outer(ia, ia) / na
    c1, s1 = np.cos(th), np.sin(th)
    eye8 = np.eye(8)
    fa_full = np.concatenate([c1, -s1], axis=0)
    g_full = np.kron(fa_full, eye8)
    g_half = np.kron(fa_full[:, : na // 2], eye8)
    g_out = np.kron(np.concatenate([c1[: na // 2], -s1[: na // 2]], axis=1) / n, eye8)
    ib = np.arange(nb, dtype=np.float64)
    ph = 2.0 * np.pi * np.outer(ib, ib) / nb
    c2, s2 = np.cos(ph), np.sin(ph)
    fb = np.block([[c2, s2], [-s2, c2]])
    fbc = np.block([[c2, -s2], [s2, c2]])
    ps = 2.0 * np.pi * np.outer(ia, ib) / n
    twr = np.broadcast_to(np.cos(ps)[:, :, None], (na, nb, LANES))
    twi = np.broadcast_to(-np.sin(ps)[:, :, None], (na, nb, LANES))
    as_f32 = lambda x: jnp.asarray(x, F32)
    return dict(g_full=as_f32(g_full), g_half=as_f32(g_half), g_out=as_f32(g_out), fb=as_f32(fb), fbc=as_f32(fbc),
                twr=as_f32(twr), twi=as_f32(twi))


def _lane_tile(x, reps):
    return x if reps == 1 else jnp.concatenate([x] * reps, axis=-1)


def _hy_spec_kernel(a_ref, twr_ref, twi_ref, fb_ref, ss_ref, o_ref, *, kb, reps):
    scale = lax.rsqrt(ss_ref[...])
    fb = fb_ref[...].astype(BF16)
    for j in range(kb):
        ar, ai = a_ref[0, j], a_ref[1, j]
        twr, twi = _lane_tile(twr_ref[j], reps), _lane_tile(twi_ref[j], reps)
        br = twr * ar - twi * ai
        bi = twr * ai + twi * ar
        x = jnp.dot(fb, jnp.concatenate([br, bi], axis=0).astype(BF16), preferred_element_type=F32)
        o_ref[0, j] = x[:FFT_NB] * scale
        o_ref[1, j] = x[FFT_NB:] * scale


def _hy_mid_kernel(a_ref, h_ref, twr_ref, twi_ref, fb_ref, fbc_ref, o_ref, *, kb, reps):
    fb, fbc = fb_ref[...].astype(BF16), fbc_ref[...].astype(BF16)
    for j in range(kb):
        ar, ai = a_ref[0, 0, j], a_ref[0, 1, j]
        twr, twi = _lane_tile(twr_ref[j], reps), _lane_tile(twi_ref[j], reps)
        br = twr * ar - twi * ai
        bi = twr * ai + twi * ar
        x = jnp.dot(fb, jnp.concatenate([br, bi], axis=0).astype(BF16), preferred_element_type=F32)
        xr, xi = x[:FFT_NB], x[FFT_NB:]
        hr, hi = h_ref[0, j], h_ref[1, j]
        yr = xr * hr - xi * hi
        yi = xr * hi + xi * hr
        z = jnp.dot(fbc, jnp.concatenate([yr, yi], axis=0).astype(BF16), preferred_element_type=F32)
        zr, zi = z[:FFT_NB], z[FFT_NB:]
        o_ref[0, 0, j] = twr * zr + twi * zi
        o_ref[0, 1, j] = twr * zi - twi * zr


def _hy_dft1_kernel(g_ref, x_ref, o_ref, *, qb):
    na_in, c = x_ref.shape[1], x_ref.shape[4]
    na = o_ref.shape[2]
    g = g_ref[...].astype(BF16)
    for q in range(qb):
        x = x_ref[0, :, q].reshape(na_in * 8, c).astype(BF16)
        a = jnp.dot(g, x, preferred_element_type=F32)
        o_ref[0, :, :, q] = a.reshape(2, na, 8, c)


def _hy_fdft1_kernel(g_ref, xt_ref, xb_ref, o_ref, *, qb):
    nah, c = xt_ref.shape[1], xt_ref.shape[4]
    na = o_ref.shape[2]
    kh = nah * 8
    gt, gb = g_ref[:, :kh].astype(BF16), g_ref[:, kh:].astype(BF16)
    for q in range(qb):
        xt = xt_ref[0, :, q].reshape(kh, c).astype(BF16)
        xb = xb_ref[0, :, q].reshape(kh, c).astype(BF16)
        a = jnp.dot(gt, xt, preferred_element_type=F32) + jnp.dot(gb, xb, preferred_element_type=F32)
        o_ref[0, :, :, q] = a.reshape(2, na, 8, c)


def _hy_dft1(g, x5, ncol, name):
    bsz, na_in, nq = x5.shape[:3]
    na = g.shape[0] // 16
    c = D_GROUP
    qb = FFT_QB
    return pl.pallas_call(
        functools.partial(_hy_dft1_kernel, qb=qb),
        out_shape=jax.ShapeDtypeStruct((bsz, 2, na, nq, 8, ncol * c), F32), grid=(bsz, ncol, nq // qb),
        in_specs=[pl.BlockSpec(g.shape, lambda b, j, q: (0, 0)),
                  pl.BlockSpec((1, na_in, qb, 8, c), lambda b, j, q: (b, 0, q, 0, j))],
        out_specs=pl.BlockSpec((1, 2, na, qb, 8, c), lambda b, j, q: (b, 0, 0, q, 0, j)),
        compiler_params=_cparams(("parallel", "parallel", "parallel"), 48), name=name)(g, x5)


def _hy_out_kernel(g_ref, z_ref, x_ref, v_ref, b_ref, o_ref, *, qb):
    na2, c = z_ref.shape[1] * z_ref.shape[2], z_ref.shape[5]
    nah = o_ref.shape[1]
    bias = b_ref[...].reshape(1, 1, c)
    g = g_ref[...].astype(BF16)
    for q in range(qb):
        z = z_ref[0, :, :, q].reshape(na2 * 8, c).astype(BF16)
        y = jnp.dot(g, z, preferred_element_type=F32).reshape(nah, 8, c)
        o_ref[0, :, q] = x_ref[0, :, q] * (y + v_ref[0, :, q] * bias)


def _hyena(proj, conv_w, conv_b, w1, b1, w2, b2, w3, freq, bias):
    bsz, l, _ = proj.shape
    nb = FFT_NB
    na = 2 * l // nb
    nah = na // 2
    nq = nb // 8
    c = D_GROUP
    qb = FFT_QB
    dc = _dft_consts(na)
    pc = _shortconv(proj, CB_HV, 3, conv_w, conv_b, act=False)

    h, ss = _hy_filters(l, w1, b1, w2, b2, w3, freq)
    ncf = HY_ORDER * c
    ssn = ss[:, :ncf] + ss[:, ncf:]
    h5 = h.reshape(1, nah, nq, 8, 2 * ncf)
    ka = pl.pallas_call(
        functools.partial(_hy_fdft1_kernel, qb=qb),
        out_shape=jax.ShapeDtypeStruct((1, 2, na, nq, 8, ncf), F32), grid=(HY_ORDER, nq // qb),
        in_specs=[pl.BlockSpec(dc["g_full"].shape, lambda j, q: (0, 0)),
                  pl.BlockSpec((1, nah, qb, 8, c), lambda j, q: (0, 0, q, 0, j)),
                  pl.BlockSpec((1, nah, qb, 8, c), lambda j, q: (0, 0, q, 0, HY_ORDER + j))],
        out_specs=pl.BlockSpec((1, 2, na, qb, 8, c), lambda j, q: (0, 0, 0, q, 0, j)),
        compiler_params=_cparams(("parallel", "parallel"), 48), name="hyena_filter_dft1",
    )(dc["g_full"], h5, h5).reshape(2, na, nb, ncf)
    kb = 4 if na % 4 == 0 else 1
    reps = c // LANES
    tw = pl.BlockSpec((kb, nb, LANES), lambda j, k: (k, 0, 0))
    mat = pl.BlockSpec((2 * nb, 2 * nb), lambda j, k: (0, 0))
    hspec = pl.pallas_call(
        functools.partial(_hy_spec_kernel, kb=kb, reps=reps),
        out_shape=jax.ShapeDtypeStruct((2, na, nb, ncf), F32), grid=(ncf // c, na // kb),
        in_specs=[pl.BlockSpec((2, kb, nb, c), lambda j, k: (0, k, 0, j)), tw, tw, mat,
                  pl.BlockSpec((1, c), lambda j, k: (0, j))],
        out_specs=pl.BlockSpec((2, kb, nb, c), lambda j, k: (0, k, 0, j)),
        compiler_params=_cparams(("parallel", "parallel"), 48), name="hyena_filter_dft2",
    )(ka, dc["twr"], dc["twi"], dc["fb"], ssn)

    pc5 = pc.reshape(bsz, nah, nq, 8, 3 * c)

    def long_conv_gate(z5, order, xcol):
        a = _hy_dft1(dc["g_half"], z5, 1, "hyena_dft1").reshape(bsz, 2, na, nb, c)
        zmid = pl.pallas_call(
            functools.partial(_hy_mid_kernel, kb=kb, reps=reps),
            out_shape=jax.ShapeDtypeStruct((bsz, 2, na, nb, c), F32), grid=(bsz, na // kb),
            in_specs=[pl.BlockSpec((1, 2, kb, nb, c), lambda b, k: (b, 0, k, 0, 0)),
                      pl.BlockSpec((2, kb, nb, c), lambda b, k: (0, k, 0, order)), tw, tw, mat, mat],
            out_specs=pl.BlockSpec((1, 2, kb, nb, c), lambda b, k: (b, 0, k, 0, 0)),
            compiler_params=_cparams(("parallel", "parallel"), 48), name="hyena_dft_mid",
        )(a, hspec, dc["twr"], dc["twi"], dc["fb"], dc["fbc"])
        zmid = zmid.reshape(bsz, 2, na, nq, 8, c)
        sig = lambda col: pl.BlockSpec((1, nah, qb, 8, c), lambda b, q: (b, 0, q, 0, col))
        return pl.pallas_call(
            functools.partial(_hy_out_kernel, qb=qb),
            out_shape=jax.ShapeDtypeStruct((bsz, nah, nq, 8, c), F32), grid=(bsz, nq // qb),
            in_specs=[pl.BlockSpec(dc["g_out"].shape, lambda b, q: (0, 0)),
                      pl.BlockSpec((1, 2, na, qb, 8, c), lambda b, q: (b, 0, 0, q, 0, 0)),
                      sig(xcol), sig(0), pl.BlockSpec((1, c), lambda b, q: (0, 0))],
            out_specs=sig(0),
            compiler_params=_cparams(("parallel", "parallel"), 48), name="hyena_idft_gate",
        )(dc["g_out"], zmid, pc5, z5, bias[order].astype(F32).reshape(1, c))

    z1 = long_conv_gate(pc5, 0, 1)
    z2 = long_conv_gate(z1, 1, 2)
    return z2.reshape(bsz, l, c)


def _ffn_kernel(x_ref, w1_ref, w3_ref, w2_ref, lw_ref, lb_ref, o_ref, xb_ref, acc_ref, *, nf):
    f = pl.program_id(1)

    @pl.when(f == 0)
    def _():
        xb_ref[...] = x_ref[...].astype(BF16)
        acc_ref[...] = jnp.zeros_like(acc_ref)

    xb = xb_ref[...]
    a = jnp.dot(xb, w1_ref[...], preferred_element_type=F32)
    b = jnp.dot(xb, w3_ref[...], preferred_element_type=F32)
    acc_ref[...] += jnp.dot((_silu(a) * b).astype(BF16), w2_ref[...], preferred_element_type=F32)

    @pl.when(f == nf - 1)
    def _():
        o_ref[...] = _ln_core(DN_ALPHA * x_ref[...] + acc_ref[...], lw_ref[...], lb_ref[...])


def _ffn_ln(x, w1, w3, w2, lw, lb):
    t, d = x.shape
    ff = w1.shape[1]
    tm = _tile(t, 1024)
    tf = 512 if ff % 512 == 0 else (256 if ff % 256 == 0 else ff)
    nf = ff // tf
    vec = pl.BlockSpec((1, d), lambda i, f: (0, 0))
    return pl.pallas_call(
        functools.partial(_ffn_kernel, nf=nf),
        out_shape=jax.ShapeDtypeStruct((t, d), F32), grid=(t // tm, nf),
        in_specs=[pl.BlockSpec((tm, d), lambda i, f: (i, 0)),
                  pl.BlockSpec((d, tf), lambda i, f: (0, f)),
                  pl.BlockSpec((d, tf), lambda i, f: (0, f)),
                  pl.BlockSpec((tf, d), lambda i, f: (f, 0)), vec, vec],
        out_specs=pl.BlockSpec((tm, d), lambda i, f: (i, 0)),
        scratch_shapes=[pltpu.VMEM((tm, d), BF16), pltpu.VMEM((tm, d), F32)],
        compiler_params=_cparams(("parallel", "arbitrary"), 52), name="swiglu_ffn_ln",
    )(x, w1, w3, w2, lw.reshape(1, d), lb.reshape(1, d))


MOE_SB = 832
MOE_NSB = 2
MOE_SUB = 256
MOE_CUM = 64
MOE_MAXP = -(-MOE_SB // MOE_SUB)


def _moe_kernel(cnt_ref, x_ref, cmb_ref, lt_ref, w1_ref, w3_ref, w2_ref, lw_ref, lb_ref, o_ref,
                xb_ref, xs_ref, ys_ref, gs_ref, pos_ref, *, nf, nsb, t_total):
    i = pl.program_id(0)
    e = pl.program_id(1)
    f = pl.program_id(2)
    sb = x_ref.shape[0] // nsb
    npass = [(cnt_ref[(i * nsb + s) * N_EXPERTS + e] + (MOE_SUB - 1)) // MOE_SUB for s in range(nsb)]
    sub = [slice(s * sb, (s + 1) * sb) for s in range(nsb)]

    @pl.when((e == 0) & (f == 0))
    def _():
        o_ref[...] = jnp.zeros_like(o_ref)
        for s in range(nsb):
            valid = lax.broadcasted_iota(jnp.int32, (sb, 1), 0) < t_total - (i * nsb + s) * sb
            xb_ref[sub[s]] = jnp.where(valid, x_ref[sub[s]], 0.0).astype(BF16)
            carry = jnp.zeros((1, LANES), F32)
            for c in range(sb // MOE_CUM):
                rows = slice(s * sb + c * MOE_CUM, s * sb + (c + 1) * MOE_CUM)
                vrows = slice(c * MOE_CUM, (c + 1) * MOE_CUM)
                m = jnp.where(valid[vrows] & (cmb_ref[rows] > 0.0), 1.0, 0.0)
                inc = jnp.dot(lt_ref[...], m, preferred_element_type=F32) + carry
                pos_ref[rows] = jnp.where(m > 0.0, inc - 1.0, -1.0)
                carry = inc[MOE_CUM - 1:MOE_CUM]

    lane = lax.broadcasted_iota(jnp.int32, (sb, LANES), 1)

    def one_hot(s, j):
        pos = jnp.sum(jnp.where(lane == e, pos_ref[sub[s]], 0.0), axis=1, keepdims=True)
        slot = lax.broadcasted_iota(jnp.int32, (sb, MOE_SUB), 1).astype(F32) + (j * MOE_SUB).astype(F32)
        return pos == slot

    @pl.when(f == 0)
    def _():
        for s in range(nsb):
            gate = jnp.sum(jnp.where(lane == e, cmb_ref[sub[s]], 0.0), axis=1, keepdims=True)

            def gather(j, carry, s=s, gate=gate):
                hit = one_hot(s, j)
                k = s * MOE_MAXP + j
                xs_ref[k] = _dot_tn(xb_ref[sub[s]], jnp.where(hit, 1.0, 0.0).astype(BF16)).astype(BF16)
                g = jnp.sum(jnp.where(hit, gate, 0.0), axis=0, keepdims=True)
                gs_ref[k] = jnp.broadcast_to(g, (8, MOE_SUB))
                ys_ref[k] = jnp.zeros(ys_ref.shape[1:], F32)
                return carry

            lax.fori_loop(0, npass[s], gather, 0)

    for s in range(nsb):
        def expert(j, carry, s=s):
            k = s * MOE_MAXP + j
            xs = xs_ref[k]
            a = jnp.dot(w1_ref[0], xs, preferred_element_type=F32)
            b = jnp.dot(w3_ref[0], xs, preferred_element_type=F32)
            hid = (_silu(a) * b * gs_ref[k][0:1]).astype(BF16)
            ys_ref[k] += jnp.dot(w2_ref[0], hid, preferred_element_type=F32)
            return carry

        lax.fori_loop(0, npass[s], expert, 0)

    @pl.when(f == nf - 1)
    def _():
        for s in range(nsb):
            def scatter(j, carry, s=s):
                hit = one_hot(s, j)
                o_ref[sub[s]] += _dot_nt(jnp.where(hit, 1.0, 0.0).astype(BF16),
                                         ys_ref[s * MOE_MAXP + j].astype(BF16))
                return carry

            lax.fori_loop(0, npass[s], scatter, 0)

        @pl.when(e == pl.num_programs(1) - 1)
        def _():
            o_ref[...] = _ln_core(DN_ALPHA * x_ref[...] + o_ref[...], lw_ref[...], lb_ref[...])


def _moe_ln(x, cmb, w1, w3, w2, lw, lb):
    t, d = x.shape
    ne, ff, _ = w1.shape
    nsb = MOE_NSB
    tb = nsb * MOE_SB
    nb = -(-t // tb)
    tf = 896 if ff % 896 == 0 else ff
    nf = ff // tf
    max_pass = nsb * MOE_MAXP
    cmb_p = jnp.pad(cmb, ((0, nb * tb - t), (0, 0)))
    counts = jnp.sum((cmb_p[:, :N_EXPERTS] > 0.0).reshape(nb * nsb, MOE_SB, N_EXPERTS), axis=1)
    counts = counts.astype(jnp.int32).reshape(-1)
    idx = np.arange(MOE_CUM)
    lt = jnp.asarray(idx[None, :] <= idx[:, None], F32)
    once = pl.Buffered(1)
    grid_spec = pltpu.PrefetchScalarGridSpec(
        num_scalar_prefetch=1, grid=(nb, ne, nf),
        in_specs=[pl.BlockSpec((tb, d), lambda i, e, f, c: (i, 0), pipeline_mode=once),
                  pl.BlockSpec((tb, LANES), lambda i, e, f, c: (i, 0), pipeline_mode=once),
                  pl.BlockSpec((MOE_CUM, MOE_CUM), lambda i, e, f, c: (0, 0)),
                  pl.BlockSpec((1, tf, d), lambda i, e, f, c: (e, f, 0)),
                  pl.BlockSpec((1, tf, d), lambda i, e, f, c: (e, f, 0)),
                  pl.BlockSpec((1, d, tf), lambda i, e, f, c: (e, 0, f)),
                  pl.BlockSpec((1, d), lambda i, e, f, c: (0, 0)),
                  pl.BlockSpec((1, d), lambda i, e, f, c: (0, 0))],
        out_specs=pl.BlockSpec((tb, d), lambda i, e, f, c: (i, 0), pipeline_mode=once),
        scratch_shapes=[pltpu.VMEM((tb, d), BF16), pltpu.VMEM((max_pass, d, MOE_SUB), BF16),
                        pltpu.VMEM((max_pass, d, MOE_SUB), F32), pltpu.VMEM((max_pass, 8, MOE_SUB), F32),
                        pltpu.VMEM((tb, LANES), F32)])
    return pl.pallas_call(
        functools.partial(_moe_kernel, nf=nf, nsb=nsb, t_total=t), out_shape=jax.ShapeDtypeStruct((t, d), F32),
        grid_spec=grid_spec,
        compiler_params=_cparams(("parallel", "arbitrary", "arbitrary"), 58), name="moe_routed",
    )(counts, x, cmb_p, lt, w1, w3, w2, lw.reshape(1, d), lb.reshape(1, d))


def _router_kernel(x_ref, rh_ref, rl_ref, o_ref):
    x = x_ref[...]
    xh = x.astype(BF16)
    xl = (x - xh.astype(F32)).astype(BF16)
    logits = (jnp.dot(xh, rh_ref[...], preferred_element_type=F32)
              + jnp.dot(xl, rh_ref[...], preferred_element_type=F32)
              + jnp.dot(xh, rl_ref[...], preferred_element_type=F32))
    lane = lax.broadcasted_iota(jnp.int32, logits.shape, 1).astype(F32)
    logits = jnp.where(lane < N_EXPERTS, logits, -jnp.inf)
    m1 = jnp.max(logits, axis=1, keepdims=True)
    i1 = jnp.min(jnp.where(logits == m1, lane, float(LANES)), axis=1, keepdims=True)
    rest = jnp.where(lane == i1, -jnp.inf, logits)
    m2 = jnp.max(rest, axis=1, keepdims=True)
    i2 = jnp.min(jnp.where(rest == m2, lane, float(LANES)), axis=1, keepdims=True)
    e2 = jnp.exp(m2 - m1)
    g1 = 1.0 / (1.0 + e2)
    g2 = e2 / (1.0 + e2)
    o_ref[...] = jnp.where(lane == i1, g1, 0.0) + jnp.where(lane == i2, g2, 0.0)


def _router(x, router):
    t, d = x.shape
    r = jnp.pad(router.astype(F32), ((0, 0), (0, LANES - N_EXPERTS)))
    rh = r.astype(BF16)
    rl = (r - rh.astype(F32)).astype(BF16)
    tm = _tile(t, 1024)
    return pl.pallas_call(
        _router_kernel, out_shape=jax.ShapeDtypeStruct((t, LANES), F32), grid=(t // tm,),
        in_specs=[pl.BlockSpec((tm, d), lambda i: (i, 0)), pl.BlockSpec((d, LANES), lambda i: (0, 0)),
                  pl.BlockSpec((d, LANES), lambda i: (0, 0))],
        out_specs=pl.BlockSpec((tm, LANES), lambda i: (i, 0)),
        compiler_params=_cparams(("parallel",)), name="moe_router")(x, rh, rl)


def _extended_w_in(w_in):
    w = w_in.astype(F32)
    scale = HEAD_DIM ** -0.5

    def rot_half(cols):
        c4 = cols.reshape(-1, N_HEADS, 2, HEAD_DIM // 2)
        return jnp.stack([-c4[:, :, 1], c4[:, :, 0]], axis=2).reshape(-1, D_GROUP)

    wq = w[:, 0:256]
    wk = w[:, 256:512] * scale
    main = jnp.concatenate([wq, wk, w[:, 512:3072]], axis=1)
    gates = jnp.pad(w[:, 3072:3088], ((0, 0), (0, LANES - 16)))
    ext = jnp.concatenate([main, rot_half(wq), rot_half(wk), gates], axis=1)
    return jnp.pad(ext, ((0, 0), (0, N_EXT - ext.shape[1]))).astype(BF16)


def kernel(x, ln_in_w, ln_in_b, w_in, w_out, ret_gn_w, s5_a_re, s5_a_im, s5_log_dt, s5_b_re, s5_b_im, s5_c_re, s5_c_im, s5_d, s5_w_glu, hy_conv_w, hy_conv_b, hy_w1, hy_b1, hy_w2, hy_b2, hy_w3, hy_freq, hy_bias, ml_conv_w, ml_conv_b, ml_gate_b, ml_gn_w, ln1_w, ln1_b, ln2_w, ln2_b, ffn_w1, ffn_w3, ffn_w2, moe_router, moe_w1, moe_w3, moe_w2):
    bsz, l, d = x.shape
    t = bsz * l
    cos_full, sin_full = _rope_tables(l)
    h = _layer_norm(x.reshape(t, d), ln_in_w, ln_in_b)
    for layer in range(DEPTH):
        proj, gates = _in_proj(h, _extended_w_in(w_in[layer]))
        proj, gates = proj.reshape(bsz, l, N_EXT), gates.reshape(bsz, l, LANES)
        y_ret = _retention(proj, ret_gn_w[layer], cos_full, sin_full)
        y_s5 = _s5(proj, s5_a_re[layer], s5_a_im[layer], s5_log_dt[layer], s5_b_re[layer], s5_b_im[layer],
                   s5_c_re[layer], s5_c_im[layer], s5_d[layer], s5_w_glu[layer])
        y_hy = _hyena(proj, hy_conv_w[layer], hy_conv_b[layer], hy_w1[layer], hy_b1[layer], hy_w2[layer],
                      hy_b2[layer], hy_w3[layer], hy_freq[layer], hy_bias[layer])
        qk = _shortconv(proj, CB_MQ, 2, ml_conv_w[layer], ml_conv_b[layer], act=True)
        y_ml = _mlstm(proj, qk, gates, ml_gate_b[layer], ml_gn_w[layer])
        ys = [y.reshape(t, D_GROUP) for y in (y_ret, y_s5, y_hy, y_ml)]
        h = _outproj_ln(ys, w_out[layer], h, ln1_w[layer], ln1_b[layer])
        j = layer // 2
        if layer % 2 == 0:
            h = _ffn_ln(h, ffn_w1[j].astype(BF16), ffn_w3[j].astype(BF16), ffn_w2[j].astype(BF16),
                        ln2_w[layer], ln2_b[layer])
        else:
            cmb = _router(h, moe_router[j])
            wt = [jnp.swapaxes(w[j], 1, 2).astype(BF16) for w in (moe_w1, moe_w3, moe_w2)]
            h = _moe_ln(h, cmb, *wt, ln2_w[layer], ln2_b[layer])
    return h.reshape(bsz, l, d)
```

```python
import functools
import math

import numpy as np
import jax
import jax.numpy as jnp
from jax import lax
from jax.experimental import pallas as pl
from jax.experimental.pallas import tpu as pltpu

F32 = jnp.float32
BF16 = jnp.bfloat16

D_MODEL = 1024
DEPTH = 2
D_GROUP = 256
HEAD_DIM = 64
N_HEADS = 4
CHUNK = 128
S5_CH = 16
S5_GROUPS = 16
S5_STATE = 64
HY_ORDER = 2
HY_EMB = 33
HY_BANDS = 16
HY_FFN = 64
HY_FAST_DECAY = 0.3
HY_SLOW_DECAY = 1.5
HY_TARGET = 1e-2
N_EXPERTS = 8
ROPE_BASE = 10000.0
EPS = 1e-5
DN_ALPHA = (2 * DEPTH) ** 0.25

LANES = 128
HALO = 16
S5_TC = 32
FFT_NB = 256
FFT_QB = 2
N_EXT = 3840

CB_RQ, CB_RK, CB_RV, CB_RG, CB_S5, CB_HV, CB_HX1, CB_HX2 = 0, 1, 2, 3, 4, 5, 6, 7
CB_MQ, CB_MK, CB_MV, CB_MO, CB_RQR, CB_RKR = 8, 9, 10, 11, 12, 13
GATE_COL128 = 28


def _cparams(sem, vmem_mb=None):
    kw = dict(dimension_semantics=sem)
    if vmem_mb is not None:
        kw["vmem_limit_bytes"] = vmem_mb * 1024 * 1024
    return pltpu.CompilerParams(**kw)


def _tile(n, pref):
    return pref if n % pref == 0 else n


def _split_dot(x, m, parts=2):
    acc = None
    r = x
    for _ in range(parts):
        hi = r.astype(BF16)
        t = jnp.dot(hi, m, preferred_element_type=F32)
        acc = t if acc is None else acc + t
        r = r - hi.astype(F32)
    return acc


def _split_dot_left(m, x, parts=2):
    acc = None
    r = x
    for _ in range(parts):
        hi = r.astype(BF16)
        t = jnp.dot(m, hi, preferred_element_type=F32)
        acc = t if acc is None else acc + t
        r = r - hi.astype(F32)
    return acc


def _dot_nt(a, b):
    return lax.dot_general(a, b, (((1,), (1,)), ((), ())), preferred_element_type=F32)


def _dot_tn(a, b):
    return lax.dot_general(a, b, (((0,), (0,)), ((), ())), preferred_element_type=F32)


def _sigmoid(x):
    return 1.0 / (1.0 + jnp.exp(-x))


def _silu(x):
    return x * _sigmoid(x)


def _log_sigmoid(x):
    return jnp.minimum(x, 0.0) - jnp.log(1.0 + jnp.exp(-jnp.abs(x)))


def _head_masks(dtype):
    lane = lax.broadcasted_iota(jnp.int32, (1, D_GROUP), 1)
    return [((lane >= h * HEAD_DIM) & (lane < (h + 1) * HEAD_DIM)).astype(dtype) for h in range(N_HEADS)]


def _ln_core(x, w, b):
    mu = jnp.mean(x, -1, keepdims=True)
    xc = x - mu
    var = jnp.mean(xc * xc, -1, keepdims=True)
    return xc * lax.rsqrt(var + EPS) * w + b


def _ln_kernel(x_ref, w_ref, b_ref, o_ref):
    o_ref[...] = _ln_core(x_ref[...], w_ref[...], b_ref[...])


def _layer_norm(x, w, b):
    t, d = x.shape
    tm = _tile(t, 512)
    row = pl.BlockSpec((tm, d), lambda i: (i, 0))
    vec = pl.BlockSpec((1, d), lambda i: (0, 0))
    return pl.pallas_call(_ln_kernel, out_shape=jax.ShapeDtypeStruct((t, d), F32), grid=(t // tm,),
                          in_specs=[row, vec, vec], out_specs=row,
                          compiler_params=_cparams(("parallel",)), name="layer_norm")(x, w.reshape(1, d), b.reshape(1, d))


IN_PROJ_TN = 1280


def _in_proj_kernel(a_ref, b_ref, o_ref, g_ref, *, nj, gate_off):
    acc = jnp.dot(a_ref[...].astype(BF16), b_ref[...], preferred_element_type=F32)
    o_ref[...] = acc.astype(o_ref.dtype)

    @pl.when(pl.program_id(1) == nj - 1)
    def _():
        g_ref[...] = acc[:, gate_off:gate_off + LANES]


def _in_proj(h, w_ext):
    m, k = h.shape
    n = w_ext.shape[1]
    tm, tn = _tile(m, 1024), IN_PROJ_TN
    nj = n // tn
    gate_off = GATE_COL128 * LANES - (nj - 1) * tn
    return pl.pallas_call(
        functools.partial(_in_proj_kernel, nj=nj, gate_off=gate_off),
        out_shape=(jax.ShapeDtypeStruct((m, n), BF16), jax.ShapeDtypeStruct((m, LANES), F32)), grid=(m // tm, nj),
        in_specs=[pl.BlockSpec((tm, k), lambda i, j: (i, 0)), pl.BlockSpec((k, tn), lambda i, j: (0, j))],
        out_specs=(pl.BlockSpec((tm, tn), lambda i, j: (i, j)), pl.BlockSpec((tm, LANES), lambda i, j: (i, 0))),
        compiler_params=_cparams(("parallel", "arbitrary"), 48), name="in_proj")(h, w_ext)


def _outproj_ln_kernel(y0_ref, y1_ref, y2_ref, y3_ref, w_ref, h_ref, lw_ref, lb_ref, o_ref):
    mix = None
    for g, y_ref in enumerate((y0_ref, y1_ref, y2_ref, y3_ref)):
        part = jnp.dot(y_ref[...].astype(BF16), w_ref[g * D_GROUP:(g + 1) * D_GROUP, :], preferred_element_type=F32)
        mix = part if mix is None else mix + part
    o_ref[...] = _ln_core(DN_ALPHA * h_ref[...] + mix, lw_ref[...], lb_ref[...])


def _outproj_ln(ys, w_out, h, lw, lb):
    t, d = h.shape
    tm = _tile(t, 1024)
    grp = pl.BlockSpec((tm, D_GROUP), lambda i: (i, 0))
    row = pl.BlockSpec((tm, d), lambda i: (i, 0))
    vec = pl.BlockSpec((1, d), lambda i: (0, 0))
    return pl.pallas_call(
        _outproj_ln_kernel, out_shape=jax.ShapeDtypeStruct((t, d), F32), grid=(t // tm,),
        in_specs=[grp, grp, grp, grp, pl.BlockSpec((d, d), lambda i: (0, 0)), row, vec, vec], out_specs=row,
        compiler_params=_cparams(("parallel",), 48), name="out_proj_ln",
    )(*ys, w_out.astype(BF16), h, lw.reshape(1, d), lb.reshape(1, d))


def _shortconv_kernel(x_ref, xp_ref, xn_ref, w_ref, b_ref, o_ref, *, nt, act):
    i = pl.program_id(1)
    x = x_ref[0].astype(F32)
    tl = x.shape[0]
    row = lax.broadcasted_iota(jnp.int32, x.shape, 0)
    prev_row = jnp.where(i == 0, 0.0, xp_ref[0, HALO - 1:HALO, :].astype(F32))
    next_row = jnp.where(i == nt - 1, 0.0, xn_ref[0, 0:1, :].astype(F32))
    x_prev = jnp.where(row == 0, prev_row, pltpu.roll(x, 1, 0))
    x_next = jnp.where(row == tl - 1, next_row, pltpu.roll(x, tl - 1, 0))
    w = w_ref[0]
    y = b_ref[0, 0:1] + x_prev * w[0:1] + x * w[1:2] + x_next * w[2:3]
    if act:
        y = _silu(y)
    o_ref[0] = y


def _shortconv(proj, col0, nblk, w, b, act):
    bsz, l, _ = proj.shape
    tl = _tile(l, 1024)
    nt = l // tl
    w3 = jnp.transpose(w.reshape(3, nblk, D_GROUP), (1, 0, 2))
    w3 = jnp.pad(w3, ((0, 0), (0, 5), (0, 0)))
    b3 = jnp.broadcast_to(b.reshape(nblk, 1, D_GROUP), (nblk, 8, D_GROUP))
    rh = tl // HALO
    return pl.pallas_call(
        functools.partial(_shortconv_kernel, nt=nt, act=act),
        out_shape=jax.ShapeDtypeStruct((bsz, l, nblk * D_GROUP), F32), grid=(bsz, nt, nblk),
        in_specs=[
            pl.BlockSpec((1, tl, D_GROUP), lambda bb, i, j: (bb, i, col0 + j)),
            pl.BlockSpec((1, HALO, D_GROUP), lambda bb, i, j: (bb, jnp.maximum(i * rh - 1, 0), col0 + j)),
            pl.BlockSpec((1, HALO, D_GROUP), lambda bb, i, j: (bb, jnp.minimum((i + 1) * rh, l // HALO - 1), col0 + j)),
            pl.BlockSpec((1, 8, D_GROUP), lambda bb, i, j: (j, 0, 0)),
            pl.BlockSpec((1, 8, D_GROUP), lambda bb, i, j: (j, 0, 0)),
        ],
        out_specs=pl.BlockSpec((1, tl, D_GROUP), lambda bb, i, j: (bb, i, j)),
        compiler_params=_cparams(("parallel", "parallel", "parallel")), name="shortconv")(proj, proj, proj, w3, b3)


def _stack_heads(xb, masks):
    return jnp.concatenate([xb * masks[h] for h in range(N_HEADS)], axis=0)


def _compact(s):
    return s[0:64] + s[64:128] + s[128:192] + s[192:256]


def _expand(c, bd):
    return jnp.concatenate([c, c, c, c], axis=0) * bd


def _head_norm(o, avg, gn):
    mu = _split_dot(o, avg, parts=2)
    oc = o - mu
    var = _split_dot(oc * oc, avg, parts=2)
    return oc * lax.rsqrt(var + EPS) * gn


def _ret_kernel(q_ref, qr_ref, k_ref, kr_ref, v_ref, g_ref, cos_ref, sin_ref,
                dsym_ref, qdf_ref, qdb_ref, kdf_ref, kdb_ref, cdec_ref, bd_ref, avg_ref, gn_ref,
                o_ref, sfw_ref, sbw_ref, save_ref, *, cb, nblk):
    p = pl.program_id(0)
    i = pl.program_id(1)
    bsz = q_ref.shape[0]
    masks = _head_masks(BF16)
    bd = bd_ref[...]
    cdec = cdec_ref[...]

    def rope_k(b, rows):
        return k_ref[b, rows] * cos_ref[rows] + kr_ref[b, rows] * sin_ref[rows]

    def kv_update(s, k, decay, vb):
        kv = _dot_tn((k * decay).astype(BF16), vb)
        return s * cdec + kv * bd

    @pl.when(p == 0)
    def _():
        @pl.when(i == 0)
        def _():
            sbw_ref[...] = jnp.zeros_like(sbw_ref)

        blk = nblk - 1 - i
        for c in reversed(range(cb)):
            rows = slice(c * CHUNK, (c + 1) * CHUNK)
            for b in range(bsz):
                s = sbw_ref[b]
                save_ref[b, blk * cb + c] = _compact(s)
                sbw_ref[b] = kv_update(s, rope_k(b, rows), kdb_ref[...], v_ref[b, rows].astype(BF16))

    @pl.when(p == 1)
    def _():
        @pl.when(i == 0)
        def _():
            sfw_ref[...] = jnp.zeros_like(sfw_ref)

        for c in range(cb):
            rows = slice(c * CHUNK, (c + 1) * CHUNK)
            for b in range(bsz):
                q = q_ref[b, rows] * cos_ref[rows] + qr_ref[b, rows] * sin_ref[rows]
                k = rope_k(b, rows)
                qb, kb, vb = q.astype(BF16), k.astype(BF16), v_ref[b, rows].astype(BF16)
                s_all = _dot_nt(qb, _stack_heads(kb, masks))
                pmat = (s_all * dsym_ref[...]).astype(BF16)
                o = jnp.dot(pmat, _stack_heads(vb, masks), preferred_element_type=F32)
                sfw = sfw_ref[b]
                sbw = _expand(save_ref[b, i * cb + c], bd)
                o = o + jnp.dot(qb, sfw.astype(BF16), preferred_element_type=F32) * qdf_ref[...]
                o = o + jnp.dot(qb, sbw.astype(BF16), preferred_element_type=F32) * qdb_ref[...]
                y = _head_norm(o, avg_ref[...], gn_ref[...])
                o_ref[b, rows] = _silu(g_ref[b, rows].astype(F32)) * y
                sfw_ref[b] = kv_update(sfw, k, kdf_ref[...], vb)


def _ret_tables():
    lg = np.log(1.0 - 2.0 ** (-5.0 - np.arange(N_HEADS, dtype=np.float64)))
    pos = np.arange(CHUNK, dtype=np.float64)
    lag = np.abs(pos[:, None] - pos[None, :])
    dsym = np.concatenate([np.exp(lg[h] * lag) for h in range(N_HEADS)], axis=1)
    lane_lg = np.repeat(lg, HEAD_DIM)[None, :]
    qdf = np.exp(lane_lg * (pos[:, None] + 1.0))
    qdb = np.exp(lane_lg * (CHUNK - pos[:, None]))
    kdf = np.exp(lane_lg * (CHUNK - 1.0 - pos[:, None]))
    kdb = np.exp(lane_lg * pos[:, None])
    cdec = np.exp(lane_lg * CHUNK)
    return [jnp.asarray(t, F32) for t in (dsym, qdf, qdb, kdf, kdb, cdec)]


def _block_diag_mask():
    hid = np.arange(D_GROUP) // HEAD_DIM
    return (hid[:, None] == hid[None, :]).astype(np.float32)


def _rope_tables(l):
    half = HEAD_DIM // 2
    inv = ROPE_BASE ** (-np.arange(half, dtype=np.float64) / half)
    ang = np.arange(l, dtype=np.float64)[:, None] * inv[None, :]
    cos, sin = np.cos(ang), np.sin(ang)
    cos_full = np.tile(np.concatenate([cos, cos], -1), (1, N_HEADS))
    sin_full = np.tile(np.concatenate([sin, sin], -1), (1, N_HEADS))
    return jnp.asarray(cos_full, F32), jnp.asarray(sin_full, F32)


def _retention(proj, gn_w, cos_full, sin_full):
    bsz, l, _ = proj.shape
    nc = l // CHUNK
    cb = 4 if nc % 4 == 0 else 1
    nblk = nc // cb
    tl = cb * CHUNK
    dsym, qdf, qdb, kdf, kdb, cdec = _ret_tables()
    bd = jnp.asarray(_block_diag_mask())
    avg = jnp.asarray(_block_diag_mask() / HEAD_DIM, BF16)

    def both(col):
        return pl.BlockSpec((bsz, tl, D_GROUP), lambda p, i: (0, i + (1 - p) * (nblk - 1 - 2 * i), col))

    def fwd_only(col):
        return pl.BlockSpec((bsz, tl, D_GROUP), lambda p, i: (0, p * i, col))

    tab = pl.BlockSpec((tl, D_GROUP), lambda p, i: (i + (1 - p) * (nblk - 1 - 2 * i), 0))

    def const(shape):
        return pl.BlockSpec(shape, lambda p, i: (0,) * len(shape))

    return pl.pallas_call(
        functools.partial(_ret_kernel, cb=cb, nblk=nblk),
        out_shape=jax.ShapeDtypeStruct((bsz, l, D_GROUP), F32), grid=(2, nblk),
        in_specs=[fwd_only(CB_RQ), fwd_only(CB_RQR), both(CB_RK), both(CB_RKR), both(CB_RV), fwd_only(CB_RG),
                  tab, tab, const((CHUNK, 4 * CHUNK)), const((CHUNK, D_GROUP)), const((CHUNK, D_GROUP)),
                  const((CHUNK, D_GROUP)), const((CHUNK, D_GROUP)), const((1, D_GROUP)),
                  const((D_GROUP, D_GROUP)), const((D_GROUP, D_GROUP)), const((1, D_GROUP))],
        out_specs=pl.BlockSpec((bsz, tl, D_GROUP), lambda p, i: (0, p * i, 0)),
        scratch_shapes=[pltpu.VMEM((bsz, D_GROUP, D_GROUP), F32), pltpu.VMEM((bsz, D_GROUP, D_GROUP), F32),
                        pltpu.VMEM((bsz, nc, HEAD_DIM, D_GROUP), F32)],
        compiler_params=_cparams(("arbitrary", "arbitrary"), 48), name="retention",
    )(proj, proj, proj, proj, proj, proj, cos_full, sin_full, dsym, qdf, qdb, kdf, kdb, cdec, bd, avg,
      gn_w.reshape(1, D_GROUP))


def _mlstm_kernel(q_ref, k_ref, v_ref, og_ref, gc_ref, gr_ref, bc_ref, br_ref, ex_ref, lt_ref, ut_ref,
                  ones_ref, obd_ref, bd_ref, avg_ref, gn_ref,
                  o_ref, cfw_ref, cbw_ref, nmfw_ref, nmbw_ref, csave_ref, nmsave_ref, *, cb, nblk):
    p = pl.program_id(0)
    i = pl.program_id(1)
    bsz = q_ref.shape[0]
    masks = _head_masks(BF16)
    bd = bd_ref[...]
    lt = lt_ref[...]
    ut = ut_ref[...]
    ri = lax.broadcasted_iota(jnp.int32, (CHUNK, CHUNK), 0)
    ci = lax.broadcasted_iota(jnp.int32, (CHUNK, CHUNK), 1)
    lane = lax.broadcasted_iota(jnp.int32, (1, D_GROUP), 1)

    def gates_expanded(b, rows):
        return _split_dot(gc_ref[b, rows] + bc_ref[...], ex_ref[...])

    def state_update(c_ref, nm_ref, b, total, cum, i_x, k, vb):
        m_prev = nm_ref[b, 1:2]
        g = (total - cum) + i_x
        m_new = jnp.maximum(total + m_prev, jnp.max(g, axis=0, keepdims=True))
        wk = jnp.exp(g - m_new) * k
        decay = jnp.exp(total + m_prev - m_new)
        c_ref[b] = c_ref[b] * decay + _dot_tn(wk.astype(BF16), vb) * bd
        nm_ref[b, 0:1] = decay * nm_ref[b, 0:1] + jnp.sum(wk, axis=0, keepdims=True)
        nm_ref[b, 1:2] = m_new

    @pl.when(p == 0)
    def _():
        @pl.when(i == 0)
        def _():
            cbw_ref[...] = jnp.zeros_like(cbw_ref)
            nmbw_ref[...] = jnp.zeros_like(nmbw_ref)

        blk = nblk - 1 - i
        for c in reversed(range(cb)):
            rows = slice(c * CHUNK, (c + 1) * CHUNK)
            for b in range(bsz):
                csave_ref[b, blk * cb + c] = _compact(cbw_ref[b])
                nmsave_ref[b, blk * cb + c] = nmbw_ref[b]
                gx = gates_expanded(b, rows)
                cum = _split_dot_left(ut, _log_sigmoid(gx[:, 768:1024]))
                k = k_ref[b, rows] * (HEAD_DIM ** -0.5)
                state_update(cbw_ref, nmbw_ref, b, cum[0:1], cum, gx[:, 512:768], k, v_ref[b, rows].astype(BF16))

    def chunk_out(b, rows, cidx):
        q = q_ref[b, rows]
        k = k_ref[b, rows] * (HEAD_DIM ** -0.5)
        qb, kb, vb = q.astype(BF16), k.astype(BF16), v_ref[b, rows].astype(BF16)
        s_all = _dot_nt(qb, _stack_heads(kb, masks))
        vaug = jnp.concatenate([_stack_heads(vb, masks), ones_ref[...]], axis=1)
        gx = gates_expanded(b, rows)
        graw = gr_ref[b, :, rows] + br_ref[...]
        gls = _log_sigmoid(graw)
        cum_r_fw = _split_dot(gls, ut)
        cum_r_bw = _split_dot(gls, lt)
        ccomp = csave_ref[b, cidx]
        nmb = nmsave_ref[b, cidx]

        def direction(i_x, f_x, tri, cum_r, i_row0, f_row0, mask, c_state, n_vec, m_prev, total_row):
            cum = _split_dot_left(tri, _log_sigmoid(f_x))
            total = cum[total_row:total_row + 1]
            inter = cum + m_prev
            ps, rmax = [], []
            dms = []
            for h in range(N_HEADS):
                a_col = cum[:, h * HEAD_DIM:h * HEAD_DIM + 1]
                dm = a_col - cum_r[f_row0 + h:f_row0 + h + 1] + graw[i_row0 + h:i_row0 + h + 1]
                dm = jnp.where(mask, dm, -jnp.inf)
                dms.append(dm)
                rmax.append(jnp.max(dm, axis=-1, keepdims=True))
            rmax256 = jnp.where(lane < 64, rmax[0], jnp.where(lane < 128, rmax[1],
                                jnp.where(lane < 192, rmax[2], rmax[3])))
            m_row = jnp.maximum(inter, rmax256)
            for h in range(N_HEADS):
                m_h = m_row[:, h * HEAD_DIM:h * HEAD_DIM + 1]
                ps.append(s_all[:, h * CHUNK:(h + 1) * CHUNK] * jnp.exp(dms[h] - m_h))
            pmat = jnp.concatenate(ps, axis=1).astype(BF16)
            nd = jnp.dot(pmat, vaug, preferred_element_type=F32)
            w_inter = jnp.exp(inter - m_row)
            qc = jnp.dot(qb, c_state.astype(BF16), preferred_element_type=F32)
            qn = _split_dot(q * n_vec, obd_ref[...])
            num = nd[:, :D_GROUP] + w_inter * qc
            den = nd[:, D_GROUP:] + w_inter * qn
            hdir = num / jnp.maximum(jnp.abs(den), jnp.exp(-m_row))
            return hdir, total, cum

        h_fw, tot_fw, cum_fw = direction(gx[:, 0:256], gx[:, 256:512], lt, cum_r_fw, 0, 4, ri >= ci,
                                         cfw_ref[b], nmfw_ref[b, 0:1], nmfw_ref[b, 1:2], CHUNK - 1)
        h_bw, _, _ = direction(gx[:, 512:768], gx[:, 768:1024], ut, cum_r_bw, 8, 12, ci >= ri,
                               _expand(ccomp, bd), nmb[0:1], nmb[1:2], 0)
        y = _head_norm(h_fw + h_bw, avg_ref[...], gn_ref[...])
        o_ref[b, rows] = _sigmoid(og_ref[b, rows].astype(F32)) * y
        state_update(cfw_ref, nmfw_ref, b, tot_fw, cum_fw, gx[:, 0:256], k, vb)

    @pl.when(p == 1)
    def _():
        @pl.when(i == 0)
        def _():
            cfw_ref[...] = jnp.zeros_like(cfw_ref)
            nmfw_ref[...] = jnp.zeros_like(nmfw_ref)

        for c in range(cb):
            rows = slice(c * CHUNK, (c + 1) * CHUNK)
            for b in range(bsz):
                chunk_out(b, rows, i * cb + c)


def _mlstm(proj, qk, gates, gate_b, gn_w):
    gates_row = jnp.transpose(gates[:, :, :16], (0, 2, 1))
    bsz, l, _ = proj.shape
    nc = l // CHUNK
    cb = 2 if nc % 2 == 0 else 1
    nblk = nc // cb
    tl = cb * CHUNK
    bd_np = _block_diag_mask()
    bd = jnp.asarray(bd_np)
    avg = jnp.asarray(bd_np / HEAD_DIM, BF16)
    obd = jnp.asarray(bd_np, BF16)
    ex = np.zeros((LANES, 4 * D_GROUP), np.float32)
    for j in range(16):
        typ, h = divmod(j, N_HEADS)
        ex[j, typ * D_GROUP + h * HEAD_DIM: typ * D_GROUP + (h + 1) * HEAD_DIM] = 1.0
    idx = np.arange(CHUNK)
    lt = (idx[None, :] <= idx[:, None]).astype(np.float32)
    ones_st = np.repeat(np.repeat(np.eye(N_HEADS, dtype=np.float32), CHUNK, 0), HEAD_DIM, 1)
    gb = gate_b.astype(F32).reshape(16)
    bias_col = jnp.pad(gb, (0, LANES - 16)).reshape(1, LANES)
    bias_row = jnp.broadcast_to(gb.reshape(16, 1), (16, CHUNK))

    def both(arr_col, width=D_GROUP):
        return pl.BlockSpec((bsz, tl, width), lambda p, i: (0, i + (1 - p) * (nblk - 1 - 2 * i), arr_col))

    def fwd_only(arr_col):
        return pl.BlockSpec((bsz, tl, D_GROUP), lambda p, i: (0, p * i, arr_col))

    def const(shape):
        return pl.BlockSpec(shape, lambda p, i: (0,) * len(shape))

    return pl.pallas_call(
        functools.partial(_mlstm_kernel, cb=cb, nblk=nblk),
        out_shape=jax.ShapeDtypeStruct((bsz, l, D_GROUP), F32), grid=(2, nblk),
        in_specs=[fwd_only(0), both(1), both(CB_MV), fwd_only(CB_MO), both(0, LANES),
                  pl.BlockSpec((bsz, 16, tl), lambda p, i: (0, 0, p * i)),
                  const((1, LANES)), const((16, CHUNK)), const((LANES, 4 * D_GROUP)),
                  const((CHUNK, CHUNK)), const((CHUNK, CHUNK)), const((4 * CHUNK, D_GROUP)),
                  const((D_GROUP, D_GROUP)), const((D_GROUP, D_GROUP)), const((D_GROUP, D_GROUP)),
                  const((1, D_GROUP))],
        out_specs=pl.BlockSpec((bsz, tl, D_GROUP), lambda p, i: (0, p * i, 0)),
        scratch_shapes=[pltpu.VMEM((bsz, D_GROUP, D_GROUP), F32), pltpu.VMEM((bsz, D_GROUP, D_GROUP), F32),
                        pltpu.VMEM((bsz, 8, D_GROUP), F32), pltpu.VMEM((bsz, 8, D_GROUP), F32),
                        pltpu.VMEM((bsz, nc, HEAD_DIM, D_GROUP), F32), pltpu.VMEM((bsz, nc, 8, D_GROUP), F32)],
        compiler_params=_cparams(("arbitrary", "arbitrary"), 48), name="mlstm",
    )(qk, qk, proj, proj, gates, gates_row, bias_col, bias_row, jnp.asarray(ex, BF16), jnp.asarray(lt, BF16),
      jnp.asarray(lt.T, BF16), jnp.asarray(ones_st, BF16), obd, bd, avg, gn_w.reshape(1, D_GROUP))


def _s5_kernel(u_ref, mt_ref, bg_ref, cg_ref, pa_ref, pb_ref, o_ref, *, nsteps):
    ub = u_ref[0].astype(BF16)
    e = jnp.dot(ub, bg_ref[0], preferred_element_type=F32)
    r = e.shape[0]
    row = lax.broadcasted_iota(jnp.int32, (r, LANES), 0)
    xf, xb = e[:, :LANES], e[:, LANES:]
    pa, pb = pa_ref[0], pb_ref[0]
    for s in range(nsteps):
        sh = 1 << s
        a_f, b_f = pa[s:s + 1, :LANES], pb[s:s + 1, :LANES]
        a_b, b_b = pa[s:s + 1, LANES:], pb[s:s + 1, LANES:]
        yf = jnp.where(row >= sh, pltpu.roll(xf, sh, 0), 0.0)
        yb = jnp.where(row < r - sh, pltpu.roll(xb, r - sh, 0), 0.0)
        xf = xf + a_f * yf + b_f * pltpu.roll(yf, LANES // 2, 1)
        xb = xb + a_b * yb + b_b * pltpu.roll(yb, LANES // 2, 1)
    sprev = jnp.where(row >= 1, pltpu.roll(xf, 1, 0), 0.0)
    snext = jnp.where(row < r - 1, pltpu.roll(xb, r - 1, 0), 0.0)
    st = jnp.concatenate([sprev, snext], axis=1).astype(BF16)
    o_ref[0] = (jnp.dot(ub, mt_ref[0], preferred_element_type=F32)
                + jnp.dot(st, cg_ref[0], preferred_element_type=F32))


def _s5_tables(a_re, a_im, log_dt, b_re, b_im, c_re, c_im, d_skip, tc, nsteps):
    g, p, ch = S5_GROUPS, S5_STATE, S5_CH
    hp = lax.Precision.HIGHEST
    are, aim = a_re.astype(F32), a_im.astype(F32)
    delta = jnp.exp(log_dt.astype(F32))[..., None]
    lre, lim = are * delta, aim * delta

    class Cx:
        def __init__(self, re, im):
            self.re, self.im = re, im

        def __mul__(self, o):
            return Cx(self.re * o.re - self.im * o.im, self.re * o.im + self.im * o.re)

        def __getitem__(self, idx):
            return Cx(self.re[idx], self.im[idx])

    def apow(n):
        n = jnp.asarray(n, F32)[None, None, :, None]
        mag, ang = jnp.exp(lre[:, :, None, :] * n), lim[:, :, None, :] * n
        return Cx(mag * jnp.cos(ang), mag * jnp.sin(ang))

    abr, abi = jnp.exp(lre) * jnp.cos(lim), jnp.exp(lre) * jnp.sin(lim)
    den = are * are + aim * aim
    quo = Cx(((abr - 1.0) * are + abi * aim) / den, (abi * are - (abr - 1.0) * aim) / den)
    b_bar = quo[..., None] * Cx(b_re.astype(F32)[None], b_im.astype(F32)[None])
    c = Cx(c_re.astype(F32), c_im.astype(F32))
    taus = np.arange(tc)
    cp = c[:, :, None] * apow(taus)[:, :, :, None, :]
    kk = (jnp.einsum("dgtop,dgpi->dgtoi", cp.re, b_bar.re, precision=hp)
          - jnp.einsum("dgtop,dgpi->dgtoi", cp.im, b_bar.im, precision=hp))
    dsk = d_skip.astype(F32).reshape(g, ch)[:, :, None] * jnp.eye(ch, dtype=F32)[None]
    kdiag = kk[0][:, 0] + kk[1][:, 0] + dsk
    lags = jnp.concatenate([kk[1][:, :0:-1], kdiag[:, None], kk[0][:, 1:]], axis=1)
    diff = taus[None, :] - taus[:, None]
    sel = (diff[None] + (tc - 1) == np.arange(2 * tc - 1)[:, None, None]).astype(np.float32)
    mt = jnp.einsum("jst,gjoi->gsito", jnp.asarray(sel), lags, precision=hp).reshape(g, tc * ch, tc * ch)

    zf = apow(tc - 1 - taus)[0][..., None] * b_bar[0][:, None]
    zb = apow(taus)[1][..., None] * b_bar[1][:, None]

    def to_rows(z):
        return jnp.transpose(z, (0, 1, 3, 2)).reshape(g, tc * ch, p)

    bg = jnp.concatenate([to_rows(zf.re), to_rows(zf.im), to_rows(zb.re), to_rows(zb.im)], axis=-1)

    yf = c[0][:, None] * apow(taus + 1)[0][:, :, None, :]
    yb = c[1][:, None] * apow(tc - taus)[1][:, :, None, :]

    def to_cols(z):
        return jnp.transpose(z, (0, 3, 1, 2)).reshape(g, p, tc * ch)

    cg = jnp.concatenate([to_cols(yf.re), -to_cols(yf.im), to_cols(yb.re), -to_cols(yb.im)], axis=1)

    steps = tc * (2.0 ** np.arange(nsteps))
    pw = apow(steps)
    re0, im0, re1, im1 = pw.re[0], pw.im[0], pw.re[1], pw.im[1]
    pa = jnp.concatenate([re0, re0, re1, re1], axis=-1)
    pb = jnp.concatenate([-im0, im0, -im1, im1], axis=-1)
    pad = (-nsteps) % 8
    pa = jnp.pad(pa, ((0, 0), (0, pad), (0, 0)))
    pb = jnp.pad(pb, ((0, 0), (0, pad), (0, 0)))
    return mt.astype(BF16), bg.astype(BF16), cg.astype(BF16), pa, pb


def _s5_glu_kernel(y_ref, w_ref, o_ref):
    y = y_ref[...]
    z = 0.5 * y * (1.0 + jnp.tanh(math.sqrt(2.0 / math.pi) * (y + 0.044715 * (y * y * y))))
    o_ref[...] = z * _sigmoid(jnp.dot(z.astype(BF16), w_ref[...], preferred_element_type=F32))


def _s5(proj, a_re, a_im, log_dt, b_re, b_im, c_re, c_im, d_skip, w_glu):
    bsz, l, _ = proj.shape
    tc = S5_TC
    r = l // tc
    nsteps = max(1, int(math.ceil(math.log2(r))))
    w = tc * S5_CH
    mt, bg, cg, pa, pb = _s5_tables(a_re, a_im, log_dt, b_re, b_im, c_re, c_im, d_skip, tc, nsteps)
    u = proj[:, :, CB_S5 * D_GROUP:(CB_S5 + 1) * D_GROUP].astype(F32)
    ug = jnp.transpose(u.reshape(bsz, r, tc, S5_GROUPS, S5_CH), (3, 0, 1, 2, 4)).reshape(S5_GROUPS, bsz * r, w)
    ns8 = pa.shape[1]
    yg = pl.pallas_call(
        functools.partial(_s5_kernel, nsteps=nsteps),
        out_shape=jax.ShapeDtypeStruct((S5_GROUPS, bsz * r, w), F32), grid=(S5_GROUPS, bsz),
        in_specs=[pl.BlockSpec((1, r, w), lambda g, b: (g, b, 0)),
                  pl.BlockSpec((1, w, w), lambda g, b: (g, 0, 0)),
                  pl.BlockSpec((1, w, D_GROUP), lambda g, b: (g, 0, 0)),
                  pl.BlockSpec((1, D_GROUP, w), lambda g, b: (g, 0, 0)),
                  pl.BlockSpec((1, ns8, D_GROUP), lambda g, b: (g, 0, 0)),
                  pl.BlockSpec((1, ns8, D_GROUP), lambda g, b: (g, 0, 0))],
        out_specs=pl.BlockSpec((1, r, w), lambda g, b: (g, b, 0)),
        compiler_params=_cparams(("parallel", "parallel"), 48), name="s5_ssm")(ug, mt, bg, cg, pa, pb)
    y = jnp.transpose(yg.reshape(S5_GROUPS, bsz, r, tc, S5_CH), (1, 2, 3, 0, 4)).reshape(bsz * l, D_GROUP)
    t = bsz * l
    tm = _tile(t, 2048)
    out = pl.pallas_call(
        _s5_glu_kernel, out_shape=jax.ShapeDtypeStruct((t, D_GROUP), F32), grid=(t // tm,),
        in_specs=[pl.BlockSpec((tm, D_GROUP), lambda i: (i, 0)), pl.BlockSpec((D_GROUP, D_GROUP), lambda i: (0, 0))],
        out_specs=pl.BlockSpec((tm, D_GROUP), lambda i: (i, 0)),
        compiler_params=_cparams(("parallel",)), name="s5_glu")(y, w_glu.astype(BF16))
    return out.reshape(bsz, l, D_GROUP)


def _hy_filter_kernel(z_ref, w1_ref, b1_ref, w2_ref, b2_ref, w3_ref, fr_ref, dec_ref, h_ref, ss_ref):
    i = pl.program_id(0)
    hp = lax.Precision.HIGHEST
    fr = fr_ref[...]
    a = jnp.sin(fr * (jnp.dot(z_ref[...], w1_ref[...], precision=hp, preferred_element_type=F32) + b1_ref[...]))
    a = jnp.sin(fr * (jnp.dot(a, w2_ref[...], precision=hp, preferred_element_type=F32) + b2_ref[...]))
    h = jnp.dot(a, w3_ref[...], precision=hp, preferred_element_type=F32)
    df, db = dec_ref[:, :D_GROUP], dec_ref[:, D_GROUP:]
    h = h * jnp.concatenate([df, df, db, db], axis=1)

    @pl.when(i == 0)
    def _():
        ss_ref[...] = jnp.zeros_like(ss_ref)

    ss_ref[...] += jnp.sum(h * h, axis=0, keepdims=True)
    nhalf = h.shape[1] // 2
    row = lax.broadcasted_iota(jnp.int32, h.shape, 0)
    col = lax.broadcasted_iota(jnp.int32, h.shape, 1)
    h_ref[...] = jnp.where((row == 0) & (col >= nhalf) & (i == 0), 0.0, h)


def _hy_filters(l, w1, b1, w2, b2, w3, freq):
    t = np.linspace(0.0, 1.0, l)[:, None]
    w = 2.0 * np.pi * np.arange(l, dtype=np.float64)[:, None] / l
    bands = np.linspace(1e-4, HY_BANDS - 1, HY_BANDS)[None, :]
    z = np.concatenate([t, np.cos(bands * w), -np.sin(bands * w)], axis=-1)
    max_decay = math.log(HY_TARGET) / HY_FAST_DECAY
    min_decay = math.log(HY_TARGET) / HY_SLOW_DECAY
    rates = np.abs(np.linspace(min_decay, max_decay, D_GROUP))
    dec = np.exp(-t * rates)
    rev = np.concatenate([[0], np.arange(l - 1, 0, -1)])
    half = LANES // 2
    zz = np.zeros((l, LANES))
    zz[:, :HY_EMB] = z
    zz[:, half:half + HY_EMB] = z[rev]
    zz = jnp.asarray(zz, F32)
    dec2 = jnp.asarray(np.concatenate([dec, dec[rev]], axis=1), F32)

    def two(m):
        m = m.astype(F32)
        top = jnp.pad(m, ((0, half - m.shape[0]), (0, half - m.shape[1])))
        zero = jnp.zeros_like(top)
        return jnp.concatenate([jnp.concatenate([top, zero], 1), jnp.concatenate([zero, top], 1)], 0)

    def twice(v):
        v = jnp.pad(v.astype(F32), (0, half - v.shape[0]))
        return jnp.concatenate([v, v]).reshape(1, LANES)

    w3r = w3.astype(F32).reshape(HY_FFN, HY_ORDER, 2, D_GROUP)
    nhalf = HY_ORDER * D_GROUP
    w3f = jnp.pad(w3r[:, :, 0].reshape(HY_FFN, nhalf), ((0, half - HY_FFN), (0, 0)))
    w3b = jnp.pad(w3r[:, :, 1].reshape(HY_FFN, nhalf), ((0, half - HY_FFN), (0, 0)))
    zero = jnp.zeros_like(w3f)
    w3p = jnp.concatenate([jnp.concatenate([w3f, zero], 1), jnp.concatenate([zero, w3b], 1)], 0)
    nout = 2 * nhalf
    tl = _tile(l, 512)

    def const(shape):
        return pl.BlockSpec(shape, lambda i: (0, 0))

    return pl.pallas_call(
        _hy_filter_kernel,
        out_shape=(jax.ShapeDtypeStruct((l, nout), F32), jax.ShapeDtypeStruct((1, nout), F32)), grid=(l // tl,),
        in_specs=[pl.BlockSpec((tl, LANES), lambda i: (i, 0)), const((LANES, LANES)), const((1, LANES)),
                  const((LANES, LANES)), const((1, LANES)), const((LANES, nout)), const((1, LANES)),
                  pl.BlockSpec((tl, 2 * D_GROUP), lambda i: (i, 0))],
        out_specs=(pl.BlockSpec((tl, nout), lambda i: (i, 0)), const((1, nout))),
        compiler_params=_cparams(("arbitrary",)), name="hyena_filter_mlp",
    )(zz, two(w1), twice(b1), two(w2), twice(b2), w3p, twice(freq), dec2)


def _dft_consts(na):
    nb = FFT_NB
    n = na * nb
    ia = np.arange(na, dtype=np.float64)
    th = 2.0 * np.pi * np.outer(ia, ia) / na
    c1, s1 = np.cos(th), np.sin(th)
    eye8 = np.eye(8)
    fa_full = np.concatenate([c1, -s1], axis=0)
    g_full = np.kron(fa_full, eye8)
    g_half = np.kron(fa_full[:, : na // 2], eye8)
    g_out = np.kron(np.concatenate([c1[: na // 2], -s1[: na // 2]], axis=1) / n, eye8)
    ib = np.arange(nb, dtype=np.float64)
    ph = 2.0 * np.pi * np.outer(ib, ib) / nb
    c2, s2 = np.cos(ph), np.sin(ph)
    fb = np.block([[c2, s2], [-s2, c2]])
    fbc = np.block([[c2, -s2], [s2, c2]])
    ps = 2.0 * np.pi * np.outer(ia, ib) / n
    twr = np.broadcast_to(np.cos(ps)[:, :, None], (na, nb, LANES))
    twi = np.broadcast_to(-np.sin(ps)[:, :, None], (na, nb, LANES))
    as_f32 = lambda x: jnp.asarray(x, F32)
    return dict(g_full=as_f32(g_full), g_half=as_f32(g_half), g_out=as_f32(g_out), fb=as_f32(fb), fbc=as_f32(fbc),
                twr=as_f32(twr), twi=as_f32(twi))


def _lane_tile(x, reps):
    return x if reps == 1 else jnp.concatenate([x] * reps, axis=-1)


def _hy_spec_kernel(a_ref, twr_ref, twi_ref, fb_ref, ss_ref, o_ref, *, kb, reps):
    scale = lax.rsqrt(ss_ref[...])
    fb = fb_ref[...].astype(BF16)
    for j in range(kb):
        ar, ai = a_ref[0, j], a_ref[1, j]
        twr, twi = _lane_tile(twr_ref[j], reps), _lane_tile(twi_ref[j], reps)
        br = twr * ar - twi * ai
        bi = twr * ai + twi * ar
        x = jnp.dot(fb, jnp.concatenate([br, bi], axis=0).astype(BF16), preferred_element_type=F32)
        o_ref[0, j] = x[:FFT_NB] * scale
        o_ref[1, j] = x[FFT_NB:] * scale


def _hy_mid_kernel(a_ref, h_ref, twr_ref, twi_ref, fb_ref, fbc_ref, o_ref, *, kb, reps):
    fb, fbc = fb_ref[...].astype(BF16), fbc_ref[...].astype(BF16)
    for j in range(kb):
        ar, ai = a_ref[0, 0, j], a_ref[0, 1, j]
        twr, twi = _lane_tile(twr_ref[j], reps), _lane_tile(twi_ref[j], reps)
        br = twr * ar - twi * ai
        bi = twr * ai + twi * ar
        x = jnp.dot(fb, jnp.concatenate([br, bi], axis=0).astype(BF16), preferred_element_type=F32)
        xr, xi = x[:FFT_NB], x[FFT_NB:]
        hr, hi = h_ref[0, j], h_ref[1, j]
        yr = xr * hr - xi * hi
        yi = xr * hi + xi * hr
        z = jnp.dot(fbc, jnp.concatenate([yr, yi], axis=0).astype(BF16), preferred_element_type=F32)
        zr, zi = z[:FFT_NB], z[FFT_NB:]
        o_ref[0, 0, j] = twr * zr + twi * zi
        o_ref[0, 1, j] = twr * zi - twi * zr


def _hy_dft1_kernel(g_ref, x_ref, o_ref, *, qb):
    na_in, c = x_ref.shape[1], x_ref.shape[4]
    na = o_ref.shape[2]
    g = g_ref[...].astype(BF16)
    for q in range(qb):
        x = x_ref[0, :, q].reshape(na_in * 8, c).astype(BF16)
        a = jnp.dot(g, x, preferred_element_type=F32)
        o_ref[0, :, :, q] = a.reshape(2, na, 8, c)


def _hy_fdft1_kernel(g_ref, xt_ref, xb_ref, o_ref, *, qb):
    nah, c = xt_ref.shape[1], xt_ref.shape[4]
    na = o_ref.shape[2]
    kh = nah * 8
    gt, gb = g_ref[:, :kh].astype(BF16), g_ref[:, kh:].astype(BF16)
    for q in range(qb):
        xt = xt_ref[0, :, q].reshape(kh, c).astype(BF16)
        xb = xb_ref[0, :, q].reshape(kh, c).astype(BF16)
        a = jnp.dot(gt, xt, preferred_element_type=F32) + jnp.dot(gb, xb, preferred_element_type=F32)
        o_ref[0, :, :, q] = a.reshape(2, na, 8, c)


def _hy_dft1(g, x5, ncol, name):
    bsz, na_in, nq = x5.shape[:3]
    na = g.shape[0] // 16
    c = D_GROUP
    qb = FFT_QB
    return pl.pallas_call(
        functools.partial(_hy_dft1_kernel, qb=qb),
        out_shape=jax.ShapeDtypeStruct((bsz, 2, na, nq, 8, ncol * c), F32), grid=(bsz, ncol, nq // qb),
        in_specs=[pl.BlockSpec(g.shape, lambda b, j, q: (0, 0)),
                  pl.BlockSpec((1, na_in, qb, 8, c), lambda b, j, q: (b, 0, q, 0, j))],
        out_specs=pl.BlockSpec((1, 2, na, qb, 8, c), lambda b, j, q: (b, 0, 0, q, 0, j)),
        compiler_params=_cparams(("parallel", "parallel", "parallel"), 48), name=name)(g, x5)


def _hy_out_kernel(g_ref, z_ref, x_ref, v_ref, b_ref, o_ref, *, qb):
    na2, c = z_ref.shape[1] * z_ref.shape[2], z_ref.shape[5]
    nah = o_ref.shape[1]
    bias = b_ref[...].reshape(1, 1, c)
    g = g_ref[...].astype(BF16)
    for q in range(qb):
        z = z_ref[0, :, :, q].reshape(na2 * 8, c).astype(BF16)
        y = jnp.dot(g, z, preferred_element_type=F32).reshape(nah, 8, c)
        o_ref[0, :, q] = x_ref[0, :, q] * (y + v_ref[0, :, q] * bias)


def _hyena(proj, conv_w, conv_b, w1, b1, w2, b2, w3, freq, bias):
    bsz, l, _ = proj.shape
    nb = FFT_NB
    na = 2 * l // nb
    nah = na // 2
    nq = nb // 8
    c = D_GROUP
    qb = FFT_QB
    dc = _dft_consts(na)
    pc = _shortconv(proj, CB_HV, 3, conv_w, conv_b, act=False)

    h, ss = _hy_filters(l, w1, b1, w2, b2, w3, freq)
    ncf = HY_ORDER * c
    ssn = ss[:, :ncf] + ss[:, ncf:]
    h5 = h.reshape(1, nah, nq, 8, 2 * ncf)
    ka = pl.pallas_call(
        functools.partial(_hy_fdft1_kernel, qb=qb),
        out_shape=jax.ShapeDtypeStruct((1, 2, na, nq, 8, ncf), F32), grid=(HY_ORDER, nq // qb),
        in_specs=[pl.BlockSpec(dc["g_full"].shape, lambda j, q: (0, 0)),
                  pl.BlockSpec((1, nah, qb, 8, c), lambda j, q: (0, 0, q, 0, j)),
                  pl.BlockSpec((1, nah, qb, 8, c), lambda j, q: (0, 0, q, 0, HY_ORDER + j))],
        out_specs=pl.BlockSpec((1, 2, na, qb, 8, c), lambda j, q: (0, 0, 0, q, 0, j)),
        compiler_params=_cparams(("parallel", "parallel"), 48), name="hyena_filter_dft1",
    )(dc["g_full"], h5, h5).reshape(2, na, nb, ncf)
    kb = 4 if na % 4 == 0 else 1
    reps = c // LANES
    tw = pl.BlockSpec((kb, nb, LANES), lambda j, k: (k, 0, 0))
    mat = pl.BlockSpec((2 * nb, 2 * nb), lambda j, k: (0, 0))
    hspec = pl.pallas_call(
        functools.partial(_hy_spec_kernel, kb=kb, reps=reps),
        out_shape=jax.ShapeDtypeStruct((2, na, nb, ncf), F32), grid=(ncf // c, na // kb),
        in_specs=[pl.BlockSpec((2, kb, nb, c), lambda j, k: (0, k, 0, j)), tw, tw, mat,
                  pl.BlockSpec((1, c), lambda j, k: (0, j))],
        out_specs=pl.BlockSpec((2, kb, nb, c), lambda j, k: (0, k, 0, j)),
        compiler_params=_cparams(("parallel", "parallel"), 48), name="hyena_filter_dft2",
    )(ka, dc["twr"], dc["twi"], dc["fb"], ssn)

    pc5 = pc.reshape(bsz, nah, nq, 8, 3 * c)

    def long_conv_gate(z5, order, xcol):
        a = _hy_dft1(dc["g_half"], z5, 1, "hyena_dft1").reshape(bsz, 2, na, nb, c)
        zmid = pl.pallas_call(
            functools.partial(_hy_mid_kernel, kb=kb, reps=reps),
            out_shape=jax.ShapeDtypeStruct((bsz, 2, na, nb, c), F32), grid=(bsz, na // kb),
            in_specs=[pl.BlockSpec((1, 2, kb, nb, c), lambda b, k: (b, 0, k, 0, 0)),
                      pl.BlockSpec((2, kb, nb, c), lambda b, k: (0, k, 0, order)), tw, tw, mat, mat],
            out_specs=pl.BlockSpec((1, 2, kb, nb, c), lambda b, k: (b, 0, k, 0, 0)),
            compiler_params=_cparams(("parallel", "parallel"), 48), name="hyena_dft_mid",
        )(a, hspec, dc["twr"], dc["twi"], dc["fb"], dc["fbc"])
        zmid = zmid.reshape(bsz, 2, na, nq, 8, c)
        sig = lambda col: pl.BlockSpec((1, nah, qb, 8, c), lambda b, q: (b, 0, q, 0, col))
        return pl.pallas_call(
            functools.partial(_hy_out_kernel, qb=qb),
            out_shape=jax.ShapeDtypeStruct((bsz, nah, nq, 8, c), F32), grid=(bsz, nq // qb),
            in_specs=[pl.BlockSpec(dc["g_out"].shape, lambda b, q: (0, 0)),
                      pl.BlockSpec((1, 2, na, qb, 8, c), lambda b, q: (b, 0, 0, q, 0, 0)),
                      sig(xcol), sig(0), pl.BlockSpec((1, c), lambda b, q: (0, 0))],
            out_specs=sig(0),
            compiler_params=_cparams(("parallel", "parallel"), 48), name="hyena_idft_gate",
        )(dc["g_out"], zmid, pc5, z5, bias[order].astype(F32).reshape(1, c))

    z1 = long_conv_gate(pc5, 0, 1)
    z2 = long_conv_gate(z1, 1, 2)
    return z2.reshape(bsz, l, c)


def _ffn_kernel(x_ref, w1_ref, w3_ref, w2_ref, lw_ref, lb_ref, o_ref, xb_ref, acc_ref, *, nf):
    f = pl.program_id(1)

    @pl.when(f == 0)
    def _():
        xb_ref[...] = x_ref[...].astype(BF16)
        acc_ref[...] = jnp.zeros_like(acc_ref)

    xb = xb_ref[...]
    a = jnp.dot(xb, w1_ref[...], preferred_element_type=F32)
    b = jnp.dot(xb, w3_ref[...], preferred_element_type=F32)
    acc_ref[...] += jnp.dot((_silu(a) * b).astype(BF16), w2_ref[...], preferred_element_type=F32)

    @pl.when(f == nf - 1)
    def _():
        o_ref[...] = _ln_core(DN_ALPHA * x_ref[...] + acc_ref[...], lw_ref[...], lb_ref[...])


def _ffn_ln(x, w1, w3, w2, lw, lb):
    t, d = x.shape
    ff = w1.shape[1]
    tm = _tile(t, 1024)
    tf = 512 if ff % 512 == 0 else (256 if ff % 256 == 0 else ff)
    nf = ff // tf
    vec = pl.BlockSpec((1, d), lambda i, f: (0, 0))
    return pl.pallas_call(
        functools.partial(_ffn_kernel, nf=nf),
        out_shape=jax.ShapeDtypeStruct((t, d), F32), grid=(t // tm, nf),
        in_specs=[pl.BlockSpec((tm, d), lambda i, f: (i, 0)),
                  pl.BlockSpec((d, tf), lambda i, f: (0, f)),
                  pl.BlockSpec((d, tf), lambda i, f: (0, f)),
                  pl.BlockSpec((tf, d), lambda i, f: (f, 0)), vec, vec],
        out_specs=pl.BlockSpec((tm, d), lambda i, f: (i, 0)),
        scratch_shapes=[pltpu.VMEM((tm, d), BF16), pltpu.VMEM((tm, d), F32)],
        compiler_params=_cparams(("parallel", "arbitrary"), 52), name="swiglu_ffn_ln",
    )(x, w1, w3, w2, lw.reshape(1, d), lb.reshape(1, d))


MOE_SB = 832
MOE_NSB = 2
MOE_SUB = 256
MOE_CUM = 64
MOE_MAXP = -(-MOE_SB // MOE_SUB)


def _moe_kernel(cnt_ref, x_ref, cmb_ref, lt_ref, w13_ref, w2_ref, lw_ref, lb_ref, o_ref,
                xb_ref, xs_ref, ys_ref, gs_ref, pos_ref, *, nf, nsb, t_total):
    i = pl.program_id(0)
    e = pl.program_id(1)
    f = pl.program_id(2)
    sb = x_ref.shape[0] // nsb
    npass = [(cnt_ref[(i * nsb + s) * N_EXPERTS + e] + (MOE_SUB - 1)) // MOE_SUB for s in range(nsb)]
    sub = [slice(s * sb, (s + 1) * sb) for s in range(nsb)]

    @pl.when((e == 0) & (f == 0))
    def _():
        o_ref[...] = jnp.zeros_like(o_ref)
        for s in range(nsb):
            valid = lax.broadcasted_iota(jnp.int32, (sb, 1), 0) < t_total - (i * nsb + s) * sb
            xb_ref[sub[s]] = jnp.where(valid, x_ref[sub[s]], 0.0).astype(BF16)
            carry = jnp.zeros((1, LANES), F32)
            for c in range(sb // MOE_CUM):
                rows = slice(s * sb + c * MOE_CUM, s * sb + (c + 1) * MOE_CUM)
                vrows = slice(c * MOE_CUM, (c + 1) * MOE_CUM)
                m = jnp.where(valid[vrows] & (cmb_ref[rows] > 0.0), 1.0, 0.0)
                inc = jnp.dot(lt_ref[...], m, preferred_element_type=F32) + carry
                pos_ref[rows] = jnp.where(m > 0.0, inc - 1.0, -1.0)
                carry = inc[MOE_CUM - 1:MOE_CUM]

    lane = lax.broadcasted_iota(jnp.int32, (sb, LANES), 1)

    def one_hot(s, j):
        pos = jnp.sum(jnp.where(lane == e, pos_ref[sub[s]], 0.0), axis=1, keepdims=True)
        slot = lax.broadcasted_iota(jnp.int32, (sb, MOE_SUB), 1).astype(F32) + (j * MOE_SUB).astype(F32)
        return pos == slot

    @pl.when(f == 0)
    def _():
        for s in range(nsb):
            gate = jnp.sum(jnp.where(lane == e, cmb_ref[sub[s]], 0.0), axis=1, keepdims=True)

            def gather(j, carry, s=s, gate=gate):
                hit = one_hot(s, j)
                k = s * MOE_MAXP + j
                xs_ref[k] = _dot_tn(xb_ref[sub[s]], jnp.where(hit, 1.0, 0.0).astype(BF16)).astype(BF16)
                g = jnp.sum(jnp.where(hit, gate, 0.0), axis=0, keepdims=True)
                gs_ref[k] = jnp.broadcast_to(g, (8, MOE_SUB))
                ys_ref[k] = jnp.zeros(ys_ref.shape[1:], F32)
                return carry

            lax.fori_loop(0, npass[s], gather, 0)

    for s in range(nsb):
        def expert(j, carry, s=s):
            k = s * MOE_MAXP + j
            xs = xs_ref[k]
            ab = jnp.dot(w13_ref[0, 0], xs, preferred_element_type=F32)
            tf = ab.shape[0] // 2
            hid = (_silu(ab[:tf]) * ab[tf:] * gs_ref[k][0:1]).astype(BF16)
            ys_ref[k] += jnp.dot(w2_ref[0], hid, preferred_element_type=F32)
            return carry

        lax.fori_loop(0, npass[s], expert, 0)

    @pl.when(f == nf - 1)
    def _():
        for s in range(nsb):
            def scatter(j, carry, s=s):
                hit = one_hot(s, j)
                o_ref[sub[s]] += _dot_nt(jnp.where(hit, 1.0, 0.0).astype(BF16),
                                         ys_ref[s * MOE_MAXP + j].astype(BF16))
                return carry

            lax.fori_loop(0, npass[s], scatter, 0)

        @pl.when(e == pl.num_programs(1) - 1)
        def _():
            o_ref[...] = _ln_core(DN_ALPHA * x_ref[...] + o_ref[...], lw_ref[...], lb_ref[...])


def _moe_ln(x, cmb, w1, w3, w2, lw, lb):
    t, d = x.shape
    ne, ff, _ = w1.shape
    nsb = MOE_NSB
    tb = nsb * MOE_SB
    nb = -(-t // tb)
    tf = 896 if ff % 896 == 0 else ff
    nf = ff // tf
    max_pass = nsb * MOE_MAXP
    cmb_p = jnp.pad(cmb, ((0, nb * tb - t), (0, 0)))
    counts = jnp.sum((cmb_p[:, :N_EXPERTS] > 0.0).reshape(nb * nsb, MOE_SB, N_EXPERTS), axis=1)
    counts = counts.astype(jnp.int32).reshape(-1)
    idx = np.arange(MOE_CUM)
    lt = jnp.asarray(idx[None, :] <= idx[:, None], F32)
    once = pl.Buffered(1)
    w13 = jnp.stack([w1.reshape(ne, nf, tf, d), w3.reshape(ne, nf, tf, d)], axis=2).reshape(ne, nf, 2 * tf, d)
    grid_spec = pltpu.PrefetchScalarGridSpec(
        num_scalar_prefetch=1, grid=(nb, ne, nf),
        in_specs=[pl.BlockSpec((tb, d), lambda i, e, f, c: (i, 0), pipeline_mode=once),
                  pl.BlockSpec((tb, LANES), lambda i, e, f, c: (i, 0), pipeline_mode=once),
                  pl.BlockSpec((MOE_CUM, MOE_CUM), lambda i, e, f, c: (0, 0)),
                  pl.BlockSpec((1, 1, 2 * tf, d), lambda i, e, f, c: (e, f, 0, 0)),
                  pl.BlockSpec((1, d, tf), lambda i, e, f, c: (e, 0, f)),
                  pl.BlockSpec((1, d), lambda i, e, f, c: (0, 0)),
                  pl.BlockSpec((1, d), lambda i, e, f, c: (0, 0))],
        out_specs=pl.BlockSpec((tb, d), lambda i, e, f, c: (i, 0), pipeline_mode=once),
        scratch_shapes=[pltpu.VMEM((tb, d), BF16), pltpu.VMEM((max_pass, d, MOE_SUB), BF16),
                        pltpu.VMEM((max_pass, d, MOE_SUB), F32), pltpu.VMEM((max_pass, 8, MOE_SUB), F32),
                        pltpu.VMEM((tb, LANES), F32)])
    return pl.pallas_call(
        functools.partial(_moe_kernel, nf=nf, nsb=nsb, t_total=t), out_shape=jax.ShapeDtypeStruct((t, d), F32),
        grid_spec=grid_spec,
        compiler_params=_cparams(("parallel", "arbitrary", "arbitrary"), 58), name="moe_routed",
    )(counts, x, cmb_p, lt, w13, w2, lw.reshape(1, d), lb.reshape(1, d))


def _router_kernel(x_ref, rh_ref, rl_ref, o_ref):
    x = x_ref[...]
    xh = x.astype(BF16)
    xl = (x - xh.astype(F32)).astype(BF16)
    logits = (jnp.dot(xh, rh_ref[...], preferred_element_type=F32)
              + jnp.dot(xl, rh_ref[...], preferred_element_type=F32)
              + jnp.dot(xh, rl_ref[...], preferred_element_type=F32))
    lane = lax.broadcasted_iota(jnp.int32, logits.shape, 1).astype(F32)
    logits = jnp.where(lane < N_EXPERTS, logits, -jnp.inf)
    m1 = jnp.max(logits, axis=1, keepdims=True)
    i1 = jnp.min(jnp.where(logits == m1, lane, float(LANES)), axis=1, keepdims=True)
    rest = jnp.where(lane == i1, -jnp.inf, logits)
    m2 = jnp.max(rest, axis=1, keepdims=True)
    i2 = jnp.min(jnp.where(rest == m2, lane, float(LANES)), axis=1, keepdims=True)
    e2 = jnp.exp(m2 - m1)
    g1 = 1.0 / (1.0 + e2)
    g2 = e2 / (1.0 + e2)
    o_ref[...] = jnp.where(lane == i1, g1, 0.0) + jnp.where(lane == i2, g2, 0.0)


def _router(x, router):
    t, d = x.shape
    r = jnp.pad(router.astype(F32), ((0, 0), (0, LANES - N_EXPERTS)))
    rh = r.astype(BF16)
    rl = (r - rh.astype(F32)).astype(BF16)
    tm = _tile(t, 1024)
    return pl.pallas_call(
        _router_kernel, out_shape=jax.ShapeDtypeStruct((t, LANES), F32), grid=(t // tm,),
        in_specs=[pl.BlockSpec((tm, d), lambda i: (i, 0)), pl.BlockSpec((d, LANES), lambda i: (0, 0)),
                  pl.BlockSpec((d, LANES), lambda i: (0, 0))],
        out_specs=pl.BlockSpec((tm, LANES), lambda i: (i, 0)),
        compiler_params=_cparams(("parallel",)), name="moe_router")(x, rh, rl)


def _extended_w_in(w_in):
    w = w_in.astype(F32)
    scale = HEAD_DIM ** -0.5

    def rot_half(cols):
        c4 = cols.reshape(-1, N_HEADS, 2, HEAD_DIM // 2)
        return jnp.stack([-c4[:, :, 1], c4[:, :, 0]], axis=2).reshape(-1, D_GROUP)

    wq = w[:, 0:256]
    wk = w[:, 256:512] * scale
    main = jnp.concatenate([wq, wk, w[:, 512:3072]], axis=1)
    gates = jnp.pad(w[:, 3072:3088], ((0, 0), (0, LANES - 16)))
    ext = jnp.concatenate([main, rot_half(wq), rot_half(wk), gates], axis=1)
    return jnp.pad(ext, ((0, 0), (0, N_EXT - ext.shape[1]))).astype(BF16)


def kernel(x, ln_in_w, ln_in_b, w_in, w_out, ret_gn_w, s5_a_re, s5_a_im, s5_log_dt, s5_b_re, s5_b_im, s5_c_re, s5_c_im, s5_d, s5_w_glu, hy_conv_w, hy_conv_b, hy_w1, hy_b1, hy_w2, hy_b2, hy_w3, hy_freq, hy_bias, ml_conv_w, ml_conv_b, ml_gate_b, ml_gn_w, ln1_w, ln1_b, ln2_w, ln2_b, ffn_w1, ffn_w3, ffn_w2, moe_router, moe_w1, moe_w3, moe_w2):
    bsz, l, d = x.shape
    t = bsz * l
    cos_full, sin_full = _rope_tables(l)
    h = _layer_norm(x.reshape(t, d), ln_in_w, ln_in_b)
    for layer in range(DEPTH):
        proj, gates = _in_proj(h, _extended_w_in(w_in[layer]))
        proj, gates = proj.reshape(bsz, l, N_EXT), gates.reshape(bsz, l, LANES)
        y_ret = _retention(proj, ret_gn_w[layer], cos_full, sin_full)
        y_s5 = _s5(proj, s5_a_re[layer], s5_a_im[layer], s5_log_dt[layer], s5_b_re[layer], s5_b_im[layer],
                   s5_c_re[layer], s5_c_im[layer], s5_d[layer], s5_w_glu[layer])
        y_hy = _hyena(proj, hy_conv_w[layer], hy_conv_b[layer], hy_w1[layer], hy_b1[layer], hy_w2[layer],
                      hy_b2[layer], hy_w3[layer], hy_freq[layer], hy_bias[layer])
        qk = _shortconv(proj, CB_MQ, 2, ml_conv_w[layer], ml_conv_b[layer], act=True)
        y_ml = _mlstm(proj, qk, gates, ml_gate_b[layer], ml_gn_w[layer])
        ys = [y.reshape(t, D_GROUP) for y in (y_ret, y_s5, y_hy, y_ml)]
        h = _outproj_ln(ys, w_out[layer], h, ln1_w[layer], ln1_b[layer])
        j = layer // 2
        if layer % 2 == 0:
            h = _ffn_ln(h, ffn_w1[j].astype(BF16), ffn_w3[j].astype(BF16), ffn_w2[j].astype(BF16),
                        ln2_w[layer], ln2_b[layer])
        else:
            cmb = _router(h, moe_router[j])
            wt = [jnp.swapaxes(w[j], 1, 2).astype(BF16) for w in (moe_w1, moe_w3, moe_w2)]
            h = _moe_ln(h, cmb, *wt, ln2_w[layer], ln2_b[layer])
    return h.reshape(bsz, l, d)
```
